```python
import math
import jax
import jax.numpy as jnp
from jax import lax
import numpy as np

D_MODEL = 2048
BATCH = 8
SEQ = 8192
DEPTH = 4

GDN_HEADS = 8
GDN_HEAD_DIM = 128
GDN_WIDTH = GDN_HEADS * GDN_HEAD_DIM
S5_GROUP_SIZE = 16
S5_GROUPS = 48
S5_STATE = 64
S5_WIDTH = S5_GROUPS * S5_GROUP_SIZE
M2_HEADS = 16
M2_HEAD_DIM = 64
M2_WIDTH = M2_HEADS * M2_HEAD_DIM
M2_GROUPS = 4
M2_STATE = 128
M2_CONV_DIM = M2_WIDTH + 2 * M2_GROUPS * M2_STATE
CONV_K = 4
CHUNK = 64
N_BRANCHES = 3
NORM_EPS = 1e-6
IN_SPLITS = (3 * GDN_WIDTH, GDN_WIDTH, GDN_HEADS, GDN_HEADS,
             S5_WIDTH, S5_WIDTH,
             M2_WIDTH, M2_CONV_DIM, M2_HEADS,
             N_BRANCHES * D_MODEL)
IN_DIM = 4 * GDN_WIDTH + 2 * GDN_HEADS + 2 * S5_WIDTH + M2_WIDTH + M2_CONV_DIM + M2_HEADS + N_BRANCHES * D_MODEL

kernel_name = 'hybrid_gdn_s5_ssd_gated_merge'


def rms_norm(x, w):
    x32 = x.astype(jnp.float32)
    y = x32 * lax.rsqrt(jnp.mean(x32 * x32, axis=-1, keepdims=True) + NORM_EPS)
    return (y * w.astype(jnp.float32)).astype(x.dtype)


def _rms_f32(x, w):
    return x * lax.rsqrt(jnp.mean(x * x, axis=-1, keepdims=True) + NORM_EPS) * w


def _l2norm(x):
    return x * lax.rsqrt(jnp.sum(x * x, axis=-1, keepdims=True) + NORM_EPS)


def causal_depthwise_conv(x, w, b=None):
    k, c = w.shape
    y = lax.conv_general_dilated(x, w[:, None, :].astype(x.dtype), window_strides=(1,),
                                 padding=[(k - 1, 0)], dimension_numbers=('NWC', 'WIO', 'NWC'),
                                 feature_group_count=c)
    if b is not None:
        y = y + b.astype(x.dtype)
    return y


def gated_delta_rule_chunked(q, k, v, g, beta):
    bsz, seqlen, heads, dk = q.shape
    dv = v.shape[-1]
    n = seqlen // CHUNK

    def to_chunks(t):
        return t.reshape(bsz, n, CHUNK, heads, -1).transpose(0, 1, 3, 2, 4)

    q = to_chunks(q) * (dk ** -0.5)
    k = to_chunks(k)
    v = to_chunks(v)
    beta = beta.reshape(bsz, n, CHUNK, heads).transpose(0, 1, 3, 2)
    g_cum = jnp.cumsum(g.reshape(bsz, n, CHUNK, heads).transpose(0, 1, 3, 2), axis=-1)
    causal = jnp.tril(jnp.ones((CHUNK, CHUNK), dtype=bool))
    strict = jnp.tril(jnp.ones((CHUNK, CHUNK), dtype=bool), k=-1)
    decay = jnp.exp(jnp.where(causal, g_cum[..., :, None] - g_cum[..., None, :], -jnp.inf))
    k_beta = k * beta[..., None]
    a_mat = jnp.where(strict, jnp.einsum('bnhid,bnhjd->bnhij', k_beta, k) * decay, 0.0)
    eye = jnp.eye(CHUNK, dtype=q.dtype)
    t_inv = lax.linalg.triangular_solve(a_mat + eye, jnp.broadcast_to(eye, a_mat.shape),
                                        left_side=True, lower=True)
    u = t_inv @ (v * beta[..., None])
    w = t_inv @ (k_beta * jnp.exp(g_cum)[..., None])
    qk = jnp.einsum('bnhid,bnhjd->bnhij', q, k) * decay
    q_dec = q * jnp.exp(g_cum)[..., None]
    g_last = g_cum[..., -1]
    k_tail = k * jnp.exp(g_last[..., None] - g_cum)[..., None]

    def step(state, inp):
        q_i, qk_i, u_i, w_i, kt_i, gl_i = inp
        v_new = u_i - jnp.einsum('bhck,bhkv->bhcv', w_i, state)
        out = jnp.einsum('bhck,bhkv->bhcv', q_i, state) + jnp.einsum('bhij,bhjv->bhiv', qk_i, v_new)
        state = state * jnp.exp(gl_i)[..., None, None] + jnp.einsum('bhck,bhcv->bhkv', kt_i, v_new)
        return state, out

    init = jnp.zeros((bsz, heads, dk, dv), q.dtype)
    xs = tuple(jnp.moveaxis(t, 1, 0) for t in (q_dec, qk, u, w, k_tail, g_last))
    _, out = lax.scan(step, init, xs)
    return out.transpose(1, 0, 3, 2, 4).reshape(bsz, seqlen, heads, dv)


def gdn_branch(qkv, z, b_raw, a_raw, conv_w, a_log, dt_bias, norm_w):
    bsz, seqlen, _ = qkv.shape
    f32 = jnp.float32
    qkv = jax.nn.silu(causal_depthwise_conv(qkv.astype(f32), conv_w.astype(f32)))
    q, k, v = jnp.split(qkv, 3, axis=-1)
    shp = (bsz, seqlen, GDN_HEADS, GDN_HEAD_DIM)
    q = _l2norm(q.reshape(shp))
    k = _l2norm(k.reshape(shp))
    v = v.reshape(shp)
    beta = jax.nn.sigmoid(b_raw.astype(f32))
    g = -jnp.exp(a_log.astype(f32)) * jax.nn.softplus(a_raw.astype(f32) + dt_bias.astype(f32))
    o = gated_delta_rule_chunked(q, k, v, g, beta)
    o = _rms_f32(o, norm_w.astype(f32)) * jax.nn.silu(z.astype(f32).reshape(shp))
    return o.reshape(bsz, seqlen, GDN_WIDTH)


def _complex_affine_combine(left, right):
    a1r, a1i, b1r, b1i = left
    a2r, a2i, b2r, b2i = right
    return (a2r * a1r - a2i * a1i,
            a2r * a1i + a2i * a1r,
            a2r * b1r - a2i * b1i + b2r,
            a2r * b1i + a2i * b1r + b2i)


def s5_branch(u, gate, lam_re, lam_im, log_step, b_re, b_im, c_re, c_im, d_skip, glu_w, glu_b):
    bsz, seqlen, _ = u.shape
    f32 = jnp.float32
    u = u.astype(f32).reshape(bsz, seqlen, S5_GROUPS, S5_GROUP_SIZE)
    lam_re = jnp.minimum(lam_re.astype(f32), -1e-4)
    lam_im = lam_im.astype(f32)
    step = jnp.exp(log_step.astype(f32))[:, None]
    mag = jnp.exp(lam_re * step)
    ab_re = mag * jnp.cos(lam_im * step)
    ab_im = mag * jnp.sin(lam_im * step)
    den = lam_re * lam_re + lam_im * lam_im
    f_re = ((ab_re - 1.0) * lam_re + ab_im * lam_im) / den
    f_im = (ab_im * lam_re - (ab_re - 1.0) * lam_im) / den
    b_re = b_re.astype(f32)
    b_im = b_im.astype(f32)
    bb_re = f_re[..., None] * b_re - f_im[..., None] * b_im
    bb_im = f_re[..., None] * b_im + f_im[..., None] * b_re
    bu_re = jnp.einsum('blgh,gph->blgp', u, bb_re)
    bu_im = jnp.einsum('blgh,gph->blgp', u, bb_im)
    a_re = jnp.broadcast_to(ab_re, (seqlen,) + ab_re.shape)
    a_im = jnp.broadcast_to(ab_im, (seqlen,) + ab_im.shape)

    def scan_one(br, bi):
        return lax.associative_scan(_complex_affine_combine, (a_re, a_im, br, bi), axis=0)[2:]

    s_re, s_im = jax.vmap(scan_one)(bu_re, bu_im)
    y = (jnp.einsum('blgp,ghp->blgh', s_re, c_re.astype(f32))
         - jnp.einsum('blgp,ghp->blgh', s_im, c_im.astype(f32))
         + d_skip.astype(f32) * u)
    y = jax.nn.gelu(y.reshape(bsz, seqlen, S5_WIDTH))
    y = y * jax.nn.sigmoid(y @ glu_w.astype(f32) + glu_b.astype(f32))
    return y * jax.nn.silu(gate.astype(f32))


def ssd_chunked(x, a, bm, cm):
    bsz, seqlen, heads, hd = x.shape
    groups, dstate = bm.shape[2], bm.shape[3]
    rep = heads // groups
    n = seqlen // CHUNK
    x = x.reshape(bsz, n, CHUNK, groups, rep, hd)
    bm = bm.reshape(bsz, n, CHUNK, groups, dstate)
    cm = cm.reshape(bsz, n, CHUNK, groups, dstate)
    a_cum = jnp.cumsum(a.reshape(bsz, n, CHUNK, groups, rep).transpose(0, 1, 3, 4, 2), axis=-1)
    causal = jnp.tril(jnp.ones((CHUNK, CHUNK), dtype=bool))
    seg = jnp.exp(jnp.where(causal, a_cum[..., :, None] - a_cum[..., None, :], -jnp.inf))
    scores = jnp.einsum('bclgn,bcsgn->bcgls', cm, bm)
    y_diag = jnp.einsum('bcgls,bcgrls,bcsgrp->bclgrp', scores, seg, x)
    decay_states = jnp.exp(a_cum[..., -1:] - a_cum)
    states = jnp.einsum('bclgn,bcgrl,bclgrp->bcgrpn', bm, decay_states, x)
    chunk_decay = jnp.exp(a_cum[..., -1])

    def step(s, inp):
        st, dec = inp
        return s * dec[..., None, None] + st, s

    init = jnp.zeros((bsz, groups, rep, hd, dstate), x.dtype)
    _, s_prev = lax.scan(step, init, (jnp.moveaxis(states, 1, 0), jnp.moveaxis(chunk_decay, 1, 0)))
    s_prev = jnp.moveaxis(s_prev, 0, 1)
    y_off = jnp.einsum('bclgn,bcgrpn,bcgrl->bclgrp', cm, s_prev, jnp.exp(a_cum))
    return (y_diag + y_off).reshape(bsz, seqlen, heads, hd)


def mamba2_branch(z, xbc, dt_raw, conv_w, conv_b, a_log, dt_bias, d_skip, norm_w):
    bsz, seqlen, _ = z.shape
    f32 = jnp.float32
    xbc = jax.nn.silu(causal_depthwise_conv(xbc.astype(f32), conv_w.astype(f32), conv_b.astype(f32)))
    xs, bm, cm = jnp.split(xbc, [M2_WIDTH, M2_WIDTH + M2_GROUPS * M2_STATE], axis=-1)
    xs = xs.reshape(bsz, seqlen, M2_HEADS, M2_HEAD_DIM)
    bm = bm.reshape(bsz, seqlen, M2_GROUPS, M2_STATE)
    cm = cm.reshape(bsz, seqlen, M2_GROUPS, M2_STATE)
    dt = jax.nn.softplus(dt_raw.astype(f32) + dt_bias.astype(f32))
    a = -jnp.exp(a_log.astype(f32))
    y = ssd_chunked(xs * dt[..., None], a * dt, bm, cm) + d_skip.astype(f32)[:, None] * xs
    y = y.reshape(bsz, seqlen, M2_WIDTH) * jax.nn.silu(z.astype(f32))
    y = _rms_f32(y.reshape(bsz, seqlen, M2_GROUPS, M2_WIDTH // M2_GROUPS), 1.0)
    return y.reshape(bsz, seqlen, M2_WIDTH) * norm_w.astype(f32)


def hybrid_layer(x, norm_w, w_in, gdn_conv_w, gdn_a_log, gdn_dt_bias, gdn_norm_w,
                 s5_lam_re, s5_lam_im, s5_log_step, s5_b_re, s5_b_im, s5_c_re, s5_c_im, s5_d,
                 s5_glu_w, s5_glu_b, m2_conv_w, m2_conv_b, m2_a_log, m2_dt_bias, m2_d, m2_norm_w,
                 proj_a, proj_b, proj_c, w_out):
    h = rms_norm(x, norm_w)
    proj = h @ w_in
    idx = [int(i) for i in np.cumsum(IN_SPLITS)[:-1]]
    (qkv, a_z, a_beta, a_decay, s_u, s_gate, c_z, c_xbc, c_dt, merge) = jnp.split(proj, idx, axis=-1)
    y_a = gdn_branch(qkv, a_z, a_beta, a_decay, gdn_conv_w, gdn_a_log, gdn_dt_bias, gdn_norm_w)
    y_b = s5_branch(s_u, s_gate, s5_lam_re, s5_lam_im, s5_log_step, s5_b_re, s5_b_im,
                    s5_c_re, s5_c_im, s5_d, s5_glu_w, s5_glu_b)
    y_c = mamba2_branch(c_z, c_xbc, c_dt, m2_conv_w, m2_conv_b, m2_a_log, m2_dt_bias, m2_d, m2_norm_w)
    g_a, g_b, g_c = jnp.split(jax.nn.sigmoid(merge), N_BRANCHES, axis=-1)
    dt = x.dtype
    merged = (g_a * (y_a.astype(dt) @ proj_a)
              + g_b * (y_b.astype(dt) @ proj_b)
              + g_c * (y_c.astype(dt) @ proj_c))
    return x + merged @ w_out


def _inv_softplus(y):
    return y + jnp.log(-jnp.expm1(-y))


def _log_uniform(key, shape, lo, hi):
    return jnp.exp(jax.random.uniform(key, shape, jnp.float32, math.log(lo), math.log(hi)))


def _fwd_setup_inputs(seed: int = 0) -> dict:
    key = jax.random.key(seed)
    ks = jax.random.split(key, 32)
    f32 = jnp.float32

    def nrm(k, shape, scale):
        return scale * jax.random.normal(k, shape, f32)

    x = jax.random.normal(ks[0], (BATCH, SEQ, D_MODEL), f32)
    norm_w = 1.0 + nrm(ks[1], (DEPTH, D_MODEL), 0.02)
    w_in = nrm(ks[2], (DEPTH, D_MODEL, IN_DIM), D_MODEL ** -0.5)
    gdn_conv_w = nrm(ks[3], (DEPTH, CONV_K, 3 * GDN_WIDTH), CONV_K ** -0.5)
    gdn_a_log = jnp.log(jax.random.uniform(ks[4], (DEPTH, GDN_HEADS), f32, 1.0, 16.0))
    gdn_dt_bias = _inv_softplus(_log_uniform(ks[5], (DEPTH, GDN_HEADS), 1e-3, 1e-1))
    gdn_norm_w = 1.0 + nrm(ks[6], (DEPTH, GDN_HEAD_DIM), 0.02)
    n_idx = jnp.arange(S5_STATE, dtype=f32)
    s5_lam_re = -0.5 + nrm(ks[7], (DEPTH, S5_GROUPS, S5_STATE), 1e-3)
    s5_lam_im = math.pi * n_idx + nrm(ks[8], (DEPTH, S5_GROUPS, S5_STATE), 1e-3)
    s5_log_step = jax.random.uniform(ks[9], (DEPTH, S5_GROUPS), f32, math.log(1e-3), math.log(1e-1))
    s5_b_re = nrm(ks[10], (DEPTH, S5_GROUPS, S5_STATE, S5_GROUP_SIZE), S5_GROUP_SIZE ** -0.5)
    s5_b_im = nrm(ks[11], (DEPTH, S5_GROUPS, S5_STATE, S5_GROUP_SIZE), S5_GROUP_SIZE ** -0.5)
    s5_c_re = nrm(ks[12], (DEPTH, S5_GROUPS, S5_GROUP_SIZE, S5_STATE), 0.5)
    s5_c_im = nrm(ks[13], (DEPTH, S5_GROUPS, S5_GROUP_SIZE, S5_STATE), 0.5)
    s5_d = nrm(ks[14], (DEPTH, S5_GROUPS, S5_GROUP_SIZE), 1.0)
    s5_glu_w = nrm(ks[15], (DEPTH, S5_WIDTH, S5_WIDTH), S5_WIDTH ** -0.5)
    s5_glu_b = nrm(ks[16], (DEPTH, S5_WIDTH), 0.01)
    m2_conv_w = nrm(ks[17], (DEPTH, CONV_K, M2_CONV_DIM), CONV_K ** -0.5)
    m2_conv_b = nrm(ks[18], (DEPTH, M2_CONV_DIM), 0.01)
    m2_a_log = jnp.log(jax.random.uniform(ks[19], (DEPTH, M2_HEADS), f32, 1.0, 16.0))
    m2_dt_bias = _inv_softplus(_log_uniform(ks[20], (DEPTH, M2_HEADS), 1e-3, 1e-1))
    m2_d = 1.0 + nrm(ks[21], (DEPTH, M2_HEADS), 0.1)
    m2_norm_w = 1.0 + nrm(ks[22], (DEPTH, M2_WIDTH), 0.02)
    proj_a = nrm(ks[23], (DEPTH, GDN_WIDTH, D_MODEL), GDN_WIDTH ** -0.5)
    proj_b = nrm(ks[24], (DEPTH, S5_WIDTH, D_MODEL), S5_WIDTH ** -0.5)
    proj_c = nrm(ks[25], (DEPTH, M2_WIDTH, D_MODEL), M2_WIDTH ** -0.5)
    w_out = nrm(ks[26], (DEPTH, D_MODEL, D_MODEL), D_MODEL ** -0.5)
    final_norm_w = 1.0 + nrm(ks[27], (D_MODEL,), 0.02)
    return {'x': x, 'norm_w': norm_w, 'w_in': w_in,
            'gdn_conv_w': gdn_conv_w, 'gdn_a_log': gdn_a_log, 'gdn_dt_bias': gdn_dt_bias, 'gdn_norm_w': gdn_norm_w,
            's5_lam_re': s5_lam_re, 's5_lam_im': s5_lam_im, 's5_log_step': s5_log_step,
            's5_b_re': s5_b_re, 's5_b_im': s5_b_im, 's5_c_re': s5_c_re, 's5_c_im': s5_c_im, 's5_d': s5_d,
            's5_glu_w': s5_glu_w, 's5_glu_b': s5_glu_b,
            'm2_conv_w': m2_conv_w, 'm2_conv_b': m2_conv_b, 'm2_a_log': m2_a_log, 'm2_dt_bias': m2_dt_bias,
            'm2_d': m2_d, 'm2_norm_w': m2_norm_w,
            'proj_a': proj_a, 'proj_b': proj_b, 'proj_c': proj_c, 'w_out': w_out,
            'final_norm_w': final_norm_w}


def _fwd_reference(x, norm_w, w_in, gdn_conv_w, gdn_a_log, gdn_dt_bias, gdn_norm_w,
              s5_lam_re, s5_lam_im, s5_log_step, s5_b_re, s5_b_im, s5_c_re, s5_c_im, s5_d,
              s5_glu_w, s5_glu_b, m2_conv_w, m2_conv_b, m2_a_log, m2_dt_bias, m2_d, m2_norm_w,
              proj_a, proj_b, proj_c, w_out, final_norm_w):
    for i in range(DEPTH):
        x = hybrid_layer(x, norm_w[i], w_in[i], gdn_conv_w[i], gdn_a_log[i], gdn_dt_bias[i], gdn_norm_w[i],
                         s5_lam_re[i], s5_lam_im[i], s5_log_step[i], s5_b_re[i], s5_b_im[i],
                         s5_c_re[i], s5_c_im[i], s5_d[i], s5_glu_w[i], s5_glu_b[i],
                         m2_conv_w[i], m2_conv_b[i], m2_a_log[i], m2_dt_bias[i], m2_d[i], m2_norm_w[i],
                         proj_a[i], proj_b[i], proj_c[i], w_out[i])
    return rms_norm(x, final_norm_w)


import jax as _jax
import jax.numpy as _jnp

TWIN_FORMAT = 'train_step'
FWD_PARAMS = ['x', 'norm_w', 'w_in', 'gdn_conv_w', 'gdn_a_log', 'gdn_dt_bias', 'gdn_norm_w', 's5_lam_re', 's5_lam_im', 's5_log_step', 's5_b_re', 's5_b_im', 's5_c_re', 's5_c_im', 's5_d', 's5_glu_w', 's5_glu_b', 'm2_conv_w', 'm2_conv_b', 'm2_a_log', 'm2_dt_bias', 'm2_d', 'm2_norm_w', 'proj_a', 'proj_b', 'proj_c', 'w_out', 'final_norm_w']
TWIN_WEIGHTS = ['norm_w', 'w_in', 'gdn_conv_w', 'gdn_a_log', 'gdn_dt_bias', 'gdn_norm_w', 's5_lam_re', 's5_lam_im', 's5_log_step', 's5_b_re', 's5_b_im', 's5_c_re', 's5_c_im', 's5_d', 's5_glu_w', 's5_glu_b', 'm2_conv_w', 'm2_conv_b', 'm2_a_log', 'm2_dt_bias', 'm2_d', 'm2_norm_w', 'proj_a', 'proj_b', 'proj_c', 'w_out', 'final_norm_w']
TWIN_DIFF_INPUT = 'x'
TWIN_INPUTS = ['x', 'norm_w', 'w_in', 'gdn_conv_w', 'gdn_a_log', 'gdn_dt_bias', 'gdn_norm_w', 's5_lam_re', 's5_lam_im', 's5_log_step', 's5_b_re', 's5_b_im', 's5_c_re', 's5_c_im', 's5_d', 's5_glu_w', 's5_glu_b', 'm2_conv_w', 'm2_conv_b', 'm2_a_log', 'm2_dt_bias', 'm2_d', 'm2_norm_w', 'proj_a', 'proj_b', 'proj_c', 'w_out', 'final_norm_w', 'loss_target', 'm_norm_w', 'm_w_in', 'm_gdn_conv_w', 'm_gdn_a_log', 'm_gdn_dt_bias', 'm_gdn_norm_w', 'm_s5_lam_re', 'm_s5_lam_im', 'm_s5_log_step', 'm_s5_b_re', 'm_s5_b_im', 'm_s5_c_re', 'm_s5_c_im', 'm_s5_d', 'm_s5_glu_w', 'm_s5_glu_b', 'm_m2_conv_w', 'm_m2_conv_b', 'm_m2_a_log', 'm_m2_dt_bias', 'm_m2_d', 'm_m2_norm_w', 'm_proj_a', 'm_proj_b', 'm_proj_c', 'm_w_out', 'm_final_norm_w', 'v_norm_w', 'v_w_in', 'v_gdn_conv_w', 'v_gdn_a_log', 'v_gdn_dt_bias', 'v_gdn_norm_w', 'v_s5_lam_re', 'v_s5_lam_im', 'v_s5_log_step', 'v_s5_b_re', 'v_s5_b_im', 'v_s5_c_re', 'v_s5_c_im', 'v_s5_d', 'v_s5_glu_w', 'v_s5_glu_b', 'v_m2_conv_w', 'v_m2_conv_b', 'v_m2_a_log', 'v_m2_dt_bias', 'v_m2_d', 'v_m2_norm_w', 'v_proj_a', 'v_proj_b', 'v_proj_c', 'v_w_out', 'v_final_norm_w']
TWIN_OUTPUTS = ['loss', 'grad_x', 'grad_norm_w', 'grad_w_in', 'grad_gdn_conv_w', 'grad_gdn_a_log', 'grad_gdn_dt_bias', 'grad_gdn_norm_w', 'grad_s5_lam_re', 'grad_s5_lam_im', 'grad_s5_log_step', 'grad_s5_b_re', 'grad_s5_b_im', 'grad_s5_c_re', 'grad_s5_c_im', 'grad_s5_d', 'grad_s5_glu_w', 'grad_s5_glu_b', 'grad_m2_conv_w', 'grad_m2_conv_b', 'grad_m2_a_log', 'grad_m2_dt_bias', 'grad_m2_d', 'grad_m2_norm_w', 'grad_proj_a', 'grad_proj_b', 'grad_proj_c', 'grad_w_out', 'grad_final_norm_w', 'delta_norm_w', 'delta_w_in', 'delta_gdn_conv_w', 'delta_gdn_a_log', 'delta_gdn_dt_bias', 'delta_gdn_norm_w', 'delta_s5_lam_re', 'delta_s5_lam_im', 'delta_s5_log_step', 'delta_s5_b_re', 'delta_s5_b_im', 'delta_s5_c_re', 'delta_s5_c_im', 'delta_s5_d', 'delta_s5_glu_w', 'delta_s5_glu_b', 'delta_m2_conv_w', 'delta_m2_conv_b', 'delta_m2_a_log', 'delta_m2_dt_bias', 'delta_m2_d', 'delta_m2_norm_w', 'delta_proj_a', 'delta_proj_b', 'delta_proj_c', 'delta_w_out', 'delta_final_norm_w', 'new_m_norm_w', 'new_m_w_in', 'new_m_gdn_conv_w', 'new_m_gdn_a_log', 'new_m_gdn_dt_bias', 'new_m_gdn_norm_w', 'new_m_s5_lam_re', 'new_m_s5_lam_im', 'new_m_s5_log_step', 'new_m_s5_b_re', 'new_m_s5_b_im', 'new_m_s5_c_re', 'new_m_s5_c_im', 'new_m_s5_d', 'new_m_s5_glu_w', 'new_m_s5_glu_b', 'new_m_m2_conv_w', 'new_m_m2_conv_b', 'new_m_m2_a_log', 'new_m_m2_dt_bias', 'new_m_m2_d', 'new_m_m2_norm_w', 'new_m_proj_a', 'new_m_proj_b', 'new_m_proj_c', 'new_m_w_out', 'new_m_final_norm_w', 'new_v_norm_w', 'new_v_w_in', 'new_v_gdn_conv_w', 'new_v_gdn_a_log', 'new_v_gdn_dt_bias', 'new_v_gdn_norm_w', 'new_v_s5_lam_re', 'new_v_s5_lam_im', 'new_v_s5_log_step', 'new_v_s5_b_re', 'new_v_s5_b_im', 'new_v_s5_c_re', 'new_v_s5_c_im', 'new_v_s5_d', 'new_v_s5_glu_w', 'new_v_s5_glu_b', 'new_v_m2_conv_w', 'new_v_m2_conv_b', 'new_v_m2_a_log', 'new_v_m2_dt_bias', 'new_v_m2_d', 'new_v_m2_norm_w', 'new_v_proj_a', 'new_v_proj_b', 'new_v_proj_c', 'new_v_w_out', 'new_v_final_norm_w']
TWIN_LEAF_KINDS = {'loss': 'loss', 'grad_x': 'grad_x', 'grad_norm_w': 'grad_w', 'grad_w_in': 'grad_w', 'grad_gdn_conv_w': 'grad_w', 'grad_gdn_a_log': 'grad_w', 'grad_gdn_dt_bias': 'grad_w', 'grad_gdn_norm_w': 'grad_w', 'grad_s5_lam_re': 'grad_w', 'grad_s5_lam_im': 'grad_w', 'grad_s5_log_step': 'grad_w', 'grad_s5_b_re': 'grad_w', 'grad_s5_b_im': 'grad_w', 'grad_s5_c_re': 'grad_w', 'grad_s5_c_im': 'grad_w', 'grad_s5_d': 'grad_w', 'grad_s5_glu_w': 'grad_w', 'grad_s5_glu_b': 'grad_w', 'grad_m2_conv_w': 'grad_w', 'grad_m2_conv_b': 'grad_w', 'grad_m2_a_log': 'grad_w', 'grad_m2_dt_bias': 'grad_w', 'grad_m2_d': 'grad_w', 'grad_m2_norm_w': 'grad_w', 'grad_proj_a': 'grad_w', 'grad_proj_b': 'grad_w', 'grad_proj_c': 'grad_w', 'grad_w_out': 'grad_w', 'grad_final_norm_w': 'grad_w', 'delta_norm_w': 'delta_w', 'delta_w_in': 'delta_w', 'delta_gdn_conv_w': 'delta_w', 'delta_gdn_a_log': 'delta_w', 'delta_gdn_dt_bias': 'delta_w', 'delta_gdn_norm_w': 'delta_w', 'delta_s5_lam_re': 'delta_w', 'delta_s5_lam_im': 'delta_w', 'delta_s5_log_step': 'delta_w', 'delta_s5_b_re': 'delta_w', 'delta_s5_b_im': 'delta_w', 'delta_s5_c_re': 'delta_w', 'delta_s5_c_im': 'delta_w', 'delta_s5_d': 'delta_w', 'delta_s5_glu_w': 'delta_w', 'delta_s5_glu_b': 'delta_w', 'delta_m2_conv_w': 'delta_w', 'delta_m2_conv_b': 'delta_w', 'delta_m2_a_log': 'delta_w', 'delta_m2_dt_bias': 'delta_w', 'delta_m2_d': 'delta_w', 'delta_m2_norm_w': 'delta_w', 'delta_proj_a': 'delta_w', 'delta_proj_b': 'delta_w', 'delta_proj_c': 'delta_w', 'delta_w_out': 'delta_w', 'delta_final_norm_w': 'delta_w', 'new_m_norm_w': 'new_m', 'new_m_w_in': 'new_m', 'new_m_gdn_conv_w': 'new_m', 'new_m_gdn_a_log': 'new_m', 'new_m_gdn_dt_bias': 'new_m', 'new_m_gdn_norm_w': 'new_m', 'new_m_s5_lam_re': 'new_m', 'new_m_s5_lam_im': 'new_m', 'new_m_s5_log_step': 'new_m', 'new_m_s5_b_re': 'new_m', 'new_m_s5_b_im': 'new_m', 'new_m_s5_c_re': 'new_m', 'new_m_s5_c_im': 'new_m', 'new_m_s5_d': 'new_m', 'new_m_s5_glu_w': 'new_m', 'new_m_s5_glu_b': 'new_m', 'new_m_m2_conv_w': 'new_m', 'new_m_m2_conv_b': 'new_m', 'new_m_m2_a_log': 'new_m', 'new_m_m2_dt_bias': 'new_m', 'new_m_m2_d': 'new_m', 'new_m_m2_norm_w': 'new_m', 'new_m_proj_a': 'new_m', 'new_m_proj_b': 'new_m', 'new_m_proj_c': 'new_m', 'new_m_w_out': 'new_m', 'new_m_final_norm_w': 'new_m', 'new_v_norm_w': 'new_v', 'new_v_w_in': 'new_v', 'new_v_gdn_conv_w': 'new_v', 'new_v_gdn_a_log': 'new_v', 'new_v_gdn_dt_bias': 'new_v', 'new_v_gdn_norm_w': 'new_v', 'new_v_s5_lam_re': 'new_v', 'new_v_s5_lam_im': 'new_v', 'new_v_s5_log_step': 'new_v', 'new_v_s5_b_re': 'new_v', 'new_v_s5_b_im': 'new_v', 'new_v_s5_c_re': 'new_v', 'new_v_s5_c_im': 'new_v', 'new_v_s5_d': 'new_v', 'new_v_s5_glu_w': 'new_v', 'new_v_s5_glu_b': 'new_v', 'new_v_m2_conv_w': 'new_v', 'new_v_m2_conv_b': 'new_v', 'new_v_m2_a_log': 'new_v', 'new_v_m2_dt_bias': 'new_v', 'new_v_m2_d': 'new_v', 'new_v_m2_norm_w': 'new_v', 'new_v_proj_a': 'new_v', 'new_v_proj_b': 'new_v', 'new_v_proj_c': 'new_v', 'new_v_w_out': 'new_v', 'new_v_final_norm_w': 'new_v'}


def _forward(args):
    return _fwd_reference(*[args[k] for k in FWD_PARAMS])


def _output_shape():
    def fwd():
        inp = _fwd_setup_inputs(0)
        return _fwd_reference(*[inp[k] for k in FWD_PARAMS])
    out = _jax.eval_shape(fwd)
    return out.shape, out.dtype

N_MICROBATCH = 1
ADAM_LR = 0.001
ADAM_B1 = 0.9
ADAM_B2 = 0.999
ADAM_EPS = 1e-08
ADAM_WD = 0.01
ADAM_STEP = 10
PER_EXAMPLE_BATCH_AXIS = {'x': 0, 'loss_target': 0}
SHARED_INPUTS = []
_WEIGHT_DTYPES = {'norm_w': _jnp.float32, 'w_in': _jnp.float32, 'gdn_conv_w': _jnp.float32, 'gdn_a_log': _jnp.float32, 'gdn_dt_bias': _jnp.float32, 'gdn_norm_w': _jnp.float32, 's5_lam_re': _jnp.float32, 's5_lam_im': _jnp.float32, 's5_log_step': _jnp.float32, 's5_b_re': _jnp.float32, 's5_b_im': _jnp.float32, 's5_c_re': _jnp.float32, 's5_c_im': _jnp.float32, 's5_d': _jnp.float32, 's5_glu_w': _jnp.float32, 's5_glu_b': _jnp.float32, 'm2_conv_w': _jnp.float32, 'm2_conv_b': _jnp.float32, 'm2_a_log': _jnp.float32, 'm2_dt_bias': _jnp.float32, 'm2_d': _jnp.float32, 'm2_norm_w': _jnp.float32, 'proj_a': _jnp.float32, 'proj_b': _jnp.float32, 'proj_c': _jnp.float32, 'w_out': _jnp.float32, 'final_norm_w': _jnp.float32}
MOMENT_SCALE = {'norm_w': 1.198459e-01, 'w_in': 4.376090e-02, 'gdn_conv_w': 4.034291e-02, 'gdn_a_log': 2.315541e-01, 'gdn_dt_bias': 2.217546e-01, 'gdn_norm_w': 1.524228e-01, 's5_lam_re': 1.061241e-02, 's5_lam_im': 1.145389e-02, 's5_log_step': 1.019408e+01, 's5_b_re': 4.746373e-03, 's5_b_im': 4.651475e-03, 's5_c_re': 2.378830e-03, 's5_c_im': 2.325983e-03, 's5_d': 2.394249e-02, 's5_glu_w': 8.104472e-03, 's5_glu_b': 1.069112e-02, 'm2_conv_w': 6.713986e-02, 'm2_conv_b': 9.329387e-02, 'm2_a_log': 2.951688e-01, 'm2_dt_bias': 2.857763e-01, 'm2_d': 4.889157e-01, 'm2_norm_w': 8.822635e-02, 'proj_a': 3.648291e-02, 'proj_b': 1.532127e-02, 'proj_c': 6.245925e-02, 'w_out': 7.400873e-02, 'final_norm_w': 3.197310e+01}


def _to_microbatches(a, axis):
    t = _jnp.moveaxis(a, axis, 0)
    t = t.reshape((N_MICROBATCH, t.shape[0] // N_MICROBATCH) + t.shape[1:])
    return _jnp.moveaxis(t, 1, axis + 1)


def setup_inputs(seed: int = 0) -> dict:
    inp = _fwd_setup_inputs(seed)
    key = _jax.random.fold_in(_jax.random.key(seed), 7919)
    shape, _ = _output_shape()
    out = dict(inp)
    out["loss_target"] = _jax.random.normal(_jax.random.fold_in(key, 0), shape, _jnp.float32)
    for i, name in enumerate(TWIN_WEIGHTS):
        w = inp[name].astype(_jnp.float32)
        if MOMENT_SCALE is None:
            s = _jnp.sqrt(_jnp.mean(_jnp.square(w)) + 1e-30)
        else:
            s = MOMENT_SCALE[name]
        km, kv = _jax.random.split(_jax.random.fold_in(key, i + 1))
        out[name] = w
        out["m_" + name] = s * _jax.random.normal(km, w.shape, _jnp.float32)
        out["v_" + name] = (s * s) * _jax.random.uniform(kv, w.shape, _jnp.float32, 0.5, 1.5)
    if N_MICROBATCH > 1:
        for name, axis in PER_EXAMPLE_BATCH_AXIS.items():
            out[name] = _to_microbatches(out[name], axis)
    return {'x': out['x'], 'norm_w': out['norm_w'], 'w_in': out['w_in'], 'gdn_conv_w': out['gdn_conv_w'], 'gdn_a_log': out['gdn_a_log'], 'gdn_dt_bias': out['gdn_dt_bias'], 'gdn_norm_w': out['gdn_norm_w'], 's5_lam_re': out['s5_lam_re'], 's5_lam_im': out['s5_lam_im'], 's5_log_step': out['s5_log_step'], 's5_b_re': out['s5_b_re'], 's5_b_im': out['s5_b_im'], 's5_c_re': out['s5_c_re'], 's5_c_im': out['s5_c_im'], 's5_d': out['s5_d'], 's5_glu_w': out['s5_glu_w'], 's5_glu_b': out['s5_glu_b'], 'm2_conv_w': out['m2_conv_w'], 'm2_conv_b': out['m2_conv_b'], 'm2_a_log': out['m2_a_log'], 'm2_dt_bias': out['m2_dt_bias'], 'm2_d': out['m2_d'], 'm2_norm_w': out['m2_norm_w'], 'proj_a': out['proj_a'], 'proj_b': out['proj_b'], 'proj_c': out['proj_c'], 'w_out': out['w_out'], 'final_norm_w': out['final_norm_w'], 'loss_target': out['loss_target'], 'm_norm_w': out['m_norm_w'], 'm_w_in': out['m_w_in'], 'm_gdn_conv_w': out['m_gdn_conv_w'], 'm_gdn_a_log': out['m_gdn_a_log'], 'm_gdn_dt_bias': out['m_gdn_dt_bias'], 'm_gdn_norm_w': out['m_gdn_norm_w'], 'm_s5_lam_re': out['m_s5_lam_re'], 'm_s5_lam_im': out['m_s5_lam_im'], 'm_s5_log_step': out['m_s5_log_step'], 'm_s5_b_re': out['m_s5_b_re'], 'm_s5_b_im': out['m_s5_b_im'], 'm_s5_c_re': out['m_s5_c_re'], 'm_s5_c_im': out['m_s5_c_im'], 'm_s5_d': out['m_s5_d'], 'm_s5_glu_w': out['m_s5_glu_w'], 'm_s5_glu_b': out['m_s5_glu_b'], 'm_m2_conv_w': out['m_m2_conv_w'], 'm_m2_conv_b': out['m_m2_conv_b'], 'm_m2_a_log': out['m_m2_a_log'], 'm_m2_dt_bias': out['m_m2_dt_bias'], 'm_m2_d': out['m_m2_d'], 'm_m2_norm_w': out['m_m2_norm_w'], 'm_proj_a': out['m_proj_a'], 'm_proj_b': out['m_proj_b'], 'm_proj_c': out['m_proj_c'], 'm_w_out': out['m_w_out'], 'm_final_norm_w': out['m_final_norm_w'], 'v_norm_w': out['v_norm_w'], 'v_w_in': out['v_w_in'], 'v_gdn_conv_w': out['v_gdn_conv_w'], 'v_gdn_a_log': out['v_gdn_a_log'], 'v_gdn_dt_bias': out['v_gdn_dt_bias'], 'v_gdn_norm_w': out['v_gdn_norm_w'], 'v_s5_lam_re': out['v_s5_lam_re'], 'v_s5_lam_im': out['v_s5_lam_im'], 'v_s5_log_step': out['v_s5_log_step'], 'v_s5_b_re': out['v_s5_b_re'], 'v_s5_b_im': out['v_s5_b_im'], 'v_s5_c_re': out['v_s5_c_re'], 'v_s5_c_im': out['v_s5_c_im'], 'v_s5_d': out['v_s5_d'], 'v_s5_glu_w': out['v_s5_glu_w'], 'v_s5_glu_b': out['v_s5_glu_b'], 'v_m2_conv_w': out['v_m2_conv_w'], 'v_m2_conv_b': out['v_m2_conv_b'], 'v_m2_a_log': out['v_m2_a_log'], 'v_m2_dt_bias': out['v_m2_dt_bias'], 'v_m2_d': out['v_m2_d'], 'v_m2_norm_w': out['v_m2_norm_w'], 'v_proj_a': out['v_proj_a'], 'v_proj_b': out['v_proj_b'], 'v_proj_c': out['v_proj_c'], 'v_w_out': out['v_w_out'], 'v_final_norm_w': out['v_final_norm_w']}


def _loss(weights, diff, rest, loss_target):
    with _jax.named_scope("forward"):
        args = {**rest, TWIN_DIFF_INPUT: diff, **{k: w.astype(_WEIGHT_DTYPES[k]) for k, w in weights.items()}}
        y = _forward(args)
    with _jax.named_scope("loss_head"):
        err = _jnp.square(y.astype(_jnp.float32) - loss_target)
        return 0.5 * _jnp.sum(_jnp.mean(err, axis=-1)) if err.ndim else 0.5 * err


def _adamw(w, g, m, v):
    m = ADAM_B1 * m + (1.0 - ADAM_B1) * g
    v = ADAM_B2 * v + (1.0 - ADAM_B2) * _jnp.square(g)
    m_hat = m / (1.0 - ADAM_B1 ** ADAM_STEP)
    v_hat = v / (1.0 - ADAM_B2 ** ADAM_STEP)
    delta = -ADAM_LR * (m_hat / (_jnp.sqrt(v_hat) + ADAM_EPS) + ADAM_WD * w)
    return delta, m, v


def reference(x, norm_w, w_in, gdn_conv_w, gdn_a_log, gdn_dt_bias, gdn_norm_w, s5_lam_re, s5_lam_im, s5_log_step, s5_b_re, s5_b_im, s5_c_re, s5_c_im, s5_d, s5_glu_w, s5_glu_b, m2_conv_w, m2_conv_b, m2_a_log, m2_dt_bias, m2_d, m2_norm_w, proj_a, proj_b, proj_c, w_out, final_norm_w, loss_target, m_norm_w, m_w_in, m_gdn_conv_w, m_gdn_a_log, m_gdn_dt_bias, m_gdn_norm_w, m_s5_lam_re, m_s5_lam_im, m_s5_log_step, m_s5_b_re, m_s5_b_im, m_s5_c_re, m_s5_c_im, m_s5_d, m_s5_glu_w, m_s5_glu_b, m_m2_conv_w, m_m2_conv_b, m_m2_a_log, m_m2_dt_bias, m_m2_d, m_m2_norm_w, m_proj_a, m_proj_b, m_proj_c, m_w_out, m_final_norm_w, v_norm_w, v_w_in, v_gdn_conv_w, v_gdn_a_log, v_gdn_dt_bias, v_gdn_norm_w, v_s5_lam_re, v_s5_lam_im, v_s5_log_step, v_s5_b_re, v_s5_b_im, v_s5_c_re, v_s5_c_im, v_s5_d, v_s5_glu_w, v_s5_glu_b, v_m2_conv_w, v_m2_conv_b, v_m2_a_log, v_m2_dt_bias, v_m2_d, v_m2_norm_w, v_proj_a, v_proj_b, v_proj_c, v_w_out, v_final_norm_w):
    given = dict(x=x, norm_w=norm_w, w_in=w_in, gdn_conv_w=gdn_conv_w, gdn_a_log=gdn_a_log, gdn_dt_bias=gdn_dt_bias, gdn_norm_w=gdn_norm_w, s5_lam_re=s5_lam_re, s5_lam_im=s5_lam_im, s5_log_step=s5_log_step, s5_b_re=s5_b_re, s5_b_im=s5_b_im, s5_c_re=s5_c_re, s5_c_im=s5_c_im, s5_d=s5_d, s5_glu_w=s5_glu_w, s5_glu_b=s5_glu_b, m2_conv_w=m2_conv_w, m2_conv_b=m2_conv_b, m2_a_log=m2_a_log, m2_dt_bias=m2_dt_bias, m2_d=m2_d, m2_norm_w=m2_norm_w, proj_a=proj_a, proj_b=proj_b, proj_c=proj_c, w_out=w_out, final_norm_w=final_norm_w, loss_target=loss_target, m_norm_w=m_norm_w, m_w_in=m_w_in, m_gdn_conv_w=m_gdn_conv_w, m_gdn_a_log=m_gdn_a_log, m_gdn_dt_bias=m_gdn_dt_bias, m_gdn_norm_w=m_gdn_norm_w, m_s5_lam_re=m_s5_lam_re, m_s5_lam_im=m_s5_lam_im, m_s5_log_step=m_s5_log_step, m_s5_b_re=m_s5_b_re, m_s5_b_im=m_s5_b_im, m_s5_c_re=m_s5_c_re, m_s5_c_im=m_s5_c_im, m_s5_d=m_s5_d, m_s5_glu_w=m_s5_glu_w, m_s5_glu_b=m_s5_glu_b, m_m2_conv_w=m_m2_conv_w, m_m2_conv_b=m_m2_conv_b, m_m2_a_log=m_m2_a_log, m_m2_dt_bias=m_m2_dt_bias, m_m2_d=m_m2_d, m_m2_norm_w=m_m2_norm_w, m_proj_a=m_proj_a, m_proj_b=m_proj_b, m_proj_c=m_proj_c, m_w_out=m_w_out, m_final_norm_w=m_final_norm_w, v_norm_w=v_norm_w, v_w_in=v_w_in, v_gdn_conv_w=v_gdn_conv_w, v_gdn_a_log=v_gdn_a_log, v_gdn_dt_bias=v_gdn_dt_bias, v_gdn_norm_w=v_gdn_norm_w, v_s5_lam_re=v_s5_lam_re, v_s5_lam_im=v_s5_lam_im, v_s5_log_step=v_s5_log_step, v_s5_b_re=v_s5_b_re, v_s5_b_im=v_s5_b_im, v_s5_c_re=v_s5_c_re, v_s5_c_im=v_s5_c_im, v_s5_d=v_s5_d, v_s5_glu_w=v_s5_glu_w, v_s5_glu_b=v_s5_glu_b, v_m2_conv_w=v_m2_conv_w, v_m2_conv_b=v_m2_conv_b, v_m2_a_log=v_m2_a_log, v_m2_dt_bias=v_m2_dt_bias, v_m2_d=v_m2_d, v_m2_norm_w=v_m2_norm_w, v_proj_a=v_proj_a, v_proj_b=v_proj_b, v_proj_c=v_proj_c, v_w_out=v_w_out, v_final_norm_w=v_final_norm_w)
    weights = {n: given[n] for n in TWIN_WEIGHTS}
    shared = {n: given[n] for n in SHARED_INPUTS}
    per_example = {n: given[n] for n in ['x']}
    grad_fn = _jax.value_and_grad(_loss, argnums=(0, 1))

    def one_microbatch(ex, loss_target):
        ex = dict(ex)
        diff = ex.pop(TWIN_DIFF_INPUT)
        return grad_fn(weights, diff, {**shared, **ex}, loss_target)

    if N_MICROBATCH == 1:
        loss, (grad_w, grad_x) = one_microbatch(per_example, given["loss_target"])
    else:
        def body(carry, xs):
            loss_sum, grad_sum = carry
            l_k, (gw_k, gx_k) = one_microbatch(xs[0], xs[1])
            with _jax.named_scope("update"):
                return (loss_sum + l_k, _jax.tree.map(_jnp.add, grad_sum, gw_k)), gx_k

        init = (_jnp.zeros((), _jnp.float32), _jax.tree.map(_jnp.zeros_like, weights))
        (loss, grad_w), grad_x = _jax.lax.scan(body, init, (per_example, given["loss_target"]))
    with _jax.named_scope("update"):
        delta_w, new_m, new_v = {}, {}, {}
        for n in TWIN_WEIGHTS:
            delta_w[n], new_m[n], new_v[n] = _adamw(weights[n], grad_w[n], given["m_" + n], given["v_" + n])
    return (loss, grad_x, *[grad_w[n] for n in TWIN_WEIGHTS], *[delta_w[n] for n in TWIN_WEIGHTS],
            *[new_m[n] for n in TWIN_WEIGHTS], *[new_v[n] for n in TWIN_WEIGHTS])
```

```python
import functools

import jax
import jax.numpy as jnp
from jax import lax
from jax.experimental import pallas as pl
from jax.experimental.pallas import tpu as pltpu

f32 = jnp.float32
bf16 = jnp.bfloat16

N_DEV = 8
DEPTH = 4
D_MODEL = 2048
GDN_HEADS = 8
HEAD_DIM = 128
GDN_WIDTH = 1024
S5_GROUPS = 48
S5_GROUP_SIZE = 16
S5_STATE = 64
S5_WIDTH = 768
S5_LANES = S5_GROUPS * S5_STATE
S5_BLK = 6
M2_HEADS = 16
M2_HEAD_DIM = 64
M2_WIDTH = 1024
M2_GROUPS = 4
M2_STATE = 128
CONV_K = 4
CHUNK = 64
HALO = 8
NORM_EPS = 1e-6
IN_DIM = 14880
LANE = 128
VMEM_LIMIT = 48 * 1024 * 1024

ADAM_LR = 0.001
ADAM_B1 = 0.9
ADAM_B2 = 0.999
ADAM_EPS = 1e-08
ADAM_WD = 0.01
ADAM_STEP = 10

U_MERGE, U_SU, U_SG, U_Q, U_K, U_V, U_AZ, U_CZ, U_CX, U_CB, U_CC, U_SMA, U_SMC = (
    0, 48, 54, 60, 68, 76, 84, 92, 100, 108, 112, 116, 117)
NP_UNITS = 120
NP_COLS = NP_UNITS * LANE


def _cparams(sem=None):
    return pltpu.CompilerParams(dimension_semantics=sem, vmem_limit_bytes=VMEM_LIMIT)


def _pick(dim, target):
    if dim <= target:
        return dim
    for t in range(target - target % LANE, 0, -LANE):
        if dim % t == 0:
            return t
    raise ValueError(f"no tile for {dim}")


def _bd(a, b, dims):
    return lax.dot_general(a.astype(bf16), b.astype(bf16), (dims, ((), ())), preferred_element_type=f32)


@jax.custom_vjp
def dot_nn(a, b):
    return _bd(a, b, ((1,), (0,)))


@jax.custom_vjp
def dot_nt(a, b):
    return _bd(a, b, ((1,), (1,)))


@jax.custom_vjp
def dot_tn(a, b):
    return _bd(a, b, ((0,), (0,)))


dot_nn.defvjp(lambda a, b: (dot_nn(a, b), (a, b)), lambda r, ct: (dot_nt(ct, r[1]), dot_tn(r[0], ct)))
dot_nt.defvjp(lambda a, b: (dot_nt(a, b), (a, b)), lambda r, ct: (dot_nn(ct, r[1]), dot_tn(ct, r[0])))
dot_tn.defvjp(lambda a, b: (dot_tn(a, b), (a, b)), lambda r, ct: (dot_nt(r[1], ct), dot_nn(r[0], ct)))

_HI = lax.Precision.HIGHEST


def _silu(x):
    return x * jax.nn.sigmoid(x)


def _softplus(x):
    return jnp.maximum(x, 0.0) + jnp.log(1.0 + jnp.exp(-jnp.abs(x)))


def _tri_masks(c):
    row = lax.broadcasted_iota(jnp.int32, (c, c), 0)
    col = lax.broadcasted_iota(jnp.int32, (c, c), 1)
    return row >= col, row > col, (row >= col).astype(f32), (row == col).astype(f32)


def _cumsum_cols(g, ltri, c):
    gb = g * jnp.ones((1, LANE), f32)
    cum = jnp.dot(ltri, gb, precision=_HI, preferred_element_type=f32)
    cum_t = lax.dot_general(gb, ltri, (((0,), (1,)), ((), ())), precision=_HI, preferred_element_type=f32)[:c]
    return cum, cum_t


def _mm(name, a, b, mode, residual=None, out_dtype=f32, tm=1024, tn=1024, tk=512):
    if mode == "nn":
        (m, k), (_, n) = a.shape, b.shape
    elif mode == "nt":
        (m, k), (n, _) = a.shape, b.shape
    else:
        (k, m), (_, n) = a.shape, b.shape
    tm, tn, tk = _pick(m, tm), _pick(n, tn), _pick(k, tk)
    nk = k // tk
    dims = {"nn": ((1,), (0,)), "nt": ((1,), (1,)), "tn": ((0,), (0,))}[mode]
    has_res = residual is not None

    def body(*refs):
        if has_res:
            a_ref, b_ref, r_ref, o_ref, acc_ref = refs
        else:
            a_ref, b_ref, o_ref, acc_ref = refs
        kk = pl.program_id(2)

        @pl.when(kk == 0)
        def _():
            acc_ref[...] = jnp.zeros_like(acc_ref)

        acc_ref[...] += _bd(a_ref[...], b_ref[...], dims)

        @pl.when(kk == nk - 1)
        def _():
            out = acc_ref[...]
            if has_res:
                out = out + r_ref[...].astype(f32)
            o_ref[...] = out.astype(o_ref.dtype)

    if mode == "tn":
        a_spec = pl.BlockSpec((tk, tm), lambda i, j, kk: (kk, i))
    else:
        a_spec = pl.BlockSpec((tm, tk), lambda i, j, kk: (i, kk))
    if mode == "nt":
        b_spec = pl.BlockSpec((tn, tk), lambda i, j, kk: (j, kk))
    else:
        b_spec = pl.BlockSpec((tk, tn), lambda i, j, kk: (kk, j))
    o_spec = pl.BlockSpec((tm, tn), lambda i, j, kk: (i, j))
    in_specs = [a_spec, b_spec] + ([o_spec] if has_res else [])
    args = (a, b) + ((residual,) if has_res else ())
    return pl.pallas_call(
        body, name=name, grid=(m // tm, n // tn, nk), in_specs=in_specs, out_specs=o_spec,
        out_shape=jax.ShapeDtypeStruct((m, n), out_dtype), scratch_shapes=[pltpu.VMEM((tm, tn), f32)],
        compiler_params=_cparams(("parallel", "parallel", "arbitrary")))(*args)


def _act_spec(t, width, colblk):
    return pl.BlockSpec((t, width), lambda i: (i, colblk))


def _tok_fwd(name, fn, acts, params, outs, t):
    rows = acts[0][0].shape[0]
    t = min(t, rows)
    na, npar = len(acts), len(params)

    def body(*refs):
        a = [r[...].astype(f32) for r in refs[:na]]
        p = [r[...].astype(f32) for r in refs[na:na + npar]]
        res = fn(*a, *p)
        for o_ref, o in zip(refs[na + npar:], res):
            o_ref[...] = o.astype(o_ref.dtype)

    in_specs = [_act_spec(t, w, cb) for (_, w, cb) in acts]
    in_specs += [pl.BlockSpec(p.shape, lambda i: (0, 0)) for p in params]
    res = pl.pallas_call(
        body, name=name, grid=(rows // t,), in_specs=in_specs,
        out_specs=[_act_spec(t, w, 0) for (w, _) in outs],
        out_shape=[jax.ShapeDtypeStruct((rows, w), dt) for (w, dt) in outs],
        compiler_params=_cparams(("arbitrary",)))(*[a for (a, _, _) in acts], *params)
    return res


def _tok_bwd(name, fn, acts, params, cts, dact, t, residuals=None):
    rows = acts[0][0].shape[0]
    t = min(t, rows)
    residuals = residuals or {}
    res_ids = sorted(residuals)
    na, npar, nc, nr, nd = len(acts), len(params), len(cts), len(res_ids), len(dact)

    def body(*refs):
        a = [r[...].astype(f32) for r in refs[:na]]
        p = [r[...].astype(f32) for r in refs[na:na + npar]]
        ct = tuple(r[...].astype(f32) for r in refs[na + npar:na + npar + nc])
        rs = {idx: r[...].astype(f32) for idx, r in zip(res_ids, refs[na + npar + nc:na + npar + nc + nr])}
        orefs = refs[na + npar + nc + nr:]
        _, vjp = jax.vjp(fn, *a, *p)
        grads = vjp(ct)
        for o_ref, (idx, _) in zip(orefs[:nd], dact):
            g = grads[idx]
            if idx in rs:
                g = g + rs[idx]
            o_ref[...] = g.astype(o_ref.dtype)

        if npar:
            @pl.when(pl.program_id(0) == 0)
            def _():
                for o_ref in orefs[nd:]:
                    o_ref[...] = jnp.zeros_like(o_ref)

            for o_ref, g in zip(orefs[nd:], grads[na:]):
                o_ref[...] += g

    in_specs = [_act_spec(t, w, cb) for (_, w, cb) in acts]
    in_specs += [pl.BlockSpec(p.shape, lambda i: (0, 0)) for p in params]
    in_specs += [_act_spec(t, w, cb) for (_, w, cb) in cts]
    in_specs += [_act_spec(t, acts[idx][1], 0) for idx in res_ids]
    out_specs = [_act_spec(t, acts[idx][1], 0) for (idx, _) in dact]
    out_specs += [pl.BlockSpec(p.shape, lambda i: (0, 0)) for p in params]
    out_shape = [jax.ShapeDtypeStruct((rows, acts[idx][1]), dt) for (idx, dt) in dact]
    out_shape += [jax.ShapeDtypeStruct(p.shape, f32) for p in params]
    return pl.pallas_call(
        body, name=name, grid=(rows // t,), in_specs=in_specs, out_specs=out_specs, out_shape=out_shape,
        compiler_params=_cparams(("arbitrary",)))(
            *[a for (a, _, _) in acts], *params, *[c for (c, _, _) in cts], *[residuals[i] for i in res_ids])


def _rms_fn(x, w):
    return (x * lax.rsqrt(jnp.mean(x * x, axis=-1, keepdims=True) + NORM_EPS) * w,)


def _gate_fn(la, lb, lc, pa, pb, pc):
    return (jax.nn.sigmoid(la) * pa + jax.nn.sigmoid(lb) * pb + jax.nn.sigmoid(lc) * pc,)


def _s5_tail_fn(ypre, gate, glu_w, glu_b):
    y = jax.nn.gelu(ypre)
    y = y * jax.nn.sigmoid(dot_nn(y, glu_w) + glu_b)
    return (y * _silu(gate),)


def _loss_grad(x, w, target, t=256):
    rows, d = x.shape
    t = min(t, rows)

    def fn(xt, wt, tt):
        y = _rms_fn(xt, wt)[0]
        err = y - tt
        return 0.5 * jnp.sum(jnp.sum(err * err, axis=-1, keepdims=True), axis=0, keepdims=True) / d

    def body(x_ref, w_ref, t_ref, loss_ref, dx_ref, dw_ref):
        tt = t_ref[...]
        val, vjp = jax.vjp(lambda a, b: fn(a, b, tt), x_ref[...], w_ref[...])
        dx, dw = vjp(jnp.ones((1, 1), f32))
        dx_ref[...] = dx

        @pl.when(pl.program_id(0) == 0)
        def _():
            loss_ref[...] = jnp.zeros_like(loss_ref)
            dw_ref[...] = jnp.zeros_like(dw_ref)

        loss_ref[...] += val * jnp.ones((1, LANE), f32)
        dw_ref[...] += dw

    return pl.pallas_call(
        body, name="loss_grad", grid=(rows // t,),
        in_specs=[_act_spec(t, d, 0), pl.BlockSpec((1, d), lambda i: (0, 0)), _act_spec(t, d, 0)],
        out_specs=[pl.BlockSpec((1, LANE), lambda i: (0, 0)), _act_spec(t, d, 0), pl.BlockSpec((1, d), lambda i: (0, 0))],
        out_shape=[jax.ShapeDtypeStruct((1, LANE), f32), jax.ShapeDtypeStruct((rows, d), f32),
                   jax.ShapeDtypeStruct((1, d), f32)],
        compiler_params=_cparams(("arbitrary",)))(x, w, target)


def _tri_inv(a, eye, c):
    n = -a
    p = eye + n
    npow = n
    steps = c.bit_length() - 2
    for _ in range(steps):
        npow = dot_nn(npow, npow)
        p = p + dot_nn(p, npow)
    return p


def _gdn_chunk(qc, kc, vc, z, small, alog_row, dtb_row, normw, state, oh_b, oh_a):
    c = qc.shape[0]
    causal, strict, ltri, eye = _tri_masks(c)
    q = _silu(qc)
    k = _silu(kc)
    v = _silu(vc)
    q = q * lax.rsqrt(jnp.sum(q * q, axis=-1, keepdims=True) + NORM_EPS) * (HEAD_DIM ** -0.5)
    k = k * lax.rsqrt(jnp.sum(k * k, axis=-1, keepdims=True) + NORM_EPS)
    beta = jnp.sum(jax.nn.sigmoid(small) * oh_b, axis=-1, keepdims=True)
    g_all = -jnp.exp(alog_row) * _softplus(small + dtb_row)
    g = jnp.sum(g_all * oh_a, axis=-1, keepdims=True)
    gc, gc_t = _cumsum_cols(g, ltri, c)
    decay = jnp.exp(jnp.where(causal, gc[:, :c] - gc_t, -1e30))
    egc = jnp.exp(gc)
    kb = k * beta
    a_mat = jnp.where(strict, dot_nt(kb, k) * decay, 0.0)
    t_inv = _tri_inv(a_mat, eye, c)
    u = dot_nn(t_inv, v * beta)
    w = dot_nn(t_inv, kb * egc)
    qk = dot_nt(q, k) * decay
    v_new = u - dot_nn(w, state)
    out = dot_nn(q * egc, state) + dot_nn(qk, v_new)
    g_last = gc[c - 1:c, :]
    k_tail = k * jnp.exp(g_last - gc)
    new_state = state * jnp.exp(g_last) + dot_tn(k_tail, v_new)
    o = out * lax.rsqrt(jnp.mean(out * out, axis=-1, keepdims=True) + NORM_EPS) * normw * _silu(z)
    return o, new_state


def _onehot_lane(idx):
    return (lax.broadcasted_iota(jnp.int32, (1, LANE), 1) == idx).astype(f32)


def _conv_windows(xin_ref, p, cw_ref, c):
    acc = None
    for k in range(CONV_K):
        term = cw_ref[pl.ds(k, 1), :] * xin_ref[p, pl.ds(HALO - CONV_K + 1 + k, c), :]
        acc = term if acc is None else acc + term
    return acc


def _gdn_fwd(proj, conv_w, alog_row, dtb_row, normw):
    rows = proj.shape[0]
    c = min(CHUNK, rows)
    n = rows // c

    def body(q_ref, k_ref, v_ref, z_ref, sm_ref, cwq, cwk, cwv, al_ref, dt_ref, nw_ref, y_ref, ck_ref, s_ref, xin_ref):
        h = pl.program_id(0)
        i = pl.program_id(1)

        @pl.when(i == 0)
        def _():
            s_ref[...] = jnp.zeros_like(s_ref)
            xin_ref[:, 0:HALO, :] = jnp.zeros((3, HALO, LANE), f32)

        @pl.when(i > 0)
        def _():
            xin_ref[:, 0:HALO, :] = xin_ref[:, c:c + HALO, :]

        xin_ref[0, HALO:, :] = q_ref[...]
        xin_ref[1, HALO:, :] = k_ref[...]
        xin_ref[2, HALO:, :] = v_ref[...]
        qc = _conv_windows(xin_ref, 0, cwq, c)
        kc = _conv_windows(xin_ref, 1, cwk, c)
        vc = _conv_windows(xin_ref, 2, cwv, c)
        state = s_ref[...]
        ck_ref[...] = state
        o, new_state = _gdn_chunk(qc, kc, vc, z_ref[...], sm_ref[...], al_ref[...], dt_ref[...], nw_ref[...], state,
                                  _onehot_lane(h), _onehot_lane(h + GDN_HEADS))
        y_ref[...] = o.astype(y_ref.dtype)
        s_ref[...] = new_state

    def blk(unit):
        return pl.BlockSpec((c, LANE), lambda h, i: (i, unit + h))

    def cw(unit):
        return pl.BlockSpec((CONV_K, LANE), lambda h, i: (0, unit + h))

    row = pl.BlockSpec((1, LANE), lambda h, i: (0, 0))
    return pl.pallas_call(
        body, name="gdn_fwd", grid=(GDN_HEADS, n),
        in_specs=[blk(U_Q), blk(U_K), blk(U_V), blk(U_AZ), pl.BlockSpec((c, LANE), lambda h, i: (i, U_SMA)),
                  cw(0), cw(8), cw(16), row, row, row],
        out_specs=[pl.BlockSpec((c, LANE), lambda h, i: (i, h)),
                   pl.BlockSpec((None, None, HEAD_DIM, HEAD_DIM), lambda h, i: (h, i, 0, 0))],
        out_shape=[jax.ShapeDtypeStruct((rows, GDN_WIDTH), bf16),
                   jax.ShapeDtypeStruct((GDN_HEADS, n, HEAD_DIM, HEAD_DIM), f32)],
        scratch_shapes=[pltpu.VMEM((HEAD_DIM, HEAD_DIM), f32), pltpu.VMEM((3, c + HALO, LANE), f32)],
        compiler_params=_cparams(("arbitrary", "arbitrary")))(
            proj, proj, proj, proj, proj, conv_w, conv_w, conv_w, alog_row, dtb_row, normw)


def _conv_bwd(xin_ref, dyext_ref, p, cw_ref, dxc, dx_ref, dcw_ref, c):
    dyext_ref[p, 0:c, :] = dxc
    acc = None
    for k in range(CONV_K):
        term = cw_ref[pl.ds(k, 1), :] * dyext_ref[p, pl.ds(CONV_K - 1 - k, c), :]
        acc = term if acc is None else acc + term
        dcw_ref[pl.ds(k, 1), :] += jnp.sum(xin_ref[p, pl.ds(HALO - CONV_K + 1 + k, c), :] * dxc, axis=0, keepdims=True)
    dx_ref[...] = acc


def _gdn_bwd(proj, dy, ck, conv_w, alog_row, dtb_row, normw):
    rows = proj.shape[0]
    c = min(CHUNK, rows)
    n = rows // c
    hb = c // HALO

    def body(q_ref, k_ref, v_ref, hq_ref, hk_ref, hv_ref, z_ref, sm_ref, cwq, cwk, cwv, al_ref, dt_ref, nw_ref,
             ck_ref, dy_ref, dq_ref, dk_ref, dv_ref, dz_ref, dsm_ref, dcwq, dcwk, dcwv, dal_ref, ddt_ref, dnw_ref,
             ds_ref, xin_ref, dyext_ref):
        h = pl.program_id(0)
        i = pl.program_id(1)
        ci = n - 1 - i

        @pl.when(i == 0)
        def _():
            ds_ref[...] = jnp.zeros_like(ds_ref)
            dyext_ref[:, c:c + HALO, :] = jnp.zeros((3, HALO, LANE), f32)
            for r in (dcwq, dcwk, dcwv, dal_ref, ddt_ref, dnw_ref):
                r[...] = jnp.zeros_like(r)

        @pl.when(i > 0)
        def _():
            dyext_ref[:, c:c + HALO, :] = dyext_ref[:, 0:HALO, :]

        first = (ci > 0).astype(f32)
        for p, (x_ref, halo_ref) in enumerate(((q_ref, hq_ref), (k_ref, hk_ref), (v_ref, hv_ref))):
            xin_ref[p, 0:HALO, :] = halo_ref[...] * first
            xin_ref[p, HALO:, :] = x_ref[...]
        qc = _conv_windows(xin_ref, 0, cwq, c)
        kc = _conv_windows(xin_ref, 1, cwk, c)
        vc = _conv_windows(xin_ref, 2, cwv, c)
        oh_b, oh_a = _onehot_lane(h), _onehot_lane(h + GDN_HEADS)
        fn = functools.partial(_gdn_chunk, oh_b=oh_b, oh_a=oh_a)
        _, vjp = jax.vjp(fn, qc, kc, vc, z_ref[...], sm_ref[...], al_ref[...], dt_ref[...], nw_ref[...], ck_ref[...])
        dqc, dkc, dvc, dz, dsm, dal, ddt, dnw, dstate = vjp((dy_ref[...].astype(f32), ds_ref[...]))
        ds_ref[...] = dstate
        dz_ref[...] = dz
        dsm_ref[...] = dsm
        dal_ref[...] += dal
        ddt_ref[...] += ddt
        dnw_ref[...] += dnw
        _conv_bwd(xin_ref, dyext_ref, 0, cwq, dqc, dq_ref, dcwq, c)
        _conv_bwd(xin_ref, dyext_ref, 1, cwk, dkc, dk_ref, dcwk, c)
        _conv_bwd(xin_ref, dyext_ref, 2, cwv, dvc, dv_ref, dcwv, c)

    def blk(unit):
        return pl.BlockSpec((c, LANE), lambda h, i: (n - 1 - i, unit + h))

    def halo(unit):
        return pl.BlockSpec((HALO, LANE), lambda h, i: (jnp.maximum((n - 1 - i) * hb - 1, 0), unit + h))

    def cw(unit):
        return pl.BlockSpec((CONV_K, LANE), lambda h, i: (0, unit + h))

    row = pl.BlockSpec((1, LANE), lambda h, i: (0, 0))
    hrow = pl.BlockSpec((None, 1, LANE), lambda h, i: (h, 0, 0))
    out_blk = pl.BlockSpec((c, LANE), lambda h, i: (n - 1 - i, h))
    dcw = pl.BlockSpec((CONV_K, LANE), lambda h, i: (0, h))
    wide = jax.ShapeDtypeStruct((rows, GDN_WIDTH), f32)
    hrow_shape = jax.ShapeDtypeStruct((GDN_HEADS, 1, LANE), f32)
    dcw_shape = jax.ShapeDtypeStruct((CONV_K, GDN_WIDTH), f32)
    return pl.pallas_call(
        body, name="gdn_bwd", grid=(GDN_HEADS, n),
        in_specs=[blk(U_Q), blk(U_K), blk(U_V), halo(U_Q), halo(U_K), halo(U_V), blk(U_AZ),
                  pl.BlockSpec((c, LANE), lambda h, i: (n - 1 - i, U_SMA)),
                  cw(0), cw(8), cw(16), row, row, row,
                  pl.BlockSpec((None, None, HEAD_DIM, HEAD_DIM), lambda h, i: (h, n - 1 - i, 0, 0)),
                  out_blk],
        out_specs=[out_blk, out_blk, out_blk, out_blk,
                   pl.BlockSpec((None, c, LANE), lambda h, i: (h, n - 1 - i, 0)),
                   dcw, dcw, dcw, hrow, hrow, hrow],
        out_shape=[wide, wide, wide, wide, jax.ShapeDtypeStruct((GDN_HEADS, rows, LANE), f32),
                   dcw_shape, dcw_shape, dcw_shape, hrow_shape, hrow_shape, hrow_shape],
        scratch_shapes=[pltpu.VMEM((HEAD_DIM, HEAD_DIM), f32), pltpu.VMEM((3, c + HALO, LANE), f32),
                        pltpu.VMEM((3, c + HALO, LANE), f32)],
        compiler_params=_cparams(("arbitrary", "arbitrary")))(
            proj, proj, proj, proj, proj, proj, proj, proj, conv_w, conv_w, conv_w, alog_row, dtb_row, normw, ck, dy)


M2_REP = M2_HEADS // M2_GROUPS
M2_GW = M2_REP * M2_HEAD_DIM


def _ssd_chunk(xc, bc, cc, z, small, bias_x, bias_b, bias_c, alog_row, dtb_row, d_row, normw, s0, s1, s2, s3, grp):
    c = xc.shape[0]
    causal, _, ltri, _ = _tri_masks(c)
    xs = _silu(xc + bias_x)
    bm = _silu(bc + bias_b)
    cm = _silu(cc + bias_c)
    dt_all = _softplus(small + dtb_row)
    a_all = -jnp.exp(alog_row) * dt_all
    scores = dot_nt(cm, bm)
    ys, new_states = [], []
    for r, state in enumerate((s0, s1, s2, s3)):
        oh = _onehot_lane(grp * M2_REP + r)
        dt = jnp.sum(dt_all * oh, axis=-1, keepdims=True)
        a = jnp.sum(a_all * oh, axis=-1, keepdims=True)
        dsk = jnp.sum(d_row * oh, axis=-1, keepdims=True)
        xh = xs[:, r * M2_HEAD_DIM:(r + 1) * M2_HEAD_DIM]
        ac, ac_t = _cumsum_cols(a, ltri, c)
        seg = jnp.exp(jnp.where(causal, ac[:, :c] - ac_t, -1e30))
        xdt = xh * dt
        y_diag = dot_nn(scores * seg, xdt)
        a_last = ac[c - 1:c, :]
        dec = jnp.exp(a_last - ac)[:, :M2_HEAD_DIM]
        states = dot_tn(xdt * dec, bm)
        y_off = dot_nt(cm, state) * jnp.exp(ac)[:, :M2_HEAD_DIM]
        new_states.append(state * jnp.exp(a_last) + states)
        ys.append(y_diag + y_off + dsk * xh)
    y = jnp.concatenate(ys, axis=-1) * _silu(z)
    y = y * lax.rsqrt(jnp.mean(y * y, axis=-1, keepdims=True) + NORM_EPS) * normw
    return (y, *new_states)


def _conv_windows2(xin_ref, cw_ref, c):
    acc = None
    for k in range(CONV_K):
        term = cw_ref[pl.ds(k, 1), :] * xin_ref[pl.ds(HALO - CONV_K + 1 + k, c), :]
        acc = term if acc is None else acc + term
    return acc


def _ssd_specs(n, c, rev):
    def ci(i):
        return (n - 1 - i) if rev else i

    def blk(width, unit):
        return pl.BlockSpec((c, width), lambda g, i: (ci(i), unit + g))

    def par(rows_, width, unit):
        return pl.BlockSpec((rows_, width), lambda g, i: (0, unit + g))

    return ci, blk, par


def _ssd_fwd(proj, conv_w, conv_b, alog_row, dtb_row, d_row, normw):
    rows = proj.shape[0]
    c = min(CHUNK, rows)
    n = rows // c
    _, blk, par = _ssd_specs(n, c, False)

    def body(x_ref, b_ref, c_ref, z_ref, sm_ref, cwx, cwb, cwc, bx, bb, bcc, al_ref, dt_ref, d_ref, nw_ref,
             y_ref, ck_ref, s_ref, xx_ref, xb_ref, xc_ref):
        g = pl.program_id(0)
        i = pl.program_id(1)

        @pl.when(i == 0)
        def _():
            s_ref[...] = jnp.zeros_like(s_ref)
            for r in (xx_ref, xb_ref, xc_ref):
                r[0:HALO, :] = jnp.zeros((HALO, r.shape[1]), f32)

        @pl.when(i > 0)
        def _():
            for r in (xx_ref, xb_ref, xc_ref):
                r[0:HALO, :] = r[c:c + HALO, :]

        xx_ref[HALO:, :] = x_ref[...]
        xb_ref[HALO:, :] = b_ref[...]
        xc_ref[HALO:, :] = c_ref[...]
        xc = _conv_windows2(xx_ref, cwx, c)
        bc = _conv_windows2(xb_ref, cwb, c)
        cc = _conv_windows2(xc_ref, cwc, c)
        st = [s_ref[r] for r in range(M2_REP)]
        ck_ref[...] = s_ref[...]
        res = _ssd_chunk(xc, bc, cc, z_ref[...], sm_ref[...], bx[...], bb[...], bcc[...], al_ref[...], dt_ref[...],
                         d_ref[...], nw_ref[...], *st, g)
        y_ref[...] = res[0].astype(y_ref.dtype)
        for r in range(M2_REP):
            s_ref[r] = res[1 + r]

    row = pl.BlockSpec((1, LANE), lambda g, i: (0, 0))
    return pl.pallas_call(
        body, name="ssd_fwd", grid=(M2_GROUPS, n),
        in_specs=[blk(M2_GW, U_CX // 2), blk(LANE, U_CB), blk(LANE, U_CC), blk(M2_GW, U_CZ // 2),
                  pl.BlockSpec((c, LANE), lambda g, i: (i, U_SMC)),
                  par(CONV_K, M2_GW, 0), par(CONV_K, LANE, 8), par(CONV_K, LANE, 12),
                  par(1, M2_GW, 0), par(1, LANE, 8), par(1, LANE, 12), row, row, row, par(1, M2_GW, 0)],
        out_specs=[pl.BlockSpec((c, M2_GW), lambda g, i: (i, g)),
                   pl.BlockSpec((None, None, M2_REP, M2_HEAD_DIM, M2_STATE), lambda g, i: (g, i, 0, 0, 0))],
        out_shape=[jax.ShapeDtypeStruct((rows, M2_WIDTH), bf16),
                   jax.ShapeDtypeStruct((M2_GROUPS, n, M2_REP, M2_HEAD_DIM, M2_STATE), f32)],
        scratch_shapes=[pltpu.VMEM((M2_REP, M2_HEAD_DIM, M2_STATE), f32), pltpu.VMEM((c + HALO, M2_GW), f32),
                        pltpu.VMEM((c + HALO, LANE), f32), pltpu.VMEM((c + HALO, LANE), f32)],
        compiler_params=_cparams(("arbitrary", "arbitrary")))(
            proj, proj, proj, proj, proj, conv_w, conv_w, conv_w, conv_b, conv_b, conv_b, alog_row, dtb_row, d_row, normw)


def _conv_bwd2(xin_ref, dyext_ref, cw_ref, dxc, dx_ref, dcw_ref, c):
    dyext_ref[0:c, :] = dxc
    acc = None
    for k in range(CONV_K):
        term = cw_ref[pl.ds(k, 1), :] * dyext_ref[pl.ds(CONV_K - 1 - k, c), :]
        acc = term if acc is None else acc + term
        dcw_ref[pl.ds(k, 1), :] += jnp.sum(xin_ref[pl.ds(HALO - CONV_K + 1 + k, c), :] * dxc, axis=0, keepdims=True)
    dx_ref[...] = acc


def _ssd_bwd(proj, dy, ck, conv_w, conv_b, alog_row, dtb_row, d_row, normw):
    rows = proj.shape[0]
    c = min(CHUNK, rows)
    n = rows // c
    hb = c // HALO
    _, blk, par = _ssd_specs(n, c, True)

    def body(x_ref, b_ref, c_ref, hx_ref, hb_ref, hc_ref, z_ref, sm_ref, cwx, cwb, cwc, bx, bb, bcc,
             al_ref, dt_ref, d_ref, nw_ref, ck_ref, dy_ref,
             dx_ref, db_ref, dc_ref, dz_ref, dsm_ref, dcwx, dcwb, dcwc, dbx, dbb, dbc, dal_ref, ddt_ref, dd_ref, dnw_ref,
             ds_ref, xx_ref, xb_ref, xc_ref, ex_ref, eb_ref, ec_ref):
        g = pl.program_id(0)
        i = pl.program_id(1)
        ci = n - 1 - i

        @pl.when(i == 0)
        def _():
            ds_ref[...] = jnp.zeros_like(ds_ref)
            for r in (ex_ref, eb_ref, ec_ref):
                r[c:c + HALO, :] = jnp.zeros((HALO, r.shape[1]), f32)
            for r in (dcwx, dcwb, dcwc, dbx, dbb, dbc, dal_ref, ddt_ref, dd_ref, dnw_ref):
                r[...] = jnp.zeros_like(r)

        @pl.when(i > 0)
        def _():
            for r in (ex_ref, eb_ref, ec_ref):
                r[c:c + HALO, :] = r[0:HALO, :]

        first = (ci > 0).astype(f32)
        for xin, x_in, halo_in in ((xx_ref, x_ref, hx_ref), (xb_ref, b_ref, hb_ref), (xc_ref, c_ref, hc_ref)):
            xin[0:HALO, :] = halo_in[...] * first
            xin[HALO:, :] = x_in[...]
        xc = _conv_windows2(xx_ref, cwx, c)
        bc = _conv_windows2(xb_ref, cwb, c)
        cc = _conv_windows2(xc_ref, cwc, c)
        st = [ck_ref[r] for r in range(M2_REP)]
        fn = functools.partial(_ssd_chunk, grp=g)
        _, vjp = jax.vjp(fn, xc, bc, cc, z_ref[...], sm_ref[...], bx[...], bb[...], bcc[...], al_ref[...], dt_ref[...],
                         d_ref[...], nw_ref[...], *st)
        cts = (dy_ref[...].astype(f32), *[ds_ref[r] for r in range(M2_REP)])
        (dxc, dbc_, dcc, dz, dsm, gbx, gbb, gbc, dal, ddt, dd, dnw, *dst) = vjp(cts)
        for r in range(M2_REP):
            ds_ref[r] = dst[r]
        dz_ref[...] = dz
        dsm_ref[...] = dsm
        dbx[...] += gbx
        dbb[...] += gbb
        dbc[...] += gbc
        dal_ref[...] += dal
        ddt_ref[...] += ddt
        dd_ref[...] += dd
        dnw_ref[...] += dnw
        _conv_bwd2(xx_ref, ex_ref, cwx, dxc, dx_ref, dcwx, c)
        _conv_bwd2(xb_ref, eb_ref, cwb, dbc_, db_ref, dcwb, c)
        _conv_bwd2(xc_ref, ec_ref, cwc, dcc, dc_ref, dcwc, c)

    def halo(width, unit):
        return pl.BlockSpec((HALO, width), lambda g, i: (jnp.maximum((n - 1 - i) * hb - 1, 0), unit + g))

    row = pl.BlockSpec((1, LANE), lambda g, i: (0, 0))
    grow = pl.BlockSpec((None, 1, LANE), lambda g, i: (g, 0, 0))
    grow_shape = jax.ShapeDtypeStruct((M2_GROUPS, 1, LANE), f32)
    ob_w = pl.BlockSpec((c, M2_GW), lambda g, i: (n - 1 - i, g))
    ob_n = pl.BlockSpec((c, LANE), lambda g, i: (n - 1 - i, g))
    return pl.pallas_call(
        body, name="ssd_bwd", grid=(M2_GROUPS, n),
        in_specs=[blk(M2_GW, U_CX // 2), blk(LANE, U_CB), blk(LANE, U_CC),
                  halo(M2_GW, U_CX // 2), halo(LANE, U_CB), halo(LANE, U_CC), blk(M2_GW, U_CZ // 2),
                  pl.BlockSpec((c, LANE), lambda g, i: (n - 1 - i, U_SMC)),
                  par(CONV_K, M2_GW, 0), par(CONV_K, LANE, 8), par(CONV_K, LANE, 12),
                  par(1, M2_GW, 0), par(1, LANE, 8), par(1, LANE, 12), row, row, row, par(1, M2_GW, 0),
                  pl.BlockSpec((None, None, M2_REP, M2_HEAD_DIM, M2_STATE), lambda g, i: (g, n - 1 - i, 0, 0, 0)),
                  ob_w],
        out_specs=[ob_w, ob_n, ob_n, ob_w, pl.BlockSpec((None, c, LANE), lambda g, i: (g, n - 1 - i, 0)),
                   par(CONV_K, M2_GW, 0), par(CONV_K, LANE, 0), par(CONV_K, LANE, 0),
                   par(1, M2_GW, 0), par(1, LANE, 0), par(1, LANE, 0), grow, grow, grow, par(1, M2_GW, 0)],
        out_shape=[jax.ShapeDtypeStruct((rows, M2_WIDTH), f32), jax.ShapeDtypeStruct((rows, M2_GROUPS * M2_STATE), f32),
                   jax.ShapeDtypeStruct((rows, M2_GROUPS * M2_STATE), f32), jax.ShapeDtypeStruct((rows, M2_WIDTH), f32),
                   jax.ShapeDtypeStruct((M2_GROUPS, rows, LANE), f32),
                   jax.ShapeDtypeStruct((CONV_K, M2_WIDTH), f32), jax.ShapeDtypeStruct((CONV_K, M2_GROUPS * M2_STATE), f32),
                   jax.ShapeDtypeStruct((CONV_K, M2_GROUPS * M2_STATE), f32),
                   jax.ShapeDtypeStruct((1, M2_WIDTH), f32), jax.ShapeDtypeStruct((1, M2_GROUPS * M2_STATE), f32),
                   jax.ShapeDtypeStruct((1, M2_GROUPS * M2_STATE), f32), grow_shape, grow_shape, grow_shape,
                   jax.ShapeDtypeStruct((1, M2_WIDTH), f32)],
        scratch_shapes=[pltpu.VMEM((M2_REP, M2_HEAD_DIM, M2_STATE), f32),
                        pltpu.VMEM((c + HALO, M2_GW), f32), pltpu.VMEM((c + HALO, LANE), f32), pltpu.VMEM((c + HALO, LANE), f32),
                        pltpu.VMEM((c + HALO, M2_GW), f32), pltpu.VMEM((c + HALO, LANE), f32), pltpu.VMEM((c + HALO, LANE), f32)],
        compiler_params=_cparams(("arbitrary", "arbitrary")))(
            proj, proj, proj, proj, proj, proj, proj, proj, conv_w, conv_w, conv_w, conv_b, conv_b, conv_b,
            alog_row, dtb_row, d_row, normw, ck, dy)


S5_TILE = 128
S5_SW = S5_LANES // S5_BLK


def _scan_down(br, bi, ar, ai):
    t = br.shape[0]
    row = lax.broadcasted_iota(jnp.int32, br.shape, 0)
    d = 1
    while d < t:
        keep = row >= d
        sr = jnp.where(keep, pltpu.roll(br, d, 0), 0.0)
        si = jnp.where(keep, pltpu.roll(bi, d, 0), 0.0)
        br, bi = br + ar * sr - ai * si, bi + ar * si + ai * sr
        ar, ai = ar * ar - ai * ai, 2.0 * ar * ai
        d *= 2
    return br, bi


def _scan_up(br, bi, ar, ai):
    t = br.shape[0]
    row = lax.broadcasted_iota(jnp.int32, br.shape, 0)
    d = 1
    while d < t:
        keep = row < t - d
        sr = jnp.where(keep, pltpu.roll(br, t - d, 0), 0.0)
        si = jnp.where(keep, pltpu.roll(bi, t - d, 0), 0.0)
        br, bi = br + ar * sr - ai * si, bi + ar * si + ai * sr
        ar, ai = ar * ar - ai * ai, 2.0 * ar * ai
        d *= 2
    return br, bi


def _s5_states(u_j, bbr, bbi, ar, ai, cr, ci_):
    br = dot_nn(u_j, bbr)
    bi = dot_nn(u_j, bbi)
    row0 = lax.broadcasted_iota(jnp.int32, br.shape, 0) == 0
    br = br + jnp.where(row0, ar * cr - ai * ci_, 0.0)
    bi = bi + jnp.where(row0, ar * ci_ + ai * cr, 0.0)
    return _scan_down(br, bi, ar, ai)


def _s5_fwd(proj, a_rows, bbr, bbi, ccr, cci, d_row):
    rows = proj.shape[0]
    t = min(S5_TILE, rows)
    n = rows // t

    def body(u_ref, a_ref, bbr_ref, bbi_ref, ccr_ref, cci_ref, d_ref, y_ref, ck_ref, carry_ref):
        i = pl.program_id(0)

        @pl.when(i == 0)
        def _():
            carry_ref[...] = jnp.zeros_like(carry_ref)

        ck_ref[...] = carry_ref[...]
        for j in range(S5_BLK):
            lanes = pl.ds(j * S5_SW, S5_SW)
            ch = pl.ds(j * LANE, LANE)
            u_j = u_ref[:, ch]
            sr, si = _s5_states(u_j, bbr_ref[j], bbi_ref[j], a_ref[0:1, lanes], a_ref[1:2, lanes],
                                carry_ref[0:1, lanes], carry_ref[1:2, lanes])
            y_ref[:, ch] = dot_nn(sr, ccr_ref[j]) - dot_nn(si, cci_ref[j]) + d_ref[:, ch] * u_j
            carry_ref[0:1, lanes] = sr[t - 1:t, :]
            carry_ref[1:2, lanes] = si[t - 1:t, :]

    whole3 = lambda s: pl.BlockSpec(s, lambda i: (0, 0, 0))
    return pl.pallas_call(
        body, name="s5_fwd", grid=(n,),
        in_specs=[pl.BlockSpec((t, S5_WIDTH), lambda i: (i, U_SU // S5_BLK)),
                  pl.BlockSpec((2, S5_LANES), lambda i: (0, 0)),
                  whole3(bbr.shape), whole3(bbi.shape), whole3(ccr.shape), whole3(cci.shape),
                  pl.BlockSpec((1, S5_WIDTH), lambda i: (0, 0))],
        out_specs=[pl.BlockSpec((t, S5_WIDTH), lambda i: (i, 0)),
                   pl.BlockSpec((None, 2, S5_LANES), lambda i: (i, 0, 0))],
        out_shape=[jax.ShapeDtypeStruct((rows, S5_WIDTH), f32), jax.ShapeDtypeStruct((n, 2, S5_LANES), f32)],
        scratch_shapes=[pltpu.VMEM((2, S5_LANES), f32)],
        compiler_params=_cparams(("arbitrary",)))(proj, a_rows, bbr, bbi, ccr, cci, d_row)


def _s5_bwd(proj, dy, ck, a_rows, bbr, bbi, ccr, cci, d_row):
    rows = proj.shape[0]
    t = min(S5_TILE, rows)
    n = rows // t

    def body(u_ref, dy_ref, ck_ref, a_ref, bbr_ref, bbi_ref, ccr_ref, cci_ref, d_ref,
             du_ref, da_ref, dbbr_ref, dbbi_ref, dccr_ref, dcci_ref, dd_ref, lam_ref):
        i = pl.program_id(0)

        @pl.when(i == 0)
        def _():
            lam_ref[...] = jnp.zeros_like(lam_ref)
            for r in (da_ref, dbbr_ref, dbbi_ref, dccr_ref, dcci_ref, dd_ref):
                r[...] = jnp.zeros_like(r)

        for j in range(S5_BLK):
            lanes = pl.ds(j * S5_SW, S5_SW)
            ch = pl.ds(j * LANE, LANE)
            u_j = u_ref[:, ch]
            dy_j = dy_ref[:, ch]
            ar, ai = a_ref[0:1, lanes], a_ref[1:2, lanes]
            cr, ci_ = ck_ref[0:1, lanes], ck_ref[1:2, lanes]
            sr, si = _s5_states(u_j, bbr_ref[j], bbi_ref[j], ar, ai, cr, ci_)
            gr = dot_nt(dy_j, ccr_ref[j])
            gi = -dot_nt(dy_j, cci_ref[j])
            last = lax.broadcasted_iota(jnp.int32, gr.shape, 0) == t - 1
            lr0, li0 = lam_ref[0:1, lanes], lam_ref[1:2, lanes]
            gr = gr + jnp.where(last, ar * lr0 + ai * li0, 0.0)
            gi = gi + jnp.where(last, ar * li0 - ai * lr0, 0.0)
            lr, li = _scan_up(gr, gi, ar, -ai)
            lam_ref[0:1, lanes] = lr[0:1, :]
            lam_ref[1:2, lanes] = li[0:1, :]
            du_ref[:, ch] = dot_nt(lr, bbr_ref[j]) + dot_nt(li, bbi_ref[j]) + d_ref[:, ch] * dy_j
            dbbr_ref[j] += dot_tn(u_j, lr)
            dbbi_ref[j] += dot_tn(u_j, li)
            dccr_ref[j] += dot_tn(sr, dy_j)
            dcci_ref[j] += -dot_tn(si, dy_j)
            dd_ref[:, ch] += jnp.sum(dy_j * u_j, axis=0, keepdims=True)
            row0 = lax.broadcasted_iota(jnp.int32, sr.shape, 0) == 0
            pr = jnp.where(row0, cr, pltpu.roll(sr, 1, 0))
            pi = jnp.where(row0, ci_, pltpu.roll(si, 1, 0))
            da_ref[0:1, lanes] += jnp.sum(lr * pr + li * pi, axis=0, keepdims=True)
            da_ref[1:2, lanes] += jnp.sum(li * pr - lr * pi, axis=0, keepdims=True)

    whole3 = lambda s: pl.BlockSpec(s, lambda i: (0, 0, 0))
    whole2 = lambda s: pl.BlockSpec(s, lambda i: (0, 0))
    return pl.pallas_call(
        body, name="s5_bwd", grid=(n,),
        in_specs=[pl.BlockSpec((t, S5_WIDTH), lambda i: (n - 1 - i, U_SU // S5_BLK)),
                  pl.BlockSpec((t, S5_WIDTH), lambda i: (n - 1 - i, 0)),
                  pl.BlockSpec((None, 2, S5_LANES), lambda i: (n - 1 - i, 0, 0)),
                  whole2((2, S5_LANES)), whole3(bbr.shape), whole3(bbi.shape), whole3(ccr.shape), whole3(cci.shape),
                  whole2((1, S5_WIDTH))],
        out_specs=[pl.BlockSpec((t, S5_WIDTH), lambda i: (n - 1 - i, 0)), whole2((2, S5_LANES)),
                   whole3(bbr.shape), whole3(bbi.shape), whole3(ccr.shape), whole3(cci.shape), whole2((1, S5_WIDTH))],
        out_shape=[jax.ShapeDtypeStruct((rows, S5_WIDTH), f32), jax.ShapeDtypeStruct((2, S5_LANES), f32),
                   jax.ShapeDtypeStruct(bbr.shape, f32), jax.ShapeDtypeStruct(bbi.shape, f32),
                   jax.ShapeDtypeStruct(ccr.shape, f32), jax.ShapeDtypeStruct(cci.shape, f32),
                   jax.ShapeDtypeStruct((1, S5_WIDTH), f32)],
        scratch_shapes=[pltpu.VMEM((2, S5_LANES), f32)],
        compiler_params=_cparams(("arbitrary",)))(proj, dy, ck, a_rows, bbr, bbi, ccr, cci, d_row)


def _s5_prep(lam_re, lam_im, log_step, b_re, b_im, c_re, c_im, d_skip):
    lam_re = jnp.minimum(lam_re, -1e-4)
    step = jnp.exp(log_step)[:, None]
    mag = jnp.exp(lam_re * step)
    ab_re = mag * jnp.cos(lam_im * step)
    ab_im = mag * jnp.sin(lam_im * step)
    den = lam_re * lam_re + lam_im * lam_im
    f_re = ((ab_re - 1.0) * lam_re + ab_im * lam_im) / den
    f_im = (ab_im * lam_re - (ab_re - 1.0) * lam_im) / den
    bb_re = f_re[..., None] * b_re - f_im[..., None] * b_im
    bb_im = f_re[..., None] * b_im + f_im[..., None] * b_re
    eye = jnp.eye(8, dtype=f32)

    def drive(bb):
        r = bb.reshape(S5_BLK, 8, S5_STATE, S5_GROUP_SIZE).transpose(0, 1, 3, 2)
        return (r[:, :, :, None, :] * eye[None, :, None, :, None]).reshape(S5_BLK, LANE, S5_SW)

    def readout(cc):
        r = cc.reshape(S5_BLK, 8, S5_GROUP_SIZE, S5_STATE).transpose(0, 1, 3, 2)
        return (r[:, :, :, None, :] * eye[None, :, None, :, None]).reshape(S5_BLK, S5_SW, LANE)

    a_rows = jnp.stack([ab_re.reshape(S5_LANES), ab_im.reshape(S5_LANES)])
    return a_rows, drive(bb_re), drive(bb_im), readout(c_re), readout(c_im), d_skip.reshape(1, S5_WIDTH)


def _adam_math(w, g, m, v):
    m = ADAM_B1 * m + (1.0 - ADAM_B1) * g
    v = ADAM_B2 * v + (1.0 - ADAM_B2) * (g * g)
    m_hat = m / (1.0 - ADAM_B1 ** ADAM_STEP)
    v_hat = v / (1.0 - ADAM_B2 ** ADAM_STEP)
    delta = -ADAM_LR * (m_hat / (jnp.sqrt(v_hat) + ADAM_EPS) + ADAM_WD * w)
    return delta, m, v


def _adamw(name, w, g, m, v):
    rows, width = w.shape
    t = rows
    for cand in (512, 256, 128, 64, 32, 16, 8):
        if rows % cand == 0 and cand * width * 4 * 7 * 2 <= VMEM_LIMIT // 2:
            t = cand
            break

    def body(w_ref, g_ref, m_ref, v_ref, d_ref, nm_ref, nv_ref):
        d, nm, nv = _adam_math(w_ref[...], g_ref[...], m_ref[...], v_ref[...])
        d_ref[...] = d
        nm_ref[...] = nm
        nv_ref[...] = nv

    spec = pl.BlockSpec((t, width), lambda i: (i, 0))
    shape = jax.ShapeDtypeStruct((rows, width), f32)
    return pl.pallas_call(body, name=name, grid=(rows // t,), in_specs=[spec] * 4, out_specs=[spec] * 3,
                          out_shape=[shape] * 3, compiler_params=_cparams(("parallel",)))(w, g, m, v)


def _sum_slots(name, buf):
    _, rows, width = buf.shape
    t = next(cand for cand in (512, 256, 128, 64, 32, 16) if rows % cand == 0)

    def body(b_ref, o_ref):
        acc = b_ref[0].astype(f32)
        for s in range(1, N_DEV):
            acc = acc + b_ref[s].astype(f32)
        o_ref[...] = acc

    return pl.pallas_call(
        body, name=name, grid=(rows // t,), in_specs=[pl.BlockSpec((N_DEV, t, width), lambda i: (0, i, 0))],
        out_specs=pl.BlockSpec((t, width), lambda i: (i, 0)), out_shape=jax.ShapeDtypeStruct((rows, width), f32),
        compiler_params=_cparams(("parallel",)))(buf)


def _peer(idx):
    return (idx // 4, (idx // 2) % 2, idx % 2)


def _exchange(name, buf, gather):
    blk_shape = buf.shape if gather else buf.shape[1:]

    def body(src_ref, out_ref, send_sems, recv_sems, local_sem):
        me = lax.axis_index("x") * 4 + lax.axis_index("y") * 2 + lax.axis_index("c")
        own = src_ref if gather else src_ref.at[me]
        mine = pltpu.make_async_copy(own, out_ref.at[me], local_sem)
        mine.start()
        sends = []
        for k in range(1, N_DEV):
            to = (me + k) % N_DEV
            cp = pltpu.make_async_remote_copy(
                src_ref=src_ref if gather else src_ref.at[to], dst_ref=out_ref.at[me],
                send_sem=send_sems.at[k - 1], recv_sem=recv_sems.at[k - 1],
                device_id=_peer(to), device_id_type=pl.DeviceIdType.MESH)
            cp.start()
            sends.append(cp)
        for k in range(1, N_DEV):
            frm = (me + N_DEV - k) % N_DEV
            pltpu.make_async_remote_copy(
                src_ref=own, dst_ref=out_ref.at[frm], send_sem=send_sems.at[k - 1], recv_sem=recv_sems.at[k - 1],
                device_id=_peer(frm), device_id_type=pl.DeviceIdType.MESH).wait_recv()
        for cp in sends:
            cp.wait_send()
        mine.wait()

    return pl.pallas_call(
        body, name=name, in_specs=[pl.BlockSpec(memory_space=pl.ANY)], out_specs=pl.BlockSpec(memory_space=pl.ANY),
        out_shape=jax.ShapeDtypeStruct((N_DEV,) + tuple(blk_shape), buf.dtype),
        scratch_shapes=[pltpu.SemaphoreType.DMA((N_DEV - 1,)), pltpu.SemaphoreType.DMA((N_DEV - 1,)),
                        pltpu.SemaphoreType.DMA])(buf)


PACK_W = 1024
BIG = ("w_in", "s5_glu_w", "proj_a", "proj_b", "proj_c", "w_out")
BIG_COL_SHARDED = {"w_in": True, "s5_glu_w": False, "proj_a": True, "proj_b": True, "proj_c": True, "w_out": False}
CONV = ("gdn_conv_w", "m2_conv_w")
SMALL = ("norm_w", "gdn_a_log", "gdn_dt_bias", "gdn_norm_w", "s5_lam_re", "s5_lam_im", "s5_log_step",
         "s5_b_re", "s5_b_im", "s5_c_re", "s5_c_im", "s5_d", "s5_glu_b", "m2_conv_b", "m2_a_log", "m2_dt_bias",
         "m2_d", "m2_norm_w", "final_norm_w")
WEIGHTS = ("norm_w", "w_in", "gdn_conv_w", "gdn_a_log", "gdn_dt_bias", "gdn_norm_w", "s5_lam_re", "s5_lam_im",
           "s5_log_step", "s5_b_re", "s5_b_im", "s5_c_re", "s5_c_im", "s5_d", "s5_glu_w", "s5_glu_b", "m2_conv_w",
           "m2_conv_b", "m2_a_log", "m2_dt_bias", "m2_d", "m2_norm_w", "proj_a", "proj_b", "proj_c", "w_out",
           "final_norm_w")


def _pack(arrays, width, dtype):
    flat = jnp.concatenate([a.reshape(-1).astype(dtype) for a in arrays])
    unit = width * 16
    pad = (-flat.shape[0]) % unit
    if pad:
        flat = jnp.concatenate([flat, jnp.zeros((pad,), dtype)])
    return flat.reshape(-1, width)


def _unpack(flat, shapes):
    out, off = [], 0
    for s in shapes:
        size = 1
        for d in s:
            size *= d
        out.append(flat[off:off + size].reshape(s))
        off += size
    return out


def _win_to_padded(w):
    d = w.shape[0]
    z = lambda n: jnp.zeros((d, n), w.dtype)
    return jnp.concatenate([w[:, 8736:14880], w[:, 4112:5648], w[:, 0:4096], w[:, 5648:8720],
                            w[:, 4096:4112], z(LANE - 16), w[:, 8720:8736], z(LANE - 16), z(2 * LANE)], axis=1)


def _win_from_padded(g):
    u = LANE
    return jnp.concatenate([g[:, U_Q * u:U_CZ * u], g[:, U_SMA * u:U_SMA * u + 16], g[:, U_SU * u:U_Q * u],
                            g[:, U_CZ * u:U_SMA * u], g[:, U_SMC * u:U_SMC * u + 16], g[:, 0:U_SU * u]], axis=1)


def _lane_row(vals, offset):
    n = vals.shape[0]
    return jnp.concatenate([jnp.zeros((offset,), f32), vals, jnp.zeros((LANE - offset - n,), f32)]).reshape(1, LANE)


def _layer_fwd(x, lw):
    h = _tok_fwd("rms_fwd", _rms_fn, [(x, D_MODEL, 0)], [lw["norm_w"].reshape(1, D_MODEL)], [(D_MODEL, bf16)], 256)[0]
    proj = _mm("proj_fwd", h, lw["w_in_p"], "nn")
    g_al, g_dt = _lane_row(lw["gdn_a_log"], GDN_HEADS), _lane_row(lw["gdn_dt_bias"], GDN_HEADS)
    g_nw = lw["gdn_norm_w"].reshape(1, HEAD_DIM)
    y_a, ck_a = _gdn_fwd(proj, lw["gdn_conv_w"], g_al, g_dt, g_nw)
    s5p = _s5_prep(lw["s5_lam_re"], lw["s5_lam_im"], lw["s5_log_step"], lw["s5_b_re"], lw["s5_b_im"],
                   lw["s5_c_re"], lw["s5_c_im"], lw["s5_d"])
    y_pre, ck_b = _s5_fwd(proj, *s5p)
    glu_b = lw["s5_glu_b"].reshape(1, S5_WIDTH)
    y_b = _tok_fwd("s5_tail_fwd", _s5_tail_fn, [(y_pre, S5_WIDTH, 0), (proj, S5_WIDTH, U_SG // S5_BLK)],
                   [lw["s5_glu_w"], glu_b], [(S5_WIDTH, bf16)], 256)[0]
    m_al, m_dt, m_d = _lane_row(lw["m2_a_log"], 0), _lane_row(lw["m2_dt_bias"], 0), _lane_row(lw["m2_d"], 0)
    m_cb = lw["m2_conv_b"].reshape(1, -1)
    m_nw = lw["m2_norm_w"].reshape(1, M2_WIDTH)
    y_c, ck_c = _ssd_fwd(proj, lw["m2_conv_w"], m_cb, m_al, m_dt, m_d, m_nw)
    pa = _mm("proj_a_fwd", y_a, lw["proj_a"], "nn")
    pb = _mm("proj_b_fwd", y_b, lw["proj_b"], "nn")
    pc = _mm("proj_c_fwd", y_c, lw["proj_c"], "nn")
    gate_acts = [(proj, D_MODEL, 0), (proj, D_MODEL, 1), (proj, D_MODEL, 2),
                 (pa, D_MODEL, 0), (pb, D_MODEL, 0), (pc, D_MODEL, 0)]
    merged = _tok_fwd("gate_fwd", _gate_fn, gate_acts, [], [(D_MODEL, bf16)], 128)[0]
    x_next = _mm("w_out_fwd", merged, lw["w_out"], "nn", residual=x)
    saved = dict(x=x, h=h, proj=proj, y_a=y_a, ck_a=ck_a, y_pre=y_pre, ck_b=ck_b, y_b=y_b, y_c=y_c, ck_c=ck_c,
                 pa=pa, pb=pb, pc=pc, merged=merged)
    return x_next, saved


def _layer_bwd(dx_next, lw, sv):
    rows = dx_next.shape[0]
    proj = sv["proj"]
    g = {}
    d_merged = _mm("w_out_bwd_x", dx_next, lw["w_out"], "nt", out_dtype=bf16)
    g["w_out"] = _mm("w_out_bwd_w", sv["merged"], dx_next, "tn")
    gate_acts = [(proj, D_MODEL, 0), (proj, D_MODEL, 1), (proj, D_MODEL, 2),
                 (sv["pa"], D_MODEL, 0), (sv["pb"], D_MODEL, 0), (sv["pc"], D_MODEL, 0)]
    dla, dlb, dlc, dpa, dpb, dpc = _tok_bwd(
        "gate_bwd", _gate_fn, gate_acts, [], [(d_merged, D_MODEL, 0)],
        [(0, f32), (1, f32), (2, f32), (3, bf16), (4, bf16), (5, bf16)], 128)
    dy_a = _mm("proj_a_bwd_x", dpa, lw["proj_a"], "nt", out_dtype=bf16)
    dy_b = _mm("proj_b_bwd_x", dpb, lw["proj_b"], "nt")
    dy_c = _mm("proj_c_bwd_x", dpc, lw["proj_c"], "nt", out_dtype=bf16)
    g["proj_a"] = _mm("proj_a_bwd_w", sv["y_a"], dpa, "tn")
    g["proj_b"] = _mm("proj_b_bwd_w", sv["y_b"], dpb, "tn")
    g["proj_c"] = _mm("proj_c_bwd_w", sv["y_c"], dpc, "tn")

    m_al, m_dt, m_d = _lane_row(lw["m2_a_log"], 0), _lane_row(lw["m2_dt_bias"], 0), _lane_row(lw["m2_d"], 0)
    m_cb = lw["m2_conv_b"].reshape(1, -1)
    m_nw = lw["m2_norm_w"].reshape(1, M2_WIDTH)
    (dcx, dcb, dcc, dcz, dsmc, dcwx, dcwb, dcwc, dbx, dbb, dbc, dal, ddt, ddk, dnw) = _ssd_bwd(
        proj, dy_c, sv["ck_c"], lw["m2_conv_w"], m_cb, m_al, m_dt, m_d, m_nw)
    g["m2_conv_w"] = jnp.concatenate([dcwx, dcwb, dcwc], axis=1)
    g["m2_conv_b"] = jnp.concatenate([dbx, dbb, dbc], axis=1).reshape(-1)
    g["m2_a_log"] = jnp.sum(dal, axis=(0, 1))[:M2_HEADS]
    g["m2_dt_bias"] = jnp.sum(ddt, axis=(0, 1))[:M2_HEADS]
    g["m2_d"] = jnp.sum(ddk, axis=(0, 1))[:M2_HEADS]
    g["m2_norm_w"] = dnw.reshape(-1)
    dsmc = jnp.sum(dsmc, axis=0)

    glu_b = lw["s5_glu_b"].reshape(1, S5_WIDTH)
    dypre, dsg, dglu_w, dglu_b = _tok_bwd(
        "s5_tail_bwd", _s5_tail_fn, [(sv["y_pre"], S5_WIDTH, 0), (proj, S5_WIDTH, U_SG // S5_BLK)],
        [lw["s5_glu_w"], glu_b], [(dy_b, S5_WIDTH, 0)], [(0, f32), (1, f32)], 256)
    g["s5_glu_w"] = dglu_w
    g["s5_glu_b"] = dglu_b.reshape(-1)
    s5_names = ("s5_lam_re", "s5_lam_im", "s5_log_step", "s5_b_re", "s5_b_im", "s5_c_re", "s5_c_im", "s5_d")
    s5p, s5_vjp = jax.vjp(_s5_prep, *[lw[k] for k in s5_names])
    dsu, da, dbbr, dbbi, dccr, dcci, dd = _s5_bwd(proj, dypre, sv["ck_b"], *s5p)
    for k, val in zip(s5_names, s5_vjp((da, dbbr, dbbi, dccr, dcci, dd))):
        g[k] = val

    g_al, g_dt = _lane_row(lw["gdn_a_log"], GDN_HEADS), _lane_row(lw["gdn_dt_bias"], GDN_HEADS)
    g_nw = lw["gdn_norm_w"].reshape(1, HEAD_DIM)
    (dq, dk, dv, daz, dsma, dcwq, dcwk, dcwv, dgal, dgdt, dgnw) = _gdn_bwd(
        proj, dy_a, sv["ck_a"], lw["gdn_conv_w"], g_al, g_dt, g_nw)
    g["gdn_conv_w"] = jnp.concatenate([dcwq, dcwk, dcwv], axis=1)
    g["gdn_a_log"] = jnp.sum(dgal, axis=(0, 1))[GDN_HEADS:2 * GDN_HEADS]
    g["gdn_dt_bias"] = jnp.sum(dgdt, axis=(0, 1))[GDN_HEADS:2 * GDN_HEADS]
    g["gdn_norm_w"] = jnp.sum(dgnw, axis=(0, 1))
    dsma = jnp.sum(dsma, axis=0)

    dproj = jnp.concatenate([dla, dlb, dlc, dsu, dsg, dq, dk, dv, daz, dcz, dcx, dcb, dcc, dsma, dsmc,
                             jnp.zeros((rows, 2 * LANE), f32)], axis=1)
    g["w_in_p"] = _mm("proj_bwd_w", sv["h"], dproj, "tn")
    dh = _mm("proj_bwd_x", dproj, lw["w_in_p"], "nt")
    dx, dnorm = _tok_bwd("rms_bwd", _rms_fn, [(sv["x"], D_MODEL, 0)], [lw["norm_w"].reshape(1, D_MODEL)],
                         [(dh, D_MODEL, 0)], [(0, f32)], 256, residuals={0: dx_next})
    g["norm_w"] = dnorm.reshape(-1)
    return dx, g


def kernel(x, norm_w, w_in, gdn_conv_w, gdn_a_log, gdn_dt_bias, gdn_norm_w, s5_lam_re, s5_lam_im, s5_log_step, s5_b_re, s5_b_im, s5_c_re, s5_c_im, s5_d, s5_glu_w, s5_glu_b, m2_conv_w, m2_conv_b, m2_a_log, m2_dt_bias, m2_d, m2_norm_w, proj_a, proj_b, proj_c, w_out, final_norm_w, loss_target, m_norm_w, m_w_in, m_gdn_conv_w, m_gdn_a_log, m_gdn_dt_bias, m_gdn_norm_w, m_s5_lam_re, m_s5_lam_im, m_s5_log_step, m_s5_b_re, m_s5_b_im, m_s5_c_re, m_s5_c_im, m_s5_d, m_s5_glu_w, m_s5_glu_b, m_m2_conv_w, m_m2_conv_b, m_m2_a_log, m_m2_dt_bias, m_m2_d, m_m2_norm_w, m_proj_a, m_proj_b, m_proj_c, m_w_out, m_final_norm_w, v_norm_w, v_w_in, v_gdn_conv_w, v_gdn_a_log, v_gdn_dt_bias, v_gdn_norm_w, v_s5_lam_re, v_s5_lam_im, v_s5_log_step, v_s5_b_re, v_s5_b_im, v_s5_c_re, v_s5_c_im, v_s5_d, v_s5_glu_w, v_s5_glu_b, v_m2_conv_w, v_m2_conv_b, v_m2_a_log, v_m2_dt_bias, v_m2_d, v_m2_norm_w, v_proj_a, v_proj_b, v_proj_c, v_w_out, v_final_norm_w):
    w = dict(norm_w=norm_w, w_in=w_in, gdn_conv_w=gdn_conv_w, gdn_a_log=gdn_a_log, gdn_dt_bias=gdn_dt_bias,
             gdn_norm_w=gdn_norm_w, s5_lam_re=s5_lam_re, s5_lam_im=s5_lam_im, s5_log_step=s5_log_step,
             s5_b_re=s5_b_re, s5_b_im=s5_b_im, s5_c_re=s5_c_re, s5_c_im=s5_c_im, s5_d=s5_d, s5_glu_w=s5_glu_w,
             s5_glu_b=s5_glu_b, m2_conv_w=m2_conv_w, m2_conv_b=m2_conv_b, m2_a_log=m2_a_log, m2_dt_bias=m2_dt_bias,
             m2_d=m2_d, m2_norm_w=m2_norm_w, proj_a=proj_a, proj_b=proj_b, proj_c=proj_c, w_out=w_out,
             final_norm_w=final_norm_w)
    mom = dict(norm_w=m_norm_w, w_in=m_w_in, gdn_conv_w=m_gdn_conv_w, gdn_a_log=m_gdn_a_log,
               gdn_dt_bias=m_gdn_dt_bias, gdn_norm_w=m_gdn_norm_w, s5_lam_re=m_s5_lam_re, s5_lam_im=m_s5_lam_im,
               s5_log_step=m_s5_log_step, s5_b_re=m_s5_b_re, s5_b_im=m_s5_b_im, s5_c_re=m_s5_c_re,
               s5_c_im=m_s5_c_im, s5_d=m_s5_d, s5_glu_w=m_s5_glu_w, s5_glu_b=m_s5_glu_b, m2_conv_w=m_m2_conv_w,
               m2_conv_b=m_m2_conv_b, m2_a_log=m_m2_a_log, m2_dt_bias=m_m2_dt_bias, m2_d=m_m2_d,
               m2_norm_w=m_m2_norm_w, proj_a=m_proj_a, proj_b=m_proj_b, proj_c=m_proj_c, w_out=m_w_out,
               final_norm_w=m_final_norm_w)
    var = dict(norm_w=v_norm_w, w_in=v_w_in, gdn_conv_w=v_gdn_conv_w, gdn_a_log=v_gdn_a_log,
               gdn_dt_bias=v_gdn_dt_bias, gdn_norm_w=v_gdn_norm_w, s5_lam_re=v_s5_lam_re, s5_lam_im=v_s5_lam_im,
               s5_log_step=v_s5_log_step, s5_b_re=v_s5_b_re, s5_b_im=v_s5_b_im, s5_c_re=v_s5_c_re,
               s5_c_im=v_s5_c_im, s5_d=v_s5_d, s5_glu_w=v_s5_glu_w, s5_glu_b=v_s5_glu_b, m2_conv_w=v_m2_conv_w,
               m2_conv_b=v_m2_conv_b, m2_a_log=v_m2_a_log, m2_dt_bias=v_m2_dt_bias, m2_d=v_m2_d,
               m2_norm_w=v_m2_norm_w, proj_a=v_proj_a, proj_b=v_proj_b, proj_c=v_proj_c, w_out=v_w_out,
               final_norm_w=v_final_norm_w)
    me = lax.axis_index("x") * 4 + lax.axis_index("y") * 2 + lax.axis_index("c")
    x2 = x[0]
    tgt = loss_target[0]

    big_shapes = [w[k].shape[1:] for k in BIG]
    big_local = _pack([w[k][i] for i in range(DEPTH) for k in BIG], PACK_W, bf16)
    conv_shapes = [w[k].shape[1:] for k in CONV]
    conv_local = _pack([w[k][i] for i in range(DEPTH) for k in CONV], LANE, f32)
    big_all = _exchange("gather_weights", big_local, True).reshape(N_DEV, -1)
    conv_all = _exchange("gather_conv", conv_local, True).reshape(N_DEV, -1)

    def join(pieces, col_sharded):
        if col_sharded:
            return pieces.transpose(1, 0, 2).reshape(pieces.shape[1], -1)
        return pieces.reshape(-1, pieces.shape[2])

    def split(full, col_sharded, shard_shape):
        r, c = shard_shape
        if col_sharded:
            return full.reshape(r, N_DEV, c).transpose(1, 0, 2)
        return full.reshape(N_DEV, r, c)

    layers = []
    off_b = off_c = 0
    for i in range(DEPTH):
        lw = {}
        for k, shp in zip(BIG, big_shapes):
            size = shp[0] * shp[1]
            lw[k] = join(big_all[:, off_b:off_b + size].reshape(N_DEV, *shp), BIG_COL_SHARDED[k])
            off_b += size
        for k, shp in zip(CONV, conv_shapes):
            size = shp[0] * shp[1]
            lw[k] = join(conv_all[:, off_c:off_c + size].reshape(N_DEV, *shp), True)
            off_c += size
        lw["w_in_p"] = _win_to_padded(lw.pop("w_in"))
        for k in SMALL:
            if k != "final_norm_w":
                lw[k] = w[k][i]
        layers.append(lw)

    saved = []
    act = x2
    for i in range(DEPTH):
        act, sv = _layer_fwd(act, layers[i])
        saved.append(sv)
    loss_row, dact, dfinal = _loss_grad(act, final_norm_w.reshape(1, D_MODEL), tgt)
    grads = [None] * DEPTH
    for i in reversed(range(DEPTH)):
        dact, grads[i] = _layer_bwd(dact, layers[i], saved[i])
        saved[i] = None
    loss = lax.psum(loss_row[0, 0], ("x", "y", "c"))

    pieces = []
    for i in range(DEPTH):
        grads[i]["w_in"] = _win_from_padded(grads[i].pop("w_in_p"))
        for k, shp in zip(BIG, big_shapes):
            pieces.append(split(grads[i][k], BIG_COL_SHARDED[k], shp).reshape(N_DEV, -1))
    send = jnp.concatenate(pieces, axis=1).astype(bf16)
    rows_big = big_local.shape[0]
    pad = rows_big * PACK_W - send.shape[1]
    if pad:
        send = jnp.concatenate([send, jnp.zeros((N_DEV, pad), bf16)], axis=1)
    recv = _exchange("scatter_grads", send.reshape(N_DEV, rows_big, PACK_W), False)
    g_big = _unpack(_sum_slots("sum_big", recv).reshape(-1), [s for _ in range(DEPTH) for s in big_shapes])
    g_out = {}
    for j, k in enumerate(BIG):
        g_out[k] = jnp.stack([g_big[i * len(BIG) + j] for i in range(DEPTH)])

    small_shapes = [w[k].shape[1:] for k in SMALL if k != "final_norm_w"]
    small_parts = [grads[i][k] for i in range(DEPTH) for k in SMALL if k != "final_norm_w"]
    small_parts += [grads[i][k] for i in range(DEPTH) for k in CONV]
    small_parts.append(dfinal.reshape(-1))
    small_local = _pack(small_parts, LANE, f32)
    small_sum = _sum_slots("sum_small", _exchange("gather_small", small_local, True)).reshape(-1)
    full_conv_shapes = [(CONV_K, s[1] * N_DEV) for s in conv_shapes]
    unp = _unpack(small_sum, [s for _ in range(DEPTH) for s in small_shapes]
                  + [s for _ in range(DEPTH) for s in full_conv_shapes] + [(D_MODEL,)])
    names_small = [k for k in SMALL if k != "final_norm_w"]
    for j, k in enumerate(names_small):
        g_out[k] = jnp.stack([unp[i * len(names_small) + j] for i in range(DEPTH)])
    base = DEPTH * len(names_small)
    for j, k in enumerate(CONV):
        width = conv_shapes[j][1]
        full = jnp.stack([unp[base + i * len(CONV) + j] for i in range(DEPTH)])
        g_out[k] = lax.dynamic_slice_in_dim(full, me * width, width, axis=2)
    g_out["final_norm_w"] = unp[-1]

    delta, new_m, new_v = {}, {}, {}
    for k in BIG:
        shp = w[k].shape
        two_d = lambda a: a.reshape(-1, shp[-1])
        d, nm, nv = _adamw("adamw_" + k, two_d(w[k]), two_d(g_out[k]), two_d(mom[k]), two_d(var[k]))
        delta[k], new_m[k], new_v[k] = d.reshape(shp), nm.reshape(shp), nv.reshape(shp)
    rest = [k for k in WEIGHTS if k not in BIG]
    rest_shapes = [w[k].shape for k in rest]
    packed = [_pack([src[k] for k in rest], LANE, f32) for src in (w, g_out, mom, var)]
    d, nm, nv = _adamw("adamw_small", *packed)
    for dst, arr in ((delta, d), (new_m, nm), (new_v, nv)):
        for k, val in zip(rest, _unpack(arr.reshape(-1), rest_shapes)):
            dst[k] = val

    grad_x = dact.reshape(x.shape)
    return (loss, grad_x, *[g_out[k] for k in WEIGHTS], *[delta[k] for k in WEIGHTS],
            *[new_m[k] for k in WEIGHTS], *[new_v[k] for k in WEIGHTS])
```

```python
import functools

import jax
import jax.numpy as jnp
from jax import lax
from jax.experimental import pallas as pl
from jax.experimental.pallas import tpu as pltpu

f32 = jnp.float32
bf16 = jnp.bfloat16

N_DEV = 8
DEPTH = 4
D_MODEL = 2048
GDN_HEADS = 8
HEAD_DIM = 128
GDN_WIDTH = 1024
S5_GROUPS = 48
S5_GROUP_SIZE = 16
S5_STATE = 64
S5_WIDTH = 768
S5_LANES = S5_GROUPS * S5_STATE
S5_BLK = 6
M2_HEADS = 16
M2_HEAD_DIM = 64
M2_WIDTH = 1024
M2_GROUPS = 4
M2_STATE = 128
CONV_K = 4
CHUNK = 64
HALO = 8
NORM_EPS = 1e-6
IN_DIM = 14880
LANE = 128
VMEM_LIMIT = 48 * 1024 * 1024

ADAM_LR = 0.001
ADAM_B1 = 0.9
ADAM_B2 = 0.999
ADAM_EPS = 1e-08
ADAM_WD = 0.01
ADAM_STEP = 10

U_MERGE, U_SU, U_SG, U_Q, U_K, U_V, U_AZ, U_CZ, U_CX, U_CB, U_CC, U_SMA, U_SMC = (
    0, 48, 54, 60, 68, 76, 84, 92, 100, 108, 112, 116, 117)
NP_UNITS = 120
NP_COLS = NP_UNITS * LANE


def _cparams(sem=None):
    return pltpu.CompilerParams(dimension_semantics=sem, vmem_limit_bytes=VMEM_LIMIT)


def _pick(dim, target):
    if dim <= target:
        return dim
    for t in range(target - target % LANE, 0, -LANE):
        if dim % t == 0:
            return t
    raise ValueError(f"no tile for {dim}")


def _bd(a, b, dims):
    return lax.dot_general(a.astype(bf16), b.astype(bf16), (dims, ((), ())), preferred_element_type=f32)


@jax.custom_vjp
def dot_nn(a, b):
    return _bd(a, b, ((1,), (0,)))


@jax.custom_vjp
def dot_nt(a, b):
    return _bd(a, b, ((1,), (1,)))


@jax.custom_vjp
def dot_tn(a, b):
    return _bd(a, b, ((0,), (0,)))


dot_nn.defvjp(lambda a, b: (dot_nn(a, b), (a, b)), lambda r, ct: (dot_nt(ct, r[1]), dot_tn(r[0], ct)))
dot_nt.defvjp(lambda a, b: (dot_nt(a, b), (a, b)), lambda r, ct: (dot_nn(ct, r[1]), dot_tn(ct, r[0])))
dot_tn.defvjp(lambda a, b: (dot_tn(a, b), (a, b)), lambda r, ct: (dot_nt(r[1], ct), dot_nn(r[0], ct)))

_HI = lax.Precision.HIGHEST


def _silu(x):
    return x * jax.nn.sigmoid(x)


def _softplus(x):
    return jnp.maximum(x, 0.0) + jnp.log(1.0 + jnp.exp(-jnp.abs(x)))


def _tri_masks(c):
    row = lax.broadcasted_iota(jnp.int32, (c, c), 0)
    col = lax.broadcasted_iota(jnp.int32, (c, c), 1)
    return row >= col, row > col, (row >= col).astype(f32), (row == col).astype(f32)


def _cumsum_all(a, ltri):
    cum = jnp.dot(ltri, a, precision=_HI, preferred_element_type=f32)
    cum_t = lax.dot_general(a, ltri, (((0,), (1,)), ((), ())), precision=_HI, preferred_element_type=f32)
    return cum, cum_t


def _onehot_lane(idx):
    return (lax.broadcasted_iota(jnp.int32, (1, LANE), 1) == idx).astype(f32)


def _onehot_sub(idx):
    return (lax.broadcasted_iota(jnp.int32, (LANE, 1), 0) == idx).astype(f32)


def _pick_col(x, idx):
    return jnp.sum(x * _onehot_lane(idx), axis=-1, keepdims=True)


def _pick_row(xt, idx):
    return jnp.sum(xt * _onehot_sub(idx), axis=0, keepdims=True)


def _mm(name, a, b, mode, residual=None, out_dtype=f32, tm=1024, tn=1024, tk=512):
    if mode == "nn":
        (m, k), (_, n) = a.shape, b.shape
    elif mode == "nt":
        (m, k), (n, _) = a.shape, b.shape
    else:
        (k, m), (_, n) = a.shape, b.shape
    tm, tn, tk = _pick(m, tm), _pick(n, tn), _pick(k, tk)
    nk = k // tk
    dims = {"nn": ((1,), (0,)), "nt": ((1,), (1,)), "tn": ((0,), (0,))}[mode]
    has_res = residual is not None

    def body(*refs):
        if has_res:
            a_ref, b_ref, r_ref, o_ref, acc_ref = refs
        else:
            a_ref, b_ref, o_ref, acc_ref = refs
        kk = pl.program_id(2)

        @pl.when(kk == 0)
        def _():
            acc_ref[...] = jnp.zeros_like(acc_ref)

        acc_ref[...] += _bd(a_ref[...], b_ref[...], dims)

        @pl.when(kk == nk - 1)
        def _():
            out = acc_ref[...]
            if has_res:
                out = out + r_ref[...].astype(f32)
            o_ref[...] = out.astype(o_ref.dtype)

    if mode == "tn":
        a_spec = pl.BlockSpec((tk, tm), lambda i, j, kk: (kk, i))
    else:
        a_spec = pl.BlockSpec((tm, tk), lambda i, j, kk: (i, kk))
    if mode == "nt":
        b_spec = pl.BlockSpec((tn, tk), lambda i, j, kk: (j, kk))
    else:
        b_spec = pl.BlockSpec((tk, tn), lambda i, j, kk: (kk, j))
    o_spec = pl.BlockSpec((tm, tn), lambda i, j, kk: (i, j))
    in_specs = [a_spec, b_spec] + ([o_spec] if has_res else [])
    args = (a, b) + ((residual,) if has_res else ())
    return pl.pallas_call(
        body, name=name, grid=(m // tm, n // tn, nk), in_specs=in_specs, out_specs=o_spec,
        out_shape=jax.ShapeDtypeStruct((m, n), out_dtype), scratch_shapes=[pltpu.VMEM((tm, tn), f32)],
        compiler_params=_cparams(("parallel", "parallel", "arbitrary")))(*args)


def _act_spec(t, width, colblk):
    return pl.BlockSpec((t, width), lambda i: (i, colblk))


def _tok_fwd(name, fn, acts, params, outs, t):
    rows = acts[0][0].shape[0]
    t = min(t, rows)
    na, npar = len(acts), len(params)

    def body(*refs):
        a = [r[...].astype(f32) for r in refs[:na]]
        p = [r[...].astype(f32) for r in refs[na:na + npar]]
        res = fn(*a, *p)
        for o_ref, o in zip(refs[na + npar:], res):
            o_ref[...] = o.astype(o_ref.dtype)

    in_specs = [_act_spec(t, w, cb) for (_, w, cb) in acts]
    in_specs += [pl.BlockSpec(p.shape, lambda i: (0, 0)) for p in params]
    res = pl.pallas_call(
        body, name=name, grid=(rows // t,), in_specs=in_specs,
        out_specs=[_act_spec(t, w, 0) for (w, _) in outs],
        out_shape=[jax.ShapeDtypeStruct((rows, w), dt) for (w, dt) in outs],
        compiler_params=_cparams(("arbitrary",)))(*[a for (a, _, _) in acts], *params)
    return res


def _tok_bwd(name, fn, acts, params, cts, dact, t, residuals=None):
    rows = acts[0][0].shape[0]
    t = min(t, rows)
    residuals = residuals or {}
    res_ids = sorted(residuals)
    na, npar, nc, nr, nd = len(acts), len(params), len(cts), len(res_ids), len(dact)

    def body(*refs):
        a = [r[...].astype(f32) for r in refs[:na]]
        p = [r[...].astype(f32) for r in refs[na:na + npar]]
        ct = tuple(r[...].astype(f32) for r in refs[na + npar:na + npar + nc])
        rs = {idx: r[...].astype(f32) for idx, r in zip(res_ids, refs[na + npar + nc:na + npar + nc + nr])}
        orefs = refs[na + npar + nc + nr:]
        _, vjp = jax.vjp(fn, *a, *p)
        grads = vjp(ct)
        for o_ref, (idx, _) in zip(orefs[:nd], dact):
            g = grads[idx]
            if idx in rs:
                g = g + rs[idx]
            o_ref[...] = g.astype(o_ref.dtype)

        if npar:
            @pl.when(pl.program_id(0) == 0)
            def _():
                for o_ref in orefs[nd:]:
                    o_ref[...] = jnp.zeros_like(o_ref)

            for o_ref, g in zip(orefs[nd:], grads[na:]):
                o_ref[...] += g

    in_specs = [_act_spec(t, w, cb) for (_, w, cb) in acts]
    in_specs += [pl.BlockSpec(p.shape, lambda i: (0, 0)) for p in params]
    in_specs += [_act_spec(t, w, cb) for (_, w, cb) in cts]
    in_specs += [_act_spec(t, acts[idx][1], 0) for idx in res_ids]
    out_specs = [_act_spec(t, acts[idx][1], 0) for (idx, _) in dact]
    out_specs += [pl.BlockSpec(p.shape, lambda i: (0, 0)) for p in params]
    out_shape = [jax.ShapeDtypeStruct((rows, acts[idx][1]), dt) for (idx, dt) in dact]
    out_shape += [jax.ShapeDtypeStruct(p.shape, f32) for p in params]
    return pl.pallas_call(
        body, name=name, grid=(rows // t,), in_specs=in_specs, out_specs=out_specs, out_shape=out_shape,
        compiler_params=_cparams(("arbitrary",)))(
            *[a for (a, _, _) in acts], *params, *[c for (c, _, _) in cts], *[residuals[i] for i in res_ids])


def _rms_fn(x, w):
    return (x * lax.rsqrt(jnp.mean(x * x, axis=-1, keepdims=True) + NORM_EPS) * w,)


def _gate_fn(la, lb, lc, pa, pb, pc):
    return (jax.nn.sigmoid(la) * pa + jax.nn.sigmoid(lb) * pb + jax.nn.sigmoid(lc) * pc,)


def _s5_tail_fn(ypre, gate, glu_w, glu_b):
    y = jax.nn.gelu(ypre)
    y = y * jax.nn.sigmoid(dot_nn(y, glu_w) + glu_b)
    return (y * _silu(gate),)


def _loss_grad(x, w, target, t=256):
    rows, d = x.shape
    t = min(t, rows)

    def fn(xt, wt, tt):
        y = _rms_fn(xt, wt)[0]
        err = y - tt
        return 0.5 * jnp.sum(jnp.sum(err * err, axis=-1, keepdims=True), axis=0, keepdims=True) / d

    def body(x_ref, w_ref, t_ref, loss_ref, dx_ref, dw_ref):
        tt = t_ref[...]
        val, vjp = jax.vjp(lambda a, b: fn(a, b, tt), x_ref[...], w_ref[...])
        dx, dw = vjp(jnp.ones((1, 1), f32))
        dx_ref[...] = dx

        @pl.when(pl.program_id(0) == 0)
        def _():
            loss_ref[...] = jnp.zeros_like(loss_ref)
            dw_ref[...] = jnp.zeros_like(dw_ref)

        loss_ref[...] += val * jnp.ones((1, LANE), f32)
        dw_ref[...] += dw

    return pl.pallas_call(
        body, name="loss_grad", grid=(rows // t,),
        in_specs=[_act_spec(t, d, 0), pl.BlockSpec((1, d), lambda i: (0, 0)), _act_spec(t, d, 0)],
        out_specs=[pl.BlockSpec((1, LANE), lambda i: (0, 0)), _act_spec(t, d, 0), pl.BlockSpec((1, d), lambda i: (0, 0))],
        out_shape=[jax.ShapeDtypeStruct((1, LANE), f32), jax.ShapeDtypeStruct((rows, d), f32),
                   jax.ShapeDtypeStruct((1, d), f32)],
        compiler_params=_cparams(("arbitrary",)))(x, w, target)


def _tri_inv(a, eye, c):
    n = -a
    p = eye + n
    npow = n
    steps = c.bit_length() - 2
    for _ in range(steps):
        npow = dot_nn(npow, npow)
        p = p + dot_nn(p, npow)
    return p


GDN_HB = 4
GDN_SW = GDN_HB * HEAD_DIM
GDN_STEPS = GDN_HEADS // GDN_HB


def _gdn_step(qc, kc, vc, z, small, alog_row, dtb_row, normw, *states, head0):
    c = qc.shape[0]
    causal, strict, ltri, eye = _tri_masks(c)
    beta_all = jax.nn.sigmoid(small)
    g_all = -jnp.exp(alog_row) * _softplus(small + dtb_row)
    gc_all, gct_all = _cumsum_all(g_all, ltri)
    outs, new_states = [], []
    for r, state in enumerate(states):
        head = head0 + r
        sl = slice(r * HEAD_DIM, (r + 1) * HEAD_DIM)
        q = _silu(qc[:, sl])
        k = _silu(kc[:, sl])
        v = _silu(vc[:, sl])
        q = q * lax.rsqrt(jnp.sum(q * q, axis=-1, keepdims=True) + NORM_EPS) * (HEAD_DIM ** -0.5)
        k = k * lax.rsqrt(jnp.sum(k * k, axis=-1, keepdims=True) + NORM_EPS)
        beta = _pick_col(beta_all, head)
        gc = _pick_col(gc_all, head + GDN_HEADS)
        gc_t = _pick_row(gct_all, head + GDN_HEADS)
        decay = jnp.exp(jnp.where(causal, gc - gc_t, -1e30))
        egc = jnp.exp(gc)
        kb = k * beta
        a_mat = jnp.where(strict, dot_nt(kb, k) * decay, 0.0)
        t_inv = _tri_inv(a_mat, eye, c)
        u = dot_nn(t_inv, v * beta)
        w = dot_nn(t_inv, kb * egc)
        qk = dot_nt(q, k) * decay
        v_new = u - dot_nn(w, state)
        out = dot_nn(q * egc, state) + dot_nn(qk, v_new)
        g_last = gc[c - 1:c, :]
        k_tail = k * jnp.exp(g_last - gc)
        new_states.append(state * jnp.exp(g_last) + dot_tn(k_tail, v_new))
        outs.append(out * lax.rsqrt(jnp.mean(out * out, axis=-1, keepdims=True) + NORM_EPS) * normw * _silu(z[:, sl]))
    return (jnp.concatenate(outs, axis=-1), *new_states)


def _conv_windows(xin_ref, p, cw_ref, c):
    acc = None
    for k in range(CONV_K):
        term = cw_ref[pl.ds(k, 1), :] * xin_ref[p, pl.ds(HALO - CONV_K + 1 + k, c), :]
        acc = term if acc is None else acc + term
    return acc


def _gdn_fwd(proj, conv_w, alog_row, dtb_row, normw):
    rows = proj.shape[0]
    c = min(CHUNK, rows)
    n = rows // c

    def body(q_ref, k_ref, v_ref, z_ref, sm_ref, cwq, cwk, cwv, al_ref, dt_ref, nw_ref, y_ref, ck_ref, s_ref, xin_ref):
        hb = pl.program_id(0)
        i = pl.program_id(1)

        @pl.when(i == 0)
        def _():
            s_ref[...] = jnp.zeros_like(s_ref)
            xin_ref[:, 0:HALO, :] = jnp.zeros((3, HALO, GDN_SW), f32)

        @pl.when(i > 0)
        def _():
            xin_ref[:, 0:HALO, :] = xin_ref[:, c:c + HALO, :]

        xin_ref[0, HALO:, :] = q_ref[...]
        xin_ref[1, HALO:, :] = k_ref[...]
        xin_ref[2, HALO:, :] = v_ref[...]
        qc = _conv_windows(xin_ref, 0, cwq, c)
        kc = _conv_windows(xin_ref, 1, cwk, c)
        vc = _conv_windows(xin_ref, 2, cwv, c)
        states = [s_ref[r] for r in range(GDN_HB)]
        ck_ref[...] = s_ref[...]
        res = _gdn_step(qc, kc, vc, z_ref[...], sm_ref[...], al_ref[...], dt_ref[...], nw_ref[...], *states,
                        head0=hb * GDN_HB)
        y_ref[...] = res[0].astype(y_ref.dtype)
        for r in range(GDN_HB):
            s_ref[r] = res[1 + r]

    def blk(unit):
        return pl.BlockSpec((c, GDN_SW), lambda hb, i: (i, unit // GDN_HB + hb))

    def cw(part):
        return pl.BlockSpec((CONV_K, GDN_SW), lambda hb, i: (0, part * GDN_STEPS + hb))

    row = pl.BlockSpec((1, LANE), lambda hb, i: (0, 0))
    return pl.pallas_call(
        body, name="gdn_fwd", grid=(GDN_STEPS, n),
        in_specs=[blk(U_Q), blk(U_K), blk(U_V), blk(U_AZ), pl.BlockSpec((c, LANE), lambda hb, i: (i, U_SMA)),
                  cw(0), cw(1), cw(2), row, row, row],
        out_specs=[pl.BlockSpec((c, GDN_SW), lambda hb, i: (i, hb)),
                   pl.BlockSpec((GDN_HB, None, HEAD_DIM, HEAD_DIM), lambda hb, i: (hb, i, 0, 0))],
        out_shape=[jax.ShapeDtypeStruct((rows, GDN_WIDTH), bf16),
                   jax.ShapeDtypeStruct((GDN_HEADS, n, HEAD_DIM, HEAD_DIM), f32)],
        scratch_shapes=[pltpu.VMEM((GDN_HB, HEAD_DIM, HEAD_DIM), f32), pltpu.VMEM((3, c + HALO, GDN_SW), f32)],
        compiler_params=_cparams(("arbitrary", "arbitrary")))(
            proj, proj, proj, proj, proj, conv_w, conv_w, conv_w, alog_row, dtb_row, normw)


def _conv_bwd(xin_ref, dyext_ref, p, cw_ref, dxc, dx_ref, dcw_ref, c):
    dyext_ref[p, 0:c, :] = dxc
    acc = None
    for k in range(CONV_K):
        term = cw_ref[pl.ds(k, 1), :] * dyext_ref[p, pl.ds(CONV_K - 1 - k, c), :]
        acc = term if acc is None else acc + term
        dcw_ref[pl.ds(k, 1), :] += jnp.sum(xin_ref[p, pl.ds(HALO - CONV_K + 1 + k, c), :] * dxc, axis=0, keepdims=True)
    dx_ref[...] = acc


def _gdn_bwd(proj, dy, ck, conv_w, alog_row, dtb_row, normw):
    rows = proj.shape[0]
    c = min(CHUNK, rows)
    n = rows // c
    halo_blocks = c // HALO

    def body(q_ref, k_ref, v_ref, hq_ref, hk_ref, hv_ref, z_ref, sm_ref, cwq, cwk, cwv, al_ref, dt_ref, nw_ref,
             ck_ref, dy_ref, dq_ref, dk_ref, dv_ref, dz_ref, dsm_ref, dcwq, dcwk, dcwv, dal_ref, ddt_ref, dnw_ref,
             ds_ref, xin_ref, dyext_ref):
        hb = pl.program_id(0)
        i = pl.program_id(1)
        ci = n - 1 - i

        @pl.when(i == 0)
        def _():
            ds_ref[...] = jnp.zeros_like(ds_ref)
            dyext_ref[:, c:c + HALO, :] = jnp.zeros((3, HALO, GDN_SW), f32)
            for r in (dcwq, dcwk, dcwv, dal_ref, ddt_ref, dnw_ref):
                r[...] = jnp.zeros_like(r)

        @pl.when(i > 0)
        def _():
            dyext_ref[:, c:c + HALO, :] = dyext_ref[:, 0:HALO, :]

        first = (ci > 0).astype(f32)
        for p, (x_ref, halo_ref) in enumerate(((q_ref, hq_ref), (k_ref, hk_ref), (v_ref, hv_ref))):
            xin_ref[p, 0:HALO, :] = halo_ref[...] * first
            xin_ref[p, HALO:, :] = x_ref[...]
        qc = _conv_windows(xin_ref, 0, cwq, c)
        kc = _conv_windows(xin_ref, 1, cwk, c)
        vc = _conv_windows(xin_ref, 2, cwv, c)
        fn = functools.partial(_gdn_step, head0=hb * GDN_HB)
        states = [ck_ref[r] for r in range(GDN_HB)]
        _, vjp = jax.vjp(fn, qc, kc, vc, z_ref[...], sm_ref[...], al_ref[...], dt_ref[...], nw_ref[...], *states)
        cts = (dy_ref[...].astype(f32), *[ds_ref[r] for r in range(GDN_HB)])
        dqc, dkc, dvc, dz, dsm, dal, ddt, dnw, *dstates = vjp(cts)
        for r in range(GDN_HB):
            ds_ref[r] = dstates[r]
        dz_ref[...] = dz
        dsm_ref[...] = dsm
        dal_ref[...] += dal
        ddt_ref[...] += ddt
        dnw_ref[...] += dnw
        _conv_bwd(xin_ref, dyext_ref, 0, cwq, dqc, dq_ref, dcwq, c)
        _conv_bwd(xin_ref, dyext_ref, 1, cwk, dkc, dk_ref, dcwk, c)
        _conv_bwd(xin_ref, dyext_ref, 2, cwv, dvc, dv_ref, dcwv, c)

    def blk(unit):
        return pl.BlockSpec((c, GDN_SW), lambda hb, i: (n - 1 - i, unit // GDN_HB + hb))

    def halo(unit):
        return pl.BlockSpec((HALO, GDN_SW),
                            lambda hb, i: (jnp.maximum((n - 1 - i) * halo_blocks - 1, 0), unit // GDN_HB + hb))

    def cw(part):
        return pl.BlockSpec((CONV_K, GDN_SW), lambda hb, i: (0, part * GDN_STEPS + hb))

    row = pl.BlockSpec((1, LANE), lambda hb, i: (0, 0))
    hrow = pl.BlockSpec((None, 1, LANE), lambda hb, i: (hb, 0, 0))
    out_blk = pl.BlockSpec((c, GDN_SW), lambda hb, i: (n - 1 - i, hb))
    dcw = pl.BlockSpec((CONV_K, GDN_SW), lambda hb, i: (0, hb))
    wide = jax.ShapeDtypeStruct((rows, GDN_WIDTH), f32)
    hrow_shape = jax.ShapeDtypeStruct((GDN_STEPS, 1, LANE), f32)
    dcw_shape = jax.ShapeDtypeStruct((CONV_K, GDN_WIDTH), f32)
    return pl.pallas_call(
        body, name="gdn_bwd", grid=(GDN_STEPS, n),
        in_specs=[blk(U_Q), blk(U_K), blk(U_V), halo(U_Q), halo(U_K), halo(U_V), blk(U_AZ),
                  pl.BlockSpec((c, LANE), lambda hb, i: (n - 1 - i, U_SMA)),
                  cw(0), cw(1), cw(2), row, row, row,
                  pl.BlockSpec((GDN_HB, None, HEAD_DIM, HEAD_DIM), lambda hb, i: (hb, n - 1 - i, 0, 0)),
                  out_blk],
        out_specs=[out_blk, out_blk, out_blk, out_blk,
                   pl.BlockSpec((None, c, LANE), lambda hb, i: (hb, n - 1 - i, 0)),
                   dcw, dcw, dcw, hrow, hrow, hrow],
        out_shape=[wide, wide, wide, wide, jax.ShapeDtypeStruct((GDN_STEPS, rows, LANE), f32),
                   dcw_shape, dcw_shape, dcw_shape, hrow_shape, hrow_shape, hrow_shape],
        scratch_shapes=[pltpu.VMEM((GDN_HB, HEAD_DIM, HEAD_DIM), f32), pltpu.VMEM((3, c + HALO, GDN_SW), f32),
                        pltpu.VMEM((3, c + HALO, GDN_SW), f32)],
        compiler_params=_cparams(("arbitrary", "arbitrary")))(
            proj, proj, proj, proj, proj, proj, proj, proj, conv_w, conv_w, conv_w, alog_row, dtb_row, normw, ck, dy)


M2_REP = M2_HEADS // M2_GROUPS
M2_GW = M2_REP * M2_HEAD_DIM
M2_GB = 2
M2_STEPS = M2_GROUPS // M2_GB
M2_XW = M2_GB * M2_GW
M2_BW = M2_GB * M2_STATE
M2_SH = M2_GB * M2_REP


def _ssd_step(xc, bc, cc, z, small, bias_x, bias_b, bias_c, alog_row, dtb_row, d_row, normw, *states, grp0):
    c = xc.shape[0]
    causal, _, ltri, _ = _tri_masks(c)
    xs = _silu(xc + bias_x)
    bms = _silu(bc + bias_b)
    cms = _silu(cc + bias_c)
    dt_all = _softplus(small + dtb_row)
    a_all = -jnp.exp(alog_row) * dt_all
    ac_all, act_all = _cumsum_all(a_all, ltri)
    ys, new_states = [], []
    for gi in range(M2_GB):
        bm = bms[:, gi * M2_STATE:(gi + 1) * M2_STATE]
        cm = cms[:, gi * M2_STATE:(gi + 1) * M2_STATE]
        scores = dot_nt(cm, bm)
        yg = []
        for r in range(M2_REP):
            state = states[gi * M2_REP + r]
            head = (grp0 + gi) * M2_REP + r
            dt = _pick_col(dt_all, head)
            ac = _pick_col(ac_all, head)
            ac_t = _pick_row(act_all, head)
            dsk = _pick_col(d_row, head)
            lo = gi * M2_GW + r * M2_HEAD_DIM
            xh = xs[:, lo:lo + M2_HEAD_DIM]
            seg = jnp.exp(jnp.where(causal, ac - ac_t, -1e30))
            xdt = xh * dt
            y_diag = dot_nn(scores * seg, xdt)
            a_last = ac[c - 1:c, :]
            states_new = dot_tn(xdt * jnp.exp(a_last - ac), bm)
            y_off = dot_nt(cm, state) * jnp.exp(ac)
            new_states.append(state * jnp.exp(a_last) + states_new)
            yg.append(y_diag + y_off + dsk * xh)
        y = jnp.concatenate(yg, axis=-1) * _silu(z[:, gi * M2_GW:(gi + 1) * M2_GW])
        ys.append(y * lax.rsqrt(jnp.mean(y * y, axis=-1, keepdims=True) + NORM_EPS)
                  * normw[:, gi * M2_GW:(gi + 1) * M2_GW])
    return (jnp.concatenate(ys, axis=-1), *new_states)


def _conv_windows2(xin_ref, cw_ref, c):
    acc = None
    for k in range(CONV_K):
        term = cw_ref[pl.ds(k, 1), :] * xin_ref[pl.ds(HALO - CONV_K + 1 + k, c), :]
        acc = term if acc is None else acc + term
    return acc


def _ssd_specs(n, c, rev):
    def ci(i):
        return (n - 1 - i) if rev else i

    def blk(width, unit):
        return pl.BlockSpec((c, width), lambda g, i: (ci(i), unit * LANE // width + g))

    def par(rows_, width, col0):
        return pl.BlockSpec((rows_, width), lambda g, i: (0, col0 // width + g))

    return ci, blk, par


def _ssd_fwd(proj, conv_w, conv_b, alog_row, dtb_row, d_row, normw):
    rows = proj.shape[0]
    c = min(CHUNK, rows)
    n = rows // c
    _, blk, par = _ssd_specs(n, c, False)

    def body(x_ref, b_ref, c_ref, z_ref, sm_ref, cwx, cwb, cwc, bx, bb, bcc, al_ref, dt_ref, d_ref, nw_ref,
             y_ref, ck_ref, s_ref, xx_ref, xb_ref, xc_ref):
        g = pl.program_id(0)
        i = pl.program_id(1)

        @pl.when(i == 0)
        def _():
            s_ref[...] = jnp.zeros_like(s_ref)
            for r in (xx_ref, xb_ref, xc_ref):
                r[0:HALO, :] = jnp.zeros((HALO, r.shape[1]), f32)

        @pl.when(i > 0)
        def _():
            for r in (xx_ref, xb_ref, xc_ref):
                r[0:HALO, :] = r[c:c + HALO, :]

        xx_ref[HALO:, :] = x_ref[...]
        xb_ref[HALO:, :] = b_ref[...]
        xc_ref[HALO:, :] = c_ref[...]
        xc = _conv_windows2(xx_ref, cwx, c)
        bc = _conv_windows2(xb_ref, cwb, c)
        cc = _conv_windows2(xc_ref, cwc, c)
        st = [s_ref[r] for r in range(M2_SH)]
        ck_ref[...] = s_ref[...]
        res = _ssd_step(xc, bc, cc, z_ref[...], sm_ref[...], bx[...], bb[...], bcc[...], al_ref[...], dt_ref[...],
                        d_ref[...], nw_ref[...], *st, grp0=g * M2_GB)
        y_ref[...] = res[0].astype(y_ref.dtype)
        for r in range(M2_SH):
            s_ref[r] = res[1 + r]

    row = pl.BlockSpec((1, LANE), lambda g, i: (0, 0))
    off_b, off_c = M2_WIDTH, M2_WIDTH + M2_GROUPS * M2_STATE
    return pl.pallas_call(
        body, name="ssd_fwd", grid=(M2_STEPS, n),
        in_specs=[blk(M2_XW, U_CX), blk(M2_BW, U_CB), blk(M2_BW, U_CC), blk(M2_XW, U_CZ),
                  pl.BlockSpec((c, LANE), lambda g, i: (i, U_SMC)),
                  par(CONV_K, M2_XW, 0), par(CONV_K, M2_BW, off_b), par(CONV_K, M2_BW, off_c),
                  par(1, M2_XW, 0), par(1, M2_BW, off_b), par(1, M2_BW, off_c), row, row, row, par(1, M2_XW, 0)],
        out_specs=[pl.BlockSpec((c, M2_XW), lambda g, i: (i, g)),
                   pl.BlockSpec((None, None, M2_SH, M2_HEAD_DIM, M2_STATE), lambda g, i: (g, i, 0, 0, 0))],
        out_shape=[jax.ShapeDtypeStruct((rows, M2_WIDTH), bf16),
                   jax.ShapeDtypeStruct((M2_STEPS, n, M2_SH, M2_HEAD_DIM, M2_STATE), f32)],
        scratch_shapes=[pltpu.VMEM((M2_SH, M2_HEAD_DIM, M2_STATE), f32), pltpu.VMEM((c + HALO, M2_XW), f32),
                        pltpu.VMEM((c + HALO, M2_BW), f32), pltpu.VMEM((c + HALO, M2_BW), f32)],
        compiler_params=_cparams(("arbitrary", "arbitrary")))(
            proj, proj, proj, proj, proj, conv_w, conv_w, conv_w, conv_b, conv_b, conv_b, alog_row, dtb_row, d_row, normw)


def _conv_bwd2(xin_ref, dyext_ref, cw_ref, dxc, dx_ref, dcw_ref, c):
    dyext_ref[0:c, :] = dxc
    acc = None
    for k in range(CONV_K):
        term = cw_ref[pl.ds(k, 1), :] * dyext_ref[pl.ds(CONV_K - 1 - k, c), :]
        acc = term if acc is None else acc + term
        dcw_ref[pl.ds(k, 1), :] += jnp.sum(xin_ref[pl.ds(HALO - CONV_K + 1 + k, c), :] * dxc, axis=0, keepdims=True)
    dx_ref[...] = acc


def _ssd_bwd(proj, dy, ck, conv_w, conv_b, alog_row, dtb_row, d_row, normw):
    rows = proj.shape[0]
    c = min(CHUNK, rows)
    n = rows // c
    halo_blocks = c // HALO
    _, blk, par = _ssd_specs(n, c, True)

    def body(x_ref, b_ref, c_ref, hx_ref, hb_ref, hc_ref, z_ref, sm_ref, cwx, cwb, cwc, bx, bb, bcc,
             al_ref, dt_ref, d_ref, nw_ref, ck_ref, dy_ref,
             dx_ref, db_ref, dc_ref, dz_ref, dsm_ref, dcwx, dcwb, dcwc, dbx, dbb, dbc, dal_ref, ddt_ref, dd_ref, dnw_ref,
             ds_ref, xx_ref, xb_ref, xc_ref, ex_ref, eb_ref, ec_ref):
        g = pl.program_id(0)
        i = pl.program_id(1)
        ci = n - 1 - i

        @pl.when(i == 0)
        def _():
            ds_ref[...] = jnp.zeros_like(ds_ref)
            for r in (ex_ref, eb_ref, ec_ref):
                r[c:c + HALO, :] = jnp.zeros((HALO, r.shape[1]), f32)
            for r in (dcwx, dcwb, dcwc, dbx, dbb, dbc, dal_ref, ddt_ref, dd_ref, dnw_ref):
                r[...] = jnp.zeros_like(r)

        @pl.when(i > 0)
        def _():
            for r in (ex_ref, eb_ref, ec_ref):
                r[c:c + HALO, :] = r[0:HALO, :]

        first = (ci > 0).astype(f32)
        for xin, x_in, halo_in in ((xx_ref, x_ref, hx_ref), (xb_ref, b_ref, hb_ref), (xc_ref, c_ref, hc_ref)):
            xin[0:HALO, :] = halo_in[...] * first
            xin[HALO:, :] = x_in[...]
        xc = _conv_windows2(xx_ref, cwx, c)
        bc = _conv_windows2(xb_ref, cwb, c)
        cc = _conv_windows2(xc_ref, cwc, c)
        st = [ck_ref[r] for r in range(M2_SH)]
        fn = functools.partial(_ssd_step, grp0=g * M2_GB)
        _, vjp = jax.vjp(fn, xc, bc, cc, z_ref[...], sm_ref[...], bx[...], bb[...], bcc[...], al_ref[...], dt_ref[...],
                         d_ref[...], nw_ref[...], *st)
        cts = (dy_ref[...].astype(f32), *[ds_ref[r] for r in range(M2_SH)])
        (dxc, dbc_, dcc, dz, dsm, gbx, gbb, gbc, dal, ddt, dd, dnw, *dst) = vjp(cts)
        for r in range(M2_SH):
            ds_ref[r] = dst[r]
        dz_ref[...] = dz
        dsm_ref[...] = dsm
        dbx[...] += gbx
        dbb[...] += gbb
        dbc[...] += gbc
        dal_ref[...] += dal
        ddt_ref[...] += ddt
        dd_ref[...] += dd
        dnw_ref[...] += dnw
        _conv_bwd2(xx_ref, ex_ref, cwx, dxc, dx_ref, dcwx, c)
        _conv_bwd2(xb_ref, eb_ref, cwb, dbc_, db_ref, dcwb, c)
        _conv_bwd2(xc_ref, ec_ref, cwc, dcc, dc_ref, dcwc, c)

    def halo(width, unit):
        return pl.BlockSpec((HALO, width),
                            lambda g, i: (jnp.maximum((n - 1 - i) * halo_blocks - 1, 0), unit * LANE // width + g))

    row = pl.BlockSpec((1, LANE), lambda g, i: (0, 0))
    grow = pl.BlockSpec((None, 1, LANE), lambda g, i: (g, 0, 0))
    grow_shape = jax.ShapeDtypeStruct((M2_STEPS, 1, LANE), f32)
    ob_w = pl.BlockSpec((c, M2_XW), lambda g, i: (n - 1 - i, g))
    ob_n = pl.BlockSpec((c, M2_BW), lambda g, i: (n - 1 - i, g))
    off_b, off_c = M2_WIDTH, M2_WIDTH + M2_GROUPS * M2_STATE
    bc_w = M2_GROUPS * M2_STATE
    return pl.pallas_call(
        body, name="ssd_bwd", grid=(M2_STEPS, n),
        in_specs=[blk(M2_XW, U_CX), blk(M2_BW, U_CB), blk(M2_BW, U_CC),
                  halo(M2_XW, U_CX), halo(M2_BW, U_CB), halo(M2_BW, U_CC), blk(M2_XW, U_CZ),
                  pl.BlockSpec((c, LANE), lambda g, i: (n - 1 - i, U_SMC)),
                  par(CONV_K, M2_XW, 0), par(CONV_K, M2_BW, off_b), par(CONV_K, M2_BW, off_c),
                  par(1, M2_XW, 0), par(1, M2_BW, off_b), par(1, M2_BW, off_c), row, row, row, par(1, M2_XW, 0),
                  pl.BlockSpec((None, None, M2_SH, M2_HEAD_DIM, M2_STATE), lambda g, i: (g, n - 1 - i, 0, 0, 0)),
                  ob_w],
        out_specs=[ob_w, ob_n, ob_n, ob_w, pl.BlockSpec((None, c, LANE), lambda g, i: (g, n - 1 - i, 0)),
                   par(CONV_K, M2_XW, 0), par(CONV_K, M2_BW, 0), par(CONV_K, M2_BW, 0),
                   par(1, M2_XW, 0), par(1, M2_BW, 0), par(1, M2_BW, 0), grow, grow, grow, par(1, M2_XW, 0)],
        out_shape=[jax.ShapeDtypeStruct((rows, M2_WIDTH), f32), jax.ShapeDtypeStruct((rows, bc_w), f32),
                   jax.ShapeDtypeStruct((rows, bc_w), f32), jax.ShapeDtypeStruct((rows, M2_WIDTH), f32),
                   jax.ShapeDtypeStruct((M2_STEPS, rows, LANE), f32),
                   jax.ShapeDtypeStruct((CONV_K, M2_WIDTH), f32), jax.ShapeDtypeStruct((CONV_K, bc_w), f32),
                   jax.ShapeDtypeStruct((CONV_K, bc_w), f32),
                   jax.ShapeDtypeStruct((1, M2_WIDTH), f32), jax.ShapeDtypeStruct((1, bc_w), f32),
                   jax.ShapeDtypeStruct((1, bc_w), f32), grow_shape, grow_shape, grow_shape,
                   jax.ShapeDtypeStruct((1, M2_WIDTH), f32)],
        scratch_shapes=[pltpu.VMEM((M2_SH, M2_HEAD_DIM, M2_STATE), f32),
                        pltpu.VMEM((c + HALO, M2_XW), f32), pltpu.VMEM((c + HALO, M2_BW), f32), pltpu.VMEM((c + HALO, M2_BW), f32),
                        pltpu.VMEM((c + HALO, M2_XW), f32), pltpu.VMEM((c + HALO, M2_BW), f32), pltpu.VMEM((c + HALO, M2_BW), f32)],
        compiler_params=_cparams(("arbitrary", "arbitrary")))(
            proj, proj, proj, proj, proj, proj, proj, proj, conv_w, conv_w, conv_w, conv_b, conv_b, conv_b,
            alog_row, dtb_row, d_row, normw, ck, dy)


S5_TILE = 128
S5_SW = S5_LANES // S5_BLK


def _scan_down(br, bi, ar, ai):
    t = br.shape[0]
    row = lax.broadcasted_iota(jnp.int32, br.shape, 0)
    d = 1
    while d < t:
        keep = row >= d
        sr = jnp.where(keep, pltpu.roll(br, d, 0), 0.0)
        si = jnp.where(keep, pltpu.roll(bi, d, 0), 0.0)
        br, bi = br + ar * sr - ai * si, bi + ar * si + ai * sr
        ar, ai = ar * ar - ai * ai, 2.0 * ar * ai
        d *= 2
    return br, bi


def _scan_up(br, bi, ar, ai):
    t = br.shape[0]
    row = lax.broadcasted_iota(jnp.int32, br.shape, 0)
    d = 1
    while d < t:
        keep = row < t - d
        sr = jnp.where(keep, pltpu.roll(br, t - d, 0), 0.0)
        si = jnp.where(keep, pltpu.roll(bi, t - d, 0), 0.0)
        br, bi = br + ar * sr - ai * si, bi + ar * si + ai * sr
        ar, ai = ar * ar - ai * ai, 2.0 * ar * ai
        d *= 2
    return br, bi


def _s5_states(u_j, bbr, bbi, ar, ai, cr, ci_):
    br = dot_nn(u_j, bbr)
    bi = dot_nn(u_j, bbi)
    row0 = lax.broadcasted_iota(jnp.int32, br.shape, 0) == 0
    br = br + jnp.where(row0, ar * cr - ai * ci_, 0.0)
    bi = bi + jnp.where(row0, ar * ci_ + ai * cr, 0.0)
    return _scan_down(br, bi, ar, ai)


def _s5_fwd(proj, a_rows, bbr, bbi, ccr, cci, d_row):
    rows = proj.shape[0]
    t = min(S5_TILE, rows)
    n = rows // t

    def body(u_ref, a_ref, bbr_ref, bbi_ref, ccr_ref, cci_ref, d_ref, y_ref, ck_ref, carry_ref):
        i = pl.program_id(0)

        @pl.when(i == 0)
        def _():
            carry_ref[...] = jnp.zeros_like(carry_ref)

        ck_ref[...] = carry_ref[...]
        for j in range(S5_BLK):
            lanes = pl.ds(j * S5_SW, S5_SW)
            ch = pl.ds(j * LANE, LANE)
            u_j = u_ref[:, ch]
            sr, si = _s5_states(u_j, bbr_ref[j], bbi_ref[j], a_ref[0:1, lanes], a_ref[1:2, lanes],
                                carry_ref[0:1, lanes], carry_ref[1:2, lanes])
            y_ref[:, ch] = dot_nn(sr, ccr_ref[j]) - dot_nn(si, cci_ref[j]) + d_ref[:, ch] * u_j
            carry_ref[0:1, lanes] = sr[t - 1:t, :]
            carry_ref[1:2, lanes] = si[t - 1:t, :]

    whole3 = lambda s: pl.BlockSpec(s, lambda i: (0, 0, 0))
    return pl.pallas_call(
        body, name="s5_fwd", grid=(n,),
        in_specs=[pl.BlockSpec((t, S5_WIDTH), lambda i: (i, U_SU // S5_BLK)),
                  pl.BlockSpec((2, S5_LANES), lambda i: (0, 0)),
                  whole3(bbr.shape), whole3(bbi.shape), whole3(ccr.shape), whole3(cci.shape),
                  pl.BlockSpec((1, S5_WIDTH), lambda i: (0, 0))],
        out_specs=[pl.BlockSpec((t, S5_WIDTH), lambda i: (i, 0)),
                   pl.BlockSpec((None, 2, S5_LANES), lambda i: (i, 0, 0))],
        out_shape=[jax.ShapeDtypeStruct((rows, S5_WIDTH), f32), jax.ShapeDtypeStruct((n, 2, S5_LANES), f32)],
        scratch_shapes=[pltpu.VMEM((2, S5_LANES), f32)],
        compiler_params=_cparams(("arbitrary",)))(proj, a_rows, bbr, bbi, ccr, cci, d_row)


def _s5_bwd(proj, dy, ck, a_rows, bbr, bbi, ccr, cci, d_row):
    rows = proj.shape[0]
    t = min(S5_TILE, rows)
    n = rows // t

    def body(u_ref, dy_ref, ck_ref, a_ref, bbr_ref, bbi_ref, ccr_ref, cci_ref, d_ref,
             du_ref, da_ref, dbbr_ref, dbbi_ref, dccr_ref, dcci_ref, dd_ref, lam_ref):
        i = pl.program_id(0)

        @pl.when(i == 0)
        def _():
            lam_ref[...] = jnp.zeros_like(lam_ref)
            for r in (da_ref, dbbr_ref, dbbi_ref, dccr_ref, dcci_ref, dd_ref):
                r[...] = jnp.zeros_like(r)

        for j in range(S5_BLK):
            lanes = pl.ds(j * S5_SW, S5_SW)
            ch = pl.ds(j * LANE, LANE)
            u_j = u_ref[:, ch]
            dy_j = dy_ref[:, ch]
            ar, ai = a_ref[0:1, lanes], a_ref[1:2, lanes]
            cr, ci_ = ck_ref[0:1, lanes], ck_ref[1:2, lanes]
            sr, si = _s5_states(u_j, bbr_ref[j], bbi_ref[j], ar, ai, cr, ci_)
            gr = dot_nt(dy_j, ccr_ref[j])
            gi = -dot_nt(dy_j, cci_ref[j])
            last = lax.broadcasted_iota(jnp.int32, gr.shape, 0) == t - 1
            lr0, li0 = lam_ref[0:1, lanes], lam_ref[1:2, lanes]
            gr = gr + jnp.where(last, ar * lr0 + ai * li0, 0.0)
            gi = gi + jnp.where(last, ar * li0 - ai * lr0, 0.0)
            lr, li = _scan_up(gr, gi, ar, -ai)
            lam_ref[0:1, lanes] = lr[0:1, :]
            lam_ref[1:2, lanes] = li[0:1, :]
            du_ref[:, ch] = dot_nt(lr, bbr_ref[j]) + dot_nt(li, bbi_ref[j]) + d_ref[:, ch] * dy_j
            dbbr_ref[j] += dot_tn(u_j, lr)
            dbbi_ref[j] += dot_tn(u_j, li)
            dccr_ref[j] += dot_tn(sr, dy_j)
            dcci_ref[j] += -dot_tn(si, dy_j)
            dd_ref[:, ch] += jnp.sum(dy_j * u_j, axis=0, keepdims=True)
            row0 = lax.broadcasted_iota(jnp.int32, sr.shape, 0) == 0
            pr = jnp.where(row0, cr, pltpu.roll(sr, 1, 0))
            pi = jnp.where(row0, ci_, pltpu.roll(si, 1, 0))
            da_ref[0:1, lanes] += jnp.sum(lr * pr + li * pi, axis=0, keepdims=True)
            da_ref[1:2, lanes] += jnp.sum(li * pr - lr * pi, axis=0, keepdims=True)

    whole3 = lambda s: pl.BlockSpec(s, lambda i: (0, 0, 0))
    whole2 = lambda s: pl.BlockSpec(s, lambda i: (0, 0))
    return pl.pallas_call(
        body, name="s5_bwd", grid=(n,),
        in_specs=[pl.BlockSpec((t, S5_WIDTH), lambda i: (n - 1 - i, U_SU // S5_BLK)),
                  pl.BlockSpec((t, S5_WIDTH), lambda i: (n - 1 - i, 0)),
                  pl.BlockSpec((None, 2, S5_LANES), lambda i: (n - 1 - i, 0, 0)),
                  whole2((2, S5_LANES)), whole3(bbr.shape), whole3(bbi.shape), whole3(ccr.shape), whole3(cci.shape),
                  whole2((1, S5_WIDTH))],
        out_specs=[pl.BlockSpec((t, S5_WIDTH), lambda i: (n - 1 - i, 0)), whole2((2, S5_LANES)),
                   whole3(bbr.shape), whole3(bbi.shape), whole3(ccr.shape), whole3(cci.shape), whole2((1, S5_WIDTH))],
        out_shape=[jax.ShapeDtypeStruct((rows, S5_WIDTH), f32), jax.ShapeDtypeStruct((2, S5_LANES), f32),
                   jax.ShapeDtypeStruct(bbr.shape, f32), jax.ShapeDtypeStruct(bbi.shape, f32),
                   jax.ShapeDtypeStruct(ccr.shape, f32), jax.ShapeDtypeStruct(cci.shape, f32),
                   jax.ShapeDtypeStruct((1, S5_WIDTH), f32)],
        scratch_shapes=[pltpu.VMEM((2, S5_LANES), f32)],
        compiler_params=_cparams(("arbitrary",)))(proj, dy, ck, a_rows, bbr, bbi, ccr, cci, d_row)


def _s5_prep(lam_re, lam_im, log_step, b_re, b_im, c_re, c_im, d_skip):
    lam_re = jnp.minimum(lam_re, -1e-4)
    step = jnp.exp(log_step)[:, None]
    mag = jnp.exp(lam_re * step)
    ab_re = mag * jnp.cos(lam_im * step)
    ab_im = mag * jnp.sin(lam_im * step)
    den = lam_re * lam_re + lam_im * lam_im
    f_re = ((ab_re - 1.0) * lam_re + ab_im * lam_im) / den
    f_im = (ab_im * lam_re - (ab_re - 1.0) * lam_im) / den
    bb_re = f_re[..., None] * b_re - f_im[..., None] * b_im
    bb_im = f_re[..., None] * b_im + f_im[..., None] * b_re
    eye = jnp.eye(8, dtype=f32)

    def drive(bb):
        r = bb.reshape(S5_BLK, 8, S5_STATE, S5_GROUP_SIZE).transpose(0, 1, 3, 2)
        return (r[:, :, :, None, :] * eye[None, :, None, :, None]).reshape(S5_BLK, LANE, S5_SW)

    def readout(cc):
        r = cc.reshape(S5_BLK, 8, S5_GROUP_SIZE, S5_STATE).transpose(0, 1, 3, 2)
        return (r[:, :, :, None, :] * eye[None, :, None, :, None]).reshape(S5_BLK, S5_SW, LANE)

    a_rows = jnp.stack([ab_re.reshape(S5_LANES), ab_im.reshape(S5_LANES)])
    return a_rows, drive(bb_re), drive(bb_im), readout(c_re), readout(c_im), d_skip.reshape(1, S5_WIDTH)


def _adam_math(w, g, m, v):
    m = ADAM_B1 * m + (1.0 - ADAM_B1) * g
    v = ADAM_B2 * v + (1.0 - ADAM_B2) * (g * g)
    m_hat = m / (1.0 - ADAM_B1 ** ADAM_STEP)
    v_hat = v / (1.0 - ADAM_B2 ** ADAM_STEP)
    delta = -ADAM_LR * (m_hat / (jnp.sqrt(v_hat) + ADAM_EPS) + ADAM_WD * w)
    return delta, m, v


def _adamw(name, w, g, m, v):
    rows, width = w.shape
    t = rows
    for cand in (512, 256, 128, 64, 32, 16, 8):
        if rows % cand == 0 and cand * width * 4 * 7 * 2 <= VMEM_LIMIT // 2:
            t = cand
            break

    def body(w_ref, g_ref, m_ref, v_ref, d_ref, nm_ref, nv_ref):
        d, nm, nv = _adam_math(w_ref[...], g_ref[...], m_ref[...], v_ref[...])
        d_ref[...] = d
        nm_ref[...] = nm
        nv_ref[...] = nv

    spec = pl.BlockSpec((t, width), lambda i: (i, 0))
    shape = jax.ShapeDtypeStruct((rows, width), f32)
    return pl.pallas_call(body, name=name, grid=(rows // t,), in_specs=[spec] * 4, out_specs=[spec] * 3,
                          out_shape=[shape] * 3, compiler_params=_cparams(("parallel",)))(w, g, m, v)


def _adamw_slots(name, recv, row_off, w, m, v):
    depth, rows, width = w.shape
    per_row = width * (N_DEV * recv.dtype.itemsize + 7 * 4) * 2
    t = next(cand for cand in (256, 128, 96, 64, 32, 16)
             if rows % cand == 0 and row_off % cand == 0 and cand * per_row <= VMEM_LIMIT // 2)

    def body(r_ref, w_ref, m_ref, v_ref, g_ref, d_ref, nm_ref, nv_ref):
        g = r_ref[0].astype(f32)
        for s in range(1, N_DEV):
            g = g + r_ref[s].astype(f32)
        d, nm, nv = _adam_math(w_ref[...], g, m_ref[...], v_ref[...])
        g_ref[...] = g
        d_ref[...] = d
        nm_ref[...] = nm
        nv_ref[...] = nv

    spec = pl.BlockSpec((None, t, width), lambda l, i: (l, i, 0))
    shape = jax.ShapeDtypeStruct((depth, rows, width), f32)
    return pl.pallas_call(
        body, name=name, grid=(depth, rows // t),
        in_specs=[pl.BlockSpec((N_DEV, None, t, width), lambda l, i: (0, l, row_off // t + i, 0)), spec, spec, spec],
        out_specs=[spec] * 4, out_shape=[shape] * 4, compiler_params=_cparams(("parallel", "parallel")))(recv, w, m, v)


def _sum_slots(name, buf):
    _, rows, width = buf.shape
    t = next(cand for cand in (512, 256, 128, 64, 32, 16) if rows % cand == 0)

    def body(b_ref, o_ref):
        acc = b_ref[0].astype(f32)
        for s in range(1, N_DEV):
            acc = acc + b_ref[s].astype(f32)
        o_ref[...] = acc

    return pl.pallas_call(
        body, name=name, grid=(rows // t,), in_specs=[pl.BlockSpec((N_DEV, t, width), lambda i: (0, i, 0))],
        out_specs=pl.BlockSpec((t, width), lambda i: (i, 0)), out_shape=jax.ShapeDtypeStruct((rows, width), f32),
        compiler_params=_cparams(("parallel",)))(buf)


def _peer(idx):
    return (idx // 4, (idx // 2) % 2, idx % 2)


def _exchange(name, bufs, gather):
    nb = len(bufs)

    def body(*refs):
        srcs, outs = refs[:nb], refs[nb:2 * nb]
        send_sems, recv_sems, local_sems = refs[2 * nb:]
        me = lax.axis_index("x") * 4 + lax.axis_index("y") * 2 + lax.axis_index("c")
        own = [s if gather else s.at[me] for s in srcs]
        mine = [pltpu.make_async_copy(own[b], outs[b].at[me], local_sems.at[b]) for b in range(nb)]
        for cp in mine:
            cp.start()
        sends = []
        for k in range(1, N_DEV):
            to = (me + k) % N_DEV
            for b in range(nb):
                cp = pltpu.make_async_remote_copy(
                    src_ref=srcs[b] if gather else srcs[b].at[to], dst_ref=outs[b].at[me],
                    send_sem=send_sems.at[(k - 1) * nb + b], recv_sem=recv_sems.at[(k - 1) * nb + b],
                    device_id=_peer(to), device_id_type=pl.DeviceIdType.MESH)
                cp.start()
                sends.append(cp)
        for k in range(1, N_DEV):
            frm = (me + N_DEV - k) % N_DEV
            for b in range(nb):
                pltpu.make_async_remote_copy(
                    src_ref=own[b], dst_ref=outs[b].at[frm],
                    send_sem=send_sems.at[(k - 1) * nb + b], recv_sem=recv_sems.at[(k - 1) * nb + b],
                    device_id=_peer(frm), device_id_type=pl.DeviceIdType.MESH).wait_recv()
        for cp in sends:
            cp.wait_send()
        for cp in mine:
            cp.wait()

    any_spec = pl.BlockSpec(memory_space=pl.ANY)
    return pl.pallas_call(
        body, name=name, in_specs=[any_spec] * nb, out_specs=[any_spec] * nb,
        out_shape=[jax.ShapeDtypeStruct(((N_DEV,) + tuple(b.shape)) if gather else tuple(b.shape), b.dtype) for b in bufs],
        scratch_shapes=[pltpu.SemaphoreType.DMA(((N_DEV - 1) * nb,)), pltpu.SemaphoreType.DMA(((N_DEV - 1) * nb,)),
                        pltpu.SemaphoreType.DMA((nb,))])(*bufs)


BIG = ("w_in", "s5_glu_w", "proj_a", "proj_b", "proj_c", "w_out")
CONV = ("gdn_conv_w", "m2_conv_w")
SMALL = ("norm_w", "gdn_a_log", "gdn_dt_bias", "gdn_norm_w", "s5_lam_re", "s5_lam_im", "s5_log_step",
         "s5_b_re", "s5_b_im", "s5_c_re", "s5_c_im", "s5_d", "s5_glu_b", "m2_conv_b", "m2_a_log", "m2_dt_bias",
         "m2_d", "m2_norm_w", "final_norm_w")
WEIGHTS = ("norm_w", "w_in", "gdn_conv_w", "gdn_a_log", "gdn_dt_bias", "gdn_norm_w", "s5_lam_re", "s5_lam_im",
           "s5_log_step", "s5_b_re", "s5_b_im", "s5_c_re", "s5_c_im", "s5_d", "s5_glu_w", "s5_glu_b", "m2_conv_w",
           "m2_conv_b", "m2_a_log", "m2_dt_bias", "m2_d", "m2_norm_w", "proj_a", "proj_b", "proj_c", "w_out",
           "final_norm_w")


def _pack(arrays, width, dtype):
    flat = jnp.concatenate([a.reshape(-1).astype(dtype) for a in arrays])
    unit = width * 16
    pad = (-flat.shape[0]) % unit
    if pad:
        flat = jnp.concatenate([flat, jnp.zeros((pad,), dtype)])
    return flat.reshape(-1, width)


def _unpack(flat, shapes):
    out, off = [], 0
    for s in shapes:
        size = 1
        for d in s:
            size *= d
        out.append(flat[off:off + size].reshape(s))
        off += size
    return out


def _win_to_padded(w):
    d = w.shape[0]
    z = lambda n: jnp.zeros((d, n), w.dtype)
    return jnp.concatenate([w[:, 8736:14880], w[:, 4112:5648], w[:, 0:4096], w[:, 5648:8720],
                            w[:, 4096:4112], z(LANE - 16), w[:, 8720:8736], z(LANE - 16), z(2 * LANE)], axis=1)


def _win_from_padded(g):
    u = LANE
    return jnp.concatenate([g[:, U_Q * u:U_CZ * u], g[:, U_SMA * u:U_SMA * u + 16], g[:, U_SU * u:U_Q * u],
                            g[:, U_CZ * u:U_SMA * u], g[:, U_SMC * u:U_SMC * u + 16], g[:, 0:U_SU * u]], axis=1)


def _lane_row(vals, offset):
    n = vals.shape[0]
    return jnp.concatenate([jnp.zeros((offset,), f32), vals, jnp.zeros((LANE - offset - n,), f32)]).reshape(1, LANE)


def _layer_fwd(x, lw):
    h = _tok_fwd("rms_fwd", _rms_fn, [(x, D_MODEL, 0)], [lw["norm_w"].reshape(1, D_MODEL)], [(D_MODEL, bf16)], 256)[0]
    proj = _mm("proj_fwd", h, lw["w_in_p"], "nn")
    g_al, g_dt = _lane_row(lw["gdn_a_log"], GDN_HEADS), _lane_row(lw["gdn_dt_bias"], GDN_HEADS)
    g_nw = lw["gdn_norm_w"].reshape(1, HEAD_DIM)
    y_a, ck_a = _gdn_fwd(proj, lw["gdn_conv_w"], g_al, g_dt, g_nw)
    s5p = _s5_prep(lw["s5_lam_re"], lw["s5_lam_im"], lw["s5_log_step"], lw["s5_b_re"], lw["s5_b_im"],
                   lw["s5_c_re"], lw["s5_c_im"], lw["s5_d"])
    y_pre, ck_b = _s5_fwd(proj, *s5p)
    glu_b = lw["s5_glu_b"].reshape(1, S5_WIDTH)
    y_b = _tok_fwd("s5_tail_fwd", _s5_tail_fn, [(y_pre, S5_WIDTH, 0), (proj, S5_WIDTH, U_SG // S5_BLK)],
                   [lw["s5_glu_w"], glu_b], [(S5_WIDTH, bf16)], 256)[0]
    m_al, m_dt, m_d = _lane_row(lw["m2_a_log"], 0), _lane_row(lw["m2_dt_bias"], 0), _lane_row(lw["m2_d"], 0)
    m_cb = lw["m2_conv_b"].reshape(1, -1)
    m_nw = lw["m2_norm_w"].reshape(1, M2_WIDTH)
    y_c, ck_c = _ssd_fwd(proj, lw["m2_conv_w"], m_cb, m_al, m_dt, m_d, m_nw)
    pa = _mm("proj_a_fwd", y_a, lw["proj_a"], "nn")
    pb = _mm("proj_b_fwd", y_b, lw["proj_b"], "nn")
    pc = _mm("proj_c_fwd", y_c, lw["proj_c"], "nn")
    gate_acts = [(proj, D_MODEL, 0), (proj, D_MODEL, 1), (proj, D_MODEL, 2),
                 (pa, D_MODEL, 0), (pb, D_MODEL, 0), (pc, D_MODEL, 0)]
    merged = _tok_fwd("gate_fwd", _gate_fn, gate_acts, [], [(D_MODEL, bf16)], 128)[0]
    x_next = _mm("w_out_fwd", merged, lw["w_out"], "nn", residual=x)
    saved = dict(x=x, h=h, proj=proj, y_a=y_a, ck_a=ck_a, y_pre=y_pre, ck_b=ck_b, y_b=y_b, y_c=y_c, ck_c=ck_c,
                 pa=pa, pb=pb, pc=pc, merged=merged)
    return x_next, saved


def _layer_bwd(dx_next, lw, sv):
    rows = dx_next.shape[0]
    proj = sv["proj"]
    g = {}
    d_merged = _mm("w_out_bwd_x", dx_next, lw["w_out"], "nt", out_dtype=bf16)
    g["w_out"] = _mm("w_out_bwd_w", sv["merged"], dx_next, "tn", out_dtype=bf16)
    gate_acts = [(proj, D_MODEL, 0), (proj, D_MODEL, 1), (proj, D_MODEL, 2),
                 (sv["pa"], D_MODEL, 0), (sv["pb"], D_MODEL, 0), (sv["pc"], D_MODEL, 0)]
    dla, dlb, dlc, dpa, dpb, dpc = _tok_bwd(
        "gate_bwd", _gate_fn, gate_acts, [], [(d_merged, D_MODEL, 0)],
        [(0, f32), (1, f32), (2, f32), (3, bf16), (4, bf16), (5, bf16)], 128)
    dy_a = _mm("proj_a_bwd_x", dpa, lw["proj_a"], "nt", out_dtype=bf16)
    dy_b = _mm("proj_b_bwd_x", dpb, lw["proj_b"], "nt")
    dy_c = _mm("proj_c_bwd_x", dpc, lw["proj_c"], "nt", out_dtype=bf16)
    g["proj_a"] = _mm("proj_a_bwd_w", sv["y_a"], dpa, "tn", out_dtype=bf16)
    g["proj_b"] = _mm("proj_b_bwd_w", sv["y_b"], dpb, "tn", out_dtype=bf16)
    g["proj_c"] = _mm("proj_c_bwd_w", sv["y_c"], dpc, "tn", out_dtype=bf16)

    m_al, m_dt, m_d = _lane_row(lw["m2_a_log"], 0), _lane_row(lw["m2_dt_bias"], 0), _lane_row(lw["m2_d"], 0)
    m_cb = lw["m2_conv_b"].reshape(1, -1)
    m_nw = lw["m2_norm_w"].reshape(1, M2_WIDTH)
    (dcx, dcb, dcc, dcz, dsmc, dcwx, dcwb, dcwc, dbx, dbb, dbc, dal, ddt, ddk, dnw) = _ssd_bwd(
        proj, dy_c, sv["ck_c"], lw["m2_conv_w"], m_cb, m_al, m_dt, m_d, m_nw)
    g["m2_conv_w"] = jnp.concatenate([dcwx, dcwb, dcwc], axis=1)
    g["m2_conv_b"] = jnp.concatenate([dbx, dbb, dbc], axis=1).reshape(-1)
    g["m2_a_log"] = jnp.sum(dal, axis=(0, 1))[:M2_HEADS]
    g["m2_dt_bias"] = jnp.sum(ddt, axis=(0, 1))[:M2_HEADS]
    g["m2_d"] = jnp.sum(ddk, axis=(0, 1))[:M2_HEADS]
    g["m2_norm_w"] = dnw.reshape(-1)
    dsmc = jnp.sum(dsmc, axis=0)

    glu_b = lw["s5_glu_b"].reshape(1, S5_WIDTH)
    dypre, dsg, dglu_w, dglu_b = _tok_bwd(
        "s5_tail_bwd", _s5_tail_fn, [(sv["y_pre"], S5_WIDTH, 0), (proj, S5_WIDTH, U_SG // S5_BLK)],
        [lw["s5_glu_w"], glu_b], [(dy_b, S5_WIDTH, 0)], [(0, f32), (1, f32)], 256)
    g["s5_glu_w"] = dglu_w
    g["s5_glu_b"] = dglu_b.reshape(-1)
    s5_names = ("s5_lam_re", "s5_lam_im", "s5_log_step", "s5_b_re", "s5_b_im", "s5_c_re", "s5_c_im", "s5_d")
    s5p, s5_vjp = jax.vjp(_s5_prep, *[lw[k] for k in s5_names])
    dsu, da, dbbr, dbbi, dccr, dcci, dd = _s5_bwd(proj, dypre, sv["ck_b"], *s5p)
    for k, val in zip(s5_names, s5_vjp((da, dbbr, dbbi, dccr, dcci, dd))):
        g[k] = val

    g_al, g_dt = _lane_row(lw["gdn_a_log"], GDN_HEADS), _lane_row(lw["gdn_dt_bias"], GDN_HEADS)
    g_nw = lw["gdn_norm_w"].reshape(1, HEAD_DIM)
    (dq, dk, dv, daz, dsma, dcwq, dcwk, dcwv, dgal, dgdt, dgnw) = _gdn_bwd(
        proj, dy_a, sv["ck_a"], lw["gdn_conv_w"], g_al, g_dt, g_nw)
    g["gdn_conv_w"] = jnp.concatenate([dcwq, dcwk, dcwv], axis=1)
    g["gdn_a_log"] = jnp.sum(dgal, axis=(0, 1))[GDN_HEADS:2 * GDN_HEADS]
    g["gdn_dt_bias"] = jnp.sum(dgdt, axis=(0, 1))[GDN_HEADS:2 * GDN_HEADS]
    g["gdn_norm_w"] = jnp.sum(dgnw, axis=(0, 1))
    dsma = jnp.sum(dsma, axis=0)

    dproj = jnp.concatenate([dla, dlb, dlc, dsu, dsg, dq, dk, dv, daz, dcz, dcx, dcb, dcc, dsma, dsmc,
                             jnp.zeros((rows, 2 * LANE), f32)], axis=1)
    g["w_in_p"] = _mm("proj_bwd_w", sv["h"], dproj, "tn", out_dtype=bf16)
    dh = _mm("proj_bwd_x", dproj, lw["w_in_p"], "nt")
    dx, dnorm = _tok_bwd("rms_bwd", _rms_fn, [(sv["x"], D_MODEL, 0)], [lw["norm_w"].reshape(1, D_MODEL)],
                         [(dh, D_MODEL, 0)], [(0, f32)], 256, residuals={0: dx_next})
    g["norm_w"] = dnorm.reshape(-1)
    return dx, g


def kernel(x, norm_w, w_in, gdn_conv_w, gdn_a_log, gdn_dt_bias, gdn_norm_w, s5_lam_re, s5_lam_im, s5_log_step, s5_b_re, s5_b_im, s5_c_re, s5_c_im, s5_d, s5_glu_w, s5_glu_b, m2_conv_w, m2_conv_b, m2_a_log, m2_dt_bias, m2_d, m2_norm_w, proj_a, proj_b, proj_c, w_out, final_norm_w, loss_target, m_norm_w, m_w_in, m_gdn_conv_w, m_gdn_a_log, m_gdn_dt_bias, m_gdn_norm_w, m_s5_lam_re, m_s5_lam_im, m_s5_log_step, m_s5_b_re, m_s5_b_im, m_s5_c_re, m_s5_c_im, m_s5_d, m_s5_glu_w, m_s5_glu_b, m_m2_conv_w, m_m2_conv_b, m_m2_a_log, m_m2_dt_bias, m_m2_d, m_m2_norm_w, m_proj_a, m_proj_b, m_proj_c, m_w_out, m_final_norm_w, v_norm_w, v_w_in, v_gdn_conv_w, v_gdn_a_log, v_gdn_dt_bias, v_gdn_norm_w, v_s5_lam_re, v_s5_lam_im, v_s5_log_step, v_s5_b_re, v_s5_b_im, v_s5_c_re, v_s5_c_im, v_s5_d, v_s5_glu_w, v_s5_glu_b, v_m2_conv_w, v_m2_conv_b, v_m2_a_log, v_m2_dt_bias, v_m2_d, v_m2_norm_w, v_proj_a, v_proj_b, v_proj_c, v_w_out, v_final_norm_w):
    w = dict(norm_w=norm_w, w_in=w_in, gdn_conv_w=gdn_conv_w, gdn_a_log=gdn_a_log, gdn_dt_bias=gdn_dt_bias,
             gdn_norm_w=gdn_norm_w, s5_lam_re=s5_lam_re, s5_lam_im=s5_lam_im, s5_log_step=s5_log_step,
             s5_b_re=s5_b_re, s5_b_im=s5_b_im, s5_c_re=s5_c_re, s5_c_im=s5_c_im, s5_d=s5_d, s5_glu_w=s5_glu_w,
             s5_glu_b=s5_glu_b, m2_conv_w=m2_conv_w, m2_conv_b=m2_conv_b, m2_a_log=m2_a_log, m2_dt_bias=m2_dt_bias,
             m2_d=m2_d, m2_norm_w=m2_norm_w, proj_a=proj_a, proj_b=proj_b, proj_c=proj_c, w_out=w_out,
             final_norm_w=final_norm_w)
    mom = dict(norm_w=m_norm_w, w_in=m_w_in, gdn_conv_w=m_gdn_conv_w, gdn_a_log=m_gdn_a_log,
               gdn_dt_bias=m_gdn_dt_bias, gdn_norm_w=m_gdn_norm_w, s5_lam_re=m_s5_lam_re, s5_lam_im=m_s5_lam_im,
               s5_log_step=m_s5_log_step, s5_b_re=m_s5_b_re, s5_b_im=m_s5_b_im, s5_c_re=m_s5_c_re,
               s5_c_im=m_s5_c_im, s5_d=m_s5_d, s5_glu_w=m_s5_glu_w, s5_glu_b=m_s5_glu_b, m2_conv_w=m_m2_conv_w,
               m2_conv_b=m_m2_conv_b, m2_a_log=m_m2_a_log, m2_dt_bias=m_m2_dt_bias, m2_d=m_m2_d,
               m2_norm_w=m_m2_norm_w, proj_a=m_proj_a, proj_b=m_proj_b, proj_c=m_proj_c, w_out=m_w_out,
               final_norm_w=m_final_norm_w)
    var = dict(norm_w=v_norm_w, w_in=v_w_in, gdn_conv_w=v_gdn_conv_w, gdn_a_log=v_gdn_a_log,
               gdn_dt_bias=v_gdn_dt_bias, gdn_norm_w=v_gdn_norm_w, s5_lam_re=v_s5_lam_re, s5_lam_im=v_s5_lam_im,
               s5_log_step=v_s5_log_step, s5_b_re=v_s5_b_re, s5_b_im=v_s5_b_im, s5_c_re=v_s5_c_re,
               s5_c_im=v_s5_c_im, s5_d=v_s5_d, s5_glu_w=v_s5_glu_w, s5_glu_b=v_s5_glu_b, m2_conv_w=v_m2_conv_w,
               m2_conv_b=v_m2_conv_b, m2_a_log=v_m2_a_log, m2_dt_bias=v_m2_dt_bias, m2_d=v_m2_d,
               m2_norm_w=v_m2_norm_w, proj_a=v_proj_a, proj_b=v_proj_b, proj_c=v_proj_c, w_out=v_w_out,
               final_norm_w=v_final_norm_w)
    me = lax.axis_index("x") * 4 + lax.axis_index("y") * 2 + lax.axis_index("c")
    x2 = x[0]
    tgt = loss_target[0]

    ra, rb = proj_a.shape[1], proj_b.shape[1]
    pabc = jnp.concatenate([proj_a, proj_b, proj_c], axis=1).astype(bf16)
    g_win, g_glu, g_pabc, g_wout, g_gcv, g_mcv = _exchange(
        "gather_weights", [w_in.astype(bf16), s5_glu_w.astype(bf16), pabc, w_out.astype(bf16), gdn_conv_w, m2_conv_w],
        True)

    def cols(gathered, i):
        return jnp.concatenate([gathered[d, i] for d in range(N_DEV)], axis=1)

    def rows_of(gathered, i):
        return gathered[:, i].reshape(-1, gathered.shape[-1])

    layers = []
    for i in range(DEPTH):
        pf = cols(g_pabc, i)
        lw = dict(w_in_p=_win_to_padded(cols(g_win, i)), s5_glu_w=rows_of(g_glu, i), proj_a=pf[:ra],
                  proj_b=pf[ra:ra + rb], proj_c=pf[ra + rb:], w_out=rows_of(g_wout, i),
                  gdn_conv_w=cols(g_gcv, i), m2_conv_w=cols(g_mcv, i))
        for k in SMALL:
            if k != "final_norm_w":
                lw[k] = w[k][i]
        layers.append(lw)

    saved = []
    act = x2
    for i in range(DEPTH):
        act, sv = _layer_fwd(act, layers[i])
        saved.append(sv)
    loss_row, dact, dfinal = _loss_grad(act, final_norm_w.reshape(1, D_MODEL), tgt)
    grads = [None] * DEPTH
    for i in reversed(range(DEPTH)):
        dact, grads[i] = _layer_bwd(dact, layers[i], saved[i])
        saved[i] = None
    loss = lax.psum(loss_row[0, 0], ("x", "y", "c"))

    def col_blocks(full):
        r = full.shape[0]
        return full.reshape(r, N_DEV, -1).transpose(1, 0, 2)

    def row_blocks(full):
        return full.reshape(N_DEV, -1, full.shape[1])

    s_win = jnp.stack([col_blocks(_win_from_padded(grads[i]["w_in_p"])) for i in range(DEPTH)], axis=1)
    s_glu = jnp.stack([row_blocks(grads[i]["s5_glu_w"].astype(bf16)) for i in range(DEPTH)], axis=1)
    s_pabc = jnp.stack([col_blocks(jnp.concatenate([grads[i]["proj_a"], grads[i]["proj_b"], grads[i]["proj_c"]], axis=0))
                        for i in range(DEPTH)], axis=1)
    s_wout = jnp.stack([row_blocks(grads[i]["w_out"]) for i in range(DEPTH)], axis=1)
    r_win, r_glu, r_pabc, r_wout = _exchange("scatter_grads", [s_win, s_glu, s_pabc, s_wout], False)

    g_out, delta, new_m, new_v = {}, {}, {}, {}
    for k, recv, row_off in (("w_in", r_win, 0), ("s5_glu_w", r_glu, 0), ("proj_a", r_pabc, 0),
                             ("proj_b", r_pabc, ra), ("proj_c", r_pabc, ra + rb), ("w_out", r_wout, 0)):
        g_out[k], delta[k], new_m[k], new_v[k] = _adamw_slots("adamw_" + k, recv, row_off, w[k], mom[k], var[k])

    names_small = [k for k in SMALL if k != "final_norm_w"]
    small_parts = [jnp.stack([grads[i][k] for i in range(DEPTH)]) for k in names_small + list(CONV)]
    small_parts.append(dfinal.reshape(-1))
    small_shapes = [p.shape for p in small_parts]
    small_local = _pack(small_parts, LANE, f32)
    small_sum = _sum_slots("sum_small", _exchange("gather_small", [small_local], True)[0]).reshape(-1)
    unp = _unpack(small_sum, small_shapes)
    for k, val in zip(names_small, unp):
        g_out[k] = val
    for j, k in enumerate(CONV):
        width = w[k].shape[2]
        g_out[k] = lax.dynamic_slice_in_dim(unp[len(names_small) + j], me * width, width, axis=2)
    g_out["final_norm_w"] = unp[-1]

    rest = [k for k in WEIGHTS if k not in BIG]
    rest_shapes = [w[k].shape for k in rest]
    packed = [_pack([src[k] for k in rest], LANE, f32) for src in (w, g_out, mom, var)]
    d, nm, nv = _adamw("adamw_small", *packed)
    for dst, arr in ((delta, d), (new_m, nm), (new_v, nv)):
        for k, val in zip(rest, _unpack(arr.reshape(-1), rest_shapes)):
            dst[k] = val

    grad_x = dact.reshape(x.shape)
    return (loss, grad_x, *[g_out[k] for k in WEIGHTS], *[delta[k] for k in WEIGHTS],
            *[new_m[k] for k in WEIGHTS], *[new_v[k] for k in WEIGHTS])
```

```python
import functools

import jax
import jax.numpy as jnp
from jax import lax
from jax.experimental import pallas as pl
from jax.experimental.pallas import tpu as pltpu

f32 = jnp.float32
bf16 = jnp.bfloat16

N_DEV = 8
DEPTH = 4
D_MODEL = 2048
GDN_HEADS = 8
HEAD_DIM = 128
GDN_WIDTH = 1024
S5_GROUPS = 48
S5_GROUP_SIZE = 16
S5_STATE = 64
S5_WIDTH = 768
S5_LANES = S5_GROUPS * S5_STATE
S5_BLK = 6
M2_HEADS = 16
M2_HEAD_DIM = 64
M2_WIDTH = 1024
M2_GROUPS = 4
M2_STATE = 128
CONV_K = 4
CHUNK = 64
HALO = 8
NORM_EPS = 1e-6
IN_DIM = 14880
LANE = 128
VMEM_LIMIT = 48 * 1024 * 1024

ADAM_LR = 0.001
ADAM_B1 = 0.9
ADAM_B2 = 0.999
ADAM_EPS = 1e-08
ADAM_WD = 0.01
ADAM_STEP = 10

U_MERGE, U_SU, U_SG, U_Q, U_K, U_V, U_AZ, U_CZ, U_CX, U_CB, U_CC, U_SMA, U_SMC = (
    0, 48, 54, 60, 68, 76, 84, 92, 100, 108, 112, 116, 117)
NP_UNITS = 120
NP_COLS = NP_UNITS * LANE


def _cparams(sem=None):
    return pltpu.CompilerParams(dimension_semantics=sem, vmem_limit_bytes=VMEM_LIMIT)


def _pick(dim, target):
    if dim <= target:
        return dim
    for t in range(target - target % LANE, 0, -LANE):
        if dim % t == 0:
            return t
    raise ValueError(f"no tile for {dim}")


def _bd(a, b, dims):
    return lax.dot_general(a.astype(bf16), b.astype(bf16), (dims, ((), ())), preferred_element_type=f32)


@jax.custom_vjp
def dot_nn(a, b):
    return _bd(a, b, ((1,), (0,)))


@jax.custom_vjp
def dot_nt(a, b):
    return _bd(a, b, ((1,), (1,)))


@jax.custom_vjp
def dot_tn(a, b):
    return _bd(a, b, ((0,), (0,)))


dot_nn.defvjp(lambda a, b: (dot_nn(a, b), (a, b)), lambda r, ct: (dot_nt(ct, r[1]), dot_tn(r[0], ct)))
dot_nt.defvjp(lambda a, b: (dot_nt(a, b), (a, b)), lambda r, ct: (dot_nn(ct, r[1]), dot_tn(ct, r[0])))
dot_tn.defvjp(lambda a, b: (dot_tn(a, b), (a, b)), lambda r, ct: (dot_nt(r[1], ct), dot_nn(r[0], ct)))

_HI = lax.Precision.HIGHEST


def _silu(x):
    return x * jax.nn.sigmoid(x)


def _softplus(x):
    return jnp.maximum(x, 0.0) + jnp.log(1.0 + jnp.exp(-jnp.abs(x)))


def _tri_masks(c):
    row = lax.broadcasted_iota(jnp.int32, (c, c), 0)
    col = lax.broadcasted_iota(jnp.int32, (c, c), 1)
    return row >= col, row > col, (row >= col).astype(f32), (row == col).astype(f32)


def _scan_add(x, reverse):
    t = x.shape[0]
    row = lax.broadcasted_iota(jnp.int32, x.shape, 0)
    d = 1
    while d < t:
        if reverse:
            x = x + jnp.where(row < t - d, pltpu.roll(x, t - d, 0), 0.0)
        else:
            x = x + jnp.where(row >= d, pltpu.roll(x, d, 0), 0.0)
        d *= 2
    return x


@jax.custom_vjp
def _cumsum_rows(x):
    return _scan_add(x, False)


_cumsum_rows.defvjp(lambda x: (_scan_add(x, False), None), lambda _, ct: (_scan_add(ct, True),))


def _cumsum_all(a):
    cum = _cumsum_rows(a)
    return cum, cum.T


def _onehot_lane(idx):
    return (lax.broadcasted_iota(jnp.int32, (1, LANE), 1) == idx).astype(f32)


def _onehot_sub(idx):
    return (lax.broadcasted_iota(jnp.int32, (LANE, 1), 0) == idx).astype(f32)


def _pick_col(x, idx):
    return jnp.sum(x * _onehot_lane(idx), axis=-1, keepdims=True)


def _pick_row(xt, idx):
    return jnp.sum(xt * _onehot_sub(idx), axis=0, keepdims=True)


def _peer(idx):
    return (idx // 4, (idx // 2) % 2, idx % 2)


def _comm_copies(srcs, outs, send_sems, recv_sems, local_sems, gather, landing=True):
    nb = len(srcs)
    me = lax.axis_index("x") * 4 + lax.axis_index("y") * 2 + lax.axis_index("c")
    own = [s if gather else s.at[me] for s in srcs]
    mine = [pltpu.make_async_copy(own[b], outs[b].at[me], local_sems.at[b]) for b in range(nb)]
    sends, lands = [], []
    for k in range(1, N_DEV):
        to = (me + k) % N_DEV
        frm = (me + N_DEV - k) % N_DEV
        for b in range(nb):
            sems = dict(send_sem=send_sems.at[(k - 1) * nb + b], recv_sem=recv_sems.at[(k - 1) * nb + b],
                        device_id_type=pl.DeviceIdType.MESH)
            sends.append(pltpu.make_async_remote_copy(src_ref=srcs[b] if gather else srcs[b].at[to],
                                                      dst_ref=outs[b].at[me], device_id=_peer(to), **sems))
            if landing:
                lands.append(pltpu.make_async_remote_copy(src_ref=own[b], dst_ref=outs[b].at[frm],
                                                          device_id=_peer(frm), **sems))
    return mine, sends, lands


def _comm_start(*refs, gather):
    mine, sends, _ = _comm_copies(*refs, gather, landing=False)
    for cp in mine + sends:
        cp.start()


def _comm_wait(*refs, gather):
    mine, sends, lands = _comm_copies(*refs, gather)
    for cp in lands:
        cp.wait_recv()
    for cp in sends:
        cp.wait_send()
    for cp in mine:
        cp.wait()


def _comm_shapes(bufs, gather):
    nb = len(bufs)
    out_shape = [jax.ShapeDtypeStruct(((N_DEV,) + tuple(b.shape)) if gather else tuple(b.shape), b.dtype) for b in bufs]
    sems = [pltpu.SemaphoreType.DMA(((N_DEV - 1) * nb,)), pltpu.SemaphoreType.DMA(((N_DEV - 1) * nb,)),
            pltpu.SemaphoreType.DMA((nb,))]
    return out_shape, sems


def _exchange(name, bufs, gather):
    nb = len(bufs)

    def body(*refs):
        args = (refs[:nb], refs[nb:2 * nb], *refs[2 * nb:])
        _comm_start(*args, gather=gather)
        _comm_wait(*args, gather=gather)

    any_spec = pl.BlockSpec(memory_space=pl.ANY)
    out_shape, sems = _comm_shapes(bufs, gather)
    return pl.pallas_call(body, name=name, in_specs=[any_spec] * nb, out_specs=[any_spec] * nb, out_shape=out_shape,
                          scratch_shapes=sems)(*bufs)


def _mm(name, a, b, mode, residual=None, out_dtype=f32, tm=1024, tn=1024, tk=512, comm=None):
    if mode == "nn":
        (m, k), (_, n) = a.shape, b.shape
    elif mode == "nt":
        (m, k), (n, _) = a.shape, b.shape
    else:
        (k, m), (_, n) = a.shape, b.shape
    tm, tn, tk = _pick(m, tm), _pick(n, tn), _pick(k, tk)
    gm, gn, nk = m // tm, n // tn, k // tk
    dims = {"nn": ((1,), (0,)), "nt": ((1,), (1,)), "tn": ((0,), (0,))}[mode]
    has_res = residual is not None
    bufs, gather = comm if comm is not None else ([], True)
    nb = len(bufs)
    n_in = 2 + has_res

    def body(*refs):
        a_ref, b_ref = refs[:2]
        r_ref = refs[2] if has_res else None
        srcs = refs[n_in:n_in + nb]
        o_ref = refs[n_in + nb]
        outs = refs[n_in + nb + 1:n_in + 2 * nb + 1]
        acc_ref = refs[n_in + 2 * nb + 1]
        sems = refs[n_in + 2 * nb + 2:]
        i, j, kk = pl.program_id(0), pl.program_id(1), pl.program_id(2)

        if nb:
            @pl.when((i == 0) & (j == 0) & (kk == 0))
            def _():
                _comm_start(srcs, outs, *sems, gather=gather)

        @pl.when(kk == 0)
        def _():
            acc_ref[...] = jnp.zeros_like(acc_ref)

        acc_ref[...] += _bd(a_ref[...], b_ref[...], dims)

        @pl.when(kk == nk - 1)
        def _():
            out = acc_ref[...]
            if has_res:
                out = out + r_ref[...].astype(f32)
            o_ref[...] = out.astype(o_ref.dtype)

        if nb:
            @pl.when((i == gm - 1) & (j == gn - 1) & (kk == nk - 1))
            def _():
                _comm_wait(srcs, outs, *sems, gather=gather)

    if mode == "tn":
        a_spec = pl.BlockSpec((tk, tm), lambda i, j, kk: (kk, i))
    else:
        a_spec = pl.BlockSpec((tm, tk), lambda i, j, kk: (i, kk))
    if mode == "nt":
        b_spec = pl.BlockSpec((tn, tk), lambda i, j, kk: (j, kk))
    else:
        b_spec = pl.BlockSpec((tk, tn), lambda i, j, kk: (kk, j))
    o_spec = pl.BlockSpec((tm, tn), lambda i, j, kk: (i, j))
    any_spec = pl.BlockSpec(memory_space=pl.ANY)
    in_specs = [a_spec, b_spec] + ([o_spec] if has_res else []) + [any_spec] * nb
    args = (a, b) + ((residual,) if has_res else ()) + tuple(bufs)
    comm_shapes, comm_sems = _comm_shapes(bufs, gather) if nb else ([], [])
    res = pl.pallas_call(
        body, name=name, grid=(gm, gn, nk), in_specs=in_specs, out_specs=[o_spec] + [any_spec] * nb,
        out_shape=[jax.ShapeDtypeStruct((m, n), out_dtype)] + comm_shapes,
        scratch_shapes=[pltpu.VMEM((tm, tn), f32)] + comm_sems,
        compiler_params=_cparams(("arbitrary",) * 3 if nb else ("parallel", "parallel", "arbitrary")))(*args)
    return res if nb else res[0]


def _act_spec(t, width, colblk):
    return pl.BlockSpec((t, width), lambda i: (i, colblk))


def _tok_fwd(name, fn, acts, params, outs, t):
    rows = acts[0][0].shape[0]
    t = min(t, rows)
    na, npar = len(acts), len(params)

    def body(*refs):
        a = [r[...].astype(f32) for r in refs[:na]]
        p = [r[...].astype(f32) for r in refs[na:na + npar]]
        res = fn(*a, *p)
        for o_ref, o in zip(refs[na + npar:], res):
            o_ref[...] = o.astype(o_ref.dtype)

    in_specs = [_act_spec(t, w, cb) for (_, w, cb) in acts]
    in_specs += [pl.BlockSpec(p.shape, lambda i: (0, 0)) for p in params]
    res = pl.pallas_call(
        body, name=name, grid=(rows // t,), in_specs=in_specs,
        out_specs=[_act_spec(t, w, 0) for (w, _) in outs],
        out_shape=[jax.ShapeDtypeStruct((rows, w), dt) for (w, dt) in outs],
        compiler_params=_cparams(("arbitrary",)))(*[a for (a, _, _) in acts], *params)
    return res


def _tok_bwd(name, fn, acts, params, cts, dact, t, residuals=None):
    rows = acts[0][0].shape[0]
    t = min(t, rows)
    residuals = residuals or {}
    res_ids = sorted(residuals)
    na, npar, nc, nr, nd = len(acts), len(params), len(cts), len(res_ids), len(dact)

    def body(*refs):
        a = [r[...].astype(f32) for r in refs[:na]]
        p = [r[...].astype(f32) for r in refs[na:na + npar]]
        ct = tuple(r[...].astype(f32) for r in refs[na + npar:na + npar + nc])
        rs = {idx: r[...].astype(f32) for idx, r in zip(res_ids, refs[na + npar + nc:na + npar + nc + nr])}
        orefs = refs[na + npar + nc + nr:]
        _, vjp = jax.vjp(fn, *a, *p)
        grads = vjp(ct)
        for o_ref, (idx, _) in zip(orefs[:nd], dact):
            g = grads[idx]
            if idx in rs:
                g = g + rs[idx]
            o_ref[...] = g.astype(o_ref.dtype)

        if npar:
            @pl.when(pl.program_id(0) == 0)
            def _():
                for o_ref in orefs[nd:]:
                    o_ref[...] = jnp.zeros_like(o_ref)

            for o_ref, g in zip(orefs[nd:], grads[na:]):
                o_ref[...] += g

    in_specs = [_act_spec(t, w, cb) for (_, w, cb) in acts]
    in_specs += [pl.BlockSpec(p.shape, lambda i: (0, 0)) for p in params]
    in_specs += [_act_spec(t, w, cb) for (_, w, cb) in cts]
    in_specs += [_act_spec(t, acts[idx][1], 0) for idx in res_ids]
    out_specs = [_act_spec(t, acts[idx][1], 0) for (idx, _) in dact]
    out_specs += [pl.BlockSpec(p.shape, lambda i: (0, 0)) for p in params]
    out_shape = [jax.ShapeDtypeStruct((rows, acts[idx][1]), dt) for (idx, dt) in dact]
    out_shape += [jax.ShapeDtypeStruct(p.shape, f32) for p in params]
    return pl.pallas_call(
        body, name=name, grid=(rows // t,), in_specs=in_specs, out_specs=out_specs, out_shape=out_shape,
        compiler_params=_cparams(("arbitrary",)))(
            *[a for (a, _, _) in acts], *params, *[c for (c, _, _) in cts], *[residuals[i] for i in res_ids])


def _rms_fn(x, w):
    return (x * lax.rsqrt(jnp.mean(x * x, axis=-1, keepdims=True) + NORM_EPS) * w,)


def _gate_fn(la, lb, lc, pa, pb, pc):
    return (jax.nn.sigmoid(la) * pa + jax.nn.sigmoid(lb) * pb + jax.nn.sigmoid(lc) * pc,)


def _s5_tail_fn(ypre, gate, glu_w, glu_b):
    y = jax.nn.gelu(ypre)
    y = y * jax.nn.sigmoid(dot_nn(y, glu_w) + glu_b)
    return (y * _silu(gate),)


def _loss_grad(x, w, target, t=256):
    rows, d = x.shape
    t = min(t, rows)

    def fn(xt, wt, tt):
        y = _rms_fn(xt, wt)[0]
        err = y - tt
        return 0.5 * jnp.sum(jnp.sum(err * err, axis=-1, keepdims=True), axis=0, keepdims=True) / d

    def body(x_ref, w_ref, t_ref, loss_ref, dx_ref, dw_ref):
        tt = t_ref[...]
        val, vjp = jax.vjp(lambda a, b: fn(a, b, tt), x_ref[...], w_ref[...])
        dx, dw = vjp(jnp.ones((1, 1), f32))
        dx_ref[...] = dx

        @pl.when(pl.program_id(0) == 0)
        def _():
            loss_ref[...] = jnp.zeros_like(loss_ref)
            dw_ref[...] = jnp.zeros_like(dw_ref)

        loss_ref[...] += val * jnp.ones((1, LANE), f32)
        dw_ref[...] += dw

    return pl.pallas_call(
        body, name="loss_grad", grid=(rows // t,),
        in_specs=[_act_spec(t, d, 0), pl.BlockSpec((1, d), lambda i: (0, 0)), _act_spec(t, d, 0)],
        out_specs=[pl.BlockSpec((1, LANE), lambda i: (0, 0)), _act_spec(t, d, 0), pl.BlockSpec((1, d), lambda i: (0, 0))],
        out_shape=[jax.ShapeDtypeStruct((1, LANE), f32), jax.ShapeDtypeStruct((rows, d), f32),
                   jax.ShapeDtypeStruct((1, d), f32)],
        compiler_params=_cparams(("arbitrary",)))(x, w, target)


GDN_HB = 4
GDN_SW = GDN_HB * HEAD_DIM
GDN_STEPS = GDN_HEADS // GDN_HB


def _tri_inv(a, eye, c):
    rows = a.shape[0]
    n = -a
    p = eye + n
    npow = dot_nn(n, n)
    levels = c.bit_length() - 1
    for j in range(2, levels):
        both = dot_nn(jnp.concatenate([p, npow], axis=0), npow)
        p, npow = p + both[:rows], both[rows:]
    return p + dot_nn(p, npow)


def _block_ids(rows, c):
    ri = lax.broadcasted_iota(jnp.int32, (rows, rows), 0)
    ci = lax.broadcasted_iota(jnp.int32, (rows, rows), 1)
    r1 = lax.broadcasted_iota(jnp.int32, (rows, 1), 0)
    rb, cb, r1b = 0, 0, 0
    for edge in range(c, rows, c):
        rb = rb + (ri >= edge).astype(jnp.int32)
        cb = cb + (ci >= edge).astype(jnp.int32)
        r1b = r1b + (r1 >= edge).astype(jnp.int32)
    return ri, ci, rb, cb, r1b


def _gdn_step(qc, kc, vc, z, small, alog_row, dtb_row, normw, s_cat, head0):
    c = qc.shape[0]
    hb = GDN_HB
    rows = hb * c

    def stack(x):
        return jnp.concatenate([x[:, r * HEAD_DIM:(r + 1) * HEAD_DIM] for r in range(hb)], axis=0)

    ri, ci, rb, cb, r1b = _block_ids(rows, c)
    same = rb == cb
    causal = same & (ri >= ci)
    strict = same & (ri > ci)
    eye = (ri == ci).astype(f32)
    head_rows = [(r1b == r).astype(f32) for r in range(hb)]

    def own_block(x):
        acc = None
        for r in range(hb):
            term = x[:, r * HEAD_DIM:(r + 1) * HEAD_DIM] * head_rows[r]
            acc = term if acc is None else acc + term
        return acc

    beta_all = jax.nn.sigmoid(small)
    g_all = -jnp.exp(alog_row) * _softplus(small + dtb_row)
    gc_all, gct_all = _cumsum_all(g_all)
    beta = jnp.concatenate([_pick_col(beta_all, head0 + r) for r in range(hb)], axis=0)
    gc = jnp.concatenate([_pick_col(gc_all, head0 + r + GDN_HEADS) for r in range(hb)], axis=0)
    gc_t = jnp.concatenate([_pick_row(gct_all, head0 + r + GDN_HEADS) for r in range(hb)], axis=1)
    g_last = [gc[(r + 1) * c - 1:(r + 1) * c, :] for r in range(hb)]
    gl = sum(head_rows[r] * g_last[r] for r in range(hb))

    q = _silu(stack(qc))
    k = _silu(stack(kc))
    v = _silu(stack(vc))
    q = q * lax.rsqrt(jnp.sum(q * q, axis=-1, keepdims=True) + NORM_EPS) * (HEAD_DIM ** -0.5)
    k = k * lax.rsqrt(jnp.sum(k * k, axis=-1, keepdims=True) + NORM_EPS)
    decay = jnp.exp(jnp.where(causal, gc - gc_t, -1e30))
    egc = jnp.exp(gc)
    kb = k * beta
    a_mat = jnp.where(strict, dot_nt(kb, k) * decay, 0.0)
    t_inv = _tri_inv(a_mat, eye, c)
    uw = dot_nn(t_inv, jnp.concatenate([v * beta, kb * egc], axis=1))
    u, w = uw[:, :HEAD_DIM], uw[:, HEAD_DIM:]
    qk = dot_nt(q, k) * decay
    on_state = dot_nn(jnp.concatenate([w, q * egc], axis=0), s_cat)
    v_new = u - own_block(on_state[:rows])
    out = own_block(on_state[rows:]) + dot_nn(qk, v_new)
    k_tail = k * jnp.exp(gl - gc)
    v_bd = jnp.concatenate([v_new * head_rows[r] for r in range(hb)], axis=1)
    eg_cat = jnp.concatenate([jnp.exp(g_last[r]) * jnp.ones((1, HEAD_DIM), f32) for r in range(hb)], axis=1)
    new_s = s_cat * eg_cat + dot_tn(k_tail, v_bd)
    o = out * lax.rsqrt(jnp.mean(out * out, axis=-1, keepdims=True) + NORM_EPS) * normw * _silu(stack(z))
    o = jnp.concatenate([o[r * c:(r + 1) * c] for r in range(hb)], axis=1)
    return o, new_s


def _conv_windows(xin_ref, p, cw_ref, c):
    acc = None
    for k in range(CONV_K):
        term = cw_ref[pl.ds(k, 1), :] * xin_ref[p, pl.ds(HALO - CONV_K + 1 + k, c), :]
        acc = term if acc is None else acc + term
    return acc


def _gdn_fwd(proj, conv_w, alog_row, dtb_row, normw):
    rows = proj.shape[0]
    c = min(CHUNK, rows)
    n = rows // c

    def body(q_ref, k_ref, v_ref, z_ref, sm_ref, cwq, cwk, cwv, al_ref, dt_ref, nw_ref, y_ref, ck_ref, s_ref, xin_ref):
        hb = pl.program_id(0)
        i = pl.program_id(1)

        @pl.when(i == 0)
        def _():
            s_ref[...] = jnp.zeros_like(s_ref)
            xin_ref[:, 0:HALO, :] = jnp.zeros((3, HALO, GDN_SW), f32)

        @pl.when(i > 0)
        def _():
            xin_ref[:, 0:HALO, :] = xin_ref[:, c:c + HALO, :]

        xin_ref[0, HALO:, :] = q_ref[...]
        xin_ref[1, HALO:, :] = k_ref[...]
        xin_ref[2, HALO:, :] = v_ref[...]
        qc = _conv_windows(xin_ref, 0, cwq, c)
        kc = _conv_windows(xin_ref, 1, cwk, c)
        vc = _conv_windows(xin_ref, 2, cwv, c)
        state = s_ref[...]
        ck_ref[...] = state
        o, new_state = _gdn_step(qc, kc, vc, z_ref[...], sm_ref[...], al_ref[...], dt_ref[...], nw_ref[...], state,
                                 hb * GDN_HB)
        y_ref[...] = o.astype(y_ref.dtype)
        s_ref[...] = new_state

    def blk(unit):
        return pl.BlockSpec((c, GDN_SW), lambda hb, i: (i, unit // GDN_HB + hb))

    def cw(part):
        return pl.BlockSpec((CONV_K, GDN_SW), lambda hb, i: (0, part * GDN_STEPS + hb))

    row = pl.BlockSpec((1, LANE), lambda hb, i: (0, 0))
    return pl.pallas_call(
        body, name="gdn_fwd", grid=(GDN_STEPS, n),
        in_specs=[blk(U_Q), blk(U_K), blk(U_V), blk(U_AZ), pl.BlockSpec((c, LANE), lambda hb, i: (i, U_SMA)),
                  cw(0), cw(1), cw(2), row, row, row],
        out_specs=[pl.BlockSpec((c, GDN_SW), lambda hb, i: (i, hb)),
                   pl.BlockSpec((None, None, HEAD_DIM, GDN_SW), lambda hb, i: (hb, i, 0, 0))],
        out_shape=[jax.ShapeDtypeStruct((rows, GDN_WIDTH), bf16),
                   jax.ShapeDtypeStruct((GDN_STEPS, n, HEAD_DIM, GDN_SW), f32)],
        scratch_shapes=[pltpu.VMEM((HEAD_DIM, GDN_SW), f32), pltpu.VMEM((3, c + HALO, GDN_SW), f32)],
        compiler_params=_cparams(("arbitrary", "arbitrary")))(
            proj, proj, proj, proj, proj, conv_w, conv_w, conv_w, alog_row, dtb_row, normw)


def _conv_bwd(xin_ref, dyext_ref, p, cw_ref, dxc, dx_ref, dcw_ref, c):
    dyext_ref[p, 0:c, :] = dxc
    acc = None
    for k in range(CONV_K):
        term = cw_ref[pl.ds(k, 1), :] * dyext_ref[p, pl.ds(CONV_K - 1 - k, c), :]
        acc = term if acc is None else acc + term
        dcw_ref[pl.ds(k, 1), :] += jnp.sum(xin_ref[p, pl.ds(HALO - CONV_K + 1 + k, c), :] * dxc, axis=0, keepdims=True)
    dx_ref[...] = acc


def _gdn_bwd(proj, dy, ck, conv_w, alog_row, dtb_row, normw):
    rows = proj.shape[0]
    c = min(CHUNK, rows)
    n = rows // c
    halo_blocks = c // HALO

    def body(q_ref, k_ref, v_ref, hq_ref, hk_ref, hv_ref, z_ref, sm_ref, cwq, cwk, cwv, al_ref, dt_ref, nw_ref,
             ck_ref, dy_ref, dq_ref, dk_ref, dv_ref, dz_ref, dsm_ref, dcwq, dcwk, dcwv, dal_ref, ddt_ref, dnw_ref,
             ds_ref, xin_ref, dyext_ref):
        hb = pl.program_id(0)
        i = pl.program_id(1)
        ci = n - 1 - i

        @pl.when(i == 0)
        def _():
            ds_ref[...] = jnp.zeros_like(ds_ref)
            dyext_ref[:, c:c + HALO, :] = jnp.zeros((3, HALO, GDN_SW), f32)
            for r in (dcwq, dcwk, dcwv, dal_ref, ddt_ref, dnw_ref):
                r[...] = jnp.zeros_like(r)

        @pl.when(i > 0)
        def _():
            dyext_ref[:, c:c + HALO, :] = dyext_ref[:, 0:HALO, :]

        first = (ci > 0).astype(f32)
        for p, (x_ref, halo_ref) in enumerate(((q_ref, hq_ref), (k_ref, hk_ref), (v_ref, hv_ref))):
            xin_ref[p, 0:HALO, :] = halo_ref[...] * first
            xin_ref[p, HALO:, :] = x_ref[...]
        qc = _conv_windows(xin_ref, 0, cwq, c)
        kc = _conv_windows(xin_ref, 1, cwk, c)
        vc = _conv_windows(xin_ref, 2, cwv, c)
        fn = functools.partial(_gdn_step, head0=hb * GDN_HB)
        _, vjp = jax.vjp(fn, qc, kc, vc, z_ref[...], sm_ref[...], al_ref[...], dt_ref[...], nw_ref[...], ck_ref[...])
        dqc, dkc, dvc, dz, dsm, dal, ddt, dnw, dstate = vjp((dy_ref[...].astype(f32), ds_ref[...]))
        ds_ref[...] = dstate
        dz_ref[...] = dz
        dsm_ref[...] = dsm
        dal_ref[...] += dal
        ddt_ref[...] += ddt
        dnw_ref[...] += dnw
        _conv_bwd(xin_ref, dyext_ref, 0, cwq, dqc, dq_ref, dcwq, c)
        _conv_bwd(xin_ref, dyext_ref, 1, cwk, dkc, dk_ref, dcwk, c)
        _conv_bwd(xin_ref, dyext_ref, 2, cwv, dvc, dv_ref, dcwv, c)

    def blk(unit):
        return pl.BlockSpec((c, GDN_SW), lambda hb, i: (n - 1 - i, unit // GDN_HB + hb))

    def halo(unit):
        return pl.BlockSpec((HALO, GDN_SW),
                            lambda hb, i: (jnp.maximum((n - 1 - i) * halo_blocks - 1, 0), unit // GDN_HB + hb))

    def cw(part):
        return pl.BlockSpec((CONV_K, GDN_SW), lambda hb, i: (0, part * GDN_STEPS + hb))

    row = pl.BlockSpec((1, LANE), lambda hb, i: (0, 0))
    hrow = pl.BlockSpec((None, 1, LANE), lambda hb, i: (hb, 0, 0))
    out_blk = pl.BlockSpec((c, GDN_SW), lambda hb, i: (n - 1 - i, hb))
    dcw = pl.BlockSpec((CONV_K, GDN_SW), lambda hb, i: (0, hb))
    wide = jax.ShapeDtypeStruct((rows, GDN_WIDTH), f32)
    hrow_shape = jax.ShapeDtypeStruct((GDN_STEPS, 1, LANE), f32)
    dcw_shape = jax.ShapeDtypeStruct((CONV_K, GDN_WIDTH), f32)
    return pl.pallas_call(
        body, name="gdn_bwd", grid=(GDN_STEPS, n),
        in_specs=[blk(U_Q), blk(U_K), blk(U_V), halo(U_Q), halo(U_K), halo(U_V), blk(U_AZ),
                  pl.BlockSpec((c, LANE), lambda hb, i: (n - 1 - i, U_SMA)),
                  cw(0), cw(1), cw(2), row, row, row,
                  pl.BlockSpec((None, None, HEAD_DIM, GDN_SW), lambda hb, i: (hb, n - 1 - i, 0, 0)),
                  out_blk],
        out_specs=[out_blk, out_blk, out_blk, out_blk,
                   pl.BlockSpec((None, c, LANE), lambda hb, i: (hb, n - 1 - i, 0)),
                   dcw, dcw, dcw, hrow, hrow, hrow],
        out_shape=[wide, wide, wide, wide, jax.ShapeDtypeStruct((GDN_STEPS, rows, LANE), f32),
                   dcw_shape, dcw_shape, dcw_shape, hrow_shape, hrow_shape, hrow_shape],
        scratch_shapes=[pltpu.VMEM((HEAD_DIM, GDN_SW), f32), pltpu.VMEM((3, c + HALO, GDN_SW), f32),
                        pltpu.VMEM((3, c + HALO, GDN_SW), f32)],
        compiler_params=_cparams(("arbitrary", "arbitrary")))(
            proj, proj, proj, proj, proj, proj, proj, proj, conv_w, conv_w, conv_w, alog_row, dtb_row, normw, ck, dy)


M2_REP = M2_HEADS // M2_GROUPS
M2_GW = M2_REP * M2_HEAD_DIM
M2_GB = 2
M2_STEPS = M2_GROUPS // M2_GB
M2_XW = M2_GB * M2_GW
M2_BW = M2_GB * M2_STATE
M2_SH = M2_GB * M2_REP


def _ssd_step(xc, bc, cc, z, small, bias_x, bias_b, bias_c, alog_row, dtb_row, d_row, normw, state, grp0):
    c = xc.shape[0]
    causal = _tri_masks(c)[0]
    hd = M2_HEAD_DIM
    ones_l = jnp.ones((1, hd), f32)
    ones_r = jnp.ones((hd, 1), f32)
    lane = lax.broadcasted_iota(jnp.int32, (1, M2_GW), 1)
    lane_head = [((lane >= r * hd) & (lane < (r + 1) * hd)).astype(f32) for r in range(M2_REP)]
    xs = _silu(xc + bias_x)
    bms = _silu(bc + bias_b)
    cms = _silu(cc + bias_c)
    dt_all = _softplus(small + dtb_row)
    a_all = -jnp.exp(alog_row) * dt_all
    ac_all, act_all = _cumsum_all(a_all)
    ys, new_states = [], []
    for gi in range(M2_GB):
        bm = bms[:, gi * M2_STATE:(gi + 1) * M2_STATE]
        cm = cms[:, gi * M2_STATE:(gi + 1) * M2_STATE]
        xg = xs[:, gi * M2_GW:(gi + 1) * M2_GW]
        sg = state[gi * M2_GW:(gi + 1) * M2_GW]
        heads = [(grp0 + gi) * M2_REP + r for r in range(M2_REP)]
        ac_h = [_pick_col(ac_all, h) for h in heads]
        al_h = [a[c - 1:c, :] for a in ac_h]

        def wide(cols):
            return jnp.concatenate([v * ones_l for v in cols], axis=1)

        dt_w = wide([_pick_col(dt_all, h) for h in heads])
        ac_w = wide(ac_h)
        al_w = wide(al_h)
        dsk_w = wide([_pick_col(d_row, h) for h in heads])
        scores = dot_nt(cm, bm)
        m_wide = jnp.concatenate(
            [scores * jnp.exp(jnp.where(causal, a - _pick_row(act_all, h), -1e30)) for a, h in zip(ac_h, heads)], axis=1)
        xdt = xg * dt_w
        x_bd = jnp.concatenate([xdt * lane_head[r] for r in range(M2_REP)], axis=0)
        y_diag = dot_nn(m_wide, x_bd)
        states_new = dot_tn(xdt * jnp.exp(al_w - ac_w), bm)
        y_off = dot_nt(cm, sg) * jnp.exp(ac_w)
        eg_col = jnp.concatenate([jnp.exp(a) * ones_r for a in al_h], axis=0)
        new_states.append(sg * eg_col + states_new)
        y = (y_diag + y_off + dsk_w * xg) * _silu(z[:, gi * M2_GW:(gi + 1) * M2_GW])
        ys.append(y * lax.rsqrt(jnp.mean(y * y, axis=-1, keepdims=True) + NORM_EPS)
                  * normw[:, gi * M2_GW:(gi + 1) * M2_GW])
    return jnp.concatenate(ys, axis=-1), jnp.concatenate(new_states, axis=0)


def _conv_windows2(xin_ref, cw_ref, c):
    acc = None
    for k in range(CONV_K):
        term = cw_ref[pl.ds(k, 1), :] * xin_ref[pl.ds(HALO - CONV_K + 1 + k, c), :]
        acc = term if acc is None else acc + term
    return acc


def _ssd_specs(n, c, rev):
    def ci(i):
        return (n - 1 - i) if rev else i

    def blk(width, unit):
        return pl.BlockSpec((c, width), lambda g, i: (ci(i), unit * LANE // width + g))

    def par(rows_, width, col0):
        return pl.BlockSpec((rows_, width), lambda g, i: (0, col0 // width + g))

    return ci, blk, par


def _ssd_fwd(proj, conv_w, conv_b, alog_row, dtb_row, d_row, normw):
    rows = proj.shape[0]
    c = min(CHUNK, rows)
    n = rows // c
    _, blk, par = _ssd_specs(n, c, False)

    def body(x_ref, b_ref, c_ref, z_ref, sm_ref, cwx, cwb, cwc, bx, bb, bcc, al_ref, dt_ref, d_ref, nw_ref,
             y_ref, ck_ref, s_ref, xx_ref, xb_ref, xc_ref):
        g = pl.program_id(0)
        i = pl.program_id(1)

        @pl.when(i == 0)
        def _():
            s_ref[...] = jnp.zeros_like(s_ref)
            for r in (xx_ref, xb_ref, xc_ref):
                r[0:HALO, :] = jnp.zeros((HALO, r.shape[1]), f32)

        @pl.when(i > 0)
        def _():
            for r in (xx_ref, xb_ref, xc_ref):
                r[0:HALO, :] = r[c:c + HALO, :]

        xx_ref[HALO:, :] = x_ref[...]
        xb_ref[HALO:, :] = b_ref[...]
        xc_ref[HALO:, :] = c_ref[...]
        xc = _conv_windows2(xx_ref, cwx, c)
        bc = _conv_windows2(xb_ref, cwb, c)
        cc = _conv_windows2(xc_ref, cwc, c)
        state = s_ref[...]
        ck_ref[...] = state
        y, new_state = _ssd_step(xc, bc, cc, z_ref[...], sm_ref[...], bx[...], bb[...], bcc[...], al_ref[...],
                                 dt_ref[...], d_ref[...], nw_ref[...], state, g * M2_GB)
        y_ref[...] = y.astype(y_ref.dtype)
        s_ref[...] = new_state

    row = pl.BlockSpec((1, LANE), lambda g, i: (0, 0))
    off_b, off_c = M2_WIDTH, M2_WIDTH + M2_GROUPS * M2_STATE
    return pl.pallas_call(
        body, name="ssd_fwd", grid=(M2_STEPS, n),
        in_specs=[blk(M2_XW, U_CX), blk(M2_BW, U_CB), blk(M2_BW, U_CC), blk(M2_XW, U_CZ),
                  pl.BlockSpec((c, LANE), lambda g, i: (i, U_SMC)),
                  par(CONV_K, M2_XW, 0), par(CONV_K, M2_BW, off_b), par(CONV_K, M2_BW, off_c),
                  par(1, M2_XW, 0), par(1, M2_BW, off_b), par(1, M2_BW, off_c), row, row, row, par(1, M2_XW, 0)],
        out_specs=[pl.BlockSpec((c, M2_XW), lambda g, i: (i, g)),
                   pl.BlockSpec((None, None, M2_SH * M2_HEAD_DIM, M2_STATE), lambda g, i: (g, i, 0, 0))],
        out_shape=[jax.ShapeDtypeStruct((rows, M2_WIDTH), bf16),
                   jax.ShapeDtypeStruct((M2_STEPS, n, M2_SH * M2_HEAD_DIM, M2_STATE), f32)],
        scratch_shapes=[pltpu.VMEM((M2_SH * M2_HEAD_DIM, M2_STATE), f32), pltpu.VMEM((c + HALO, M2_XW), f32),
                        pltpu.VMEM((c + HALO, M2_BW), f32), pltpu.VMEM((c + HALO, M2_BW), f32)],
        compiler_params=_cparams(("arbitrary", "arbitrary")))(
            proj, proj, proj, proj, proj, conv_w, conv_w, conv_w, conv_b, conv_b, conv_b, alog_row, dtb_row, d_row, normw)


def _conv_bwd2(xin_ref, dyext_ref, cw_ref, dxc, dx_ref, dcw_ref, c):
    dyext_ref[0:c, :] = dxc
    acc = None
    for k in range(CONV_K):
        term = cw_ref[pl.ds(k, 1), :] * dyext_ref[pl.ds(CONV_K - 1 - k, c), :]
        acc = term if acc is None else acc + term
        dcw_ref[pl.ds(k, 1), :] += jnp.sum(xin_ref[pl.ds(HALO - CONV_K + 1 + k, c), :] * dxc, axis=0, keepdims=True)
    dx_ref[...] = acc


def _ssd_bwd(proj, dy, ck, conv_w, conv_b, alog_row, dtb_row, d_row, normw):
    rows = proj.shape[0]
    c = min(CHUNK, rows)
    n = rows // c
    halo_blocks = c // HALO
    _, blk, par = _ssd_specs(n, c, True)

    def body(x_ref, b_ref, c_ref, hx_ref, hb_ref, hc_ref, z_ref, sm_ref, cwx, cwb, cwc, bx, bb, bcc,
             al_ref, dt_ref, d_ref, nw_ref, ck_ref, dy_ref,
             dx_ref, db_ref, dc_ref, dz_ref, dsm_ref, dcwx, dcwb, dcwc, dbx, dbb, dbc, dal_ref, ddt_ref, dd_ref, dnw_ref,
             ds_ref, xx_ref, xb_ref, xc_ref, ex_ref, eb_ref, ec_ref):
        g = pl.program_id(0)
        i = pl.program_id(1)
        ci = n - 1 - i

        @pl.when(i == 0)
        def _():
            ds_ref[...] = jnp.zeros_like(ds_ref)
            for r in (ex_ref, eb_ref, ec_ref):
                r[c:c + HALO, :] = jnp.zeros((HALO, r.shape[1]), f32)
            for r in (dcwx, dcwb, dcwc, dbx, dbb, dbc, dal_ref, ddt_ref, dd_ref, dnw_ref):
                r[...] = jnp.zeros_like(r)

        @pl.when(i > 0)
        def _():
            for r in (ex_ref, eb_ref, ec_ref):
                r[c:c + HALO, :] = r[0:HALO, :]

        first = (ci > 0).astype(f32)
        for xin, x_in, halo_in in ((xx_ref, x_ref, hx_ref), (xb_ref, b_ref, hb_ref), (xc_ref, c_ref, hc_ref)):
            xin[0:HALO, :] = halo_in[...] * first
            xin[HALO:, :] = x_in[...]
        xc = _conv_windows2(xx_ref, cwx, c)
        bc = _conv_windows2(xb_ref, cwb, c)
        cc = _conv_windows2(xc_ref, cwc, c)
        fn = functools.partial(_ssd_step, grp0=g * M2_GB)
        _, vjp = jax.vjp(fn, xc, bc, cc, z_ref[...], sm_ref[...], bx[...], bb[...], bcc[...], al_ref[...], dt_ref[...],
                         d_ref[...], nw_ref[...], ck_ref[...])
        (dxc, dbc_, dcc, dz, dsm, gbx, gbb, gbc, dal, ddt, dd, dnw, dstate) = vjp((dy_ref[...].astype(f32), ds_ref[...]))
        ds_ref[...] = dstate
        dz_ref[...] = dz
        dsm_ref[...] = dsm
        dbx[...] += gbx
        dbb[...] += gbb
        dbc[...] += gbc
        dal_ref[...] += dal
        ddt_ref[...] += ddt
        dd_ref[...] += dd
        dnw_ref[...] += dnw
        _conv_bwd2(xx_ref, ex_ref, cwx, dxc, dx_ref, dcwx, c)
        _conv_bwd2(xb_ref, eb_ref, cwb, dbc_, db_ref, dcwb, c)
        _conv_bwd2(xc_ref, ec_ref, cwc, dcc, dc_ref, dcwc, c)

    def halo(width, unit):
        return pl.BlockSpec((HALO, width),
                            lambda g, i: (jnp.maximum((n - 1 - i) * halo_blocks - 1, 0), unit * LANE // width + g))

    row = pl.BlockSpec((1, LANE), lambda g, i: (0, 0))
    grow = pl.BlockSpec((None, 1, LANE), lambda g, i: (g, 0, 0))
    grow_shape = jax.ShapeDtypeStruct((M2_STEPS, 1, LANE), f32)
    ob_w = pl.BlockSpec((c, M2_XW), lambda g, i: (n - 1 - i, g))
    ob_n = pl.BlockSpec((c, M2_BW), lambda g, i: (n - 1 - i, g))
    off_b, off_c = M2_WIDTH, M2_WIDTH + M2_GROUPS * M2_STATE
    bc_w = M2_GROUPS * M2_STATE
    return pl.pallas_call(
        body, name="ssd_bwd", grid=(M2_STEPS, n),
        in_specs=[blk(M2_XW, U_CX), blk(M2_BW, U_CB), blk(M2_BW, U_CC),
                  halo(M2_XW, U_CX), halo(M2_BW, U_CB), halo(M2_BW, U_CC), blk(M2_XW, U_CZ),
                  pl.BlockSpec((c, LANE), lambda g, i: (n - 1 - i, U_SMC)),
                  par(CONV_K, M2_XW, 0), par(CONV_K, M2_BW, off_b), par(CONV_K, M2_BW, off_c),
                  par(1, M2_XW, 0), par(1, M2_BW, off_b), par(1, M2_BW, off_c), row, row, row, par(1, M2_XW, 0),
                  pl.BlockSpec((None, None, M2_SH * M2_HEAD_DIM, M2_STATE), lambda g, i: (g, n - 1 - i, 0, 0)),
                  ob_w],
        out_specs=[ob_w, ob_n, ob_n, ob_w, pl.BlockSpec((None, c, LANE), lambda g, i: (g, n - 1 - i, 0)),
                   par(CONV_K, M2_XW, 0), par(CONV_K, M2_BW, 0), par(CONV_K, M2_BW, 0),
                   par(1, M2_XW, 0), par(1, M2_BW, 0), par(1, M2_BW, 0), grow, grow, grow, par(1, M2_XW, 0)],
        out_shape=[jax.ShapeDtypeStruct((rows, M2_WIDTH), f32), jax.ShapeDtypeStruct((rows, bc_w), f32),
                   jax.ShapeDtypeStruct((rows, bc_w), f32), jax.ShapeDtypeStruct((rows, M2_WIDTH), f32),
                   jax.ShapeDtypeStruct((M2_STEPS, rows, LANE), f32),
                   jax.ShapeDtypeStruct((CONV_K, M2_WIDTH), f32), jax.ShapeDtypeStruct((CONV_K, bc_w), f32),
                   jax.ShapeDtypeStruct((CONV_K, bc_w), f32),
                   jax.ShapeDtypeStruct((1, M2_WIDTH), f32), jax.ShapeDtypeStruct((1, bc_w), f32),
                   jax.ShapeDtypeStruct((1, bc_w), f32), grow_shape, grow_shape, grow_shape,
                   jax.ShapeDtypeStruct((1, M2_WIDTH), f32)],
        scratch_shapes=[pltpu.VMEM((M2_SH * M2_HEAD_DIM, M2_STATE), f32),
                        pltpu.VMEM((c + HALO, M2_XW), f32), pltpu.VMEM((c + HALO, M2_BW), f32), pltpu.VMEM((c + HALO, M2_BW), f32),
                        pltpu.VMEM((c + HALO, M2_XW), f32), pltpu.VMEM((c + HALO, M2_BW), f32), pltpu.VMEM((c + HALO, M2_BW), f32)],
        compiler_params=_cparams(("arbitrary", "arbitrary")))(
            proj, proj, proj, proj, proj, proj, proj, proj, conv_w, conv_w, conv_w, conv_b, conv_b, conv_b,
            alog_row, dtb_row, d_row, normw, ck, dy)


S5_TILE = 128
S5_SW = S5_LANES // S5_BLK


def _scan_down(br, bi, ar, ai):
    t = br.shape[0]
    row = lax.broadcasted_iota(jnp.int32, br.shape, 0)
    d = 1
    while d < t:
        keep = row >= d
        sr = jnp.where(keep, pltpu.roll(br, d, 0), 0.0)
        si = jnp.where(keep, pltpu.roll(bi, d, 0), 0.0)
        br, bi = br + ar * sr - ai * si, bi + ar * si + ai * sr
        ar, ai = ar * ar - ai * ai, 2.0 * ar * ai
        d *= 2
    return br, bi


def _scan_up(br, bi, ar, ai):
    t = br.shape[0]
    row = lax.broadcasted_iota(jnp.int32, br.shape, 0)
    d = 1
    while d < t:
        keep = row < t - d
        sr = jnp.where(keep, pltpu.roll(br, t - d, 0), 0.0)
        si = jnp.where(keep, pltpu.roll(bi, t - d, 0), 0.0)
        br, bi = br + ar * sr - ai * si, bi + ar * si + ai * sr
        ar, ai = ar * ar - ai * ai, 2.0 * ar * ai
        d *= 2
    return br, bi


def _s5_states(u_j, bbr, bbi, ar, ai, cr, ci_):
    br = dot_nn(u_j, bbr)
    bi = dot_nn(u_j, bbi)
    row0 = lax.broadcasted_iota(jnp.int32, br.shape, 0) == 0
    br = br + jnp.where(row0, ar * cr - ai * ci_, 0.0)
    bi = bi + jnp.where(row0, ar * ci_ + ai * cr, 0.0)
    return _scan_down(br, bi, ar, ai)


def _s5_fwd(proj, a_rows, bbr, bbi, ccr, cci, d_row):
    rows = proj.shape[0]
    t = min(S5_TILE, rows)
    n = rows // t

    def body(u_ref, a_ref, bbr_ref, bbi_ref, ccr_ref, cci_ref, d_ref, y_ref, ck_ref, carry_ref):
        i = pl.program_id(0)

        @pl.when(i == 0)
        def _():
            carry_ref[...] = jnp.zeros_like(carry_ref)

        ck_ref[...] = carry_ref[...]
        for j in range(S5_BLK):
            lanes = pl.ds(j * S5_SW, S5_SW)
            ch = pl.ds(j * LANE, LANE)
            u_j = u_ref[:, ch]
            sr, si = _s5_states(u_j, bbr_ref[j], bbi_ref[j], a_ref[0:1, lanes], a_ref[1:2, lanes],
                                carry_ref[0:1, lanes], carry_ref[1:2, lanes])
            y_ref[:, ch] = dot_nn(sr, ccr_ref[j]) - dot_nn(si, cci_ref[j]) + d_ref[:, ch] * u_j
            carry_ref[0:1, lanes] = sr[t - 1:t, :]
            carry_ref[1:2, lanes] = si[t - 1:t, :]

    whole3 = lambda s: pl.BlockSpec(s, lambda i: (0, 0, 0))
    return pl.pallas_call(
        body, name="s5_fwd", grid=(n,),
        in_specs=[pl.BlockSpec((t, S5_WIDTH), lambda i: (i, U_SU // S5_BLK)),
                  pl.BlockSpec((2, S5_LANES), lambda i: (0, 0)),
                  whole3(bbr.shape), whole3(bbi.shape), whole3(ccr.shape), whole3(cci.shape),
                  pl.BlockSpec((1, S5_WIDTH), lambda i: (0, 0))],
        out_specs=[pl.BlockSpec((t, S5_WIDTH), lambda i: (i, 0)),
                   pl.BlockSpec((None, 2, S5_LANES), lambda i: (i, 0, 0))],
        out_shape=[jax.ShapeDtypeStruct((rows, S5_WIDTH), f32), jax.ShapeDtypeStruct((n, 2, S5_LANES), f32)],
        scratch_shapes=[pltpu.VMEM((2, S5_LANES), f32)],
        compiler_params=_cparams(("arbitrary",)))(proj, a_rows, bbr, bbi, ccr, cci, d_row)


def _s5_bwd(proj, dy, ck, a_rows, bbr, bbi, ccr, cci, d_row):
    rows = proj.shape[0]
    t = min(S5_TILE, rows)
    n = rows // t

    def body(u_ref, dy_ref, ck_ref, a_ref, bbr_ref, bbi_ref, ccr_ref, cci_ref, d_ref,
             du_ref, da_ref, dbbr_ref, dbbi_ref, dccr_ref, dcci_ref, dd_ref, lam_ref):
        i = pl.program_id(0)

        @pl.when(i == 0)
        def _():
            lam_ref[...] = jnp.zeros_like(lam_ref)
            for r in (da_ref, dbbr_ref, dbbi_ref, dccr_ref, dcci_ref, dd_ref):
                r[...] = jnp.zeros_like(r)

        for j in range(S5_BLK):
            lanes = pl.ds(j * S5_SW, S5_SW)
            ch = pl.ds(j * LANE, LANE)
            u_j = u_ref[:, ch]
            dy_j = dy_ref[:, ch]
            ar, ai = a_ref[0:1, lanes], a_ref[1:2, lanes]
            cr, ci_ = ck_ref[0:1, lanes], ck_ref[1:2, lanes]
            sr, si = _s5_states(u_j, bbr_ref[j], bbi_ref[j], ar, ai, cr, ci_)
            gr = dot_nt(dy_j, ccr_ref[j])
            gi = -dot_nt(dy_j, cci_ref[j])
            last = lax.broadcasted_iota(jnp.int32, gr.shape, 0) == t - 1
            lr0, li0 = lam_ref[0:1, lanes], lam_ref[1:2, lanes]
            gr = gr + jnp.where(last, ar * lr0 + ai * li0, 0.0)
            gi = gi + jnp.where(last, ar * li0 - ai * lr0, 0.0)
            lr, li = _scan_up(gr, gi, ar, -ai)
            lam_ref[0:1, lanes] = lr[0:1, :]
            lam_ref[1:2, lanes] = li[0:1, :]
            du_ref[:, ch] = dot_nt(lr, bbr_ref[j]) + dot_nt(li, bbi_ref[j]) + d_ref[:, ch] * dy_j
            dbbr_ref[j] += dot_tn(u_j, lr)
            dbbi_ref[j] += dot_tn(u_j, li)
            dccr_ref[j] += dot_tn(sr, dy_j)
            dcci_ref[j] += -dot_tn(si, dy_j)
            dd_ref[:, ch] += jnp.sum(dy_j * u_j, axis=0, keepdims=True)
            row0 = lax.broadcasted_iota(jnp.int32, sr.shape, 0) == 0
            pr = jnp.where(row0, cr, pltpu.roll(sr, 1, 0))
            pi = jnp.where(row0, ci_, pltpu.roll(si, 1, 0))
            da_ref[0:1, lanes] += jnp.sum(lr * pr + li * pi, axis=0, keepdims=True)
            da_ref[1:2, lanes] += jnp.sum(li * pr - lr * pi, axis=0, keepdims=True)

    whole3 = lambda s: pl.BlockSpec(s, lambda i: (0, 0, 0))
    whole2 = lambda s: pl.BlockSpec(s, lambda i: (0, 0))
    return pl.pallas_call(
        body, name="s5_bwd", grid=(n,),
        in_specs=[pl.BlockSpec((t, S5_WIDTH), lambda i: (n - 1 - i, U_SU // S5_BLK)),
                  pl.BlockSpec((t, S5_WIDTH), lambda i: (n - 1 - i, 0)),
                  pl.BlockSpec((None, 2, S5_LANES), lambda i: (n - 1 - i, 0, 0)),
                  whole2((2, S5_LANES)), whole3(bbr.shape), whole3(bbi.shape), whole3(ccr.shape), whole3(cci.shape),
                  whole2((1, S5_WIDTH))],
        out_specs=[pl.BlockSpec((t, S5_WIDTH), lambda i: (n - 1 - i, 0)), whole2((2, S5_LANES)),
                   whole3(bbr.shape), whole3(bbi.shape), whole3(ccr.shape), whole3(cci.shape), whole2((1, S5_WIDTH))],
        out_shape=[jax.ShapeDtypeStruct((rows, S5_WIDTH), f32), jax.ShapeDtypeStruct((2, S5_LANES), f32),
                   jax.ShapeDtypeStruct(bbr.shape, f32), jax.ShapeDtypeStruct(bbi.shape, f32),
                   jax.ShapeDtypeStruct(ccr.shape, f32), jax.ShapeDtypeStruct(cci.shape, f32),
                   jax.ShapeDtypeStruct((1, S5_WIDTH), f32)],
        scratch_shapes=[pltpu.VMEM((2, S5_LANES), f32)],
        compiler_params=_cparams(("arbitrary",)))(proj, dy, ck, a_rows, bbr, bbi, ccr, cci, d_row)


def _s5_prep(lam_re, lam_im, log_step, b_re, b_im, c_re, c_im, d_skip):
    lam_re = jnp.minimum(lam_re, -1e-4)
    step = jnp.exp(log_step)[:, None]
    mag = jnp.exp(lam_re * step)
    ab_re = mag * jnp.cos(lam_im * step)
    ab_im = mag * jnp.sin(lam_im * step)
    den = lam_re * lam_re + lam_im * lam_im
    f_re = ((ab_re - 1.0) * lam_re + ab_im * lam_im) / den
    f_im = (ab_im * lam_re - (ab_re - 1.0) * lam_im) / den
    bb_re = f_re[..., None] * b_re - f_im[..., None] * b_im
    bb_im = f_re[..., None] * b_im + f_im[..., None] * b_re
    eye = jnp.eye(8, dtype=f32)

    def drive(bb):
        r = bb.reshape(S5_BLK, 8, S5_STATE, S5_GROUP_SIZE).transpose(0, 1, 3, 2)
        return (r[:, :, :, None, :] * eye[None, :, None, :, None]).reshape(S5_BLK, LANE, S5_SW)

    def readout(cc):
        r = cc.reshape(S5_BLK, 8, S5_GROUP_SIZE, S5_STATE).transpose(0, 1, 3, 2)
        return (r[:, :, :, None, :] * eye[None, :, None, :, None]).reshape(S5_BLK, S5_SW, LANE)

    a_rows = jnp.stack([ab_re.reshape(S5_LANES), ab_im.reshape(S5_LANES)])
    return a_rows, drive(bb_re), drive(bb_im), readout(c_re), readout(c_im), d_skip.reshape(1, S5_WIDTH)


def _adam_math(w, g, m, v):
    m = ADAM_B1 * m + (1.0 - ADAM_B1) * g
    v = ADAM_B2 * v + (1.0 - ADAM_B2) * (g * g)
    m_hat = m / (1.0 - ADAM_B1 ** ADAM_STEP)
    v_hat = v / (1.0 - ADAM_B2 ** ADAM_STEP)
    delta = -ADAM_LR * (m_hat / (jnp.sqrt(v_hat) + ADAM_EPS) + ADAM_WD * w)
    return delta, m, v


def _adamw(name, w, g, m, v):
    rows, width = w.shape
    t = rows
    for cand in (512, 256, 128, 64, 32, 16, 8):
        if rows % cand == 0 and cand * width * 4 * 7 * 2 <= VMEM_LIMIT // 2:
            t = cand
            break

    def body(w_ref, g_ref, m_ref, v_ref, d_ref, nm_ref, nv_ref):
        d, nm, nv = _adam_math(w_ref[...], g_ref[...], m_ref[...], v_ref[...])
        d_ref[...] = d
        nm_ref[...] = nm
        nv_ref[...] = nv

    spec = pl.BlockSpec((t, width), lambda i: (i, 0))
    shape = jax.ShapeDtypeStruct((rows, width), f32)
    return pl.pallas_call(body, name=name, grid=(rows // t,), in_specs=[spec] * 4, out_specs=[spec] * 3,
                          out_shape=[shape] * 3, compiler_params=_cparams(("parallel",)))(w, g, m, v)


def _adamw_slots(name, recvs, row_off, w, m, v):
    depth, rows, width = w.shape
    per_row = width * (depth * N_DEV * recvs[0].dtype.itemsize + 7 * 4) * 2
    t = next(cand for cand in (256, 128, 96, 64, 32, 16)
             if rows % cand == 0 and row_off % cand == 0 and cand * per_row <= VMEM_LIMIT * 2 // 3)
    first = row_off // t

    def body(*refs):
        r_refs = refs[:depth]
        w_ref, m_ref, v_ref, g_ref, d_ref, nm_ref, nv_ref = refs[depth:]
        layer = pl.program_id(0)
        for l, r_ref in enumerate(r_refs):
            @pl.when(layer == l)
            def _():
                g = r_ref[0].astype(f32)
                for s in range(1, N_DEV):
                    g = g + r_ref[s].astype(f32)
                d, nm, nv = _adam_math(w_ref[...], g, m_ref[...], v_ref[...])
                g_ref[...] = g
                d_ref[...] = d
                nm_ref[...] = nm
                nv_ref[...] = nv

    def recv_spec(l):
        return pl.BlockSpec((N_DEV, t, width), lambda layer, i: (0, first + jnp.where(layer == l, i, 0), 0))

    spec = pl.BlockSpec((None, t, width), lambda layer, i: (layer, i, 0))
    shape = jax.ShapeDtypeStruct((depth, rows, width), f32)
    return pl.pallas_call(
        body, name=name, grid=(depth, rows // t), in_specs=[recv_spec(l) for l in range(depth)] + [spec, spec, spec],
        out_specs=[spec] * 4, out_shape=[shape] * 4,
        compiler_params=_cparams(("arbitrary", "arbitrary")))(*recvs, w, m, v)


def _sum_slots(name, buf):
    _, rows, width = buf.shape
    t = next(cand for cand in (512, 256, 128, 64, 32, 16) if rows % cand == 0)

    def body(b_ref, o_ref):
        acc = b_ref[0].astype(f32)
        for s in range(1, N_DEV):
            acc = acc + b_ref[s].astype(f32)
        o_ref[...] = acc

    return pl.pallas_call(
        body, name=name, grid=(rows // t,), in_specs=[pl.BlockSpec((N_DEV, t, width), lambda i: (0, i, 0))],
        out_specs=pl.BlockSpec((t, width), lambda i: (i, 0)), out_shape=jax.ShapeDtypeStruct((rows, width), f32),
        compiler_params=_cparams(("parallel",)))(buf)


BIG = ("w_in", "s5_glu_w", "proj_a", "proj_b", "proj_c", "w_out")
CONV = ("gdn_conv_w", "m2_conv_w")
SMALL = ("norm_w", "gdn_a_log", "gdn_dt_bias", "gdn_norm_w", "s5_lam_re", "s5_lam_im", "s5_log_step",
         "s5_b_re", "s5_b_im", "s5_c_re", "s5_c_im", "s5_d", "s5_glu_b", "m2_conv_b", "m2_a_log", "m2_dt_bias",
         "m2_d", "m2_norm_w", "final_norm_w")
WEIGHTS = ("norm_w", "w_in", "gdn_conv_w", "gdn_a_log", "gdn_dt_bias", "gdn_norm_w", "s5_lam_re", "s5_lam_im",
           "s5_log_step", "s5_b_re", "s5_b_im", "s5_c_re", "s5_c_im", "s5_d", "s5_glu_w", "s5_glu_b", "m2_conv_w",
           "m2_conv_b", "m2_a_log", "m2_dt_bias", "m2_d", "m2_norm_w", "proj_a", "proj_b", "proj_c", "w_out",
           "final_norm_w")


def _pack(arrays, width, dtype):
    flat = jnp.concatenate([a.reshape(-1).astype(dtype) for a in arrays])
    unit = width * 16
    pad = (-flat.shape[0]) % unit
    if pad:
        flat = jnp.concatenate([flat, jnp.zeros((pad,), dtype)])
    return flat.reshape(-1, width)


def _unpack(flat, shapes):
    out, off = [], 0
    for s in shapes:
        size = 1
        for d in s:
            size *= d
        out.append(flat[off:off + size].reshape(s))
        off += size
    return out


def _win_to_padded(w):
    d = w.shape[0]
    z = lambda n: jnp.zeros((d, n), w.dtype)
    return jnp.concatenate([w[:, 8736:14880], w[:, 4112:5648], w[:, 0:4096], w[:, 5648:8720],
                            w[:, 4096:4112], z(LANE - 16), w[:, 8720:8736], z(LANE - 16), z(2 * LANE)], axis=1)


def _win_from_padded(g):
    u = LANE
    return jnp.concatenate([g[:, U_Q * u:U_CZ * u], g[:, U_SMA * u:U_SMA * u + 16], g[:, U_SU * u:U_Q * u],
                            g[:, U_CZ * u:U_SMA * u], g[:, U_SMC * u:U_SMC * u + 16], g[:, 0:U_SU * u]], axis=1)


def _lane_row(vals, offset):
    n = vals.shape[0]
    return jnp.concatenate([jnp.zeros((offset,), f32), vals, jnp.zeros((LANE - offset - n,), f32)]).reshape(1, LANE)


def _layer_fwd(x, lw, next_shards):
    h = _tok_fwd("rms_fwd", _rms_fn, [(x, D_MODEL, 0)], [lw["norm_w"].reshape(1, D_MODEL)], [(D_MODEL, bf16)], 256)[0]
    if next_shards is None:
        proj, gathered = _mm("proj_fwd", h, lw["w_in_p"], "nn"), None
    else:
        proj, *gathered = _mm("proj_fwd_gather", h, lw["w_in_p"], "nn", comm=(next_shards, True))
    g_al, g_dt = _lane_row(lw["gdn_a_log"], GDN_HEADS), _lane_row(lw["gdn_dt_bias"], GDN_HEADS)
    g_nw = lw["gdn_norm_w"].reshape(1, HEAD_DIM)
    y_a, ck_a = _gdn_fwd(proj, lw["gdn_conv_w"], g_al, g_dt, g_nw)
    s5p = _s5_prep(lw["s5_lam_re"], lw["s5_lam_im"], lw["s5_log_step"], lw["s5_b_re"], lw["s5_b_im"],
                   lw["s5_c_re"], lw["s5_c_im"], lw["s5_d"])
    y_pre, ck_b = _s5_fwd(proj, *s5p)
    glu_b = lw["s5_glu_b"].reshape(1, S5_WIDTH)
    y_b = _tok_fwd("s5_tail_fwd", _s5_tail_fn, [(y_pre, S5_WIDTH, 0), (proj, S5_WIDTH, U_SG // S5_BLK)],
                   [lw["s5_glu_w"], glu_b], [(S5_WIDTH, bf16)], 256)[0]
    m_al, m_dt, m_d = _lane_row(lw["m2_a_log"], 0), _lane_row(lw["m2_dt_bias"], 0), _lane_row(lw["m2_d"], 0)
    m_cb = lw["m2_conv_b"].reshape(1, -1)
    m_nw = lw["m2_norm_w"].reshape(1, M2_WIDTH)
    y_c, ck_c = _ssd_fwd(proj, lw["m2_conv_w"], m_cb, m_al, m_dt, m_d, m_nw)
    pa = _mm("proj_a_fwd", y_a, lw["proj_a"], "nn")
    pb = _mm("proj_b_fwd", y_b, lw["proj_b"], "nn")
    pc = _mm("proj_c_fwd", y_c, lw["proj_c"], "nn")
    gate_acts = [(proj, D_MODEL, 0), (proj, D_MODEL, 1), (proj, D_MODEL, 2),
                 (pa, D_MODEL, 0), (pb, D_MODEL, 0), (pc, D_MODEL, 0)]
    merged = _tok_fwd("gate_fwd", _gate_fn, gate_acts, [], [(D_MODEL, bf16)], 128)[0]
    x_next = _mm("w_out_fwd", merged, lw["w_out"], "nn", residual=x)
    saved = dict(x=x, h=h, proj=proj, y_a=y_a, ck_a=ck_a, y_pre=y_pre, ck_b=ck_b, y_b=y_b, y_c=y_c, ck_c=ck_c,
                 pa=pa, pb=pb, pc=pc, merged=merged)
    return x_next, saved, gathered


def _layer_bwd(dx_next, lw, sv, later_grads):
    rows = dx_next.shape[0]
    proj = sv["proj"]
    g = {}
    d_merged = _mm("w_out_bwd_x", dx_next, lw["w_out"], "nt", out_dtype=bf16)
    g["w_out"] = _mm("w_out_bwd_w", sv["merged"], dx_next, "tn", out_dtype=bf16)
    gate_acts = [(proj, D_MODEL, 0), (proj, D_MODEL, 1), (proj, D_MODEL, 2),
                 (sv["pa"], D_MODEL, 0), (sv["pb"], D_MODEL, 0), (sv["pc"], D_MODEL, 0)]
    dla, dlb, dlc, dpa, dpb, dpc = _tok_bwd(
        "gate_bwd", _gate_fn, gate_acts, [], [(d_merged, D_MODEL, 0)],
        [(0, f32), (1, f32), (2, f32), (3, bf16), (4, bf16), (5, bf16)], 128)
    dy_a = _mm("proj_a_bwd_x", dpa, lw["proj_a"], "nt", out_dtype=bf16)
    dy_b = _mm("proj_b_bwd_x", dpb, lw["proj_b"], "nt")
    dy_c = _mm("proj_c_bwd_x", dpc, lw["proj_c"], "nt", out_dtype=bf16)
    g["proj_a"] = _mm("proj_a_bwd_w", sv["y_a"], dpa, "tn", out_dtype=bf16)
    g["proj_b"] = _mm("proj_b_bwd_w", sv["y_b"], dpb, "tn", out_dtype=bf16)
    g["proj_c"] = _mm("proj_c_bwd_w", sv["y_c"], dpc, "tn", out_dtype=bf16)

    m_al, m_dt, m_d = _lane_row(lw["m2_a_log"], 0), _lane_row(lw["m2_dt_bias"], 0), _lane_row(lw["m2_d"], 0)
    m_cb = lw["m2_conv_b"].reshape(1, -1)
    m_nw = lw["m2_norm_w"].reshape(1, M2_WIDTH)
    (dcx, dcb, dcc, dcz, dsmc, dcwx, dcwb, dcwc, dbx, dbb, dbc, dal, ddt, ddk, dnw) = _ssd_bwd(
        proj, dy_c, sv["ck_c"], lw["m2_conv_w"], m_cb, m_al, m_dt, m_d, m_nw)
    g["m2_conv_w"] = jnp.concatenate([dcwx, dcwb, dcwc], axis=1)
    g["m2_conv_b"] = jnp.concatenate([dbx, dbb, dbc], axis=1).reshape(-1)
    g["m2_a_log"] = jnp.sum(dal, axis=(0, 1))[:M2_HEADS]
    g["m2_dt_bias"] = jnp.sum(ddt, axis=(0, 1))[:M2_HEADS]
    g["m2_d"] = jnp.sum(ddk, axis=(0, 1))[:M2_HEADS]
    g["m2_norm_w"] = dnw.reshape(-1)
    dsmc = jnp.sum(dsmc, axis=0)

    glu_b = lw["s5_glu_b"].reshape(1, S5_WIDTH)
    dypre, dsg, dglu_w, dglu_b = _tok_bwd(
        "s5_tail_bwd", _s5_tail_fn, [(sv["y_pre"], S5_WIDTH, 0), (proj, S5_WIDTH, U_SG // S5_BLK)],
        [lw["s5_glu_w"], glu_b], [(dy_b, S5_WIDTH, 0)], [(0, f32), (1, f32)], 256)
    g["s5_glu_w"] = dglu_w
    g["s5_glu_b"] = dglu_b.reshape(-1)
    s5_names = ("s5_lam_re", "s5_lam_im", "s5_log_step", "s5_b_re", "s5_b_im", "s5_c_re", "s5_c_im", "s5_d")
    s5p, s5_vjp = jax.vjp(_s5_prep, *[lw[k] for k in s5_names])
    dsu, da, dbbr, dbbi, dccr, dcci, dd = _s5_bwd(proj, dypre, sv["ck_b"], *s5p)
    for k, val in zip(s5_names, s5_vjp((da, dbbr, dbbi, dccr, dcci, dd))):
        g[k] = val

    g_al, g_dt = _lane_row(lw["gdn_a_log"], GDN_HEADS), _lane_row(lw["gdn_dt_bias"], GDN_HEADS)
    g_nw = lw["gdn_norm_w"].reshape(1, HEAD_DIM)
    (dq, dk, dv, daz, dsma, dcwq, dcwk, dcwv, dgal, dgdt, dgnw) = _gdn_bwd(
        proj, dy_a, sv["ck_a"], lw["gdn_conv_w"], g_al, g_dt, g_nw)
    g["gdn_conv_w"] = jnp.concatenate([dcwq, dcwk, dcwv], axis=1)
    g["gdn_a_log"] = jnp.sum(dgal, axis=(0, 1))[GDN_HEADS:2 * GDN_HEADS]
    g["gdn_dt_bias"] = jnp.sum(dgdt, axis=(0, 1))[GDN_HEADS:2 * GDN_HEADS]
    g["gdn_norm_w"] = jnp.sum(dgnw, axis=(0, 1))
    dsma = jnp.sum(dsma, axis=0)

    dproj = jnp.concatenate([dla, dlb, dlc, dsu, dsg, dq, dk, dv, daz, dcz, dcx, dcb, dcc, dsma, dsmc,
                             jnp.zeros((rows, 2 * LANE), f32)], axis=1)
    if later_grads is None:
        g["w_in_p"], arrived = _mm("proj_bwd_w", sv["h"], dproj, "tn", out_dtype=bf16), None
    else:
        g["w_in_p"], *arrived = _mm("proj_bwd_w_scatter", sv["h"], dproj, "tn", out_dtype=bf16,
                                    comm=(later_grads, False))
    dh = _mm("proj_bwd_x", dproj, lw["w_in_p"], "nt")
    dx, dnorm = _tok_bwd("rms_bwd", _rms_fn, [(sv["x"], D_MODEL, 0)], [lw["norm_w"].reshape(1, D_MODEL)],
                         [(dh, D_MODEL, 0)], [(0, f32)], 256, residuals={0: dx_next})
    g["norm_w"] = dnorm.reshape(-1)
    return dx, g, arrived


def kernel(x, norm_w, w_in, gdn_conv_w, gdn_a_log, gdn_dt_bias, gdn_norm_w, s5_lam_re, s5_lam_im, s5_log_step, s5_b_re, s5_b_im, s5_c_re, s5_c_im, s5_d, s5_glu_w, s5_glu_b, m2_conv_w, m2_conv_b, m2_a_log, m2_dt_bias, m2_d, m2_norm_w, proj_a, proj_b, proj_c, w_out, final_norm_w, loss_target, m_norm_w, m_w_in, m_gdn_conv_w, m_gdn_a_log, m_gdn_dt_bias, m_gdn_norm_w, m_s5_lam_re, m_s5_lam_im, m_s5_log_step, m_s5_b_re, m_s5_b_im, m_s5_c_re, m_s5_c_im, m_s5_d, m_s5_glu_w, m_s5_glu_b, m_m2_conv_w, m_m2_conv_b, m_m2_a_log, m_m2_dt_bias, m_m2_d, m_m2_norm_w, m_proj_a, m_proj_b, m_proj_c, m_w_out, m_final_norm_w, v_norm_w, v_w_in, v_gdn_conv_w, v_gdn_a_log, v_gdn_dt_bias, v_gdn_norm_w, v_s5_lam_re, v_s5_lam_im, v_s5_log_step, v_s5_b_re, v_s5_b_im, v_s5_c_re, v_s5_c_im, v_s5_d, v_s5_glu_w, v_s5_glu_b, v_m2_conv_w, v_m2_conv_b, v_m2_a_log, v_m2_dt_bias, v_m2_d, v_m2_norm_w, v_proj_a, v_proj_b, v_proj_c, v_w_out, v_final_norm_w):
    w = dict(norm_w=norm_w, w_in=w_in, gdn_conv_w=gdn_conv_w, gdn_a_log=gdn_a_log, gdn_dt_bias=gdn_dt_bias,
             gdn_norm_w=gdn_norm_w, s5_lam_re=s5_lam_re, s5_lam_im=s5_lam_im, s5_log_step=s5_log_step,
             s5_b_re=s5_b_re, s5_b_im=s5_b_im, s5_c_re=s5_c_re, s5_c_im=s5_c_im, s5_d=s5_d, s5_glu_w=s5_glu_w,
             s5_glu_b=s5_glu_b, m2_conv_w=m2_conv_w, m2_conv_b=m2_conv_b, m2_a_log=m2_a_log, m2_dt_bias=m2_dt_bias,
             m2_d=m2_d, m2_norm_w=m2_norm_w, proj_a=proj_a, proj_b=proj_b, proj_c=proj_c, w_out=w_out,
             final_norm_w=final_norm_w)
    mom = dict(norm_w=m_norm_w, w_in=m_w_in, gdn_conv_w=m_gdn_conv_w, gdn_a_log=m_gdn_a_log,
               gdn_dt_bias=m_gdn_dt_bias, gdn_norm_w=m_gdn_norm_w, s5_lam_re=m_s5_lam_re, s5_lam_im=m_s5_lam_im,
               s5_log_step=m_s5_log_step, s5_b_re=m_s5_b_re, s5_b_im=m_s5_b_im, s5_c_re=m_s5_c_re,
               s5_c_im=m_s5_c_im, s5_d=m_s5_d, s5_glu_w=m_s5_glu_w, s5_glu_b=m_s5_glu_b, m2_conv_w=m_m2_conv_w,
               m2_conv_b=m_m2_conv_b, m2_a_log=m_m2_a_log, m2_dt_bias=m_m2_dt_bias, m2_d=m_m2_d,
               m2_norm_w=m_m2_norm_w, proj_a=m_proj_a, proj_b=m_proj_b, proj_c=m_proj_c, w_out=m_w_out,
               final_norm_w=m_final_norm_w)
    var = dict(norm_w=v_norm_w, w_in=v_w_in, gdn_conv_w=v_gdn_conv_w, gdn_a_log=v_gdn_a_log,
               gdn_dt_bias=v_gdn_dt_bias, gdn_norm_w=v_gdn_norm_w, s5_lam_re=v_s5_lam_re, s5_lam_im=v_s5_lam_im,
               s5_log_step=v_s5_log_step, s5_b_re=v_s5_b_re, s5_b_im=v_s5_b_im, s5_c_re=v_s5_c_re,
               s5_c_im=v_s5_c_im, s5_d=v_s5_d, s5_glu_w=v_s5_glu_w, s5_glu_b=v_s5_glu_b, m2_conv_w=v_m2_conv_w,
               m2_conv_b=v_m2_conv_b, m2_a_log=v_m2_a_log, m2_dt_bias=v_m2_dt_bias, m2_d=v_m2_d,
               m2_norm_w=v_m2_norm_w, proj_a=v_proj_a, proj_b=v_proj_b, proj_c=v_proj_c, w_out=v_w_out,
               final_norm_w=v_final_norm_w)
    me = lax.axis_index("x") * 4 + lax.axis_index("y") * 2 + lax.axis_index("c")
    x2 = x[0]
    tgt = loss_target[0]

    ra, rb = proj_a.shape[1], proj_b.shape[1]
    pabc = jnp.concatenate([proj_a, proj_b, proj_c], axis=1).astype(bf16)
    w_in16, glu16, w_out16 = w_in.astype(bf16), s5_glu_w.astype(bf16), w_out.astype(bf16)

    def shards(i):
        return [w_in16[i], glu16[i], pabc[i], w_out16[i], gdn_conv_w[i], m2_conv_w[i]]

    def cols(gathered):
        return jnp.concatenate([gathered[d] for d in range(N_DEV)], axis=1)

    def rows_of(gathered):
        return gathered.reshape(-1, gathered.shape[-1])

    def full_weights(i, gathered):
        g_win, g_glu, g_pabc, g_wout, g_gcv, g_mcv = gathered
        pf = cols(g_pabc)
        lw = dict(w_in_p=_win_to_padded(cols(g_win)), s5_glu_w=rows_of(g_glu), proj_a=pf[:ra],
                  proj_b=pf[ra:ra + rb], proj_c=pf[ra + rb:], w_out=rows_of(g_wout),
                  gdn_conv_w=cols(g_gcv), m2_conv_w=cols(g_mcv))
        for k in SMALL:
            if k != "final_norm_w":
                lw[k] = w[k][i]
        return lw

    layers, saved = [], []
    act = x2
    gathered = _exchange("gather_weights", shards(0), True)
    for i in range(DEPTH):
        layers.append(full_weights(i, gathered))
        act, sv, gathered = _layer_fwd(act, layers[i], shards(i + 1) if i + 1 < DEPTH else None)
        saved.append(sv)
    loss_row, dact, dfinal = _loss_grad(act, final_norm_w.reshape(1, D_MODEL), tgt)
    loss = lax.psum(loss_row[0, 0], ("x", "y", "c"))

    def col_blocks(full):
        r = full.shape[0]
        return full.reshape(r, N_DEV, -1).transpose(1, 0, 2)

    def row_blocks(full):
        return full.reshape(N_DEV, -1, full.shape[1])

    def packed_grads(g):
        return [col_blocks(_win_from_padded(g["w_in_p"])), row_blocks(g["s5_glu_w"].astype(bf16)),
                col_blocks(jnp.concatenate([g["proj_a"], g["proj_b"], g["proj_c"]], axis=0)), row_blocks(g["w_out"])]

    grads, arrived = [None] * DEPTH, [None] * DEPTH
    later = None
    for i in reversed(range(DEPTH)):
        dact, grads[i], got = _layer_bwd(dact, layers[i], saved[i], later)
        if got is not None:
            arrived[i + 1] = got
        saved[i] = None
        later = packed_grads(grads[i])
    arrived[0] = _exchange("scatter_grads", later, False)

    g_out, delta, new_m, new_v = {}, {}, {}, {}
    for k, buf, row_off in (("w_in", 0, 0), ("s5_glu_w", 1, 0), ("proj_a", 2, 0), ("proj_b", 2, ra),
                            ("proj_c", 2, ra + rb), ("w_out", 3, 0)):
        recvs = [arrived[i][buf] for i in range(DEPTH)]
        g_out[k], delta[k], new_m[k], new_v[k] = _adamw_slots("adamw_" + k, recvs, row_off, w[k], mom[k], var[k])

    names_small = [k for k in SMALL if k != "final_norm_w"]
    small_parts = [jnp.stack([grads[i][k] for i in range(DEPTH)]) for k in names_small + list(CONV)]
    small_parts.append(dfinal.reshape(-1))
    small_shapes = [p.shape for p in small_parts]
    small_local = _pack(small_parts, LANE, f32)
    small_sum = _sum_slots("sum_small", _exchange("gather_small", [small_local], True)[0]).reshape(-1)
    unp = _unpack(small_sum, small_shapes)
    for k, val in zip(names_small, unp):
        g_out[k] = val
    for j, k in enumerate(CONV):
        width = w[k].shape[2]
        g_out[k] = lax.dynamic_slice_in_dim(unp[len(names_small) + j], me * width, width, axis=2)
    g_out["final_norm_w"] = unp[-1]

    rest = [k for k in WEIGHTS if k not in BIG]
    rest_shapes = [w[k].shape for k in rest]
    packed = [_pack([src[k] for k in rest], LANE, f32) for src in (w, g_out, mom, var)]
    d, nm, nv = _adamw("adamw_small", *packed)
    for dst, arr in ((delta, d), (new_m, nm), (new_v, nv)):
        for k, val in zip(rest, _unpack(arr.reshape(-1), rest_shapes)):
            dst[k] = val

    grad_x = dact.reshape(x.shape)
    return (loss, grad_x, *[g_out[k] for k in WEIGHTS], *[delta[k] for k in WEIGHTS],
            *[new_m[k] for k in WEIGHTS], *[new_v[k] for k in WEIGHTS])
```

```python
import functools

import jax
import jax.numpy as jnp
from jax import lax
from jax.experimental import pallas as pl
from jax.experimental.pallas import tpu as pltpu

f32 = jnp.float32
bf16 = jnp.bfloat16

N_DEV = 8
DEPTH = 4
D_MODEL = 2048
GDN_HEADS = 8
HEAD_DIM = 128
GDN_WIDTH = 1024
S5_GROUPS = 48
S5_GROUP_SIZE = 16
S5_STATE = 64
S5_WIDTH = 768
S5_LANES = S5_GROUPS * S5_STATE
S5_BLK = 6
M2_HEADS = 16
M2_HEAD_DIM = 64
M2_WIDTH = 1024
M2_GROUPS = 4
M2_STATE = 128
CONV_K = 4
CHUNK = 64
HALO = 8
NORM_EPS = 1e-6
IN_DIM = 14880
LANE = 128
VMEM_LIMIT = 48 * 1024 * 1024

ADAM_LR = 0.001
ADAM_B1 = 0.9
ADAM_B2 = 0.999
ADAM_EPS = 1e-08
ADAM_WD = 0.01
ADAM_STEP = 10

U_MERGE, U_SU, U_SG, U_Q, U_K, U_V, U_AZ, U_CZ, U_CX, U_CB, U_CC, U_SMA, U_SMC = (
    0, 48, 54, 60, 68, 76, 84, 92, 100, 108, 112, 116, 117)
NP_UNITS = 120
NP_COLS = NP_UNITS * LANE


def _cparams(sem=None):
    return pltpu.CompilerParams(dimension_semantics=sem, vmem_limit_bytes=VMEM_LIMIT)


def _pick(dim, target):
    if dim <= target:
        return dim
    for t in range(target - target % LANE, 0, -LANE):
        if dim % t == 0:
            return t
    raise ValueError(f"no tile for {dim}")


def _bd(a, b, dims):
    return lax.dot_general(a.astype(bf16), b.astype(bf16), (dims, ((), ())), preferred_element_type=f32)


@jax.custom_vjp
def dot_nn(a, b):
    return _bd(a, b, ((1,), (0,)))


@jax.custom_vjp
def dot_nt(a, b):
    return _bd(a, b, ((1,), (1,)))


@jax.custom_vjp
def dot_tn(a, b):
    return _bd(a, b, ((0,), (0,)))


dot_nn.defvjp(lambda a, b: (dot_nn(a, b), (a, b)), lambda r, ct: (dot_nt(ct, r[1]), dot_tn(r[0], ct)))
dot_nt.defvjp(lambda a, b: (dot_nt(a, b), (a, b)), lambda r, ct: (dot_nn(ct, r[1]), dot_tn(ct, r[0])))
dot_tn.defvjp(lambda a, b: (dot_tn(a, b), (a, b)), lambda r, ct: (dot_nt(r[1], ct), dot_nn(r[0], ct)))

_HI = lax.Precision.HIGHEST


def _silu(x):
    return x * jax.nn.sigmoid(x)


def _softplus(x):
    return jnp.maximum(x, 0.0) + jnp.log(1.0 + jnp.exp(-jnp.abs(x)))


def _tri_masks(c):
    row = lax.broadcasted_iota(jnp.int32, (c, c), 0)
    col = lax.broadcasted_iota(jnp.int32, (c, c), 1)
    return row >= col, row > col, (row >= col).astype(f32), (row == col).astype(f32)


def _scan_add(x, reverse):
    t = x.shape[0]
    row = lax.broadcasted_iota(jnp.int32, x.shape, 0)
    d = 1
    while d < t:
        if reverse:
            x = x + jnp.where(row < t - d, pltpu.roll(x, t - d, 0), 0.0)
        else:
            x = x + jnp.where(row >= d, pltpu.roll(x, d, 0), 0.0)
        d *= 2
    return x


@jax.custom_vjp
def _cumsum_rows(x):
    return _scan_add(x, False)


_cumsum_rows.defvjp(lambda x: (_scan_add(x, False), None), lambda _, ct: (_scan_add(ct, True),))


def _cumsum_all(a):
    cum = _cumsum_rows(a)
    return cum, cum.T


def _onehot_lane(idx):
    return (lax.broadcasted_iota(jnp.int32, (1, LANE), 1) == idx).astype(f32)


def _onehot_sub(idx):
    return (lax.broadcasted_iota(jnp.int32, (LANE, 1), 0) == idx).astype(f32)


def _pick_col(x, idx):
    return jnp.sum(x * _onehot_lane(idx), axis=-1, keepdims=True)


def _pick_row(xt, idx):
    return jnp.sum(xt * _onehot_sub(idx), axis=0, keepdims=True)


def _peer(idx):
    return (idx // 4, (idx // 2) % 2, idx % 2)


def _comm_copies(srcs, outs, send_sems, recv_sems, local_sems, gather, landing=True):
    nb = len(srcs)
    me = lax.axis_index("x") * 4 + lax.axis_index("y") * 2 + lax.axis_index("c")
    own = [s if gather else s.at[me] for s in srcs]
    mine = [pltpu.make_async_copy(own[b], outs[b].at[me], local_sems.at[b]) for b in range(nb)]
    sends, lands = [], []
    for k in range(1, N_DEV):
        to = (me + k) % N_DEV
        frm = (me + N_DEV - k) % N_DEV
        for b in range(nb):
            sems = dict(send_sem=send_sems.at[(k - 1) * nb + b], recv_sem=recv_sems.at[(k - 1) * nb + b],
                        device_id_type=pl.DeviceIdType.MESH)
            sends.append(pltpu.make_async_remote_copy(src_ref=srcs[b] if gather else srcs[b].at[to],
                                                      dst_ref=outs[b].at[me], device_id=_peer(to), **sems))
            if landing:
                lands.append(pltpu.make_async_remote_copy(src_ref=own[b], dst_ref=outs[b].at[frm],
                                                          device_id=_peer(frm), **sems))
    return mine, sends, lands


def _comm_start(*refs, gather):
    mine, sends, _ = _comm_copies(*refs, gather, landing=False)
    for cp in mine + sends:
        cp.start()


def _comm_wait(*refs, gather):
    mine, sends, lands = _comm_copies(*refs, gather)
    for cp in lands:
        cp.wait_recv()
    for cp in sends:
        cp.wait_send()
    for cp in mine:
        cp.wait()


def _comm_shapes(bufs, gather):
    nb = len(bufs)
    out_shape = [jax.ShapeDtypeStruct(((N_DEV,) + tuple(b.shape)) if gather else tuple(b.shape), b.dtype) for b in bufs]
    sems = [pltpu.SemaphoreType.DMA(((N_DEV - 1) * nb,)), pltpu.SemaphoreType.DMA(((N_DEV - 1) * nb,)),
            pltpu.SemaphoreType.DMA((nb,))]
    return out_shape, sems


def _exchange(name, bufs, gather):
    nb = len(bufs)

    def body(*refs):
        args = (refs[:nb], refs[nb:2 * nb], *refs[2 * nb:])
        _comm_start(*args, gather=gather)
        _comm_wait(*args, gather=gather)

    any_spec = pl.BlockSpec(memory_space=pl.ANY)
    out_shape, sems = _comm_shapes(bufs, gather)
    return pl.pallas_call(body, name=name, in_specs=[any_spec] * nb, out_specs=[any_spec] * nb, out_shape=out_shape,
                          scratch_shapes=sems)(*bufs)


def _mm(name, a, b, mode, residual=None, out_dtype=f32, tm=1024, tn=1024, tk=1024, comm=None):
    if mode == "nn":
        (m, k), (_, n) = a.shape, b.shape
    elif mode == "nt":
        (m, k), (n, _) = a.shape, b.shape
    else:
        (k, m), (_, n) = a.shape, b.shape
    tm, tn, tk = _pick(m, tm), _pick(n, tn), _pick(k, tk)
    gm, gn, nk = m // tm, n // tn, k // tk
    dims = {"nn": ((1,), (0,)), "nt": ((1,), (1,)), "tn": ((0,), (0,))}[mode]
    has_res = residual is not None
    bufs, gather = comm if comm is not None else ([], True)
    nb = len(bufs)
    n_in = 2 + has_res

    def body(*refs):
        a_ref, b_ref = refs[:2]
        r_ref = refs[2] if has_res else None
        srcs = refs[n_in:n_in + nb]
        o_ref = refs[n_in + nb]
        outs = refs[n_in + nb + 1:n_in + 2 * nb + 1]
        scratch = refs[n_in + 2 * nb + 1:]
        acc_ref = scratch[0] if nk > 1 else None
        sems = scratch[1:] if nk > 1 else scratch
        i, j, kk = pl.program_id(0), pl.program_id(1), pl.program_id(2)

        if nb:
            @pl.when((i == 0) & (j == 0) & (kk == 0))
            def _():
                _comm_start(srcs, outs, *sems, gather=gather)

        def finish(out):
            if has_res:
                out = out + r_ref[...].astype(f32)
            o_ref[...] = out.astype(o_ref.dtype)

        if nk == 1:
            finish(_bd(a_ref[...], b_ref[...], dims))
        else:
            @pl.when(kk == 0)
            def _():
                acc_ref[...] = jnp.zeros_like(acc_ref)

            acc_ref[...] += _bd(a_ref[...], b_ref[...], dims)

            @pl.when(kk == nk - 1)
            def _():
                finish(acc_ref[...])

        if nb:
            @pl.when((i == gm - 1) & (j == gn - 1) & (kk == nk - 1))
            def _():
                _comm_wait(srcs, outs, *sems, gather=gather)

    if mode == "tn":
        a_spec = pl.BlockSpec((tk, tm), lambda i, j, kk: (kk, i))
    else:
        a_spec = pl.BlockSpec((tm, tk), lambda i, j, kk: (i, kk))
    if mode == "nt":
        b_spec = pl.BlockSpec((tn, tk), lambda i, j, kk: (j, kk))
    else:
        b_spec = pl.BlockSpec((tk, tn), lambda i, j, kk: (kk, j))
    o_spec = pl.BlockSpec((tm, tn), lambda i, j, kk: (i, j))
    any_spec = pl.BlockSpec(memory_space=pl.ANY)
    in_specs = [a_spec, b_spec] + ([o_spec] if has_res else []) + [any_spec] * nb
    args = (a, b) + ((residual,) if has_res else ()) + tuple(bufs)
    comm_shapes, comm_sems = _comm_shapes(bufs, gather) if nb else ([], [])
    res = pl.pallas_call(
        body, name=name, grid=(gm, gn, nk), in_specs=in_specs, out_specs=[o_spec] + [any_spec] * nb,
        out_shape=[jax.ShapeDtypeStruct((m, n), out_dtype)] + comm_shapes,
        scratch_shapes=([pltpu.VMEM((tm, tn), f32)] if nk > 1 else []) + comm_sems,
        compiler_params=_cparams(("arbitrary",) * 3 if nb else ("parallel", "parallel", "arbitrary")))(*args)
    return res if nb else res[0]


def _act_spec(t, width, colblk):
    return pl.BlockSpec((t, width), lambda i: (i, colblk))


def _tok_fwd(name, fn, acts, params, outs, t):
    rows = acts[0][0].shape[0]
    t = min(t, rows)
    na, npar = len(acts), len(params)

    def body(*refs):
        a = [r[...].astype(f32) for r in refs[:na]]
        p = [r[...].astype(f32) for r in refs[na:na + npar]]
        res = fn(*a, *p)
        for o_ref, o in zip(refs[na + npar:], res):
            o_ref[...] = o.astype(o_ref.dtype)

    in_specs = [_act_spec(t, w, cb) for (_, w, cb) in acts]
    in_specs += [pl.BlockSpec(p.shape, lambda i: (0, 0)) for p in params]
    res = pl.pallas_call(
        body, name=name, grid=(rows // t,), in_specs=in_specs,
        out_specs=[_act_spec(t, w, 0) for (w, _) in outs],
        out_shape=[jax.ShapeDtypeStruct((rows, w), dt) for (w, dt) in outs],
        compiler_params=_cparams(("arbitrary",)))(*[a for (a, _, _) in acts], *params)
    return res


def _tok_bwd(name, fn, acts, params, cts, dact, t, residuals=None):
    rows = acts[0][0].shape[0]
    t = min(t, rows)
    residuals = residuals or {}
    res_ids = sorted(residuals)
    na, npar, nc, nr, nd = len(acts), len(params), len(cts), len(res_ids), len(dact)

    def body(*refs):
        a = [r[...].astype(f32) for r in refs[:na]]
        p = [r[...].astype(f32) for r in refs[na:na + npar]]
        ct = tuple(r[...].astype(f32) for r in refs[na + npar:na + npar + nc])
        rs = {idx: r[...].astype(f32) for idx, r in zip(res_ids, refs[na + npar + nc:na + npar + nc + nr])}
        orefs = refs[na + npar + nc + nr:]
        _, vjp = jax.vjp(fn, *a, *p)
        grads = vjp(ct)
        for o_ref, (idx, _) in zip(orefs[:nd], dact):
            g = grads[idx]
            if idx in rs:
                g = g + rs[idx]
            o_ref[...] = g.astype(o_ref.dtype)

        if npar:
            @pl.when(pl.program_id(0) == 0)
            def _():
                for o_ref in orefs[nd:]:
                    o_ref[...] = jnp.zeros_like(o_ref)

            for o_ref, g in zip(orefs[nd:], grads[na:]):
                o_ref[...] += g

    in_specs = [_act_spec(t, w, cb) for (_, w, cb) in acts]
    in_specs += [pl.BlockSpec(p.shape, lambda i: (0, 0)) for p in params]
    in_specs += [_act_spec(t, w, cb) for (_, w, cb) in cts]
    in_specs += [_act_spec(t, acts[idx][1], 0) for idx in res_ids]
    out_specs = [_act_spec(t, acts[idx][1], 0) for (idx, _) in dact]
    out_specs += [pl.BlockSpec(p.shape, lambda i: (0, 0)) for p in params]
    out_shape = [jax.ShapeDtypeStruct((rows, acts[idx][1]), dt) for (idx, dt) in dact]
    out_shape += [jax.ShapeDtypeStruct(p.shape, f32) for p in params]
    return pl.pallas_call(
        body, name=name, grid=(rows // t,), in_specs=in_specs, out_specs=out_specs, out_shape=out_shape,
        compiler_params=_cparams(("arbitrary",)))(
            *[a for (a, _, _) in acts], *params, *[c for (c, _, _) in cts], *[residuals[i] for i in res_ids])


def _rms_fn(x, w):
    return (x * lax.rsqrt(jnp.mean(x * x, axis=-1, keepdims=True) + NORM_EPS) * w,)


def _gate_fn(la, lb, lc, pa, pb, pc):
    return (jax.nn.sigmoid(la) * pa + jax.nn.sigmoid(lb) * pb + jax.nn.sigmoid(lc) * pc,)


def _s5_tail_fn(ypre, gate, glu_w, glu_b):
    y = jax.nn.gelu(ypre)
    y = y * jax.nn.sigmoid(dot_nn(y, glu_w) + glu_b)
    return (y * _silu(gate),)


def _loss_grad(x, w, target, t=256):
    rows, d = x.shape
    t = min(t, rows)

    def fn(xt, wt, tt):
        y = _rms_fn(xt, wt)[0]
        err = y - tt
        return 0.5 * jnp.sum(jnp.sum(err * err, axis=-1, keepdims=True), axis=0, keepdims=True) / d

    def body(x_ref, w_ref, t_ref, loss_ref, dx_ref, dw_ref):
        tt = t_ref[...]
        val, vjp = jax.vjp(lambda a, b: fn(a, b, tt), x_ref[...], w_ref[...])
        dx, dw = vjp(jnp.ones((1, 1), f32))
        dx_ref[...] = dx

        @pl.when(pl.program_id(0) == 0)
        def _():
            loss_ref[...] = jnp.zeros_like(loss_ref)
            dw_ref[...] = jnp.zeros_like(dw_ref)

        loss_ref[...] += val * jnp.ones((1, LANE), f32)
        dw_ref[...] += dw

    return pl.pallas_call(
        body, name="loss_grad", grid=(rows // t,),
        in_specs=[_act_spec(t, d, 0), pl.BlockSpec((1, d), lambda i: (0, 0)), _act_spec(t, d, 0)],
        out_specs=[pl.BlockSpec((1, LANE), lambda i: (0, 0)), _act_spec(t, d, 0), pl.BlockSpec((1, d), lambda i: (0, 0))],
        out_shape=[jax.ShapeDtypeStruct((1, LANE), f32), jax.ShapeDtypeStruct((rows, d), f32),
                   jax.ShapeDtypeStruct((1, d), f32)],
        compiler_params=_cparams(("arbitrary",)))(x, w, target)


GDN_HB = 4
GDN_SW = GDN_HB * HEAD_DIM
GDN_STEPS = GDN_HEADS // GDN_HB


def _tri_inv(a, eye, c):
    rows = a.shape[0]
    n = -a
    p = eye + n
    npow = dot_nn(n, n)
    levels = c.bit_length() - 1
    for j in range(2, levels):
        both = dot_nn(jnp.concatenate([p, npow], axis=0), npow)
        p, npow = p + both[:rows], both[rows:]
    return p + dot_nn(p, npow)


def _block_ids(rows, c):
    ri = lax.broadcasted_iota(jnp.int32, (rows, rows), 0)
    ci = lax.broadcasted_iota(jnp.int32, (rows, rows), 1)
    r1 = lax.broadcasted_iota(jnp.int32, (rows, 1), 0)
    rb, cb, r1b = 0, 0, 0
    for edge in range(c, rows, c):
        rb = rb + (ri >= edge).astype(jnp.int32)
        cb = cb + (ci >= edge).astype(jnp.int32)
        r1b = r1b + (r1 >= edge).astype(jnp.int32)
    return ri, ci, rb, cb, r1b


def _gdn_step(qc, kc, vc, z, small, alog_row, dtb_row, normw, s_cat, head0):
    c = qc.shape[0]
    hb = GDN_HB
    rows = hb * c

    def stack(x):
        return jnp.concatenate([x[:, r * HEAD_DIM:(r + 1) * HEAD_DIM] for r in range(hb)], axis=0)

    ri, ci, rb, cb, r1b = _block_ids(rows, c)
    same = rb == cb
    causal = same & (ri >= ci)
    strict = same & (ri > ci)
    eye = (ri == ci).astype(f32)
    head_rows = [(r1b == r).astype(f32) for r in range(hb)]

    def own_block(x):
        acc = None
        for r in range(hb):
            term = x[:, r * HEAD_DIM:(r + 1) * HEAD_DIM] * head_rows[r]
            acc = term if acc is None else acc + term
        return acc

    beta_all = jax.nn.sigmoid(small)
    g_all = -jnp.exp(alog_row) * _softplus(small + dtb_row)
    gc_all, gct_all = _cumsum_all(g_all)
    beta = jnp.concatenate([_pick_col(beta_all, head0 + r) for r in range(hb)], axis=0)
    gc = jnp.concatenate([_pick_col(gc_all, head0 + r + GDN_HEADS) for r in range(hb)], axis=0)
    gc_t = jnp.concatenate([_pick_row(gct_all, head0 + r + GDN_HEADS) for r in range(hb)], axis=1)
    g_last = [gc[(r + 1) * c - 1:(r + 1) * c, :] for r in range(hb)]
    gl = sum(head_rows[r] * g_last[r] for r in range(hb))

    q = _silu(stack(qc))
    k = _silu(stack(kc))
    v = _silu(stack(vc))
    q = q * lax.rsqrt(jnp.sum(q * q, axis=-1, keepdims=True) + NORM_EPS) * (HEAD_DIM ** -0.5)
    k = k * lax.rsqrt(jnp.sum(k * k, axis=-1, keepdims=True) + NORM_EPS)
    decay = jnp.exp(jnp.where(causal, gc - gc_t, -1e30))
    egc = jnp.exp(gc)
    kb = k * beta
    a_mat = jnp.where(strict, dot_nt(kb, k) * decay, 0.0)
    t_inv = _tri_inv(a_mat, eye, c)
    uw = dot_nn(t_inv, jnp.concatenate([v * beta, kb * egc], axis=1))
    u, w = uw[:, :HEAD_DIM], uw[:, HEAD_DIM:]
    qk = dot_nt(q, k) * decay
    on_state = dot_nn(jnp.concatenate([w, q * egc], axis=0), s_cat)
    v_new = u - own_block(on_state[:rows])
    out = own_block(on_state[rows:]) + dot_nn(qk, v_new)
    k_tail = k * jnp.exp(gl - gc)
    v_bd = jnp.concatenate([v_new * head_rows[r] for r in range(hb)], axis=1)
    eg_cat = jnp.concatenate([jnp.exp(g_last[r]) * jnp.ones((1, HEAD_DIM), f32) for r in range(hb)], axis=1)
    new_s = s_cat * eg_cat + dot_tn(k_tail, v_bd)
    o = out * lax.rsqrt(jnp.mean(out * out, axis=-1, keepdims=True) + NORM_EPS) * normw * _silu(stack(z))
    o = jnp.concatenate([o[r * c:(r + 1) * c] for r in range(hb)], axis=1)
    return o, new_s


def _conv_windows(xin_ref, p, cw_ref, c):
    acc = None
    for k in range(CONV_K):
        term = cw_ref[pl.ds(k, 1), :] * xin_ref[p, pl.ds(HALO - CONV_K + 1 + k, c), :]
        acc = term if acc is None else acc + term
    return acc


def _gdn_fwd(proj, conv_w, alog_row, dtb_row, normw):
    rows = proj.shape[0]
    c = min(CHUNK, rows)
    n = rows // c

    def body(q_ref, k_ref, v_ref, z_ref, sm_ref, cwq, cwk, cwv, al_ref, dt_ref, nw_ref, y_ref, ck_ref, s_ref, xin_ref):
        hb = pl.program_id(0)
        i = pl.program_id(1)

        @pl.when(i == 0)
        def _():
            s_ref[...] = jnp.zeros_like(s_ref)
            xin_ref[:, 0:HALO, :] = jnp.zeros((3, HALO, GDN_SW), f32)

        @pl.when(i > 0)
        def _():
            xin_ref[:, 0:HALO, :] = xin_ref[:, c:c + HALO, :]

        xin_ref[0, HALO:, :] = q_ref[...]
        xin_ref[1, HALO:, :] = k_ref[...]
        xin_ref[2, HALO:, :] = v_ref[...]
        qc = _conv_windows(xin_ref, 0, cwq, c)
        kc = _conv_windows(xin_ref, 1, cwk, c)
        vc = _conv_windows(xin_ref, 2, cwv, c)
        state = s_ref[...]
        ck_ref[...] = state
        o, new_state = _gdn_step(qc, kc, vc, z_ref[...], sm_ref[...], al_ref[...], dt_ref[...], nw_ref[...], state,
                                 hb * GDN_HB)
        y_ref[...] = o.astype(y_ref.dtype)
        s_ref[...] = new_state

    def blk(unit):
        return pl.BlockSpec((c, GDN_SW), lambda hb, i: (i, unit // GDN_HB + hb))

    def cw(part):
        return pl.BlockSpec((CONV_K, GDN_SW), lambda hb, i: (0, part * GDN_STEPS + hb))

    row = pl.BlockSpec((1, LANE), lambda hb, i: (0, 0))
    return pl.pallas_call(
        body, name="gdn_fwd", grid=(GDN_STEPS, n),
        in_specs=[blk(U_Q), blk(U_K), blk(U_V), blk(U_AZ), pl.BlockSpec((c, LANE), lambda hb, i: (i, U_SMA)),
                  cw(0), cw(1), cw(2), row, row, row],
        out_specs=[pl.BlockSpec((c, GDN_SW), lambda hb, i: (i, hb)),
                   pl.BlockSpec((None, None, HEAD_DIM, GDN_SW), lambda hb, i: (hb, i, 0, 0))],
        out_shape=[jax.ShapeDtypeStruct((rows, GDN_WIDTH), bf16),
                   jax.ShapeDtypeStruct((GDN_STEPS, n, HEAD_DIM, GDN_SW), f32)],
        scratch_shapes=[pltpu.VMEM((HEAD_DIM, GDN_SW), f32), pltpu.VMEM((3, c + HALO, GDN_SW), f32)],
        compiler_params=_cparams(("arbitrary", "arbitrary")))(
            proj, proj, proj, proj, proj, conv_w, conv_w, conv_w, alog_row, dtb_row, normw)


def _conv_bwd(xin_ref, dyext_ref, p, cw_ref, dxc, dx_ref, dcw_ref, c):
    dyext_ref[p, 0:c, :] = dxc
    acc = None
    for k in range(CONV_K):
        term = cw_ref[pl.ds(k, 1), :] * dyext_ref[p, pl.ds(CONV_K - 1 - k, c), :]
        acc = term if acc is None else acc + term
        dcw_ref[pl.ds(k, 1), :] += jnp.sum(xin_ref[p, pl.ds(HALO - CONV_K + 1 + k, c), :] * dxc, axis=0, keepdims=True)
    dx_ref[...] = acc.astype(dx_ref.dtype)


def _gdn_bwd(proj, dy, ck, conv_w, alog_row, dtb_row, normw):
    rows = proj.shape[0]
    c = min(CHUNK, rows)
    n = rows // c
    halo_blocks = c // HALO

    def body(q_ref, k_ref, v_ref, hq_ref, hk_ref, hv_ref, z_ref, sm_ref, cwq, cwk, cwv, al_ref, dt_ref, nw_ref,
             ck_ref, dy_ref, dq_ref, dk_ref, dv_ref, dz_ref, dsm_ref, dcwq, dcwk, dcwv, dal_ref, ddt_ref, dnw_ref,
             ds_ref, xin_ref, dyext_ref):
        hb = pl.program_id(0)
        i = pl.program_id(1)
        ci = n - 1 - i

        @pl.when(i == 0)
        def _():
            ds_ref[...] = jnp.zeros_like(ds_ref)
            dyext_ref[:, c:c + HALO, :] = jnp.zeros((3, HALO, GDN_SW), f32)
            for r in (dcwq, dcwk, dcwv, dal_ref, ddt_ref, dnw_ref):
                r[...] = jnp.zeros_like(r)

        @pl.when(i > 0)
        def _():
            dyext_ref[:, c:c + HALO, :] = dyext_ref[:, 0:HALO, :]

        first = (ci > 0).astype(f32)
        for p, (x_ref, halo_ref) in enumerate(((q_ref, hq_ref), (k_ref, hk_ref), (v_ref, hv_ref))):
            xin_ref[p, 0:HALO, :] = halo_ref[...] * first
            xin_ref[p, HALO:, :] = x_ref[...]
        qc = _conv_windows(xin_ref, 0, cwq, c)
        kc = _conv_windows(xin_ref, 1, cwk, c)
        vc = _conv_windows(xin_ref, 2, cwv, c)
        fn = functools.partial(_gdn_step, head0=hb * GDN_HB)
        _, vjp = jax.vjp(fn, qc, kc, vc, z_ref[...], sm_ref[...], al_ref[...], dt_ref[...], nw_ref[...], ck_ref[...])
        dqc, dkc, dvc, dz, dsm, dal, ddt, dnw, dstate = vjp((dy_ref[...].astype(f32), ds_ref[...]))
        ds_ref[...] = dstate
        dz_ref[...] = dz.astype(dz_ref.dtype)
        dsm_ref[...] = dsm
        dal_ref[...] += dal
        ddt_ref[...] += ddt
        dnw_ref[...] += dnw
        _conv_bwd(xin_ref, dyext_ref, 0, cwq, dqc, dq_ref, dcwq, c)
        _conv_bwd(xin_ref, dyext_ref, 1, cwk, dkc, dk_ref, dcwk, c)
        _conv_bwd(xin_ref, dyext_ref, 2, cwv, dvc, dv_ref, dcwv, c)

    def blk(unit):
        return pl.BlockSpec((c, GDN_SW), lambda hb, i: (n - 1 - i, unit // GDN_HB + hb))

    def halo(unit):
        return pl.BlockSpec((HALO, GDN_SW),
                            lambda hb, i: (jnp.maximum((n - 1 - i) * halo_blocks - 1, 0), unit // GDN_HB + hb))

    def cw(part):
        return pl.BlockSpec((CONV_K, GDN_SW), lambda hb, i: (0, part * GDN_STEPS + hb))

    row = pl.BlockSpec((1, LANE), lambda hb, i: (0, 0))
    hrow = pl.BlockSpec((None, 1, LANE), lambda hb, i: (hb, 0, 0))
    out_blk = pl.BlockSpec((c, GDN_SW), lambda hb, i: (n - 1 - i, hb))
    dcw = pl.BlockSpec((CONV_K, GDN_SW), lambda hb, i: (0, hb))
    wide = jax.ShapeDtypeStruct((rows, GDN_WIDTH), bf16)
    hrow_shape = jax.ShapeDtypeStruct((GDN_STEPS, 1, LANE), f32)
    dcw_shape = jax.ShapeDtypeStruct((CONV_K, GDN_WIDTH), f32)
    return pl.pallas_call(
        body, name="gdn_bwd", grid=(GDN_STEPS, n),
        in_specs=[blk(U_Q), blk(U_K), blk(U_V), halo(U_Q), halo(U_K), halo(U_V), blk(U_AZ),
                  pl.BlockSpec((c, LANE), lambda hb, i: (n - 1 - i, U_SMA)),
                  cw(0), cw(1), cw(2), row, row, row,
                  pl.BlockSpec((None, None, HEAD_DIM, GDN_SW), lambda hb, i: (hb, n - 1 - i, 0, 0)),
                  out_blk],
        out_specs=[out_blk, out_blk, out_blk, out_blk,
                   pl.BlockSpec((None, c, LANE), lambda hb, i: (hb, n - 1 - i, 0)),
                   dcw, dcw, dcw, hrow, hrow, hrow],
        out_shape=[wide, wide, wide, wide, jax.ShapeDtypeStruct((GDN_STEPS, rows, LANE), f32),
                   dcw_shape, dcw_shape, dcw_shape, hrow_shape, hrow_shape, hrow_shape],
        scratch_shapes=[pltpu.VMEM((HEAD_DIM, GDN_SW), f32), pltpu.VMEM((3, c + HALO, GDN_SW), f32),
                        pltpu.VMEM((3, c + HALO, GDN_SW), f32)],
        compiler_params=_cparams(("arbitrary", "arbitrary")))(
            proj, proj, proj, proj, proj, proj, proj, proj, conv_w, conv_w, conv_w, alog_row, dtb_row, normw, ck, dy)


M2_REP = M2_HEADS // M2_GROUPS
M2_GW = M2_REP * M2_HEAD_DIM
M2_GB = 2
M2_STEPS = M2_GROUPS // M2_GB
M2_XW = M2_GB * M2_GW
M2_BW = M2_GB * M2_STATE
M2_SH = M2_GB * M2_REP


def _ssd_step(xc, bc, cc, z, small, bias_x, bias_b, bias_c, alog_row, dtb_row, d_row, normw, state, grp0):
    c = xc.shape[0]
    causal = _tri_masks(c)[0]
    hd = M2_HEAD_DIM
    ones_l = jnp.ones((1, hd), f32)
    ones_r = jnp.ones((hd, 1), f32)
    lane = lax.broadcasted_iota(jnp.int32, (1, M2_GW), 1)
    lane_head = [((lane >= r * hd) & (lane < (r + 1) * hd)).astype(f32) for r in range(M2_REP)]
    xs = _silu(xc + bias_x)
    bms = _silu(bc + bias_b)
    cms = _silu(cc + bias_c)
    dt_all = _softplus(small + dtb_row)
    a_all = -jnp.exp(alog_row) * dt_all
    ac_all, act_all = _cumsum_all(a_all)
    ys, new_states = [], []
    for gi in range(M2_GB):
        bm = bms[:, gi * M2_STATE:(gi + 1) * M2_STATE]
        cm = cms[:, gi * M2_STATE:(gi + 1) * M2_STATE]
        xg = xs[:, gi * M2_GW:(gi + 1) * M2_GW]
        sg = state[gi * M2_GW:(gi + 1) * M2_GW]
        heads = [(grp0 + gi) * M2_REP + r for r in range(M2_REP)]
        ac_h = [_pick_col(ac_all, h) for h in heads]
        al_h = [a[c - 1:c, :] for a in ac_h]

        def wide(cols):
            return jnp.concatenate([v * ones_l for v in cols], axis=1)

        dt_w = wide([_pick_col(dt_all, h) for h in heads])
        ac_w = wide(ac_h)
        al_w = wide(al_h)
        dsk_w = wide([_pick_col(d_row, h) for h in heads])
        scores = dot_nt(cm, bm)
        m_wide = jnp.concatenate(
            [scores * jnp.exp(jnp.where(causal, a - _pick_row(act_all, h), -1e30)) for a, h in zip(ac_h, heads)], axis=1)
        xdt = xg * dt_w
        x_bd = jnp.concatenate([xdt * lane_head[r] for r in range(M2_REP)], axis=0)
        y_diag = dot_nn(m_wide, x_bd)
        states_new = dot_tn(xdt * jnp.exp(al_w - ac_w), bm)
        y_off = dot_nt(cm, sg) * jnp.exp(ac_w)
        eg_col = jnp.concatenate([jnp.exp(a) * ones_r for a in al_h], axis=0)
        new_states.append(sg * eg_col + states_new)
        y = (y_diag + y_off + dsk_w * xg) * _silu(z[:, gi * M2_GW:(gi + 1) * M2_GW])
        ys.append(y * lax.rsqrt(jnp.mean(y * y, axis=-1, keepdims=True) + NORM_EPS)
                  * normw[:, gi * M2_GW:(gi + 1) * M2_GW])
    return jnp.concatenate(ys, axis=-1), jnp.concatenate(new_states, axis=0)


def _conv_windows2(xin_ref, cw_ref, c):
    acc = None
    for k in range(CONV_K):
        term = cw_ref[pl.ds(k, 1), :] * xin_ref[pl.ds(HALO - CONV_K + 1 + k, c), :]
        acc = term if acc is None else acc + term
    return acc


def _ssd_specs(n, c, rev):
    def ci(i):
        return (n - 1 - i) if rev else i

    def blk(width, unit):
        return pl.BlockSpec((c, width), lambda g, i: (ci(i), unit * LANE // width + g))

    def par(rows_, width, col0):
        return pl.BlockSpec((rows_, width), lambda g, i: (0, col0 // width + g))

    return ci, blk, par


def _ssd_fwd(proj, conv_w, conv_b, alog_row, dtb_row, d_row, normw):
    rows = proj.shape[0]
    c = min(CHUNK, rows)
    n = rows // c
    _, blk, par = _ssd_specs(n, c, False)

    def body(x_ref, b_ref, c_ref, z_ref, sm_ref, cwx, cwb, cwc, bx, bb, bcc, al_ref, dt_ref, d_ref, nw_ref,
             y_ref, ck_ref, s_ref, xx_ref, xb_ref, xc_ref):
        g = pl.program_id(0)
        i = pl.program_id(1)

        @pl.when(i == 0)
        def _():
            s_ref[...] = jnp.zeros_like(s_ref)
            for r in (xx_ref, xb_ref, xc_ref):
                r[0:HALO, :] = jnp.zeros((HALO, r.shape[1]), f32)

        @pl.when(i > 0)
        def _():
            for r in (xx_ref, xb_ref, xc_ref):
                r[0:HALO, :] = r[c:c + HALO, :]

        xx_ref[HALO:, :] = x_ref[...]
        xb_ref[HALO:, :] = b_ref[...]
        xc_ref[HALO:, :] = c_ref[...]
        xc = _conv_windows2(xx_ref, cwx, c)
        bc = _conv_windows2(xb_ref, cwb, c)
        cc = _conv_windows2(xc_ref, cwc, c)
        state = s_ref[...]
        ck_ref[...] = state
        y, new_state = _ssd_step(xc, bc, cc, z_ref[...], sm_ref[...], bx[...], bb[...], bcc[...], al_ref[...],
                                 dt_ref[...], d_ref[...], nw_ref[...], state, g * M2_GB)
        y_ref[...] = y.astype(y_ref.dtype)
        s_ref[...] = new_state

    row = pl.BlockSpec((1, LANE), lambda g, i: (0, 0))
    off_b, off_c = M2_WIDTH, M2_WIDTH + M2_GROUPS * M2_STATE
    return pl.pallas_call(
        body, name="ssd_fwd", grid=(M2_STEPS, n),
        in_specs=[blk(M2_XW, U_CX), blk(M2_BW, U_CB), blk(M2_BW, U_CC), blk(M2_XW, U_CZ),
                  pl.BlockSpec((c, LANE), lambda g, i: (i, U_SMC)),
                  par(CONV_K, M2_XW, 0), par(CONV_K, M2_BW, off_b), par(CONV_K, M2_BW, off_c),
                  par(1, M2_XW, 0), par(1, M2_BW, off_b), par(1, M2_BW, off_c), row, row, row, par(1, M2_XW, 0)],
        out_specs=[pl.BlockSpec((c, M2_XW), lambda g, i: (i, g)),
                   pl.BlockSpec((None, None, M2_SH * M2_HEAD_DIM, M2_STATE), lambda g, i: (g, i, 0, 0))],
        out_shape=[jax.ShapeDtypeStruct((rows, M2_WIDTH), bf16),
                   jax.ShapeDtypeStruct((M2_STEPS, n, M2_SH * M2_HEAD_DIM, M2_STATE), f32)],
        scratch_shapes=[pltpu.VMEM((M2_SH * M2_HEAD_DIM, M2_STATE), f32), pltpu.VMEM((c + HALO, M2_XW), f32),
                        pltpu.VMEM((c + HALO, M2_BW), f32), pltpu.VMEM((c + HALO, M2_BW), f32)],
        compiler_params=_cparams(("arbitrary", "arbitrary")))(
            proj, proj, proj, proj, proj, conv_w, conv_w, conv_w, conv_b, conv_b, conv_b, alog_row, dtb_row, d_row, normw)


def _conv_bwd2(xin_ref, dyext_ref, cw_ref, dxc, dx_ref, dcw_ref, c):
    dyext_ref[0:c, :] = dxc
    acc = None
    for k in range(CONV_K):
        term = cw_ref[pl.ds(k, 1), :] * dyext_ref[pl.ds(CONV_K - 1 - k, c), :]
        acc = term if acc is None else acc + term
        dcw_ref[pl.ds(k, 1), :] += jnp.sum(xin_ref[pl.ds(HALO - CONV_K + 1 + k, c), :] * dxc, axis=0, keepdims=True)
    dx_ref[...] = acc.astype(dx_ref.dtype)


def _ssd_bwd(proj, dy, ck, conv_w, conv_b, alog_row, dtb_row, d_row, normw):
    rows = proj.shape[0]
    c = min(CHUNK, rows)
    n = rows // c
    halo_blocks = c // HALO
    _, blk, par = _ssd_specs(n, c, True)

    def body(x_ref, b_ref, c_ref, hx_ref, hb_ref, hc_ref, z_ref, sm_ref, cwx, cwb, cwc, bx, bb, bcc,
             al_ref, dt_ref, d_ref, nw_ref, ck_ref, dy_ref,
             dx_ref, db_ref, dc_ref, dz_ref, dsm_ref, dcwx, dcwb, dcwc, dbx, dbb, dbc, dal_ref, ddt_ref, dd_ref, dnw_ref,
             ds_ref, xx_ref, xb_ref, xc_ref, ex_ref, eb_ref, ec_ref):
        g = pl.program_id(0)
        i = pl.program_id(1)
        ci = n - 1 - i

        @pl.when(i == 0)
        def _():
            ds_ref[...] = jnp.zeros_like(ds_ref)
            for r in (ex_ref, eb_ref, ec_ref):
                r[c:c + HALO, :] = jnp.zeros((HALO, r.shape[1]), f32)
            for r in (dcwx, dcwb, dcwc, dbx, dbb, dbc, dal_ref, ddt_ref, dd_ref, dnw_ref):
                r[...] = jnp.zeros_like(r)

        @pl.when(i > 0)
        def _():
            for r in (ex_ref, eb_ref, ec_ref):
                r[c:c + HALO, :] = r[0:HALO, :]

        first = (ci > 0).astype(f32)
        for xin, x_in, halo_in in ((xx_ref, x_ref, hx_ref), (xb_ref, b_ref, hb_ref), (xc_ref, c_ref, hc_ref)):
            xin[0:HALO, :] = halo_in[...] * first
            xin[HALO:, :] = x_in[...]
        xc = _conv_windows2(xx_ref, cwx, c)
        bc = _conv_windows2(xb_ref, cwb, c)
        cc = _conv_windows2(xc_ref, cwc, c)
        fn = functools.partial(_ssd_step, grp0=g * M2_GB)
        _, vjp = jax.vjp(fn, xc, bc, cc, z_ref[...], sm_ref[...], bx[...], bb[...], bcc[...], al_ref[...], dt_ref[...],
                         d_ref[...], nw_ref[...], ck_ref[...])
        (dxc, dbc_, dcc, dz, dsm, gbx, gbb, gbc, dal, ddt, dd, dnw, dstate) = vjp((dy_ref[...].astype(f32), ds_ref[...]))
        ds_ref[...] = dstate
        dz_ref[...] = dz.astype(dz_ref.dtype)
        dsm_ref[...] = dsm
        dbx[...] += gbx
        dbb[...] += gbb
        dbc[...] += gbc
        dal_ref[...] += dal
        ddt_ref[...] += ddt
        dd_ref[...] += dd
        dnw_ref[...] += dnw
        _conv_bwd2(xx_ref, ex_ref, cwx, dxc, dx_ref, dcwx, c)
        _conv_bwd2(xb_ref, eb_ref, cwb, dbc_, db_ref, dcwb, c)
        _conv_bwd2(xc_ref, ec_ref, cwc, dcc, dc_ref, dcwc, c)

    def halo(width, unit):
        return pl.BlockSpec((HALO, width),
                            lambda g, i: (jnp.maximum((n - 1 - i) * halo_blocks - 1, 0), unit * LANE // width + g))

    row = pl.BlockSpec((1, LANE), lambda g, i: (0, 0))
    grow = pl.BlockSpec((None, 1, LANE), lambda g, i: (g, 0, 0))
    grow_shape = jax.ShapeDtypeStruct((M2_STEPS, 1, LANE), f32)
    ob_w = pl.BlockSpec((c, M2_XW), lambda g, i: (n - 1 - i, g))
    ob_n = pl.BlockSpec((c, M2_BW), lambda g, i: (n - 1 - i, g))
    off_b, off_c = M2_WIDTH, M2_WIDTH + M2_GROUPS * M2_STATE
    bc_w = M2_GROUPS * M2_STATE
    return pl.pallas_call(
        body, name="ssd_bwd", grid=(M2_STEPS, n),
        in_specs=[blk(M2_XW, U_CX), blk(M2_BW, U_CB), blk(M2_BW, U_CC),
                  halo(M2_XW, U_CX), halo(M2_BW, U_CB), halo(M2_BW, U_CC), blk(M2_XW, U_CZ),
                  pl.BlockSpec((c, LANE), lambda g, i: (n - 1 - i, U_SMC)),
                  par(CONV_K, M2_XW, 0), par(CONV_K, M2_BW, off_b), par(CONV_K, M2_BW, off_c),
                  par(1, M2_XW, 0), par(1, M2_BW, off_b), par(1, M2_BW, off_c), row, row, row, par(1, M2_XW, 0),
                  pl.BlockSpec((None, None, M2_SH * M2_HEAD_DIM, M2_STATE), lambda g, i: (g, n - 1 - i, 0, 0)),
                  ob_w],
        out_specs=[ob_w, ob_n, ob_n, ob_w, pl.BlockSpec((None, c, LANE), lambda g, i: (g, n - 1 - i, 0)),
                   par(CONV_K, M2_XW, 0), par(CONV_K, M2_BW, 0), par(CONV_K, M2_BW, 0),
                   par(1, M2_XW, 0), par(1, M2_BW, 0), par(1, M2_BW, 0), grow, grow, grow, par(1, M2_XW, 0)],
        out_shape=[jax.ShapeDtypeStruct((rows, M2_WIDTH), bf16), jax.ShapeDtypeStruct((rows, bc_w), bf16),
                   jax.ShapeDtypeStruct((rows, bc_w), bf16), jax.ShapeDtypeStruct((rows, M2_WIDTH), bf16),
                   jax.ShapeDtypeStruct((M2_STEPS, rows, LANE), f32),
                   jax.ShapeDtypeStruct((CONV_K, M2_WIDTH), f32), jax.ShapeDtypeStruct((CONV_K, bc_w), f32),
                   jax.ShapeDtypeStruct((CONV_K, bc_w), f32),
                   jax.ShapeDtypeStruct((1, M2_WIDTH), f32), jax.ShapeDtypeStruct((1, bc_w), f32),
                   jax.ShapeDtypeStruct((1, bc_w), f32), grow_shape, grow_shape, grow_shape,
                   jax.ShapeDtypeStruct((1, M2_WIDTH), f32)],
        scratch_shapes=[pltpu.VMEM((M2_SH * M2_HEAD_DIM, M2_STATE), f32),
                        pltpu.VMEM((c + HALO, M2_XW), f32), pltpu.VMEM((c + HALO, M2_BW), f32), pltpu.VMEM((c + HALO, M2_BW), f32),
                        pltpu.VMEM((c + HALO, M2_XW), f32), pltpu.VMEM((c + HALO, M2_BW), f32), pltpu.VMEM((c + HALO, M2_BW), f32)],
        compiler_params=_cparams(("arbitrary", "arbitrary")))(
            proj, proj, proj, proj, proj, proj, proj, proj, conv_w, conv_w, conv_w, conv_b, conv_b, conv_b,
            alog_row, dtb_row, d_row, normw, ck, dy)


S5_TILE = 128
S5_SW = S5_LANES // S5_BLK


def _scan_down(br, bi, ar, ai):
    t = br.shape[0]
    row = lax.broadcasted_iota(jnp.int32, br.shape, 0)
    d = 1
    while d < t:
        keep = row >= d
        sr = jnp.where(keep, pltpu.roll(br, d, 0), 0.0)
        si = jnp.where(keep, pltpu.roll(bi, d, 0), 0.0)
        br, bi = br + ar * sr - ai * si, bi + ar * si + ai * sr
        ar, ai = ar * ar - ai * ai, 2.0 * ar * ai
        d *= 2
    return br, bi


def _scan_up(br, bi, ar, ai):
    t = br.shape[0]
    row = lax.broadcasted_iota(jnp.int32, br.shape, 0)
    d = 1
    while d < t:
        keep = row < t - d
        sr = jnp.where(keep, pltpu.roll(br, t - d, 0), 0.0)
        si = jnp.where(keep, pltpu.roll(bi, t - d, 0), 0.0)
        br, bi = br + ar * sr - ai * si, bi + ar * si + ai * sr
        ar, ai = ar * ar - ai * ai, 2.0 * ar * ai
        d *= 2
    return br, bi


def _s5_states(u_j, bbr, bbi, ar, ai, cr, ci_):
    br = dot_nn(u_j, bbr)
    bi = dot_nn(u_j, bbi)
    row0 = lax.broadcasted_iota(jnp.int32, br.shape, 0) == 0
    br = br + jnp.where(row0, ar * cr - ai * ci_, 0.0)
    bi = bi + jnp.where(row0, ar * ci_ + ai * cr, 0.0)
    return _scan_down(br, bi, ar, ai)


def _s5_fwd(proj, a_rows, bbr, bbi, ccr, cci, d_row):
    rows = proj.shape[0]
    t = min(S5_TILE, rows)
    n = rows // t

    def body(u_ref, a_ref, bbr_ref, bbi_ref, ccr_ref, cci_ref, d_ref, y_ref, ck_ref, carry_ref):
        i = pl.program_id(0)

        @pl.when(i == 0)
        def _():
            carry_ref[...] = jnp.zeros_like(carry_ref)

        ck_ref[...] = carry_ref[...]
        for j in range(S5_BLK):
            lanes = pl.ds(j * S5_SW, S5_SW)
            ch = pl.ds(j * LANE, LANE)
            u_j = u_ref[:, ch]
            sr, si = _s5_states(u_j, bbr_ref[j], bbi_ref[j], a_ref[0:1, lanes], a_ref[1:2, lanes],
                                carry_ref[0:1, lanes], carry_ref[1:2, lanes])
            y_ref[:, ch] = dot_nn(sr, ccr_ref[j]) - dot_nn(si, cci_ref[j]) + d_ref[:, ch] * u_j
            carry_ref[0:1, lanes] = sr[t - 1:t, :]
            carry_ref[1:2, lanes] = si[t - 1:t, :]

    whole3 = lambda s: pl.BlockSpec(s, lambda i: (0, 0, 0))
    return pl.pallas_call(
        body, name="s5_fwd", grid=(n,),
        in_specs=[pl.BlockSpec((t, S5_WIDTH), lambda i: (i, U_SU // S5_BLK)),
                  pl.BlockSpec((2, S5_LANES), lambda i: (0, 0)),
                  whole3(bbr.shape), whole3(bbi.shape), whole3(ccr.shape), whole3(cci.shape),
                  pl.BlockSpec((1, S5_WIDTH), lambda i: (0, 0))],
        out_specs=[pl.BlockSpec((t, S5_WIDTH), lambda i: (i, 0)),
                   pl.BlockSpec((None, 2, S5_LANES), lambda i: (i, 0, 0))],
        out_shape=[jax.ShapeDtypeStruct((rows, S5_WIDTH), f32), jax.ShapeDtypeStruct((n, 2, S5_LANES), f32)],
        scratch_shapes=[pltpu.VMEM((2, S5_LANES), f32)],
        compiler_params=_cparams(("arbitrary",)))(proj, a_rows, bbr, bbi, ccr, cci, d_row)


def _s5_bwd(proj, dy, ck, a_rows, bbr, bbi, ccr, cci, d_row):
    rows = proj.shape[0]
    t = min(S5_TILE, rows)
    n = rows // t

    def body(u_ref, dy_ref, ck_ref, a_ref, bbr_ref, bbi_ref, ccr_ref, cci_ref, d_ref,
             du_ref, da_ref, dbbr_ref, dbbi_ref, dccr_ref, dcci_ref, dd_ref, lam_ref):
        i = pl.program_id(0)

        @pl.when(i == 0)
        def _():
            lam_ref[...] = jnp.zeros_like(lam_ref)
            for r in (da_ref, dbbr_ref, dbbi_ref, dccr_ref, dcci_ref, dd_ref):
                r[...] = jnp.zeros_like(r)

        for j in range(S5_BLK):
            lanes = pl.ds(j * S5_SW, S5_SW)
            ch = pl.ds(j * LANE, LANE)
            u_j = u_ref[:, ch]
            dy_j = dy_ref[:, ch]
            ar, ai = a_ref[0:1, lanes], a_ref[1:2, lanes]
            cr, ci_ = ck_ref[0:1, lanes], ck_ref[1:2, lanes]
            sr, si = _s5_states(u_j, bbr_ref[j], bbi_ref[j], ar, ai, cr, ci_)
            gr = dot_nt(dy_j, ccr_ref[j])
            gi = -dot_nt(dy_j, cci_ref[j])
            last = lax.broadcasted_iota(jnp.int32, gr.shape, 0) == t - 1
            lr0, li0 = lam_ref[0:1, lanes], lam_ref[1:2, lanes]
            gr = gr + jnp.where(last, ar * lr0 + ai * li0, 0.0)
            gi = gi + jnp.where(last, ar * li0 - ai * lr0, 0.0)
            lr, li = _scan_up(gr, gi, ar, -ai)
            lam_ref[0:1, lanes] = lr[0:1, :]
            lam_ref[1:2, lanes] = li[0:1, :]
            du_ref[:, ch] = (dot_nt(lr, bbr_ref[j]) + dot_nt(li, bbi_ref[j]) + d_ref[:, ch] * dy_j).astype(du_ref.dtype)
            dbbr_ref[j] += dot_tn(u_j, lr)
            dbbi_ref[j] += dot_tn(u_j, li)
            dccr_ref[j] += dot_tn(sr, dy_j)
            dcci_ref[j] += -dot_tn(si, dy_j)
            dd_ref[:, ch] += jnp.sum(dy_j * u_j, axis=0, keepdims=True)
            row0 = lax.broadcasted_iota(jnp.int32, sr.shape, 0) == 0
            pr = jnp.where(row0, cr, pltpu.roll(sr, 1, 0))
            pi = jnp.where(row0, ci_, pltpu.roll(si, 1, 0))
            da_ref[0:1, lanes] += jnp.sum(lr * pr + li * pi, axis=0, keepdims=True)
            da_ref[1:2, lanes] += jnp.sum(li * pr - lr * pi, axis=0, keepdims=True)

    whole3 = lambda s: pl.BlockSpec(s, lambda i: (0, 0, 0))
    whole2 = lambda s: pl.BlockSpec(s, lambda i: (0, 0))
    return pl.pallas_call(
        body, name="s5_bwd", grid=(n,),
        in_specs=[pl.BlockSpec((t, S5_WIDTH), lambda i: (n - 1 - i, U_SU // S5_BLK)),
                  pl.BlockSpec((t, S5_WIDTH), lambda i: (n - 1 - i, 0)),
                  pl.BlockSpec((None, 2, S5_LANES), lambda i: (n - 1 - i, 0, 0)),
                  whole2((2, S5_LANES)), whole3(bbr.shape), whole3(bbi.shape), whole3(ccr.shape), whole3(cci.shape),
                  whole2((1, S5_WIDTH))],
        out_specs=[pl.BlockSpec((t, S5_WIDTH), lambda i: (n - 1 - i, 0)), whole2((2, S5_LANES)),
                   whole3(bbr.shape), whole3(bbi.shape), whole3(ccr.shape), whole3(cci.shape), whole2((1, S5_WIDTH))],
        out_shape=[jax.ShapeDtypeStruct((rows, S5_WIDTH), bf16), jax.ShapeDtypeStruct((2, S5_LANES), f32),
                   jax.ShapeDtypeStruct(bbr.shape, f32), jax.ShapeDtypeStruct(bbi.shape, f32),
                   jax.ShapeDtypeStruct(ccr.shape, f32), jax.ShapeDtypeStruct(cci.shape, f32),
                   jax.ShapeDtypeStruct((1, S5_WIDTH), f32)],
        scratch_shapes=[pltpu.VMEM((2, S5_LANES), f32)],
        compiler_params=_cparams(("arbitrary",)))(proj, dy, ck, a_rows, bbr, bbi, ccr, cci, d_row)


def _s5_prep(lam_re, lam_im, log_step, b_re, b_im, c_re, c_im, d_skip):
    lam_re = jnp.minimum(lam_re, -1e-4)
    step = jnp.exp(log_step)[:, None]
    mag = jnp.exp(lam_re * step)
    ab_re = mag * jnp.cos(lam_im * step)
    ab_im = mag * jnp.sin(lam_im * step)
    den = lam_re * lam_re + lam_im * lam_im
    f_re = ((ab_re - 1.0) * lam_re + ab_im * lam_im) / den
    f_im = (ab_im * lam_re - (ab_re - 1.0) * lam_im) / den
    bb_re = f_re[..., None] * b_re - f_im[..., None] * b_im
    bb_im = f_re[..., None] * b_im + f_im[..., None] * b_re
    eye = jnp.eye(8, dtype=f32)

    def drive(bb):
        r = bb.reshape(S5_BLK, 8, S5_STATE, S5_GROUP_SIZE).transpose(0, 1, 3, 2)
        return (r[:, :, :, None, :] * eye[None, :, None, :, None]).reshape(S5_BLK, LANE, S5_SW)

    def readout(cc):
        r = cc.reshape(S5_BLK, 8, S5_GROUP_SIZE, S5_STATE).transpose(0, 1, 3, 2)
        return (r[:, :, :, None, :] * eye[None, :, None, :, None]).reshape(S5_BLK, S5_SW, LANE)

    a_rows = jnp.stack([ab_re.reshape(S5_LANES), ab_im.reshape(S5_LANES)])
    return a_rows, drive(bb_re), drive(bb_im), readout(c_re), readout(c_im), d_skip.reshape(1, S5_WIDTH)


def _adam_math(w, g, m, v):
    m = ADAM_B1 * m + (1.0 - ADAM_B1) * g
    v = ADAM_B2 * v + (1.0 - ADAM_B2) * (g * g)
    m_hat = m / (1.0 - ADAM_B1 ** ADAM_STEP)
    v_hat = v / (1.0 - ADAM_B2 ** ADAM_STEP)
    delta = -ADAM_LR * (m_hat / (jnp.sqrt(v_hat) + ADAM_EPS) + ADAM_WD * w)
    return delta, m, v


def _adamw(name, w, g, m, v):
    rows, width = w.shape
    t = rows
    for cand in (512, 256, 128, 64, 32, 16, 8):
        if rows % cand == 0 and cand * width * 4 * 7 * 2 <= VMEM_LIMIT // 2:
            t = cand
            break

    def body(w_ref, g_ref, m_ref, v_ref, d_ref, nm_ref, nv_ref):
        d, nm, nv = _adam_math(w_ref[...], g_ref[...], m_ref[...], v_ref[...])
        d_ref[...] = d
        nm_ref[...] = nm
        nv_ref[...] = nv

    spec = pl.BlockSpec((t, width), lambda i: (i, 0))
    shape = jax.ShapeDtypeStruct((rows, width), f32)
    return pl.pallas_call(body, name=name, grid=(rows // t,), in_specs=[spec] * 4, out_specs=[spec] * 3,
                          out_shape=[shape] * 3, compiler_params=_cparams(("parallel",)))(w, g, m, v)


def _adamw_slots(name, recvs, row_off, w, m, v):
    depth, rows, width = w.shape
    per_row = width * (depth * N_DEV * recvs[0].dtype.itemsize + 7 * 4) * 2
    t = next(cand for cand in (256, 128, 96, 64, 32, 16)
             if rows % cand == 0 and row_off % cand == 0 and cand * per_row <= VMEM_LIMIT * 2 // 3)
    first = row_off // t

    def body(*refs):
        r_refs = refs[:depth]
        w_ref, m_ref, v_ref, g_ref, d_ref, nm_ref, nv_ref = refs[depth:]
        layer = pl.program_id(0)
        for l, r_ref in enumerate(r_refs):
            @pl.when(layer == l)
            def _():
                g = r_ref[0].astype(f32)
                for s in range(1, N_DEV):
                    g = g + r_ref[s].astype(f32)
                d, nm, nv = _adam_math(w_ref[...], g, m_ref[...], v_ref[...])
                g_ref[...] = g
                d_ref[...] = d
                nm_ref[...] = nm
                nv_ref[...] = nv

    def recv_spec(l):
        return pl.BlockSpec((N_DEV, t, width), lambda layer, i: (0, first + jnp.where(layer == l, i, 0), 0))

    spec = pl.BlockSpec((None, t, width), lambda layer, i: (layer, i, 0))
    shape = jax.ShapeDtypeStruct((depth, rows, width), f32)
    return pl.pallas_call(
        body, name=name, grid=(depth, rows // t), in_specs=[recv_spec(l) for l in range(depth)] + [spec, spec, spec],
        out_specs=[spec] * 4, out_shape=[shape] * 4,
        compiler_params=_cparams(("arbitrary", "arbitrary")))(*recvs, w, m, v)


def _sum_slots(name, buf):
    _, rows, width = buf.shape
    t = next(cand for cand in (512, 256, 128, 64, 32, 16) if rows % cand == 0)

    def body(b_ref, o_ref):
        acc = b_ref[0].astype(f32)
        for s in range(1, N_DEV):
            acc = acc + b_ref[s].astype(f32)
        o_ref[...] = acc

    return pl.pallas_call(
        body, name=name, grid=(rows // t,), in_specs=[pl.BlockSpec((N_DEV, t, width), lambda i: (0, i, 0))],
        out_specs=pl.BlockSpec((t, width), lambda i: (i, 0)), out_shape=jax.ShapeDtypeStruct((rows, width), f32),
        compiler_params=_cparams(("parallel",)))(buf)


BIG = ("w_in", "s5_glu_w", "proj_a", "proj_b", "proj_c", "w_out")
CONV = ("gdn_conv_w", "m2_conv_w")
SMALL = ("norm_w", "gdn_a_log", "gdn_dt_bias", "gdn_norm_w", "s5_lam_re", "s5_lam_im", "s5_log_step",
         "s5_b_re", "s5_b_im", "s5_c_re", "s5_c_im", "s5_d", "s5_glu_b", "m2_conv_b", "m2_a_log", "m2_dt_bias",
         "m2_d", "m2_norm_w", "final_norm_w")
WEIGHTS = ("norm_w", "w_in", "gdn_conv_w", "gdn_a_log", "gdn_dt_bias", "gdn_norm_w", "s5_lam_re", "s5_lam_im",
           "s5_log_step", "s5_b_re", "s5_b_im", "s5_c_re", "s5_c_im", "s5_d", "s5_glu_w", "s5_glu_b", "m2_conv_w",
           "m2_conv_b", "m2_a_log", "m2_dt_bias", "m2_d", "m2_norm_w", "proj_a", "proj_b", "proj_c", "w_out",
           "final_norm_w")


PACK_ROWS = 8
PACK_TILE = 256


def _piece_rows(shape):
    size = 1
    for d in shape:
        size *= d
    rows = -(-size // LANE)
    return size, -(-rows // PACK_ROWS) * PACK_ROWS


def _pack(arrays):
    pieces = []
    for a in arrays:
        size, rows = _piece_rows(a.shape)
        flat = a.reshape(-1)
        if size != rows * LANE:
            flat = jnp.concatenate([flat, jnp.zeros((rows * LANE - size,), f32)])
        pieces.append(flat.reshape(rows, LANE))
    total = sum(p.shape[0] for p in pieces)
    tail = -total % PACK_TILE
    if tail:
        pieces.append(jnp.zeros((tail, LANE), f32))
    return jnp.concatenate(pieces, axis=0)


def _unpack(buf, shapes):
    out, off = [], 0
    for s in shapes:
        size, rows = _piece_rows(s)
        piece = buf[off:off + rows]
        out.append(piece.reshape(s) if size == rows * LANE else piece.reshape(-1)[:size].reshape(s))
        off += rows
    return out


def _win_to_padded(w):
    d = w.shape[0]
    z = lambda n: jnp.zeros((d, n), w.dtype)
    return jnp.concatenate([w[:, 8736:14880], w[:, 4112:5648], w[:, 0:4096], w[:, 5648:8720],
                            w[:, 4096:4112], z(LANE - 16), w[:, 8720:8736], z(LANE - 16), z(2 * LANE)], axis=1)


def _win_from_padded(g):
    u = LANE
    return jnp.concatenate([g[:, U_Q * u:U_CZ * u], g[:, U_SMA * u:U_SMA * u + 16], g[:, U_SU * u:U_Q * u],
                            g[:, U_CZ * u:U_SMA * u], g[:, U_SMC * u:U_SMC * u + 16], g[:, 0:U_SU * u]], axis=1)


def _lane_row(vals, offset):
    n = vals.shape[0]
    return jnp.concatenate([jnp.zeros((offset,), f32), vals, jnp.zeros((LANE - offset - n,), f32)]).reshape(1, LANE)


def _layer_fwd(x, lw, next_shards):
    h = _tok_fwd("rms_fwd", _rms_fn, [(x, D_MODEL, 0)], [lw["norm_w"].reshape(1, D_MODEL)], [(D_MODEL, bf16)], 256)[0]
    if next_shards is None:
        proj, gathered = _mm("proj_fwd", h, lw["w_in_p"], "nn", tk=D_MODEL), None
    else:
        proj, *gathered = _mm("proj_fwd_gather", h, lw["w_in_p"], "nn", tk=D_MODEL, comm=(next_shards, True))
    g_al, g_dt = _lane_row(lw["gdn_a_log"], GDN_HEADS), _lane_row(lw["gdn_dt_bias"], GDN_HEADS)
    g_nw = lw["gdn_norm_w"].reshape(1, HEAD_DIM)
    y_a, ck_a = _gdn_fwd(proj, lw["gdn_conv_w"], g_al, g_dt, g_nw)
    s5p = _s5_prep(lw["s5_lam_re"], lw["s5_lam_im"], lw["s5_log_step"], lw["s5_b_re"], lw["s5_b_im"],
                   lw["s5_c_re"], lw["s5_c_im"], lw["s5_d"])
    y_pre, ck_b = _s5_fwd(proj, *s5p)
    glu_b = lw["s5_glu_b"].reshape(1, S5_WIDTH)
    y_b = _tok_fwd("s5_tail_fwd", _s5_tail_fn, [(y_pre, S5_WIDTH, 0), (proj, S5_WIDTH, U_SG // S5_BLK)],
                   [lw["s5_glu_w"], glu_b], [(S5_WIDTH, bf16)], 256)[0]
    m_al, m_dt, m_d = _lane_row(lw["m2_a_log"], 0), _lane_row(lw["m2_dt_bias"], 0), _lane_row(lw["m2_d"], 0)
    m_cb = lw["m2_conv_b"].reshape(1, -1)
    m_nw = lw["m2_norm_w"].reshape(1, M2_WIDTH)
    y_c, ck_c = _ssd_fwd(proj, lw["m2_conv_w"], m_cb, m_al, m_dt, m_d, m_nw)
    pa = _mm("proj_a_fwd", y_a, lw["proj_a"], "nn")
    pb = _mm("proj_b_fwd", y_b, lw["proj_b"], "nn")
    pc = _mm("proj_c_fwd", y_c, lw["proj_c"], "nn")
    gate_acts = [(proj, D_MODEL, 0), (proj, D_MODEL, 1), (proj, D_MODEL, 2),
                 (pa, D_MODEL, 0), (pb, D_MODEL, 0), (pc, D_MODEL, 0)]
    merged = _tok_fwd("gate_fwd", _gate_fn, gate_acts, [], [(D_MODEL, bf16)], 128)[0]
    x_next = _mm("w_out_fwd", merged, lw["w_out"], "nn", residual=x)
    saved = dict(x=x, h=h, proj=proj, y_a=y_a, ck_a=ck_a, y_pre=y_pre, ck_b=ck_b, y_b=y_b, y_c=y_c, ck_c=ck_c,
                 pa=pa, pb=pb, pc=pc, merged=merged)
    return x_next, saved, gathered


def _layer_bwd(dx_next, lw, sv, later_grads):
    rows = dx_next.shape[0]
    proj = sv["proj"]
    g = {}
    d_merged = _mm("w_out_bwd_x", dx_next, lw["w_out"], "nt", out_dtype=bf16)
    g["w_out"] = _mm("w_out_bwd_w", sv["merged"], dx_next, "tn", out_dtype=bf16)
    gate_acts = [(proj, D_MODEL, 0), (proj, D_MODEL, 1), (proj, D_MODEL, 2),
                 (sv["pa"], D_MODEL, 0), (sv["pb"], D_MODEL, 0), (sv["pc"], D_MODEL, 0)]
    dla, dlb, dlc, dpa, dpb, dpc = _tok_bwd(
        "gate_bwd", _gate_fn, gate_acts, [], [(d_merged, D_MODEL, 0)],
        [(0, bf16), (1, bf16), (2, bf16), (3, bf16), (4, bf16), (5, bf16)], 128)
    dy_a = _mm("proj_a_bwd_x", dpa, lw["proj_a"], "nt", out_dtype=bf16)
    dy_b = _mm("proj_b_bwd_x", dpb, lw["proj_b"], "nt")
    dy_c = _mm("proj_c_bwd_x", dpc, lw["proj_c"], "nt", out_dtype=bf16)
    g["proj_a"] = _mm("proj_a_bwd_w", sv["y_a"], dpa, "tn", out_dtype=bf16)
    g["proj_b"] = _mm("proj_b_bwd_w", sv["y_b"], dpb, "tn", out_dtype=bf16)
    g["proj_c"] = _mm("proj_c_bwd_w", sv["y_c"], dpc, "tn", out_dtype=bf16)

    m_al, m_dt, m_d = _lane_row(lw["m2_a_log"], 0), _lane_row(lw["m2_dt_bias"], 0), _lane_row(lw["m2_d"], 0)
    m_cb = lw["m2_conv_b"].reshape(1, -1)
    m_nw = lw["m2_norm_w"].reshape(1, M2_WIDTH)
    (dcx, dcb, dcc, dcz, dsmc, dcwx, dcwb, dcwc, dbx, dbb, dbc, dal, ddt, ddk, dnw) = _ssd_bwd(
        proj, dy_c, sv["ck_c"], lw["m2_conv_w"], m_cb, m_al, m_dt, m_d, m_nw)
    g["m2_conv_w"] = jnp.concatenate([dcwx, dcwb, dcwc], axis=1)
    g["m2_conv_b"] = jnp.concatenate([dbx, dbb, dbc], axis=1).reshape(-1)
    g["m2_a_log"] = jnp.sum(dal, axis=(0, 1))[:M2_HEADS]
    g["m2_dt_bias"] = jnp.sum(ddt, axis=(0, 1))[:M2_HEADS]
    g["m2_d"] = jnp.sum(ddk, axis=(0, 1))[:M2_HEADS]
    g["m2_norm_w"] = dnw.reshape(-1)
    dsmc = jnp.sum(dsmc, axis=0)

    glu_b = lw["s5_glu_b"].reshape(1, S5_WIDTH)
    dypre, dsg, dglu_w, dglu_b = _tok_bwd(
        "s5_tail_bwd", _s5_tail_fn, [(sv["y_pre"], S5_WIDTH, 0), (proj, S5_WIDTH, U_SG // S5_BLK)],
        [lw["s5_glu_w"], glu_b], [(dy_b, S5_WIDTH, 0)], [(0, f32), (1, bf16)], 256)
    g["s5_glu_w"] = dglu_w
    g["s5_glu_b"] = dglu_b.reshape(-1)
    s5_names = ("s5_lam_re", "s5_lam_im", "s5_log_step", "s5_b_re", "s5_b_im", "s5_c_re", "s5_c_im", "s5_d")
    s5p, s5_vjp = jax.vjp(_s5_prep, *[lw[k] for k in s5_names])
    dsu, da, dbbr, dbbi, dccr, dcci, dd = _s5_bwd(proj, dypre, sv["ck_b"], *s5p)
    for k, val in zip(s5_names, s5_vjp((da, dbbr, dbbi, dccr, dcci, dd))):
        g[k] = val

    g_al, g_dt = _lane_row(lw["gdn_a_log"], GDN_HEADS), _lane_row(lw["gdn_dt_bias"], GDN_HEADS)
    g_nw = lw["gdn_norm_w"].reshape(1, HEAD_DIM)
    (dq, dk, dv, daz, dsma, dcwq, dcwk, dcwv, dgal, dgdt, dgnw) = _gdn_bwd(
        proj, dy_a, sv["ck_a"], lw["gdn_conv_w"], g_al, g_dt, g_nw)
    g["gdn_conv_w"] = jnp.concatenate([dcwq, dcwk, dcwv], axis=1)
    g["gdn_a_log"] = jnp.sum(dgal, axis=(0, 1))[GDN_HEADS:2 * GDN_HEADS]
    g["gdn_dt_bias"] = jnp.sum(dgdt, axis=(0, 1))[GDN_HEADS:2 * GDN_HEADS]
    g["gdn_norm_w"] = jnp.sum(dgnw, axis=(0, 1))
    dsma = jnp.sum(dsma, axis=0)

    dproj = jnp.concatenate([dla, dlb, dlc, dsu, dsg, dq, dk, dv, daz, dcz, dcx, dcb, dcc, dsma.astype(bf16),
                             dsmc.astype(bf16), jnp.zeros((rows, 2 * LANE), bf16)], axis=1)
    if later_grads is None:
        g["w_in_p"], arrived = _mm("proj_bwd_w", sv["h"], dproj, "tn", out_dtype=bf16), None
    else:
        g["w_in_p"], *arrived = _mm("proj_bwd_w_scatter", sv["h"], dproj, "tn", out_dtype=bf16,
                                    comm=(later_grads, False))
    dh = _mm("proj_bwd_x", dproj, lw["w_in_p"], "nt")
    dx, dnorm = _tok_bwd("rms_bwd", _rms_fn, [(sv["x"], D_MODEL, 0)], [lw["norm_w"].reshape(1, D_MODEL)],
                         [(dh, D_MODEL, 0)], [(0, f32)], 256, residuals={0: dx_next})
    g["norm_w"] = dnorm.reshape(-1)
    return dx, g, arrived


def kernel(x, norm_w, w_in, gdn_conv_w, gdn_a_log, gdn_dt_bias, gdn_norm_w, s5_lam_re, s5_lam_im, s5_log_step, s5_b_re, s5_b_im, s5_c_re, s5_c_im, s5_d, s5_glu_w, s5_glu_b, m2_conv_w, m2_conv_b, m2_a_log, m2_dt_bias, m2_d, m2_norm_w, proj_a, proj_b, proj_c, w_out, final_norm_w, loss_target, m_norm_w, m_w_in, m_gdn_conv_w, m_gdn_a_log, m_gdn_dt_bias, m_gdn_norm_w, m_s5_lam_re, m_s5_lam_im, m_s5_log_step, m_s5_b_re, m_s5_b_im, m_s5_c_re, m_s5_c_im, m_s5_d, m_s5_glu_w, m_s5_glu_b, m_m2_conv_w, m_m2_conv_b, m_m2_a_log, m_m2_dt_bias, m_m2_d, m_m2_norm_w, m_proj_a, m_proj_b, m_proj_c, m_w_out, m_final_norm_w, v_norm_w, v_w_in, v_gdn_conv_w, v_gdn_a_log, v_gdn_dt_bias, v_gdn_norm_w, v_s5_lam_re, v_s5_lam_im, v_s5_log_step, v_s5_b_re, v_s5_b_im, v_s5_c_re, v_s5_c_im, v_s5_d, v_s5_glu_w, v_s5_glu_b, v_m2_conv_w, v_m2_conv_b, v_m2_a_log, v_m2_dt_bias, v_m2_d, v_m2_norm_w, v_proj_a, v_proj_b, v_proj_c, v_w_out, v_final_norm_w):
    w = dict(norm_w=norm_w, w_in=w_in, gdn_conv_w=gdn_conv_w, gdn_a_log=gdn_a_log, gdn_dt_bias=gdn_dt_bias,
             gdn_norm_w=gdn_norm_w, s5_lam_re=s5_lam_re, s5_lam_im=s5_lam_im, s5_log_step=s5_log_step,
             s5_b_re=s5_b_re, s5_b_im=s5_b_im, s5_c_re=s5_c_re, s5_c_im=s5_c_im, s5_d=s5_d, s5_glu_w=s5_glu_w,
             s5_glu_b=s5_glu_b, m2_conv_w=m2_conv_w, m2_conv_b=m2_conv_b, m2_a_log=m2_a_log, m2_dt_bias=m2_dt_bias,
             m2_d=m2_d, m2_norm_w=m2_norm_w, proj_a=proj_a, proj_b=proj_b, proj_c=proj_c, w_out=w_out,
             final_norm_w=final_norm_w)
    mom = dict(norm_w=m_norm_w, w_in=m_w_in, gdn_conv_w=m_gdn_conv_w, gdn_a_log=m_gdn_a_log,
               gdn_dt_bias=m_gdn_dt_bias, gdn_norm_w=m_gdn_norm_w, s5_lam_re=m_s5_lam_re, s5_lam_im=m_s5_lam_im,
               s5_log_step=m_s5_log_step, s5_b_re=m_s5_b_re, s5_b_im=m_s5_b_im, s5_c_re=m_s5_c_re,
               s5_c_im=m_s5_c_im, s5_d=m_s5_d, s5_glu_w=m_s5_glu_w, s5_glu_b=m_s5_glu_b, m2_conv_w=m_m2_conv_w,
               m2_conv_b=m_m2_conv_b, m2_a_log=m_m2_a_log, m2_dt_bias=m_m2_dt_bias, m2_d=m_m2_d,
               m2_norm_w=m_m2_norm_w, proj_a=m_proj_a, proj_b=m_proj_b, proj_c=m_proj_c, w_out=m_w_out,
               final_norm_w=m_final_norm_w)
    var = dict(norm_w=v_norm_w, w_in=v_w_in, gdn_conv_w=v_gdn_conv_w, gdn_a_log=v_gdn_a_log,
               gdn_dt_bias=v_gdn_dt_bias, gdn_norm_w=v_gdn_norm_w, s5_lam_re=v_s5_lam_re, s5_lam_im=v_s5_lam_im,
               s5_log_step=v_s5_log_step, s5_b_re=v_s5_b_re, s5_b_im=v_s5_b_im, s5_c_re=v_s5_c_re,
               s5_c_im=v_s5_c_im, s5_d=v_s5_d, s5_glu_w=v_s5_glu_w, s5_glu_b=v_s5_glu_b, m2_conv_w=v_m2_conv_w,
               m2_conv_b=v_m2_conv_b, m2_a_log=v_m2_a_log, m2_dt_bias=v_m2_dt_bias, m2_d=v_m2_d,
               m2_norm_w=v_m2_norm_w, proj_a=v_proj_a, proj_b=v_proj_b, proj_c=v_proj_c, w_out=v_w_out,
               final_norm_w=v_final_norm_w)
    me = lax.axis_index("x") * 4 + lax.axis_index("y") * 2 + lax.axis_index("c")
    x2 = x[0]
    tgt = loss_target[0]

    ra, rb = proj_a.shape[1], proj_b.shape[1]
    pabc = jnp.concatenate([proj_a, proj_b, proj_c], axis=1).astype(bf16)
    w_in16, glu16, w_out16 = w_in.astype(bf16), s5_glu_w.astype(bf16), w_out.astype(bf16)

    def shards(i):
        return [w_in16[i], glu16[i], pabc[i], w_out16[i], gdn_conv_w[i], m2_conv_w[i]]

    def cols(gathered):
        return jnp.concatenate([gathered[d] for d in range(N_DEV)], axis=1)

    def rows_of(gathered):
        return gathered.reshape(-1, gathered.shape[-1])

    def full_weights(i, gathered):
        g_win, g_glu, g_pabc, g_wout, g_gcv, g_mcv = gathered
        pf = cols(g_pabc)
        lw = dict(w_in_p=_win_to_padded(cols(g_win)), s5_glu_w=rows_of(g_glu), proj_a=pf[:ra],
                  proj_b=pf[ra:ra + rb], proj_c=pf[ra + rb:], w_out=rows_of(g_wout),
                  gdn_conv_w=cols(g_gcv), m2_conv_w=cols(g_mcv))
        for k in SMALL:
            if k != "final_norm_w":
                lw[k] = w[k][i]
        return lw

    layers, saved = [], []
    act = x2
    gathered = _exchange("gather_weights", shards(0), True)
    for i in range(DEPTH):
        layers.append(full_weights(i, gathered))
        act, sv, gathered = _layer_fwd(act, layers[i], shards(i + 1) if i + 1 < DEPTH else None)
        saved.append(sv)
    loss_row, dact, dfinal = _loss_grad(act, final_norm_w.reshape(1, D_MODEL), tgt)
    loss = lax.psum(loss_row[0, 0], ("x", "y", "c"))

    def col_blocks(full):
        r = full.shape[0]
        return full.reshape(r, N_DEV, -1).transpose(1, 0, 2)

    def row_blocks(full):
        return full.reshape(N_DEV, -1, full.shape[1])

    def packed_grads(g):
        return [col_blocks(_win_from_padded(g["w_in_p"])), row_blocks(g["s5_glu_w"].astype(bf16)),
                col_blocks(jnp.concatenate([g["proj_a"], g["proj_b"], g["proj_c"]], axis=0)), row_blocks(g["w_out"])]

    grads, arrived = [None] * DEPTH, [None] * DEPTH
    later = None
    for i in reversed(range(DEPTH)):
        dact, grads[i], got = _layer_bwd(dact, layers[i], saved[i], later)
        if got is not None:
            arrived[i + 1] = got
        saved[i] = None
        later = packed_grads(grads[i])
    arrived[0] = _exchange("scatter_grads", later, False)

    g_out, delta, new_m, new_v = {}, {}, {}, {}
    for k, buf, row_off in (("w_in", 0, 0), ("s5_glu_w", 1, 0), ("proj_a", 2, 0), ("proj_b", 2, ra),
                            ("proj_c", 2, ra + rb), ("w_out", 3, 0)):
        recvs = [arrived[i][buf] for i in range(DEPTH)]
        g_out[k], delta[k], new_m[k], new_v[k] = _adamw_slots("adamw_" + k, recvs, row_off, w[k], mom[k], var[k])

    names_small = [k for k in SMALL if k != "final_norm_w"]
    small_parts = [jnp.stack([grads[i][k] for i in range(DEPTH)]) for k in names_small + list(CONV)]
    small_parts.append(dfinal.reshape(-1))
    small_shapes = [p.shape for p in small_parts]
    small_local = _pack(small_parts)
    small_sum = _sum_slots("sum_small", _exchange("gather_small", [small_local], True)[0])
    unp = _unpack(small_sum, small_shapes)
    for k, val in zip(names_small, unp):
        g_out[k] = val
    for j, k in enumerate(CONV):
        width = w[k].shape[2]
        g_out[k] = lax.dynamic_slice_in_dim(unp[len(names_small) + j], me * width, width, axis=2)
    g_out["final_norm_w"] = unp[-1]

    rest = [k for k in WEIGHTS if k not in BIG]
    rest_shapes = [w[k].shape for k in rest]
    packed = [_pack([src[k] for k in rest]) for src in (w, g_out, mom, var)]
    d, nm, nv = _adamw("adamw_small", *packed)
    for dst, arr in ((delta, d), (new_m, nm), (new_v, nv)):
        for k, val in zip(rest, _unpack(arr, rest_shapes)):
            dst[k] = val

    grad_x = dact.reshape(x.shape)
    return (loss, grad_x, *[g_out[k] for k in WEIGHTS], *[delta[k] for k in WEIGHTS],
            *[new_m[k] for k in WEIGHTS], *[new_v[k] for k in WEIGHTS])
```

```python
import functools

import jax
import jax.numpy as jnp
from jax import lax
from jax.experimental import pallas as pl
from jax.experimental.pallas import tpu as pltpu

f32 = jnp.float32
bf16 = jnp.bfloat16

N_DEV = 8
DEPTH = 4
D_MODEL = 2048
GDN_HEADS = 8
HEAD_DIM = 128
GDN_WIDTH = 1024
S5_GROUPS = 48
S5_GROUP_SIZE = 16
S5_STATE = 64
S5_WIDTH = 768
S5_LANES = S5_GROUPS * S5_STATE
S5_BLK = 6
M2_HEADS = 16
M2_HEAD_DIM = 64
M2_WIDTH = 1024
M2_GROUPS = 4
M2_STATE = 128
CONV_K = 4
CHUNK = 64
HALO = 8
NORM_EPS = 1e-6
IN_DIM = 14880
LANE = 128
VMEM_LIMIT = 48 * 1024 * 1024

ADAM_LR = 0.001
ADAM_B1 = 0.9
ADAM_B2 = 0.999
ADAM_EPS = 1e-08
ADAM_WD = 0.01
ADAM_STEP = 10

U_MERGE, U_SU, U_SG, U_Q, U_K, U_V, U_AZ, U_CZ, U_CX, U_CB, U_CC, U_SMA, U_SMC = (
    0, 48, 54, 60, 68, 76, 84, 92, 100, 108, 112, 116, 117)
NP_UNITS = 120
NP_COLS = NP_UNITS * LANE


def _cparams(sem=None):
    return pltpu.CompilerParams(dimension_semantics=sem, vmem_limit_bytes=VMEM_LIMIT)


def _pick(dim, target):
    if dim <= target:
        return dim
    for t in range(target - target % LANE, 0, -LANE):
        if dim % t == 0:
            return t
    raise ValueError(f"no tile for {dim}")


def _bd(a, b, dims):
    return lax.dot_general(a.astype(bf16), b.astype(bf16), (dims, ((), ())), preferred_element_type=f32)


@jax.custom_vjp
def dot_nn(a, b):
    return _bd(a, b, ((1,), (0,)))


@jax.custom_vjp
def dot_nt(a, b):
    return _bd(a, b, ((1,), (1,)))


@jax.custom_vjp
def dot_tn(a, b):
    return _bd(a, b, ((0,), (0,)))


dot_nn.defvjp(lambda a, b: (dot_nn(a, b), (a, b)), lambda r, ct: (dot_nt(ct, r[1]), dot_tn(r[0], ct)))
dot_nt.defvjp(lambda a, b: (dot_nt(a, b), (a, b)), lambda r, ct: (dot_nn(ct, r[1]), dot_tn(ct, r[0])))
dot_tn.defvjp(lambda a, b: (dot_tn(a, b), (a, b)), lambda r, ct: (dot_nt(r[1], ct), dot_nn(r[0], ct)))

_HI = lax.Precision.HIGHEST


def _silu(x):
    return x * jax.nn.sigmoid(x)


def _softplus(x):
    return jnp.maximum(x, 0.0) + jnp.log(1.0 + jnp.exp(-jnp.abs(x)))


def _tri_masks(c):
    row = lax.broadcasted_iota(jnp.int32, (c, c), 0)
    col = lax.broadcasted_iota(jnp.int32, (c, c), 1)
    return row >= col, row > col, (row >= col).astype(f32), (row == col).astype(f32)


def _scan_add(x, reverse):
    t = x.shape[0]
    row = lax.broadcasted_iota(jnp.int32, x.shape, 0)
    d = 1
    while d < t:
        if reverse:
            x = x + jnp.where(row < t - d, pltpu.roll(x, t - d, 0), 0.0)
        else:
            x = x + jnp.where(row >= d, pltpu.roll(x, d, 0), 0.0)
        d *= 2
    return x


@jax.custom_vjp
def _cumsum_rows(x):
    return _scan_add(x, False)


_cumsum_rows.defvjp(lambda x: (_scan_add(x, False), None), lambda _, ct: (_scan_add(ct, True),))


def _cumsum_all(a):
    cum = _cumsum_rows(a)
    return cum, cum.T


def _onehot_lane(idx):
    return (lax.broadcasted_iota(jnp.int32, (1, LANE), 1) == idx).astype(f32)


def _onehot_sub(idx):
    return (lax.broadcasted_iota(jnp.int32, (LANE, 1), 0) == idx).astype(f32)


def _pick_col(x, idx):
    return jnp.sum(x * _onehot_lane(idx), axis=-1, keepdims=True)


def _pick_row(xt, idx):
    return jnp.sum(xt * _onehot_sub(idx), axis=0, keepdims=True)


def _peer(idx):
    return (idx // 4, (idx // 2) % 2, idx % 2)


def _remote(src, dst, send_sems, recv_sems, k, to):
    return pltpu.make_async_remote_copy(src_ref=src, dst_ref=dst, send_sem=send_sems.at[k], recv_sem=recv_sems.at[k],
                                        device_id=_peer(to), device_id_type=pl.DeviceIdType.MESH)


def _scatter_phase(phase, srcs, outs, send_sems, recv_sems, local_sems):
    nb = len(srcs)
    me = lax.axis_index("x") * 4 + lax.axis_index("y") * 2 + lax.axis_index("c")
    if phase == "forward":
        return
    mine = [pltpu.make_async_copy(srcs[b].at[me], outs[b].at[me], local_sems.at[b]) for b in range(nb)]
    sends = [_remote(srcs[b].at[(me + k) % N_DEV], outs[b].at[me], send_sems, recv_sems, (k - 1) * nb + b,
                     (me + k) % N_DEV) for k in range(1, N_DEV) for b in range(nb)]
    if phase == "start":
        for cp in mine + sends:
            cp.start()
    if phase == "finish":
        for k in range(1, N_DEV):
            frm = (me + N_DEV - k) % N_DEV
            for b in range(nb):
                _remote(srcs[b].at[me], outs[b].at[frm], send_sems, recv_sems, (k - 1) * nb + b, frm).wait_recv()
        for cp in sends:
            cp.wait_send()
        for cp in mine:
            cp.wait()


def _gather_phase(phase, srcs, outs, send_sems, recv_sems, local_sems):
    nb = len(srcs)
    x, y, c = lax.axis_index("x"), lax.axis_index("y"), lax.axis_index("c")
    me, sib = 4 * x + 2 * y + c, 4 * x + 2 * y + 1 - c
    chips = [(1 - x, y), (x, 1 - y), (1 - x, 1 - y)]
    same = [4 * cx + 2 * cy + c for cx, cy in chips]
    other = [4 * cx + 2 * cy + 1 - c for cx, cy in chips]

    def copy(b, k, src, block, to):
        return _remote(src, outs[b].at[block], send_sems, recv_sems, k * nb + b, to)

    def local():
        return [pltpu.make_async_copy(srcs[b], outs[b].at[me], local_sems.at[b]) for b in range(nb)]

    def first():
        return ([copy(b, 0, srcs[b], me, sib) for b in range(nb)]
                + [copy(b, 1 + j, srcs[b], me, same[j]) for j in range(3) for b in range(nb)])

    def passed(j, b):
        return copy(b, 4 + j, outs[b].at[same[j]], same[j], sib)

    if phase == "start":
        for cp in local() + first():
            cp.start()
    if phase == "forward":
        for j in range(3):
            for b in range(nb):
                copy(b, 1 + j, srcs[b], same[j], me).wait_recv()
                passed(j, b).start()
    if phase == "finish":
        for b in range(nb):
            copy(b, 0, srcs[b], sib, me).wait_recv()
        for j in range(3):
            for b in range(nb):
                copy(b, 4 + j, srcs[b], other[j], me).wait_recv()
        for cp in first() + [passed(j, b) for j in range(3) for b in range(nb)]:
            cp.wait_send()
        for cp in local():
            cp.wait()


def _comm_phase(phase, *refs, gather):
    (_gather_phase if gather else _scatter_phase)(phase, *refs)


def _comm_shapes(bufs, gather):
    nb = len(bufs)
    out_shape = [jax.ShapeDtypeStruct(((N_DEV,) + tuple(b.shape)) if gather else tuple(b.shape), b.dtype) for b in bufs]
    sems = [pltpu.SemaphoreType.DMA(((N_DEV - 1) * nb,)), pltpu.SemaphoreType.DMA(((N_DEV - 1) * nb,)),
            pltpu.SemaphoreType.DMA((nb,))]
    return out_shape, sems


def _exchange(name, bufs, gather):
    nb = len(bufs)

    def body(*refs):
        args = (refs[:nb], refs[nb:2 * nb], *refs[2 * nb:])
        for phase in ("start", "forward", "finish"):
            _comm_phase(phase, *args, gather=gather)

    any_spec = pl.BlockSpec(memory_space=pl.ANY)
    out_shape, sems = _comm_shapes(bufs, gather)
    return pl.pallas_call(body, name=name, in_specs=[any_spec] * nb, out_specs=[any_spec] * nb, out_shape=out_shape,
                          scratch_shapes=sems)(*bufs)


def _mm(name, a, b, mode, residual=None, out_dtype=f32, tm=1024, tn=1024, tk=1024, comm=None):
    if mode == "nn":
        (m, k), (_, n) = a.shape, b.shape
    elif mode == "nt":
        (m, k), (n, _) = a.shape, b.shape
    else:
        (k, m), (_, n) = a.shape, b.shape
    tm, tn, tk = _pick(m, tm), _pick(n, tn), _pick(k, tk)
    gm, gn, nk = m // tm, n // tn, k // tk
    dims = {"nn": ((1,), (0,)), "nt": ((1,), (1,)), "tn": ((0,), (0,))}[mode]
    has_res = residual is not None
    bufs, gather = comm if comm is not None else ([], True)
    nb = len(bufs)
    n_in = 2 + has_res
    forward_step = gm * gn * nk * 3 // 4

    def body(*refs):
        a_ref, b_ref = refs[:2]
        r_ref = refs[2] if has_res else None
        srcs = refs[n_in:n_in + nb]
        o_ref = refs[n_in + nb]
        outs = refs[n_in + nb + 1:n_in + 2 * nb + 1]
        scratch = refs[n_in + 2 * nb + 1:]
        acc_ref = scratch[0] if nk > 1 else None
        sems = scratch[1:] if nk > 1 else scratch
        i, j, kk = pl.program_id(0), pl.program_id(1), pl.program_id(2)

        step = (i * gn + j) * nk + kk
        if nb:
            @pl.when(step == 0)
            def _():
                _comm_phase("start", srcs, outs, *sems, gather=gather)

            @pl.when(step == forward_step)
            def _():
                _comm_phase("forward", srcs, outs, *sems, gather=gather)

        def finish(out):
            if has_res:
                out = out + r_ref[...].astype(f32)
            o_ref[...] = out.astype(o_ref.dtype)

        if nk == 1:
            finish(_bd(a_ref[...], b_ref[...], dims))
        else:
            @pl.when(kk == 0)
            def _():
                acc_ref[...] = jnp.zeros_like(acc_ref)

            acc_ref[...] += _bd(a_ref[...], b_ref[...], dims)

            @pl.when(kk == nk - 1)
            def _():
                finish(acc_ref[...])

        if nb:
            @pl.when(step == gm * gn * nk - 1)
            def _():
                _comm_phase("finish", srcs, outs, *sems, gather=gather)

    if mode == "tn":
        a_spec = pl.BlockSpec((tk, tm), lambda i, j, kk: (kk, i))
    else:
        a_spec = pl.BlockSpec((tm, tk), lambda i, j, kk: (i, kk))
    if mode == "nt":
        b_spec = pl.BlockSpec((tn, tk), lambda i, j, kk: (j, kk))
    else:
        b_spec = pl.BlockSpec((tk, tn), lambda i, j, kk: (kk, j))
    o_spec = pl.BlockSpec((tm, tn), lambda i, j, kk: (i, j))
    any_spec = pl.BlockSpec(memory_space=pl.ANY)
    in_specs = [a_spec, b_spec] + ([o_spec] if has_res else []) + [any_spec] * nb
    args = (a, b) + ((residual,) if has_res else ()) + tuple(bufs)
    comm_shapes, comm_sems = _comm_shapes(bufs, gather) if nb else ([], [])
    res = pl.pallas_call(
        body, name=name, grid=(gm, gn, nk), in_specs=in_specs, out_specs=[o_spec] + [any_spec] * nb,
        out_shape=[jax.ShapeDtypeStruct((m, n), out_dtype)] + comm_shapes,
        scratch_shapes=([pltpu.VMEM((tm, tn), f32)] if nk > 1 else []) + comm_sems,
        compiler_params=_cparams(("arbitrary",) * 3 if nb else ("parallel", "parallel", "arbitrary")))(*args)
    return res if nb else res[0]


def _act_spec(t, width, colblk):
    return pl.BlockSpec((t, width), lambda i: (i, colblk))


def _tok_fwd(name, fn, acts, params, outs, t):
    rows = acts[0][0].shape[0]
    t = min(t, rows)
    na, npar = len(acts), len(params)

    def body(*refs):
        a = [r[...].astype(f32) for r in refs[:na]]
        p = [r[...].astype(f32) for r in refs[na:na + npar]]
        res = fn(*a, *p)
        for o_ref, o in zip(refs[na + npar:], res):
            o_ref[...] = o.astype(o_ref.dtype)

    in_specs = [_act_spec(t, w, cb) for (_, w, cb) in acts]
    in_specs += [pl.BlockSpec(p.shape, lambda i: (0, 0)) for p in params]
    res = pl.pallas_call(
        body, name=name, grid=(rows // t,), in_specs=in_specs,
        out_specs=[_act_spec(t, w, 0) for (w, _) in outs],
        out_shape=[jax.ShapeDtypeStruct((rows, w), dt) for (w, dt) in outs],
        compiler_params=_cparams(("arbitrary",)))(*[a for (a, _, _) in acts], *params)
    return res


def _tok_bwd(name, fn, acts, params, cts, dact, t, residuals=None):
    rows = acts[0][0].shape[0]
    t = min(t, rows)
    residuals = residuals or {}
    res_ids = sorted(residuals)
    na, npar, nc, nr, nd = len(acts), len(params), len(cts), len(res_ids), len(dact)

    def body(*refs):
        a = [r[...].astype(f32) for r in refs[:na]]
        p = [r[...].astype(f32) for r in refs[na:na + npar]]
        ct = tuple(r[...].astype(f32) for r in refs[na + npar:na + npar + nc])
        rs = {idx: r[...].astype(f32) for idx, r in zip(res_ids, refs[na + npar + nc:na + npar + nc + nr])}
        orefs = refs[na + npar + nc + nr:]
        _, vjp = jax.vjp(fn, *a, *p)
        grads = vjp(ct)
        for o_ref, (idx, _) in zip(orefs[:nd], dact):
            g = grads[idx]
            if idx in rs:
                g = g + rs[idx]
            o_ref[...] = g.astype(o_ref.dtype)

        if npar:
            @pl.when(pl.program_id(0) == 0)
            def _():
                for o_ref in orefs[nd:]:
                    o_ref[...] = jnp.zeros_like(o_ref)

            for o_ref, g in zip(orefs[nd:], grads[na:]):
                o_ref[...] += g

    in_specs = [_act_spec(t, w, cb) for (_, w, cb) in acts]
    in_specs += [pl.BlockSpec(p.shape, lambda i: (0, 0)) for p in params]
    in_specs += [_act_spec(t, w, cb) for (_, w, cb) in cts]
    in_specs += [_act_spec(t, acts[idx][1], 0) for idx in res_ids]
    out_specs = [_act_spec(t, acts[idx][1], 0) for (idx, _) in dact]
    out_specs += [pl.BlockSpec(p.shape, lambda i: (0, 0)) for p in params]
    out_shape = [jax.ShapeDtypeStruct((rows, acts[idx][1]), dt) for (idx, dt) in dact]
    out_shape += [jax.ShapeDtypeStruct(p.shape, f32) for p in params]
    return pl.pallas_call(
        body, name=name, grid=(rows // t,), in_specs=in_specs, out_specs=out_specs, out_shape=out_shape,
        compiler_params=_cparams(("arbitrary",)))(
            *[a for (a, _, _) in acts], *params, *[c for (c, _, _) in cts], *[residuals[i] for i in res_ids])


def _rms_fn(x, w):
    return (x * lax.rsqrt(jnp.mean(x * x, axis=-1, keepdims=True) + NORM_EPS) * w,)


def _gate_fn(la, lb, lc, pa, pb, pc):
    return (jax.nn.sigmoid(la) * pa + jax.nn.sigmoid(lb) * pb + jax.nn.sigmoid(lc) * pc,)


def _s5_tail_fn(ypre, gate, glu_w, glu_b):
    y = jax.nn.gelu(ypre)
    y = y * jax.nn.sigmoid(dot_nn(y, glu_w) + glu_b)
    return (y * _silu(gate),)


def _loss_grad(x, w, target, t=256):
    rows, d = x.shape
    t = min(t, rows)

    def fn(xt, wt, tt):
        y = _rms_fn(xt, wt)[0]
        err = y - tt
        return 0.5 * jnp.sum(jnp.sum(err * err, axis=-1, keepdims=True), axis=0, keepdims=True) / d

    def body(x_ref, w_ref, t_ref, loss_ref, dx_ref, dw_ref):
        tt = t_ref[...]
        val, vjp = jax.vjp(lambda a, b: fn(a, b, tt), x_ref[...], w_ref[...])
        dx, dw = vjp(jnp.ones((1, 1), f32))
        dx_ref[...] = dx

        @pl.when(pl.program_id(0) == 0)
        def _():
            loss_ref[...] = jnp.zeros_like(loss_ref)
            dw_ref[...] = jnp.zeros_like(dw_ref)

        loss_ref[...] += val * jnp.ones((1, LANE), f32)
        dw_ref[...] += dw

    return pl.pallas_call(
        body, name="loss_grad", grid=(rows // t,),
        in_specs=[_act_spec(t, d, 0), pl.BlockSpec((1, d), lambda i: (0, 0)), _act_spec(t, d, 0)],
        out_specs=[pl.BlockSpec((1, LANE), lambda i: (0, 0)), _act_spec(t, d, 0), pl.BlockSpec((1, d), lambda i: (0, 0))],
        out_shape=[jax.ShapeDtypeStruct((1, LANE), f32), jax.ShapeDtypeStruct((rows, d), f32),
                   jax.ShapeDtypeStruct((1, d), f32)],
        compiler_params=_cparams(("arbitrary",)))(x, w, target)


GDN_HB = 4
GDN_SW = GDN_HB * HEAD_DIM
GDN_STEPS = GDN_HEADS // GDN_HB


def _tri_inv(a, eye, c):
    rows = a.shape[0]
    n = -a
    p = eye + n
    npow = dot_nn(n, n)
    levels = c.bit_length() - 1
    for j in range(2, levels):
        both = dot_nn(jnp.concatenate([p, npow], axis=0), npow)
        p, npow = p + both[:rows], both[rows:]
    return p + dot_nn(p, npow)


def _block_ids(rows, c):
    ri = lax.broadcasted_iota(jnp.int32, (rows, rows), 0)
    ci = lax.broadcasted_iota(jnp.int32, (rows, rows), 1)
    r1 = lax.broadcasted_iota(jnp.int32, (rows, 1), 0)
    rb, cb, r1b = 0, 0, 0
    for edge in range(c, rows, c):
        rb = rb + (ri >= edge).astype(jnp.int32)
        cb = cb + (ci >= edge).astype(jnp.int32)
        r1b = r1b + (r1 >= edge).astype(jnp.int32)
    return ri, ci, rb, cb, r1b


def _gdn_step(qc, kc, vc, z, small, alog_row, dtb_row, normw, s_cat, head0):
    c = qc.shape[0]
    hb = GDN_HB
    rows = hb * c

    def stack(x):
        return jnp.concatenate([x[:, r * HEAD_DIM:(r + 1) * HEAD_DIM] for r in range(hb)], axis=0)

    ri, ci, rb, cb, r1b = _block_ids(rows, c)
    same = rb == cb
    causal = same & (ri >= ci)
    strict = same & (ri > ci)
    eye = (ri == ci).astype(f32)
    head_rows = [(r1b == r).astype(f32) for r in range(hb)]

    def own_block(x):
        acc = None
        for r in range(hb):
            term = x[:, r * HEAD_DIM:(r + 1) * HEAD_DIM] * head_rows[r]
            acc = term if acc is None else acc + term
        return acc

    beta_all = jax.nn.sigmoid(small)
    g_all = -jnp.exp(alog_row) * _softplus(small + dtb_row)
    gc_all, gct_all = _cumsum_all(g_all)
    beta = jnp.concatenate([_pick_col(beta_all, head0 + r) for r in range(hb)], axis=0)
    gc = jnp.concatenate([_pick_col(gc_all, head0 + r + GDN_HEADS) for r in range(hb)], axis=0)
    gc_t = jnp.concatenate([_pick_row(gct_all, head0 + r + GDN_HEADS) for r in range(hb)], axis=1)
    g_last = [gc[(r + 1) * c - 1:(r + 1) * c, :] for r in range(hb)]
    gl = sum(head_rows[r] * g_last[r] for r in range(hb))

    q = _silu(stack(qc))
    k = _silu(stack(kc))
    v = _silu(stack(vc))
    q = q * lax.rsqrt(jnp.sum(q * q, axis=-1, keepdims=True) + NORM_EPS) * (HEAD_DIM ** -0.5)
    k = k * lax.rsqrt(jnp.sum(k * k, axis=-1, keepdims=True) + NORM_EPS)
    decay = jnp.exp(jnp.where(causal, gc - gc_t, -1e30))
    egc = jnp.exp(gc)
    kb = k * beta
    a_mat = jnp.where(strict, dot_nt(kb, k) * decay, 0.0)
    t_inv = _tri_inv(a_mat, eye, c)
    uw = dot_nn(t_inv, jnp.concatenate([v * beta, kb * egc], axis=1))
    u, w = uw[:, :HEAD_DIM], uw[:, HEAD_DIM:]
    qk = dot_nt(q, k) * decay
    on_state = dot_nn(jnp.concatenate([w, q * egc], axis=0), s_cat)
    v_new = u - own_block(on_state[:rows])
    out = own_block(on_state[rows:]) + dot_nn(qk, v_new)
    k_tail = k * jnp.exp(gl - gc)
    v_bd = jnp.concatenate([v_new * head_rows[r] for r in range(hb)], axis=1)
    eg_cat = jnp.concatenate([jnp.exp(g_last[r]) * jnp.ones((1, HEAD_DIM), f32) for r in range(hb)], axis=1)
    new_s = s_cat * eg_cat + dot_tn(k_tail, v_bd)
    o = out * lax.rsqrt(jnp.mean(out * out, axis=-1, keepdims=True) + NORM_EPS) * normw * _silu(stack(z))
    o = jnp.concatenate([o[r * c:(r + 1) * c] for r in range(hb)], axis=1)
    return o, new_s


def _conv_windows(xin_ref, p, cw_ref, c):
    acc = None
    for k in range(CONV_K):
        term = cw_ref[pl.ds(k, 1), :] * xin_ref[p, pl.ds(HALO - CONV_K + 1 + k, c), :]
        acc = term if acc is None else acc + term
    return acc


def _gdn_fwd(proj, conv_w, alog_row, dtb_row, normw):
    rows = proj.shape[0]
    c = min(CHUNK, rows)
    n = rows // c

    def body(q_ref, k_ref, v_ref, z_ref, sm_ref, cwq, cwk, cwv, al_ref, dt_ref, nw_ref, y_ref, ck_ref, s_ref, xin_ref):
        hb = pl.program_id(0)
        i = pl.program_id(1)

        @pl.when(i == 0)
        def _():
            s_ref[...] = jnp.zeros_like(s_ref)
            xin_ref[:, 0:HALO, :] = jnp.zeros((3, HALO, GDN_SW), f32)

        @pl.when(i > 0)
        def _():
            xin_ref[:, 0:HALO, :] = xin_ref[:, c:c + HALO, :]

        xin_ref[0, HALO:, :] = q_ref[...]
        xin_ref[1, HALO:, :] = k_ref[...]
        xin_ref[2, HALO:, :] = v_ref[...]
        qc = _conv_windows(xin_ref, 0, cwq, c)
        kc = _conv_windows(xin_ref, 1, cwk, c)
        vc = _conv_windows(xin_ref, 2, cwv, c)
        state = s_ref[...]
        ck_ref[...] = state
        o, new_state = _gdn_step(qc, kc, vc, z_ref[...], sm_ref[...], al_ref[...], dt_ref[...], nw_ref[...], state,
                                 hb * GDN_HB)
        y_ref[...] = o.astype(y_ref.dtype)
        s_ref[...] = new_state

    def blk(unit):
        return pl.BlockSpec((c, GDN_SW), lambda hb, i: (i, unit // GDN_HB + hb))

    def cw(part):
        return pl.BlockSpec((CONV_K, GDN_SW), lambda hb, i: (0, part * GDN_STEPS + hb))

    row = pl.BlockSpec((1, LANE), lambda hb, i: (0, 0))
    return pl.pallas_call(
        body, name="gdn_fwd", grid=(GDN_STEPS, n),
        in_specs=[blk(U_Q), blk(U_K), blk(U_V), blk(U_AZ), pl.BlockSpec((c, LANE), lambda hb, i: (i, U_SMA)),
                  cw(0), cw(1), cw(2), row, row, row],
        out_specs=[pl.BlockSpec((c, GDN_SW), lambda hb, i: (i, hb)),
                   pl.BlockSpec((None, None, HEAD_DIM, GDN_SW), lambda hb, i: (hb, i, 0, 0))],
        out_shape=[jax.ShapeDtypeStruct((rows, GDN_WIDTH), bf16),
                   jax.ShapeDtypeStruct((GDN_STEPS, n, HEAD_DIM, GDN_SW), f32)],
        scratch_shapes=[pltpu.VMEM((HEAD_DIM, GDN_SW), f32), pltpu.VMEM((3, c + HALO, GDN_SW), f32)],
        compiler_params=_cparams(("arbitrary", "arbitrary")))(
            proj, proj, proj, proj, proj, conv_w, conv_w, conv_w, alog_row, dtb_row, normw)


def _conv_bwd(xin_ref, dyext_ref, p, cw_ref, dxc, dx_ref, dcw_ref, c):
    dyext_ref[p, 0:c, :] = dxc
    acc = None
    for k in range(CONV_K):
        term = cw_ref[pl.ds(k, 1), :] * dyext_ref[p, pl.ds(CONV_K - 1 - k, c), :]
        acc = term if acc is None else acc + term
        dcw_ref[pl.ds(k, 1), :] += jnp.sum(xin_ref[p, pl.ds(HALO - CONV_K + 1 + k, c), :] * dxc, axis=0, keepdims=True)
    dx_ref[...] = acc.astype(dx_ref.dtype)


def _gdn_bwd(proj, dy, ck, conv_w, alog_row, dtb_row, normw, scatter=None):
    rows = proj.shape[0]
    c = min(CHUNK, rows)
    n = rows // c
    halo_blocks = c // HALO
    bufs = scatter or []
    nb = len(bufs)
    n_in, n_out, n_scr = 16, 11, 3

    def body(*refs):
        core = refs[:n_in] + refs[n_in + nb:n_in + nb + n_out] + refs[n_in + 2 * nb + n_out:n_in + 2 * nb + n_out + n_scr]
        comm = (refs[n_in:n_in + nb], refs[n_in + nb + n_out:n_in + 2 * nb + n_out], *refs[n_in + 2 * nb + n_out + n_scr:])
        step_id = pl.program_id(0) * n + pl.program_id(1)
        if nb:
            @pl.when(step_id == 0)
            def _():
                _comm_phase("start", *comm, gather=False)

        chunk_step(*core)
        if nb:
            @pl.when(step_id == GDN_STEPS * n - 1)
            def _():
                _comm_phase("finish", *comm, gather=False)

    def chunk_step(q_ref, k_ref, v_ref, hq_ref, hk_ref, hv_ref, z_ref, sm_ref, cwq, cwk, cwv, al_ref, dt_ref, nw_ref,
                   ck_ref, dy_ref, dq_ref, dk_ref, dv_ref, dz_ref, dsm_ref, dcwq, dcwk, dcwv, dal_ref, ddt_ref, dnw_ref,
                   ds_ref, xin_ref, dyext_ref):
        hb = pl.program_id(0)
        i = pl.program_id(1)
        ci = n - 1 - i

        @pl.when(i == 0)
        def _():
            ds_ref[...] = jnp.zeros_like(ds_ref)
            dyext_ref[:, c:c + HALO, :] = jnp.zeros((3, HALO, GDN_SW), f32)
            for r in (dcwq, dcwk, dcwv, dal_ref, ddt_ref, dnw_ref):
                r[...] = jnp.zeros_like(r)

        @pl.when(i > 0)
        def _():
            dyext_ref[:, c:c + HALO, :] = dyext_ref[:, 0:HALO, :]

        first = (ci > 0).astype(f32)
        for p, (x_ref, halo_ref) in enumerate(((q_ref, hq_ref), (k_ref, hk_ref), (v_ref, hv_ref))):
            xin_ref[p, 0:HALO, :] = halo_ref[...] * first
            xin_ref[p, HALO:, :] = x_ref[...]
        qc = _conv_windows(xin_ref, 0, cwq, c)
        kc = _conv_windows(xin_ref, 1, cwk, c)
        vc = _conv_windows(xin_ref, 2, cwv, c)
        fn = functools.partial(_gdn_step, head0=hb * GDN_HB)
        _, vjp = jax.vjp(fn, qc, kc, vc, z_ref[...], sm_ref[...], al_ref[...], dt_ref[...], nw_ref[...], ck_ref[...])
        dqc, dkc, dvc, dz, dsm, dal, ddt, dnw, dstate = vjp((dy_ref[...].astype(f32), ds_ref[...]))
        ds_ref[...] = dstate
        dz_ref[...] = dz.astype(dz_ref.dtype)
        dsm_ref[...] = dsm
        dal_ref[...] += dal
        ddt_ref[...] += ddt
        dnw_ref[...] += dnw
        _conv_bwd(xin_ref, dyext_ref, 0, cwq, dqc, dq_ref, dcwq, c)
        _conv_bwd(xin_ref, dyext_ref, 1, cwk, dkc, dk_ref, dcwk, c)
        _conv_bwd(xin_ref, dyext_ref, 2, cwv, dvc, dv_ref, dcwv, c)

    def blk(unit):
        return pl.BlockSpec((c, GDN_SW), lambda hb, i: (n - 1 - i, unit // GDN_HB + hb))

    def halo(unit):
        return pl.BlockSpec((HALO, GDN_SW),
                            lambda hb, i: (jnp.maximum((n - 1 - i) * halo_blocks - 1, 0), unit // GDN_HB + hb))

    def cw(part):
        return pl.BlockSpec((CONV_K, GDN_SW), lambda hb, i: (0, part * GDN_STEPS + hb))

    row = pl.BlockSpec((1, LANE), lambda hb, i: (0, 0))
    hrow = pl.BlockSpec((None, 1, LANE), lambda hb, i: (hb, 0, 0))
    out_blk = pl.BlockSpec((c, GDN_SW), lambda hb, i: (n - 1 - i, hb))
    dcw = pl.BlockSpec((CONV_K, GDN_SW), lambda hb, i: (0, hb))
    wide = jax.ShapeDtypeStruct((rows, GDN_WIDTH), bf16)
    hrow_shape = jax.ShapeDtypeStruct((GDN_STEPS, 1, LANE), f32)
    dcw_shape = jax.ShapeDtypeStruct((CONV_K, GDN_WIDTH), f32)
    any_spec = pl.BlockSpec(memory_space=pl.ANY)
    comm_shapes, comm_sems = _comm_shapes(bufs, False) if nb else ([], [])
    return pl.pallas_call(
        body, name="gdn_bwd_scatter" if nb else "gdn_bwd", grid=(GDN_STEPS, n),
        in_specs=[blk(U_Q), blk(U_K), blk(U_V), halo(U_Q), halo(U_K), halo(U_V), blk(U_AZ),
                  pl.BlockSpec((c, LANE), lambda hb, i: (n - 1 - i, U_SMA)),
                  cw(0), cw(1), cw(2), row, row, row,
                  pl.BlockSpec((None, None, HEAD_DIM, GDN_SW), lambda hb, i: (hb, n - 1 - i, 0, 0)),
                  out_blk] + [any_spec] * nb,
        out_specs=[out_blk, out_blk, out_blk, out_blk,
                   pl.BlockSpec((None, c, LANE), lambda hb, i: (hb, n - 1 - i, 0)),
                   dcw, dcw, dcw, hrow, hrow, hrow] + [any_spec] * nb,
        out_shape=[wide, wide, wide, wide, jax.ShapeDtypeStruct((GDN_STEPS, rows, LANE), f32),
                   dcw_shape, dcw_shape, dcw_shape, hrow_shape, hrow_shape, hrow_shape] + comm_shapes,
        scratch_shapes=[pltpu.VMEM((HEAD_DIM, GDN_SW), f32), pltpu.VMEM((3, c + HALO, GDN_SW), f32),
                        pltpu.VMEM((3, c + HALO, GDN_SW), f32)] + comm_sems,
        compiler_params=_cparams(("arbitrary", "arbitrary")))(
            proj, proj, proj, proj, proj, proj, proj, proj, conv_w, conv_w, conv_w, alog_row, dtb_row, normw, ck, dy,
            *bufs)


M2_REP = M2_HEADS // M2_GROUPS
M2_GW = M2_REP * M2_HEAD_DIM
M2_GB = 2
M2_STEPS = M2_GROUPS // M2_GB
M2_XW = M2_GB * M2_GW
M2_BW = M2_GB * M2_STATE
M2_SH = M2_GB * M2_REP


def _ssd_step(xc, bc, cc, z, small, bias_x, bias_b, bias_c, alog_row, dtb_row, d_row, normw, state, grp0):
    c = xc.shape[0]
    causal = _tri_masks(c)[0]
    hd = M2_HEAD_DIM
    ones_l = jnp.ones((1, hd), f32)
    ones_r = jnp.ones((hd, 1), f32)
    lane = lax.broadcasted_iota(jnp.int32, (1, M2_GW), 1)
    lane_head = [((lane >= r * hd) & (lane < (r + 1) * hd)).astype(f32) for r in range(M2_REP)]
    xs = _silu(xc + bias_x)
    bms = _silu(bc + bias_b)
    cms = _silu(cc + bias_c)
    dt_all = _softplus(small + dtb_row)
    a_all = -jnp.exp(alog_row) * dt_all
    ac_all, act_all = _cumsum_all(a_all)
    ys, new_states = [], []
    for gi in range(M2_GB):
        bm = bms[:, gi * M2_STATE:(gi + 1) * M2_STATE]
        cm = cms[:, gi * M2_STATE:(gi + 1) * M2_STATE]
        xg = xs[:, gi * M2_GW:(gi + 1) * M2_GW]
        sg = state[gi * M2_GW:(gi + 1) * M2_GW]
        heads = [(grp0 + gi) * M2_REP + r for r in range(M2_REP)]
        ac_h = [_pick_col(ac_all, h) for h in heads]
        al_h = [a[c - 1:c, :] for a in ac_h]

        def wide(cols):
            return jnp.concatenate([v * ones_l for v in cols], axis=1)

        dt_w = wide([_pick_col(dt_all, h) for h in heads])
        ac_w = wide(ac_h)
        al_w = wide(al_h)
        dsk_w = wide([_pick_col(d_row, h) for h in heads])
        scores = dot_nt(cm, bm)
        m_wide = jnp.concatenate(
            [scores * jnp.exp(jnp.where(causal, a - _pick_row(act_all, h), -1e30)) for a, h in zip(ac_h, heads)], axis=1)
        xdt = xg * dt_w
        x_bd = jnp.concatenate([xdt * lane_head[r] for r in range(M2_REP)], axis=0)
        y_diag = dot_nn(m_wide, x_bd)
        states_new = dot_tn(xdt * jnp.exp(al_w - ac_w), bm)
        y_off = dot_nt(cm, sg) * jnp.exp(ac_w)
        eg_col = jnp.concatenate([jnp.exp(a) * ones_r for a in al_h], axis=0)
        new_states.append(sg * eg_col + states_new)
        y = (y_diag + y_off + dsk_w * xg) * _silu(z[:, gi * M2_GW:(gi + 1) * M2_GW])
        ys.append(y * lax.rsqrt(jnp.mean(y * y, axis=-1, keepdims=True) + NORM_EPS)
                  * normw[:, gi * M2_GW:(gi + 1) * M2_GW])
    return jnp.concatenate(ys, axis=-1), jnp.concatenate(new_states, axis=0)


def _conv_windows2(xin_ref, cw_ref, c):
    acc = None
    for k in range(CONV_K):
        term = cw_ref[pl.ds(k, 1), :] * xin_ref[pl.ds(HALO - CONV_K + 1 + k, c), :]
        acc = term if acc is None else acc + term
    return acc


def _ssd_specs(n, c, rev):
    def ci(i):
        return (n - 1 - i) if rev else i

    def blk(width, unit):
        return pl.BlockSpec((c, width), lambda g, i: (ci(i), unit * LANE // width + g))

    def par(rows_, width, col0):
        return pl.BlockSpec((rows_, width), lambda g, i: (0, col0 // width + g))

    return ci, blk, par


def _ssd_fwd(proj, conv_w, conv_b, alog_row, dtb_row, d_row, normw):
    rows = proj.shape[0]
    c = min(CHUNK, rows)
    n = rows // c
    _, blk, par = _ssd_specs(n, c, False)

    def body(x_ref, b_ref, c_ref, z_ref, sm_ref, cwx, cwb, cwc, bx, bb, bcc, al_ref, dt_ref, d_ref, nw_ref,
             y_ref, ck_ref, s_ref, xx_ref, xb_ref, xc_ref):
        g = pl.program_id(0)
        i = pl.program_id(1)

        @pl.when(i == 0)
        def _():
            s_ref[...] = jnp.zeros_like(s_ref)
            for r in (xx_ref, xb_ref, xc_ref):
                r[0:HALO, :] = jnp.zeros((HALO, r.shape[1]), f32)

        @pl.when(i > 0)
        def _():
            for r in (xx_ref, xb_ref, xc_ref):
                r[0:HALO, :] = r[c:c + HALO, :]

        xx_ref[HALO:, :] = x_ref[...]
        xb_ref[HALO:, :] = b_ref[...]
        xc_ref[HALO:, :] = c_ref[...]
        xc = _conv_windows2(xx_ref, cwx, c)
        bc = _conv_windows2(xb_ref, cwb, c)
        cc = _conv_windows2(xc_ref, cwc, c)
        state = s_ref[...]
        ck_ref[...] = state
        y, new_state = _ssd_step(xc, bc, cc, z_ref[...], sm_ref[...], bx[...], bb[...], bcc[...], al_ref[...],
                                 dt_ref[...], d_ref[...], nw_ref[...], state, g * M2_GB)
        y_ref[...] = y.astype(y_ref.dtype)
        s_ref[...] = new_state

    row = pl.BlockSpec((1, LANE), lambda g, i: (0, 0))
    off_b, off_c = M2_WIDTH, M2_WIDTH + M2_GROUPS * M2_STATE
    return pl.pallas_call(
        body, name="ssd_fwd", grid=(M2_STEPS, n),
        in_specs=[blk(M2_XW, U_CX), blk(M2_BW, U_CB), blk(M2_BW, U_CC), blk(M2_XW, U_CZ),
                  pl.BlockSpec((c, LANE), lambda g, i: (i, U_SMC)),
                  par(CONV_K, M2_XW, 0), par(CONV_K, M2_BW, off_b), par(CONV_K, M2_BW, off_c),
                  par(1, M2_XW, 0), par(1, M2_BW, off_b), par(1, M2_BW, off_c), row, row, row, par(1, M2_XW, 0)],
        out_specs=[pl.BlockSpec((c, M2_XW), lambda g, i: (i, g)),
                   pl.BlockSpec((None, None, M2_SH * M2_HEAD_DIM, M2_STATE), lambda g, i: (g, i, 0, 0))],
        out_shape=[jax.ShapeDtypeStruct((rows, M2_WIDTH), bf16),
                   jax.ShapeDtypeStruct((M2_STEPS, n, M2_SH * M2_HEAD_DIM, M2_STATE), f32)],
        scratch_shapes=[pltpu.VMEM((M2_SH * M2_HEAD_DIM, M2_STATE), f32), pltpu.VMEM((c + HALO, M2_XW), f32),
                        pltpu.VMEM((c + HALO, M2_BW), f32), pltpu.VMEM((c + HALO, M2_BW), f32)],
        compiler_params=_cparams(("arbitrary", "arbitrary")))(
            proj, proj, proj, proj, proj, conv_w, conv_w, conv_w, conv_b, conv_b, conv_b, alog_row, dtb_row, d_row, normw)


def _conv_bwd2(xin_ref, dyext_ref, cw_ref, dxc, dx_ref, dcw_ref, c):
    dyext_ref[0:c, :] = dxc
    acc = None
    for k in range(CONV_K):
        term = cw_ref[pl.ds(k, 1), :] * dyext_ref[pl.ds(CONV_K - 1 - k, c), :]
        acc = term if acc is None else acc + term
        dcw_ref[pl.ds(k, 1), :] += jnp.sum(xin_ref[pl.ds(HALO - CONV_K + 1 + k, c), :] * dxc, axis=0, keepdims=True)
    dx_ref[...] = acc.astype(dx_ref.dtype)


def _ssd_bwd(proj, dy, ck, conv_w, conv_b, alog_row, dtb_row, d_row, normw):
    rows = proj.shape[0]
    c = min(CHUNK, rows)
    n = rows // c
    halo_blocks = c // HALO
    _, blk, par = _ssd_specs(n, c, True)

    def body(x_ref, b_ref, c_ref, hx_ref, hb_ref, hc_ref, z_ref, sm_ref, cwx, cwb, cwc, bx, bb, bcc,
             al_ref, dt_ref, d_ref, nw_ref, ck_ref, dy_ref,
             dx_ref, db_ref, dc_ref, dz_ref, dsm_ref, dcwx, dcwb, dcwc, dbx, dbb, dbc, dal_ref, ddt_ref, dd_ref, dnw_ref,
             ds_ref, xx_ref, xb_ref, xc_ref, ex_ref, eb_ref, ec_ref):
        g = pl.program_id(0)
        i = pl.program_id(1)
        ci = n - 1 - i

        @pl.when(i == 0)
        def _():
            ds_ref[...] = jnp.zeros_like(ds_ref)
            for r in (ex_ref, eb_ref, ec_ref):
                r[c:c + HALO, :] = jnp.zeros((HALO, r.shape[1]), f32)
            for r in (dcwx, dcwb, dcwc, dbx, dbb, dbc, dal_ref, ddt_ref, dd_ref, dnw_ref):
                r[...] = jnp.zeros_like(r)

        @pl.when(i > 0)
        def _():
            for r in (ex_ref, eb_ref, ec_ref):
                r[c:c + HALO, :] = r[0:HALO, :]

        first = (ci > 0).astype(f32)
        for xin, x_in, halo_in in ((xx_ref, x_ref, hx_ref), (xb_ref, b_ref, hb_ref), (xc_ref, c_ref, hc_ref)):
            xin[0:HALO, :] = halo_in[...] * first
            xin[HALO:, :] = x_in[...]
        xc = _conv_windows2(xx_ref, cwx, c)
        bc = _conv_windows2(xb_ref, cwb, c)
        cc = _conv_windows2(xc_ref, cwc, c)
        fn = functools.partial(_ssd_step, grp0=g * M2_GB)
        _, vjp = jax.vjp(fn, xc, bc, cc, z_ref[...], sm_ref[...], bx[...], bb[...], bcc[...], al_ref[...], dt_ref[...],
                         d_ref[...], nw_ref[...], ck_ref[...])
        (dxc, dbc_, dcc, dz, dsm, gbx, gbb, gbc, dal, ddt, dd, dnw, dstate) = vjp((dy_ref[...].astype(f32), ds_ref[...]))
        ds_ref[...] = dstate
        dz_ref[...] = dz.astype(dz_ref.dtype)
        dsm_ref[...] = dsm
        dbx[...] += gbx
        dbb[...] += gbb
        dbc[...] += gbc
        dal_ref[...] += dal
        ddt_ref[...] += ddt
        dd_ref[...] += dd
        dnw_ref[...] += dnw
        _conv_bwd2(xx_ref, ex_ref, cwx, dxc, dx_ref, dcwx, c)
        _conv_bwd2(xb_ref, eb_ref, cwb, dbc_, db_ref, dcwb, c)
        _conv_bwd2(xc_ref, ec_ref, cwc, dcc, dc_ref, dcwc, c)

    def halo(width, unit):
        return pl.BlockSpec((HALO, width),
                            lambda g, i: (jnp.maximum((n - 1 - i) * halo_blocks - 1, 0), unit * LANE // width + g))

    row = pl.BlockSpec((1, LANE), lambda g, i: (0, 0))
    grow = pl.BlockSpec((None, 1, LANE), lambda g, i: (g, 0, 0))
    grow_shape = jax.ShapeDtypeStruct((M2_STEPS, 1, LANE), f32)
    ob_w = pl.BlockSpec((c, M2_XW), lambda g, i: (n - 1 - i, g))
    ob_n = pl.BlockSpec((c, M2_BW), lambda g, i: (n - 1 - i, g))
    off_b, off_c = M2_WIDTH, M2_WIDTH + M2_GROUPS * M2_STATE
    bc_w = M2_GROUPS * M2_STATE
    return pl.pallas_call(
        body, name="ssd_bwd", grid=(M2_STEPS, n),
        in_specs=[blk(M2_XW, U_CX), blk(M2_BW, U_CB), blk(M2_BW, U_CC),
                  halo(M2_XW, U_CX), halo(M2_BW, U_CB), halo(M2_BW, U_CC), blk(M2_XW, U_CZ),
                  pl.BlockSpec((c, LANE), lambda g, i: (n - 1 - i, U_SMC)),
                  par(CONV_K, M2_XW, 0), par(CONV_K, M2_BW, off_b), par(CONV_K, M2_BW, off_c),
                  par(1, M2_XW, 0), par(1, M2_BW, off_b), par(1, M2_BW, off_c), row, row, row, par(1, M2_XW, 0),
                  pl.BlockSpec((None, None, M2_SH * M2_HEAD_DIM, M2_STATE), lambda g, i: (g, n - 1 - i, 0, 0)),
                  ob_w],
        out_specs=[ob_w, ob_n, ob_n, ob_w, pl.BlockSpec((None, c, LANE), lambda g, i: (g, n - 1 - i, 0)),
                   par(CONV_K, M2_XW, 0), par(CONV_K, M2_BW, 0), par(CONV_K, M2_BW, 0),
                   par(1, M2_XW, 0), par(1, M2_BW, 0), par(1, M2_BW, 0), grow, grow, grow, par(1, M2_XW, 0)],
        out_shape=[jax.ShapeDtypeStruct((rows, M2_WIDTH), bf16), jax.ShapeDtypeStruct((rows, bc_w), bf16),
                   jax.ShapeDtypeStruct((rows, bc_w), bf16), jax.ShapeDtypeStruct((rows, M2_WIDTH), bf16),
                   jax.ShapeDtypeStruct((M2_STEPS, rows, LANE), f32),
                   jax.ShapeDtypeStruct((CONV_K, M2_WIDTH), f32), jax.ShapeDtypeStruct((CONV_K, bc_w), f32),
                   jax.ShapeDtypeStruct((CONV_K, bc_w), f32),
                   jax.ShapeDtypeStruct((1, M2_WIDTH), f32), jax.ShapeDtypeStruct((1, bc_w), f32),
                   jax.ShapeDtypeStruct((1, bc_w), f32), grow_shape, grow_shape, grow_shape,
                   jax.ShapeDtypeStruct((1, M2_WIDTH), f32)],
        scratch_shapes=[pltpu.VMEM((M2_SH * M2_HEAD_DIM, M2_STATE), f32),
                        pltpu.VMEM((c + HALO, M2_XW), f32), pltpu.VMEM((c + HALO, M2_BW), f32), pltpu.VMEM((c + HALO, M2_BW), f32),
                        pltpu.VMEM((c + HALO, M2_XW), f32), pltpu.VMEM((c + HALO, M2_BW), f32), pltpu.VMEM((c + HALO, M2_BW), f32)],
        compiler_params=_cparams(("arbitrary", "arbitrary")))(
            proj, proj, proj, proj, proj, proj, proj, proj, conv_w, conv_w, conv_w, conv_b, conv_b, conv_b,
            alog_row, dtb_row, d_row, normw, ck, dy)


S5_TILE = 128
S5_SW = S5_LANES // S5_BLK


def _scan_down(br, bi, ar, ai):
    t = br.shape[0]
    row = lax.broadcasted_iota(jnp.int32, br.shape, 0)
    d = 1
    while d < t:
        keep = row >= d
        sr = jnp.where(keep, pltpu.roll(br, d, 0), 0.0)
        si = jnp.where(keep, pltpu.roll(bi, d, 0), 0.0)
        br, bi = br + ar * sr - ai * si, bi + ar * si + ai * sr
        ar, ai = ar * ar - ai * ai, 2.0 * ar * ai
        d *= 2
    return br, bi


def _scan_up(br, bi, ar, ai):
    t = br.shape[0]
    row = lax.broadcasted_iota(jnp.int32, br.shape, 0)
    d = 1
    while d < t:
        keep = row < t - d
        sr = jnp.where(keep, pltpu.roll(br, t - d, 0), 0.0)
        si = jnp.where(keep, pltpu.roll(bi, t - d, 0), 0.0)
        br, bi = br + ar * sr - ai * si, bi + ar * si + ai * sr
        ar, ai = ar * ar - ai * ai, 2.0 * ar * ai
        d *= 2
    return br, bi


def _s5_states(u_j, bbr, bbi, ar, ai, cr, ci_):
    br = dot_nn(u_j, bbr)
    bi = dot_nn(u_j, bbi)
    row0 = lax.broadcasted_iota(jnp.int32, br.shape, 0) == 0
    br = br + jnp.where(row0, ar * cr - ai * ci_, 0.0)
    bi = bi + jnp.where(row0, ar * ci_ + ai * cr, 0.0)
    return _scan_down(br, bi, ar, ai)


def _s5_fwd(proj, a_rows, bbr, bbi, ccr, cci, d_row):
    rows = proj.shape[0]
    t = min(S5_TILE, rows)
    n = rows // t

    def body(u_ref, a_ref, bbr_ref, bbi_ref, ccr_ref, cci_ref, d_ref, y_ref, ck_ref, carry_ref):
        i = pl.program_id(0)

        @pl.when(i == 0)
        def _():
            carry_ref[...] = jnp.zeros_like(carry_ref)

        ck_ref[...] = carry_ref[...]
        for j in range(S5_BLK):
            lanes = pl.ds(j * S5_SW, S5_SW)
            ch = pl.ds(j * LANE, LANE)
            u_j = u_ref[:, ch]
            sr, si = _s5_states(u_j, bbr_ref[j], bbi_ref[j], a_ref[0:1, lanes], a_ref[1:2, lanes],
                                carry_ref[0:1, lanes], carry_ref[1:2, lanes])
            y_ref[:, ch] = dot_nn(sr, ccr_ref[j]) - dot_nn(si, cci_ref[j]) + d_ref[:, ch] * u_j
            carry_ref[0:1, lanes] = sr[t - 1:t, :]
            carry_ref[1:2, lanes] = si[t - 1:t, :]

    whole3 = lambda s: pl.BlockSpec(s, lambda i: (0, 0, 0))
    return pl.pallas_call(
        body, name="s5_fwd", grid=(n,),
        in_specs=[pl.BlockSpec((t, S5_WIDTH), lambda i: (i, U_SU // S5_BLK)),
                  pl.BlockSpec((2, S5_LANES), lambda i: (0, 0)),
                  whole3(bbr.shape), whole3(bbi.shape), whole3(ccr.shape), whole3(cci.shape),
                  pl.BlockSpec((1, S5_WIDTH), lambda i: (0, 0))],
        out_specs=[pl.BlockSpec((t, S5_WIDTH), lambda i: (i, 0)),
                   pl.BlockSpec((None, 2, S5_LANES), lambda i: (i, 0, 0))],
        out_shape=[jax.ShapeDtypeStruct((rows, S5_WIDTH), f32), jax.ShapeDtypeStruct((n, 2, S5_LANES), f32)],
        scratch_shapes=[pltpu.VMEM((2, S5_LANES), f32)],
        compiler_params=_cparams(("arbitrary",)))(proj, a_rows, bbr, bbi, ccr, cci, d_row)


def _s5_bwd(proj, dy, ck, a_rows, bbr, bbi, ccr, cci, d_row):
    rows = proj.shape[0]
    t = min(S5_TILE, rows)
    n = rows // t

    def body(u_ref, dy_ref, ck_ref, a_ref, bbr_ref, bbi_ref, ccr_ref, cci_ref, d_ref,
             du_ref, da_ref, dbbr_ref, dbbi_ref, dccr_ref, dcci_ref, dd_ref, lam_ref):
        i = pl.program_id(0)

        @pl.when(i == 0)
        def _():
            lam_ref[...] = jnp.zeros_like(lam_ref)
            for r in (da_ref, dbbr_ref, dbbi_ref, dccr_ref, dcci_ref, dd_ref):
                r[...] = jnp.zeros_like(r)

        for j in range(S5_BLK):
            lanes = pl.ds(j * S5_SW, S5_SW)
            ch = pl.ds(j * LANE, LANE)
            u_j = u_ref[:, ch]
            dy_j = dy_ref[:, ch]
            ar, ai = a_ref[0:1, lanes], a_ref[1:2, lanes]
            cr, ci_ = ck_ref[0:1, lanes], ck_ref[1:2, lanes]
            sr, si = _s5_states(u_j, bbr_ref[j], bbi_ref[j], ar, ai, cr, ci_)
            gr = dot_nt(dy_j, ccr_ref[j])
            gi = -dot_nt(dy_j, cci_ref[j])
            last = lax.broadcasted_iota(jnp.int32, gr.shape, 0) == t - 1
            lr0, li0 = lam_ref[0:1, lanes], lam_ref[1:2, lanes]
            gr = gr + jnp.where(last, ar * lr0 + ai * li0, 0.0)
            gi = gi + jnp.where(last, ar * li0 - ai * lr0, 0.0)
            lr, li = _scan_up(gr, gi, ar, -ai)
            lam_ref[0:1, lanes] = lr[0:1, :]
            lam_ref[1:2, lanes] = li[0:1, :]
            du_ref[:, ch] = (dot_nt(lr, bbr_ref[j]) + dot_nt(li, bbi_ref[j]) + d_ref[:, ch] * dy_j).astype(du_ref.dtype)
            dbbr_ref[j] += dot_tn(u_j, lr)
            dbbi_ref[j] += dot_tn(u_j, li)
            dccr_ref[j] += dot_tn(sr, dy_j)
            dcci_ref[j] += -dot_tn(si, dy_j)
            dd_ref[:, ch] += jnp.sum(dy_j * u_j, axis=0, keepdims=True)
            row0 = lax.broadcasted_iota(jnp.int32, sr.shape, 0) == 0
            pr = jnp.where(row0, cr, pltpu.roll(sr, 1, 0))
            pi = jnp.where(row0, ci_, pltpu.roll(si, 1, 0))
            da_ref[0:1, lanes] += jnp.sum(lr * pr + li * pi, axis=0, keepdims=True)
            da_ref[1:2, lanes] += jnp.sum(li * pr - lr * pi, axis=0, keepdims=True)

    whole3 = lambda s: pl.BlockSpec(s, lambda i: (0, 0, 0))
    whole2 = lambda s: pl.BlockSpec(s, lambda i: (0, 0))
    return pl.pallas_call(
        body, name="s5_bwd", grid=(n,),
        in_specs=[pl.BlockSpec((t, S5_WIDTH), lambda i: (n - 1 - i, U_SU // S5_BLK)),
                  pl.BlockSpec((t, S5_WIDTH), lambda i: (n - 1 - i, 0)),
                  pl.BlockSpec((None, 2, S5_LANES), lambda i: (n - 1 - i, 0, 0)),
                  whole2((2, S5_LANES)), whole3(bbr.shape), whole3(bbi.shape), whole3(ccr.shape), whole3(cci.shape),
                  whole2((1, S5_WIDTH))],
        out_specs=[pl.BlockSpec((t, S5_WIDTH), lambda i: (n - 1 - i, 0)), whole2((2, S5_LANES)),
                   whole3(bbr.shape), whole3(bbi.shape), whole3(ccr.shape), whole3(cci.shape), whole2((1, S5_WIDTH))],
        out_shape=[jax.ShapeDtypeStruct((rows, S5_WIDTH), bf16), jax.ShapeDtypeStruct((2, S5_LANES), f32),
                   jax.ShapeDtypeStruct(bbr.shape, f32), jax.ShapeDtypeStruct(bbi.shape, f32),
                   jax.ShapeDtypeStruct(ccr.shape, f32), jax.ShapeDtypeStruct(cci.shape, f32),
                   jax.ShapeDtypeStruct((1, S5_WIDTH), f32)],
        scratch_shapes=[pltpu.VMEM((2, S5_LANES), f32)],
        compiler_params=_cparams(("arbitrary",)))(proj, dy, ck, a_rows, bbr, bbi, ccr, cci, d_row)


def _s5_prep(lam_re, lam_im, log_step, b_re, b_im, c_re, c_im, d_skip):
    lam_re = jnp.minimum(lam_re, -1e-4)
    step = jnp.exp(log_step)[:, None]
    mag = jnp.exp(lam_re * step)
    ab_re = mag * jnp.cos(lam_im * step)
    ab_im = mag * jnp.sin(lam_im * step)
    den = lam_re * lam_re + lam_im * lam_im
    f_re = ((ab_re - 1.0) * lam_re + ab_im * lam_im) / den
    f_im = (ab_im * lam_re - (ab_re - 1.0) * lam_im) / den
    bb_re = f_re[..., None] * b_re - f_im[..., None] * b_im
    bb_im = f_re[..., None] * b_im + f_im[..., None] * b_re
    eye = jnp.eye(8, dtype=f32)

    def drive(bb):
        r = bb.reshape(S5_BLK, 8, S5_STATE, S5_GROUP_SIZE).transpose(0, 1, 3, 2)
        return (r[:, :, :, None, :] * eye[None, :, None, :, None]).reshape(S5_BLK, LANE, S5_SW)

    def readout(cc):
        r = cc.reshape(S5_BLK, 8, S5_GROUP_SIZE, S5_STATE).transpose(0, 1, 3, 2)
        return (r[:, :, :, None, :] * eye[None, :, None, :, None]).reshape(S5_BLK, S5_SW, LANE)

    a_rows = jnp.stack([ab_re.reshape(S5_LANES), ab_im.reshape(S5_LANES)])
    return a_rows, drive(bb_re), drive(bb_im), readout(c_re), readout(c_im), d_skip.reshape(1, S5_WIDTH)


def _adam_math(w, g, m, v):
    m = ADAM_B1 * m + (1.0 - ADAM_B1) * g
    v = ADAM_B2 * v + (1.0 - ADAM_B2) * (g * g)
    m_hat = m / (1.0 - ADAM_B1 ** ADAM_STEP)
    v_hat = v / (1.0 - ADAM_B2 ** ADAM_STEP)
    delta = -ADAM_LR * (m_hat / (jnp.sqrt(v_hat) + ADAM_EPS) + ADAM_WD * w)
    return delta, m, v


def _adamw(name, w, g, m, v):
    rows, width = w.shape
    t = rows
    for cand in (512, 256, 128, 64, 32, 16, 8):
        if rows % cand == 0 and cand * width * 4 * 7 * 2 <= VMEM_LIMIT // 2:
            t = cand
            break

    def body(w_ref, g_ref, m_ref, v_ref, d_ref, nm_ref, nv_ref):
        d, nm, nv = _adam_math(w_ref[...], g_ref[...], m_ref[...], v_ref[...])
        d_ref[...] = d
        nm_ref[...] = nm
        nv_ref[...] = nv

    spec = pl.BlockSpec((t, width), lambda i: (i, 0))
    shape = jax.ShapeDtypeStruct((rows, width), f32)
    return pl.pallas_call(body, name=name, grid=(rows // t,), in_specs=[spec] * 4, out_specs=[spec] * 3,
                          out_shape=[shape] * 3, compiler_params=_cparams(("parallel",)))(w, g, m, v)


def _adamw_slots(name, recvs, row_off, w, m, v):
    depth, rows, width = w.shape
    per_row = width * (depth * N_DEV * recvs[0].dtype.itemsize + 7 * 4) * 2
    t = next(cand for cand in (256, 128, 96, 64, 32, 16)
             if rows % cand == 0 and row_off % cand == 0 and cand * per_row <= VMEM_LIMIT * 2 // 3)
    first = row_off // t

    def body(*refs):
        r_refs = refs[:depth]
        w_ref, m_ref, v_ref, g_ref, d_ref, nm_ref, nv_ref = refs[depth:]
        layer = pl.program_id(0)
        for l, r_ref in enumerate(r_refs):
            @pl.when(layer == l)
            def _():
                g = r_ref[0].astype(f32)
                for s in range(1, N_DEV):
                    g = g + r_ref[s].astype(f32)
                d, nm, nv = _adam_math(w_ref[...], g, m_ref[...], v_ref[...])
                g_ref[...] = g
                d_ref[...] = d
                nm_ref[...] = nm
                nv_ref[...] = nv

    def recv_spec(l):
        return pl.BlockSpec((N_DEV, t, width), lambda layer, i: (0, first + jnp.where(layer == l, i, 0), 0))

    spec = pl.BlockSpec((None, t, width), lambda layer, i: (layer, i, 0))
    shape = jax.ShapeDtypeStruct((depth, rows, width), f32)
    return pl.pallas_call(
        body, name=name, grid=(depth, rows // t), in_specs=[recv_spec(l) for l in range(depth)] + [spec, spec, spec],
        out_specs=[spec] * 4, out_shape=[shape] * 4,
        compiler_params=_cparams(("arbitrary", "arbitrary")))(*recvs, w, m, v)


def _sum_slots(name, buf):
    _, rows, width = buf.shape
    t = next(cand for cand in (512, 256, 128, 64, 32, 16) if rows % cand == 0)

    def body(b_ref, o_ref):
        acc = b_ref[0].astype(f32)
        for s in range(1, N_DEV):
            acc = acc + b_ref[s].astype(f32)
        o_ref[...] = acc

    return pl.pallas_call(
        body, name=name, grid=(rows // t,), in_specs=[pl.BlockSpec((N_DEV, t, width), lambda i: (0, i, 0))],
        out_specs=pl.BlockSpec((t, width), lambda i: (i, 0)), out_shape=jax.ShapeDtypeStruct((rows, width), f32),
        compiler_params=_cparams(("parallel",)))(buf)


BIG = ("w_in", "s5_glu_w", "proj_a", "proj_b", "proj_c", "w_out")
CONV = ("gdn_conv_w", "m2_conv_w")
SMALL = ("norm_w", "gdn_a_log", "gdn_dt_bias", "gdn_norm_w", "s5_lam_re", "s5_lam_im", "s5_log_step",
         "s5_b_re", "s5_b_im", "s5_c_re", "s5_c_im", "s5_d", "s5_glu_b", "m2_conv_b", "m2_a_log", "m2_dt_bias",
         "m2_d", "m2_norm_w", "final_norm_w")
WEIGHTS = ("norm_w", "w_in", "gdn_conv_w", "gdn_a_log", "gdn_dt_bias", "gdn_norm_w", "s5_lam_re", "s5_lam_im",
           "s5_log_step", "s5_b_re", "s5_b_im", "s5_c_re", "s5_c_im", "s5_d", "s5_glu_w", "s5_glu_b", "m2_conv_w",
           "m2_conv_b", "m2_a_log", "m2_dt_bias", "m2_d", "m2_norm_w", "proj_a", "proj_b", "proj_c", "w_out",
           "final_norm_w")


PACK_ROWS = 8
PACK_TILE = 256


def _piece_rows(shape):
    size = 1
    for d in shape:
        size *= d
    rows = -(-size // LANE)
    return size, -(-rows // PACK_ROWS) * PACK_ROWS


def _pack(arrays):
    pieces = []
    for a in arrays:
        size, rows = _piece_rows(a.shape)
        flat = a.reshape(-1)
        if size != rows * LANE:
            flat = jnp.concatenate([flat, jnp.zeros((rows * LANE - size,), f32)])
        pieces.append(flat.reshape(rows, LANE))
    total = sum(p.shape[0] for p in pieces)
    tail = -total % PACK_TILE
    if tail:
        pieces.append(jnp.zeros((tail, LANE), f32))
    return jnp.concatenate(pieces, axis=0)


def _unpack(buf, shapes):
    out, off = [], 0
    for s in shapes:
        size, rows = _piece_rows(s)
        piece = buf[off:off + rows]
        out.append(piece.reshape(s) if size == rows * LANE else piece.reshape(-1)[:size].reshape(s))
        off += rows
    return out


def _win_to_padded(w):
    d = w.shape[0]
    z = lambda n: jnp.zeros((d, n), w.dtype)
    return jnp.concatenate([w[:, 8736:14880], w[:, 4112:5648], w[:, 0:4096], w[:, 5648:8720],
                            w[:, 4096:4112], z(LANE - 16), w[:, 8720:8736], z(LANE - 16), z(2 * LANE)], axis=1)


def _win_from_padded(g):
    u = LANE
    return jnp.concatenate([g[:, U_Q * u:U_CZ * u], g[:, U_SMA * u:U_SMA * u + 16], g[:, U_SU * u:U_Q * u],
                            g[:, U_CZ * u:U_SMA * u], g[:, U_SMC * u:U_SMC * u + 16], g[:, 0:U_SU * u]], axis=1)


def _lane_row(vals, offset):
    n = vals.shape[0]
    return jnp.concatenate([jnp.zeros((offset,), f32), vals, jnp.zeros((LANE - offset - n,), f32)]).reshape(1, LANE)


def _layer_fwd(x, lw, next_shards):
    h = _tok_fwd("rms_fwd", _rms_fn, [(x, D_MODEL, 0)], [lw["norm_w"].reshape(1, D_MODEL)], [(D_MODEL, bf16)], 256)[0]
    if next_shards is None:
        proj, gathered = _mm("proj_fwd", h, lw["w_in_p"], "nn", tk=D_MODEL), None
    else:
        proj, *gathered = _mm("proj_fwd_gather", h, lw["w_in_p"], "nn", tk=D_MODEL, comm=(next_shards, True))
    g_al, g_dt = _lane_row(lw["gdn_a_log"], GDN_HEADS), _lane_row(lw["gdn_dt_bias"], GDN_HEADS)
    g_nw = lw["gdn_norm_w"].reshape(1, HEAD_DIM)
    y_a, ck_a = _gdn_fwd(proj, lw["gdn_conv_w"], g_al, g_dt, g_nw)
    s5p = _s5_prep(lw["s5_lam_re"], lw["s5_lam_im"], lw["s5_log_step"], lw["s5_b_re"], lw["s5_b_im"],
                   lw["s5_c_re"], lw["s5_c_im"], lw["s5_d"])
    y_pre, ck_b = _s5_fwd(proj, *s5p)
    glu_b = lw["s5_glu_b"].reshape(1, S5_WIDTH)
    y_b = _tok_fwd("s5_tail_fwd", _s5_tail_fn, [(y_pre, S5_WIDTH, 0), (proj, S5_WIDTH, U_SG // S5_BLK)],
                   [lw["s5_glu_w"], glu_b], [(S5_WIDTH, bf16)], 256)[0]
    m_al, m_dt, m_d = _lane_row(lw["m2_a_log"], 0), _lane_row(lw["m2_dt_bias"], 0), _lane_row(lw["m2_d"], 0)
    m_cb = lw["m2_conv_b"].reshape(1, -1)
    m_nw = lw["m2_norm_w"].reshape(1, M2_WIDTH)
    y_c, ck_c = _ssd_fwd(proj, lw["m2_conv_w"], m_cb, m_al, m_dt, m_d, m_nw)
    pa = _mm("proj_a_fwd", y_a, lw["proj_a"], "nn")
    pb = _mm("proj_b_fwd", y_b, lw["proj_b"], "nn")
    pc = _mm("proj_c_fwd", y_c, lw["proj_c"], "nn")
    gate_acts = [(proj, D_MODEL, 0), (proj, D_MODEL, 1), (proj, D_MODEL, 2),
                 (pa, D_MODEL, 0), (pb, D_MODEL, 0), (pc, D_MODEL, 0)]
    merged = _tok_fwd("gate_fwd", _gate_fn, gate_acts, [], [(D_MODEL, bf16)], 128)[0]
    x_next = _mm("w_out_fwd", merged, lw["w_out"], "nn", residual=x)
    saved = dict(x=x, h=h, proj=proj, y_a=y_a, ck_a=ck_a, y_pre=y_pre, ck_b=ck_b, y_b=y_b, y_c=y_c, ck_c=ck_c,
                 pa=pa, pb=pb, pc=pc, merged=merged)
    return x_next, saved, gathered


def _layer_bwd(dx_next, lw, sv, later_grads):
    rows = dx_next.shape[0]
    proj = sv["proj"]
    g = {}
    d_merged = _mm("w_out_bwd_x", dx_next, lw["w_out"], "nt", out_dtype=bf16)
    g["w_out"] = _mm("w_out_bwd_w", sv["merged"], dx_next, "tn", out_dtype=bf16)
    gate_acts = [(proj, D_MODEL, 0), (proj, D_MODEL, 1), (proj, D_MODEL, 2),
                 (sv["pa"], D_MODEL, 0), (sv["pb"], D_MODEL, 0), (sv["pc"], D_MODEL, 0)]
    dla, dlb, dlc, dpa, dpb, dpc = _tok_bwd(
        "gate_bwd", _gate_fn, gate_acts, [], [(d_merged, D_MODEL, 0)],
        [(0, bf16), (1, bf16), (2, bf16), (3, bf16), (4, bf16), (5, bf16)], 128)
    dy_a = _mm("proj_a_bwd_x", dpa, lw["proj_a"], "nt", out_dtype=bf16)
    dy_b = _mm("proj_b_bwd_x", dpb, lw["proj_b"], "nt")
    dy_c = _mm("proj_c_bwd_x", dpc, lw["proj_c"], "nt", out_dtype=bf16)
    g["proj_a"] = _mm("proj_a_bwd_w", sv["y_a"], dpa, "tn", out_dtype=bf16)
    g["proj_b"] = _mm("proj_b_bwd_w", sv["y_b"], dpb, "tn", out_dtype=bf16)
    g["proj_c"] = _mm("proj_c_bwd_w", sv["y_c"], dpc, "tn", out_dtype=bf16)

    m_al, m_dt, m_d = _lane_row(lw["m2_a_log"], 0), _lane_row(lw["m2_dt_bias"], 0), _lane_row(lw["m2_d"], 0)
    m_cb = lw["m2_conv_b"].reshape(1, -1)
    m_nw = lw["m2_norm_w"].reshape(1, M2_WIDTH)
    (dcx, dcb, dcc, dcz, dsmc, dcwx, dcwb, dcwc, dbx, dbb, dbc, dal, ddt, ddk, dnw) = _ssd_bwd(
        proj, dy_c, sv["ck_c"], lw["m2_conv_w"], m_cb, m_al, m_dt, m_d, m_nw)
    g["m2_conv_w"] = jnp.concatenate([dcwx, dcwb, dcwc], axis=1)
    g["m2_conv_b"] = jnp.concatenate([dbx, dbb, dbc], axis=1).reshape(-1)
    g["m2_a_log"] = jnp.sum(dal, axis=(0, 1))[:M2_HEADS]
    g["m2_dt_bias"] = jnp.sum(ddt, axis=(0, 1))[:M2_HEADS]
    g["m2_d"] = jnp.sum(ddk, axis=(0, 1))[:M2_HEADS]
    g["m2_norm_w"] = dnw.reshape(-1)
    dsmc = jnp.sum(dsmc, axis=0)

    glu_b = lw["s5_glu_b"].reshape(1, S5_WIDTH)
    dypre, dsg, dglu_w, dglu_b = _tok_bwd(
        "s5_tail_bwd", _s5_tail_fn, [(sv["y_pre"], S5_WIDTH, 0), (proj, S5_WIDTH, U_SG // S5_BLK)],
        [lw["s5_glu_w"], glu_b], [(dy_b, S5_WIDTH, 0)], [(0, f32), (1, bf16)], 256)
    g["s5_glu_w"] = dglu_w
    g["s5_glu_b"] = dglu_b.reshape(-1)
    s5_names = ("s5_lam_re", "s5_lam_im", "s5_log_step", "s5_b_re", "s5_b_im", "s5_c_re", "s5_c_im", "s5_d")
    s5p, s5_vjp = jax.vjp(_s5_prep, *[lw[k] for k in s5_names])
    dsu, da, dbbr, dbbi, dccr, dcci, dd = _s5_bwd(proj, dypre, sv["ck_b"], *s5p)
    for k, val in zip(s5_names, s5_vjp((da, dbbr, dbbi, dccr, dcci, dd))):
        g[k] = val

    g_al, g_dt = _lane_row(lw["gdn_a_log"], GDN_HEADS), _lane_row(lw["gdn_dt_bias"], GDN_HEADS)
    g_nw = lw["gdn_norm_w"].reshape(1, HEAD_DIM)
    (dq, dk, dv, daz, dsma, dcwq, dcwk, dcwv, dgal, dgdt, dgnw, *arrived) = _gdn_bwd(
        proj, dy_a, sv["ck_a"], lw["gdn_conv_w"], g_al, g_dt, g_nw, scatter=later_grads)
    g["gdn_conv_w"] = jnp.concatenate([dcwq, dcwk, dcwv], axis=1)
    g["gdn_a_log"] = jnp.sum(dgal, axis=(0, 1))[GDN_HEADS:2 * GDN_HEADS]
    g["gdn_dt_bias"] = jnp.sum(dgdt, axis=(0, 1))[GDN_HEADS:2 * GDN_HEADS]
    g["gdn_norm_w"] = jnp.sum(dgnw, axis=(0, 1))
    dsma = jnp.sum(dsma, axis=0)

    dproj = jnp.concatenate([dla, dlb, dlc, dsu, dsg, dq, dk, dv, daz, dcz, dcx, dcb, dcc, dsma.astype(bf16),
                             dsmc.astype(bf16), jnp.zeros((rows, 2 * LANE), bf16)], axis=1)
    g["w_in_p"] = _mm("proj_bwd_w", sv["h"], dproj, "tn", out_dtype=bf16)
    dh = _mm("proj_bwd_x", dproj, lw["w_in_p"], "nt")
    dx, dnorm = _tok_bwd("rms_bwd", _rms_fn, [(sv["x"], D_MODEL, 0)], [lw["norm_w"].reshape(1, D_MODEL)],
                         [(dh, D_MODEL, 0)], [(0, f32)], 256, residuals={0: dx_next})
    g["norm_w"] = dnorm.reshape(-1)
    return dx, g, (arrived if later_grads is not None else None)


def kernel(x, norm_w, w_in, gdn_conv_w, gdn_a_log, gdn_dt_bias, gdn_norm_w, s5_lam_re, s5_lam_im, s5_log_step, s5_b_re, s5_b_im, s5_c_re, s5_c_im, s5_d, s5_glu_w, s5_glu_b, m2_conv_w, m2_conv_b, m2_a_log, m2_dt_bias, m2_d, m2_norm_w, proj_a, proj_b, proj_c, w_out, final_norm_w, loss_target, m_norm_w, m_w_in, m_gdn_conv_w, m_gdn_a_log, m_gdn_dt_bias, m_gdn_norm_w, m_s5_lam_re, m_s5_lam_im, m_s5_log_step, m_s5_b_re, m_s5_b_im, m_s5_c_re, m_s5_c_im, m_s5_d, m_s5_glu_w, m_s5_glu_b, m_m2_conv_w, m_m2_conv_b, m_m2_a_log, m_m2_dt_bias, m_m2_d, m_m2_norm_w, m_proj_a, m_proj_b, m_proj_c, m_w_out, m_final_norm_w, v_norm_w, v_w_in, v_gdn_conv_w, v_gdn_a_log, v_gdn_dt_bias, v_gdn_norm_w, v_s5_lam_re, v_s5_lam_im, v_s5_log_step, v_s5_b_re, v_s5_b_im, v_s5_c_re, v_s5_c_im, v_s5_d, v_s5_glu_w, v_s5_glu_b, v_m2_conv_w, v_m2_conv_b, v_m2_a_log, v_m2_dt_bias, v_m2_d, v_m2_norm_w, v_proj_a, v_proj_b, v_proj_c, v_w_out, v_final_norm_w):
    w = dict(norm_w=norm_w, w_in=w_in, gdn_conv_w=gdn_conv_w, gdn_a_log=gdn_a_log, gdn_dt_bias=gdn_dt_bias,
             gdn_norm_w=gdn_norm_w, s5_lam_re=s5_lam_re, s5_lam_im=s5_lam_im, s5_log_step=s5_log_step,
             s5_b_re=s5_b_re, s5_b_im=s5_b_im, s5_c_re=s5_c_re, s5_c_im=s5_c_im, s5_d=s5_d, s5_glu_w=s5_glu_w,
             s5_glu_b=s5_glu_b, m2_conv_w=m2_conv_w, m2_conv_b=m2_conv_b, m2_a_log=m2_a_log, m2_dt_bias=m2_dt_bias,
             m2_d=m2_d, m2_norm_w=m2_norm_w, proj_a=proj_a, proj_b=proj_b, proj_c=proj_c, w_out=w_out,
             final_norm_w=final_norm_w)
    mom = dict(norm_w=m_norm_w, w_in=m_w_in, gdn_conv_w=m_gdn_conv_w, gdn_a_log=m_gdn_a_log,
               gdn_dt_bias=m_gdn_dt_bias, gdn_norm_w=m_gdn_norm_w, s5_lam_re=m_s5_lam_re, s5_lam_im=m_s5_lam_im,
               s5_log_step=m_s5_log_step, s5_b_re=m_s5_b_re, s5_b_im=m_s5_b_im, s5_c_re=m_s5_c_re,
               s5_c_im=m_s5_c_im, s5_d=m_s5_d, s5_glu_w=m_s5_glu_w, s5_glu_b=m_s5_glu_b, m2_conv_w=m_m2_conv_w,
               m2_conv_b=m_m2_conv_b, m2_a_log=m_m2_a_log, m2_dt_bias=m_m2_dt_bias, m2_d=m_m2_d,
               m2_norm_w=m_m2_norm_w, proj_a=m_proj_a, proj_b=m_proj_b, proj_c=m_proj_c, w_out=m_w_out,
               final_norm_w=m_final_norm_w)
    var = dict(norm_w=v_norm_w, w_in=v_w_in, gdn_conv_w=v_gdn_conv_w, gdn_a_log=v_gdn_a_log,
               gdn_dt_bias=v_gdn_dt_bias, gdn_norm_w=v_gdn_norm_w, s5_lam_re=v_s5_lam_re, s5_lam_im=v_s5_lam_im,
               s5_log_step=v_s5_log_step, s5_b_re=v_s5_b_re, s5_b_im=v_s5_b_im, s5_c_re=v_s5_c_re,
               s5_c_im=v_s5_c_im, s5_d=v_s5_d, s5_glu_w=v_s5_glu_w, s5_glu_b=v_s5_glu_b, m2_conv_w=v_m2_conv_w,
               m2_conv_b=v_m2_conv_b, m2_a_log=v_m2_a_log, m2_dt_bias=v_m2_dt_bias, m2_d=v_m2_d,
               m2_norm_w=v_m2_norm_w, proj_a=v_proj_a, proj_b=v_proj_b, proj_c=v_proj_c, w_out=v_w_out,
               final_norm_w=v_final_norm_w)
    me = lax.axis_index("x") * 4 + lax.axis_index("y") * 2 + lax.axis_index("c")
    x2 = x[0]
    tgt = loss_target[0]

    ra, rb = proj_a.shape[1], proj_b.shape[1]
    pabc = jnp.concatenate([proj_a, proj_b, proj_c], axis=1).astype(bf16)
    w_in16, glu16, w_out16 = w_in.astype(bf16), s5_glu_w.astype(bf16), w_out.astype(bf16)

    def shards(i):
        return [w_in16[i], glu16[i], pabc[i], w_out16[i], gdn_conv_w[i], m2_conv_w[i]]

    def cols(gathered):
        return jnp.concatenate([gathered[d] for d in range(N_DEV)], axis=1)

    def rows_of(gathered):
        return gathered.reshape(-1, gathered.shape[-1])

    def full_weights(i, gathered):
        g_win, g_glu, g_pabc, g_wout, g_gcv, g_mcv = gathered
        pf = cols(g_pabc)
        lw = dict(w_in_p=_win_to_padded(cols(g_win)), s5_glu_w=rows_of(g_glu), proj_a=pf[:ra],
                  proj_b=pf[ra:ra + rb], proj_c=pf[ra + rb:], w_out=rows_of(g_wout),
                  gdn_conv_w=cols(g_gcv), m2_conv_w=cols(g_mcv))
        for k in SMALL:
            if k != "final_norm_w":
                lw[k] = w[k][i]
        return lw

    layers, saved = [], []
    act = x2
    gathered = _exchange("gather_weights", shards(0), True)
    for i in range(DEPTH):
        layers.append(full_weights(i, gathered))
        act, sv, gathered = _layer_fwd(act, layers[i], shards(i + 1) if i + 1 < DEPTH else None)
        saved.append(sv)
    loss_row, dact, dfinal = _loss_grad(act, final_norm_w.reshape(1, D_MODEL), tgt)
    loss = lax.psum(loss_row[0, 0], ("x", "y", "c"))

    def col_blocks(full):
        r = full.shape[0]
        return full.reshape(r, N_DEV, -1).transpose(1, 0, 2)

    def row_blocks(full):
        return full.reshape(N_DEV, -1, full.shape[1])

    def packed_grads(g):
        return [col_blocks(_win_from_padded(g["w_in_p"])), row_blocks(g["s5_glu_w"].astype(bf16)),
                col_blocks(jnp.concatenate([g["proj_a"], g["proj_b"], g["proj_c"]], axis=0)), row_blocks(g["w_out"])]

    grads, arrived = [None] * DEPTH, [None] * DEPTH
    later = None
    for i in reversed(range(DEPTH)):
        dact, grads[i], got = _layer_bwd(dact, layers[i], saved[i], later)
        if got is not None:
            arrived[i + 1] = got
        saved[i] = None
        later = packed_grads(grads[i])
    arrived[0] = _exchange("scatter_grads", later, False)

    g_out, delta, new_m, new_v = {}, {}, {}, {}
    for k, buf, row_off in (("w_in", 0, 0), ("s5_glu_w", 1, 0), ("proj_a", 2, 0), ("proj_b", 2, ra),
                            ("proj_c", 2, ra + rb), ("w_out", 3, 0)):
        recvs = [arrived[i][buf] for i in range(DEPTH)]
        g_out[k], delta[k], new_m[k], new_v[k] = _adamw_slots("adamw_" + k, recvs, row_off, w[k], mom[k], var[k])

    names_small = [k for k in SMALL if k != "final_norm_w"]
    small_parts = [jnp.stack([grads[i][k] for i in range(DEPTH)]) for k in names_small + list(CONV)]
    small_parts.append(dfinal.reshape(-1))
    small_shapes = [p.shape for p in small_parts]
    small_local = _pack(small_parts)
    small_sum = _sum_slots("sum_small", _exchange("gather_small", [small_local], True)[0])
    unp = _unpack(small_sum, small_shapes)
    for k, val in zip(names_small, unp):
        g_out[k] = val
    for j, k in enumerate(CONV):
        width = w[k].shape[2]
        g_out[k] = lax.dynamic_slice_in_dim(unp[len(names_small) + j], me * width, width, axis=2)
    g_out["final_norm_w"] = unp[-1]

    rest = [k for k in WEIGHTS if k not in BIG]
    rest_shapes = [w[k].shape for k in rest]
    packed = [_pack([src[k] for k in rest]) for src in (w, g_out, mom, var)]
    d, nm, nv = _adamw("adamw_small", *packed)
    for dst, arr in ((delta, d), (new_m, nm), (new_v, nv)):
        for k, val in zip(rest, _unpack(arr, rest_shapes)):
            dst[k] = val

    grad_x = dact.reshape(x.shape)
    return (loss, grad_x, *[g_out[k] for k in WEIGHTS], *[delta[k] for k in WEIGHTS],
            *[new_m[k] for k in WEIGHTS], *[new_v[k] for k in WEIGHTS])
```

```python
import functools

import jax
import jax.numpy as jnp
from jax import lax
from jax.experimental import pallas as pl
from jax.experimental.pallas import tpu as pltpu

f32 = jnp.float32
bf16 = jnp.bfloat16

N_DEV = 8
DEPTH = 4
D_MODEL = 2048
GDN_HEADS = 8
HEAD_DIM = 128
GDN_WIDTH = 1024
S5_GROUPS = 48
S5_GROUP_SIZE = 16
S5_STATE = 64
S5_WIDTH = 768
S5_LANES = S5_GROUPS * S5_STATE
S5_BLK = 6
M2_HEADS = 16
M2_HEAD_DIM = 64
M2_WIDTH = 1024
M2_GROUPS = 4
M2_STATE = 128
CONV_K = 4
CHUNK = 64
HALO = 8
NORM_EPS = 1e-6
IN_DIM = 14880
LANE = 128
VMEM_LIMIT = 48 * 1024 * 1024

ADAM_LR = 0.001
ADAM_B1 = 0.9
ADAM_B2 = 0.999
ADAM_EPS = 1e-08
ADAM_WD = 0.01
ADAM_STEP = 10

U_MERGE, U_SU, U_SG, U_Q, U_K, U_V, U_AZ, U_CZ, U_CX, U_CB, U_CC, U_SMA, U_SMC = (
    0, 48, 54, 60, 68, 76, 84, 92, 100, 108, 112, 116, 117)
NP_UNITS = 120
NP_COLS = NP_UNITS * LANE


def _cparams(sem=None):
    return pltpu.CompilerParams(dimension_semantics=sem, vmem_limit_bytes=VMEM_LIMIT)


def _pick(dim, target):
    if dim <= target:
        return dim
    for t in range(target - target % LANE, 0, -LANE):
        if dim % t == 0:
            return t
    raise ValueError(f"no tile for {dim}")


def _bd(a, b, dims):
    return lax.dot_general(a.astype(bf16), b.astype(bf16), (dims, ((), ())), preferred_element_type=f32)


@jax.custom_vjp
def dot_nn(a, b):
    return _bd(a, b, ((1,), (0,)))


@jax.custom_vjp
def dot_nt(a, b):
    return _bd(a, b, ((1,), (1,)))


@jax.custom_vjp
def dot_tn(a, b):
    return _bd(a, b, ((0,), (0,)))


dot_nn.defvjp(lambda a, b: (dot_nn(a, b), (a, b)), lambda r, ct: (dot_nt(ct, r[1]), dot_tn(r[0], ct)))
dot_nt.defvjp(lambda a, b: (dot_nt(a, b), (a, b)), lambda r, ct: (dot_nn(ct, r[1]), dot_tn(ct, r[0])))
dot_tn.defvjp(lambda a, b: (dot_tn(a, b), (a, b)), lambda r, ct: (dot_nt(r[1], ct), dot_nn(r[0], ct)))

_HI = lax.Precision.HIGHEST


def _silu(x):
    return x * jax.nn.sigmoid(x)


def _softplus(x):
    return jnp.maximum(x, 0.0) + jnp.log(1.0 + jnp.exp(-jnp.abs(x)))


def _tri_masks(c):
    row = lax.broadcasted_iota(jnp.int32, (c, c), 0)
    col = lax.broadcasted_iota(jnp.int32, (c, c), 1)
    return row >= col, row > col, (row >= col).astype(f32), (row == col).astype(f32)


def _scan_add(x, reverse):
    t = x.shape[0]
    row = lax.broadcasted_iota(jnp.int32, x.shape, 0)
    d = 1
    while d < t:
        if reverse:
            x = x + jnp.where(row < t - d, pltpu.roll(x, t - d, 0), 0.0)
        else:
            x = x + jnp.where(row >= d, pltpu.roll(x, d, 0), 0.0)
        d *= 2
    return x


@jax.custom_vjp
def _cumsum_rows(x):
    return _scan_add(x, False)


_cumsum_rows.defvjp(lambda x: (_scan_add(x, False), None), lambda _, ct: (_scan_add(ct, True),))


def _cumsum_all(a):
    cum = _cumsum_rows(a)
    return cum, cum.T


def _onehot_lane(idx):
    return (lax.broadcasted_iota(jnp.int32, (1, LANE), 1) == idx).astype(f32)


def _onehot_sub(idx):
    return (lax.broadcasted_iota(jnp.int32, (LANE, 1), 0) == idx).astype(f32)


def _pick_col(x, idx):
    return jnp.sum(x * _onehot_lane(idx), axis=-1, keepdims=True)


def _pick_row(xt, idx):
    return jnp.sum(xt * _onehot_sub(idx), axis=0, keepdims=True)


def _peer(idx):
    return (idx // 4, (idx // 2) % 2, idx % 2)


def _remote(src, dst, send_sems, recv_sems, k, to):
    return pltpu.make_async_remote_copy(src_ref=src, dst_ref=dst, send_sem=send_sems.at[k], recv_sem=recv_sems.at[k],
                                        device_id=_peer(to), device_id_type=pl.DeviceIdType.MESH)


def _scatter_phase(phase, srcs, outs, send_sems, recv_sems, local_sems):
    nb = len(srcs)
    me = lax.axis_index("x") * 4 + lax.axis_index("y") * 2 + lax.axis_index("c")
    if phase == "forward":
        return
    mine = [pltpu.make_async_copy(srcs[b].at[me], outs[b].at[me], local_sems.at[b]) for b in range(nb)]
    sends = [_remote(srcs[b].at[(me + k) % N_DEV], outs[b].at[me], send_sems, recv_sems, (k - 1) * nb + b,
                     (me + k) % N_DEV) for k in range(1, N_DEV) for b in range(nb)]
    if phase == "start":
        for cp in mine + sends:
            cp.start()
    if phase == "finish":
        for k in range(1, N_DEV):
            frm = (me + N_DEV - k) % N_DEV
            for b in range(nb):
                _remote(srcs[b].at[me], outs[b].at[frm], send_sems, recv_sems, (k - 1) * nb + b, frm).wait_recv()
        for cp in sends:
            cp.wait_send()
        for cp in mine:
            cp.wait()


def _gather_phase(phase, srcs, outs, send_sems, recv_sems, local_sems):
    nb = len(srcs)
    x, y, c = lax.axis_index("x"), lax.axis_index("y"), lax.axis_index("c")
    me, sib = 4 * x + 2 * y + c, 4 * x + 2 * y + 1 - c
    chips = [(1 - x, y), (x, 1 - y), (1 - x, 1 - y)]
    same = [4 * cx + 2 * cy + c for cx, cy in chips]
    other = [4 * cx + 2 * cy + 1 - c for cx, cy in chips]

    def copy(b, k, src, block, to):
        return _remote(src, outs[b].at[block], send_sems, recv_sems, k * nb + b, to)

    def local():
        return [pltpu.make_async_copy(srcs[b], outs[b].at[me], local_sems.at[b]) for b in range(nb)]

    def first():
        return ([copy(b, 0, srcs[b], me, sib) for b in range(nb)]
                + [copy(b, 1 + j, srcs[b], me, same[j]) for j in range(3) for b in range(nb)])

    def passed(j, b):
        return copy(b, 4 + j, outs[b].at[same[j]], same[j], sib)

    if phase == "start":
        for cp in local() + first():
            cp.start()
    if phase == "forward":
        for j in range(3):
            for b in range(nb):
                copy(b, 1 + j, srcs[b], same[j], me).wait_recv()
                passed(j, b).start()
    if phase == "finish":
        for b in range(nb):
            copy(b, 0, srcs[b], sib, me).wait_recv()
        for j in range(3):
            for b in range(nb):
                copy(b, 4 + j, srcs[b], other[j], me).wait_recv()
        for cp in first() + [passed(j, b) for j in range(3) for b in range(nb)]:
            cp.wait_send()
        for cp in local():
            cp.wait()


def _comm_phase(phase, *refs, gather):
    (_gather_phase if gather else _scatter_phase)(phase, *refs)


def _comm_shapes(bufs, gather):
    nb = len(bufs)
    out_shape = [jax.ShapeDtypeStruct(((N_DEV,) + tuple(b.shape)) if gather else tuple(b.shape), b.dtype) for b in bufs]
    sems = [pltpu.SemaphoreType.DMA(((N_DEV - 1) * nb,)), pltpu.SemaphoreType.DMA(((N_DEV - 1) * nb,)),
            pltpu.SemaphoreType.DMA((nb,))]
    return out_shape, sems


def _exchange(name, bufs, gather):
    nb = len(bufs)

    def body(*refs):
        args = (refs[:nb], refs[nb:2 * nb], *refs[2 * nb:])
        for phase in ("start", "forward", "finish"):
            _comm_phase(phase, *args, gather=gather)

    any_spec = pl.BlockSpec(memory_space=pl.ANY)
    out_shape, sems = _comm_shapes(bufs, gather)
    return pl.pallas_call(body, name=name, in_specs=[any_spec] * nb, out_specs=[any_spec] * nb, out_shape=out_shape,
                          scratch_shapes=sems)(*bufs)


def _mm(name, a, b, mode, residual=None, out_dtype=f32, tm=1024, tn=1024, tk=1024, comm=None):
    if mode == "nn":
        (m, k), (_, n) = a.shape, b.shape
    elif mode == "nt":
        (m, k), (n, _) = a.shape, b.shape
    else:
        (k, m), (_, n) = a.shape, b.shape
    tm, tn, tk = _pick(m, tm), _pick(n, tn), _pick(k, tk)
    gm, gn, nk = m // tm, n // tn, k // tk
    dims = {"nn": ((1,), (0,)), "nt": ((1,), (1,)), "tn": ((0,), (0,))}[mode]
    has_res = residual is not None
    bufs, gather = comm if comm is not None else ([], True)
    nb = len(bufs)
    n_in = 2 + has_res
    forward_step = gm * gn * nk * 3 // 4

    def body(*refs):
        a_ref, b_ref = refs[:2]
        r_ref = refs[2] if has_res else None
        srcs = refs[n_in:n_in + nb]
        o_ref = refs[n_in + nb]
        outs = refs[n_in + nb + 1:n_in + 2 * nb + 1]
        scratch = refs[n_in + 2 * nb + 1:]
        acc_ref = scratch[0] if nk > 1 else None
        sems = scratch[1:] if nk > 1 else scratch
        i, j, kk = pl.program_id(0), pl.program_id(1), pl.program_id(2)

        step = (i * gn + j) * nk + kk
        if nb:
            @pl.when(step == 0)
            def _():
                _comm_phase("start", srcs, outs, *sems, gather=gather)

            @pl.when(step == forward_step)
            def _():
                _comm_phase("forward", srcs, outs, *sems, gather=gather)

        def finish(out):
            if has_res:
                out = out + r_ref[...].astype(f32)
            o_ref[...] = out.astype(o_ref.dtype)

        if nk == 1:
            finish(_bd(a_ref[...], b_ref[...], dims))
        else:
            @pl.when(kk == 0)
            def _():
                acc_ref[...] = jnp.zeros_like(acc_ref)

            acc_ref[...] += _bd(a_ref[...], b_ref[...], dims)

            @pl.when(kk == nk - 1)
            def _():
                finish(acc_ref[...])

        if nb:
            @pl.when(step == gm * gn * nk - 1)
            def _():
                _comm_phase("finish", srcs, outs, *sems, gather=gather)

    if mode == "tn":
        a_spec = pl.BlockSpec((tk, tm), lambda i, j, kk: (kk, i))
    else:
        a_spec = pl.BlockSpec((tm, tk), lambda i, j, kk: (i, kk))
    if mode == "nt":
        b_spec = pl.BlockSpec((tn, tk), lambda i, j, kk: (j, kk))
    else:
        b_spec = pl.BlockSpec((tk, tn), lambda i, j, kk: (kk, j))
    o_spec = pl.BlockSpec((tm, tn), lambda i, j, kk: (i, j))
    any_spec = pl.BlockSpec(memory_space=pl.ANY)
    in_specs = [a_spec, b_spec] + ([o_spec] if has_res else []) + [any_spec] * nb
    args = (a, b) + ((residual,) if has_res else ()) + tuple(bufs)
    comm_shapes, comm_sems = _comm_shapes(bufs, gather) if nb else ([], [])
    res = pl.pallas_call(
        body, name=name, grid=(gm, gn, nk), in_specs=in_specs, out_specs=[o_spec] + [any_spec] * nb,
        out_shape=[jax.ShapeDtypeStruct((m, n), out_dtype)] + comm_shapes,
        scratch_shapes=([pltpu.VMEM((tm, tn), f32)] if nk > 1 else []) + comm_sems,
        compiler_params=_cparams(("arbitrary",) * 3 if nb else ("parallel", "parallel", "arbitrary")))(*args)
    return res if nb else res[0]


def _act_spec(t, width, colblk):
    return pl.BlockSpec((t, width), lambda i: (i, colblk))


def _tok_fwd(name, fn, acts, params, outs, t):
    rows = acts[0][0].shape[0]
    t = min(t, rows)
    na, npar = len(acts), len(params)

    def body(*refs):
        a = [r[...].astype(f32) for r in refs[:na]]
        p = [r[...].astype(f32) for r in refs[na:na + npar]]
        res = fn(*a, *p)
        for o_ref, o in zip(refs[na + npar:], res):
            o_ref[...] = o.astype(o_ref.dtype)

    in_specs = [_act_spec(t, w, cb) for (_, w, cb) in acts]
    in_specs += [pl.BlockSpec(p.shape, lambda i: (0, 0)) for p in params]
    res = pl.pallas_call(
        body, name=name, grid=(rows // t,), in_specs=in_specs,
        out_specs=[_act_spec(t, w, 0) for (w, _) in outs],
        out_shape=[jax.ShapeDtypeStruct((rows, w), dt) for (w, dt) in outs],
        compiler_params=_cparams(("arbitrary",)))(*[a for (a, _, _) in acts], *params)
    return res


def _tok_bwd(name, fn, acts, params, cts, dact, t, residuals=None):
    rows = acts[0][0].shape[0]
    t = min(t, rows)
    residuals = residuals or {}
    res_ids = sorted(residuals)
    na, npar, nc, nr, nd = len(acts), len(params), len(cts), len(res_ids), len(dact)

    def body(*refs):
        a = [r[...].astype(f32) for r in refs[:na]]
        p = [r[...].astype(f32) for r in refs[na:na + npar]]
        ct = tuple(r[...].astype(f32) for r in refs[na + npar:na + npar + nc])
        rs = {idx: r[...].astype(f32) for idx, r in zip(res_ids, refs[na + npar + nc:na + npar + nc + nr])}
        orefs = refs[na + npar + nc + nr:]
        _, vjp = jax.vjp(fn, *a, *p)
        grads = vjp(ct)
        for o_ref, (idx, _) in zip(orefs[:nd], dact):
            g = grads[idx]
            if idx in rs:
                g = g + rs[idx]
            o_ref[...] = g.astype(o_ref.dtype)

        if npar:
            @pl.when(pl.program_id(0) == 0)
            def _():
                for o_ref in orefs[nd:]:
                    o_ref[...] = jnp.zeros_like(o_ref)

            for o_ref, g in zip(orefs[nd:], grads[na:]):
                o_ref[...] += g

    in_specs = [_act_spec(t, w, cb) for (_, w, cb) in acts]
    in_specs += [pl.BlockSpec(p.shape, lambda i: (0, 0)) for p in params]
    in_specs += [_act_spec(t, w, cb) for (_, w, cb) in cts]
    in_specs += [_act_spec(t, acts[idx][1], 0) for idx in res_ids]
    out_specs = [_act_spec(t, acts[idx][1], 0) for (idx, _) in dact]
    out_specs += [pl.BlockSpec(p.shape, lambda i: (0, 0)) for p in params]
    out_shape = [jax.ShapeDtypeStruct((rows, acts[idx][1]), dt) for (idx, dt) in dact]
    out_shape += [jax.ShapeDtypeStruct(p.shape, f32) for p in params]
    return pl.pallas_call(
        body, name=name, grid=(rows // t,), in_specs=in_specs, out_specs=out_specs, out_shape=out_shape,
        compiler_params=_cparams(("arbitrary",)))(
            *[a for (a, _, _) in acts], *params, *[c for (c, _, _) in cts], *[residuals[i] for i in res_ids])


def _rms_fn(x, w):
    return (x * lax.rsqrt(jnp.mean(x * x, axis=-1, keepdims=True) + NORM_EPS) * w,)


def _gate_fn(la, lb, lc, pa, pb, pc):
    return (jax.nn.sigmoid(la) * pa + jax.nn.sigmoid(lb) * pb + jax.nn.sigmoid(lc) * pc,)


def _s5_tail_fn(ypre, gate, glu_w, glu_b):
    y = jax.nn.gelu(ypre)
    y = y * jax.nn.sigmoid(dot_nn(y, glu_w) + glu_b)
    return (y * _silu(gate),)


def _loss_grad(x, w, target, t=256):
    rows, d = x.shape
    t = min(t, rows)

    def fn(xt, wt, tt):
        y = _rms_fn(xt, wt)[0]
        err = y - tt
        return 0.5 * jnp.sum(jnp.sum(err * err, axis=-1, keepdims=True), axis=0, keepdims=True) / d

    def body(x_ref, w_ref, t_ref, loss_ref, dx_ref, dw_ref):
        tt = t_ref[...]
        val, vjp = jax.vjp(lambda a, b: fn(a, b, tt), x_ref[...], w_ref[...])
        dx, dw = vjp(jnp.ones((1, 1), f32))
        dx_ref[...] = dx

        @pl.when(pl.program_id(0) == 0)
        def _():
            loss_ref[...] = jnp.zeros_like(loss_ref)
            dw_ref[...] = jnp.zeros_like(dw_ref)

        loss_ref[...] += val * jnp.ones((1, LANE), f32)
        dw_ref[...] += dw

    return pl.pallas_call(
        body, name="loss_grad", grid=(rows // t,),
        in_specs=[_act_spec(t, d, 0), pl.BlockSpec((1, d), lambda i: (0, 0)), _act_spec(t, d, 0)],
        out_specs=[pl.BlockSpec((1, LANE), lambda i: (0, 0)), _act_spec(t, d, 0), pl.BlockSpec((1, d), lambda i: (0, 0))],
        out_shape=[jax.ShapeDtypeStruct((1, LANE), f32), jax.ShapeDtypeStruct((rows, d), f32),
                   jax.ShapeDtypeStruct((1, d), f32)],
        compiler_params=_cparams(("arbitrary",)))(x, w, target)


GDN_HB = 4
GDN_SW = GDN_HB * HEAD_DIM
GDN_STEPS = GDN_HEADS // GDN_HB


def _tri_inv(a, eye, c):
    rows = a.shape[0]
    n = -a
    p = eye + n
    npow = dot_nn(n, n)
    levels = c.bit_length() - 1
    for j in range(2, levels):
        both = dot_nn(jnp.concatenate([p, npow], axis=0), npow)
        p, npow = p + both[:rows], both[rows:]
    return p + dot_nn(p, npow)


def _block_ids(rows, c):
    ri = lax.broadcasted_iota(jnp.int32, (rows, rows), 0)
    ci = lax.broadcasted_iota(jnp.int32, (rows, rows), 1)
    r1 = lax.broadcasted_iota(jnp.int32, (rows, 1), 0)
    rb, cb, r1b = 0, 0, 0
    for edge in range(c, rows, c):
        rb = rb + (ri >= edge).astype(jnp.int32)
        cb = cb + (ci >= edge).astype(jnp.int32)
        r1b = r1b + (r1 >= edge).astype(jnp.int32)
    return ri, ci, rb, cb, r1b


def _gdn_step(qc, kc, vc, z, small, alog_row, dtb_row, normw, s_cat, head0):
    c = qc.shape[0]
    hb = GDN_HB
    rows = hb * c

    def stack(x):
        return jnp.concatenate([x[:, r * HEAD_DIM:(r + 1) * HEAD_DIM] for r in range(hb)], axis=0)

    ri, ci, rb, cb, r1b = _block_ids(rows, c)
    same = rb == cb
    causal = same & (ri >= ci)
    strict = same & (ri > ci)
    eye = (ri == ci).astype(f32)
    head_rows = [(r1b == r).astype(f32) for r in range(hb)]

    def own_block(x):
        acc = None
        for r in range(hb):
            term = x[:, r * HEAD_DIM:(r + 1) * HEAD_DIM] * head_rows[r]
            acc = term if acc is None else acc + term
        return acc

    beta_all = jax.nn.sigmoid(small)
    g_all = -jnp.exp(alog_row) * _softplus(small + dtb_row)
    gc_all, gct_all = _cumsum_all(g_all)
    beta = jnp.concatenate([_pick_col(beta_all, head0 + r) for r in range(hb)], axis=0)
    gc = jnp.concatenate([_pick_col(gc_all, head0 + r + GDN_HEADS) for r in range(hb)], axis=0)
    gc_t = jnp.concatenate([_pick_row(gct_all, head0 + r + GDN_HEADS) for r in range(hb)], axis=1)
    g_last = [gc[(r + 1) * c - 1:(r + 1) * c, :] for r in range(hb)]
    gl = sum(head_rows[r] * g_last[r] for r in range(hb))

    q = _silu(stack(qc))
    k = _silu(stack(kc))
    v = _silu(stack(vc))
    q = q * lax.rsqrt(jnp.sum(q * q, axis=-1, keepdims=True) + NORM_EPS) * (HEAD_DIM ** -0.5)
    k = k * lax.rsqrt(jnp.sum(k * k, axis=-1, keepdims=True) + NORM_EPS)
    decay = jnp.exp(jnp.where(causal, gc - gc_t, -1e30))
    egc = jnp.exp(gc)
    kb = k * beta
    a_mat = jnp.where(strict, dot_nt(kb, k) * decay, 0.0)
    t_inv = _tri_inv(a_mat, eye, c)
    uw = dot_nn(t_inv, jnp.concatenate([v * beta, kb * egc], axis=1))
    u, w = uw[:, :HEAD_DIM], uw[:, HEAD_DIM:]
    qk = dot_nt(q, k) * decay
    on_state = dot_nn(jnp.concatenate([w, q * egc], axis=0), s_cat)
    v_new = u - own_block(on_state[:rows])
    out = own_block(on_state[rows:]) + dot_nn(qk, v_new)
    k_tail = k * jnp.exp(gl - gc)
    v_bd = jnp.concatenate([v_new * head_rows[r] for r in range(hb)], axis=1)
    eg_cat = jnp.concatenate([jnp.exp(g_last[r]) * jnp.ones((1, HEAD_DIM), f32) for r in range(hb)], axis=1)
    new_s = s_cat * eg_cat + dot_tn(k_tail, v_bd)
    o = out * lax.rsqrt(jnp.mean(out * out, axis=-1, keepdims=True) + NORM_EPS) * normw * _silu(stack(z))
    o = jnp.concatenate([o[r * c:(r + 1) * c] for r in range(hb)], axis=1)
    return o, new_s


def _conv_windows(xin_ref, p, cw_ref, c):
    acc = None
    for k in range(CONV_K):
        term = cw_ref[pl.ds(k, 1), :] * xin_ref[p, pl.ds(HALO - CONV_K + 1 + k, c), :]
        acc = term if acc is None else acc + term
    return acc


def _gdn_fwd(proj, conv_w, alog_row, dtb_row, normw):
    rows = proj.shape[0]
    c = min(CHUNK, rows)
    n = rows // c

    def body(q_ref, k_ref, v_ref, z_ref, sm_ref, cwq, cwk, cwv, al_ref, dt_ref, nw_ref, y_ref, ck_ref, s_ref, xin_ref):
        hb = pl.program_id(0)
        i = pl.program_id(1)

        @pl.when(i == 0)
        def _():
            s_ref[...] = jnp.zeros_like(s_ref)
            xin_ref[:, 0:HALO, :] = jnp.zeros((3, HALO, GDN_SW), f32)

        @pl.when(i > 0)
        def _():
            xin_ref[:, 0:HALO, :] = xin_ref[:, c:c + HALO, :]

        xin_ref[0, HALO:, :] = q_ref[...]
        xin_ref[1, HALO:, :] = k_ref[...]
        xin_ref[2, HALO:, :] = v_ref[...]
        qc = _conv_windows(xin_ref, 0, cwq, c)
        kc = _conv_windows(xin_ref, 1, cwk, c)
        vc = _conv_windows(xin_ref, 2, cwv, c)
        state = s_ref[...]
        ck_ref[...] = state
        o, new_state = _gdn_step(qc, kc, vc, z_ref[...], sm_ref[...], al_ref[...], dt_ref[...], nw_ref[...], state,
                                 hb * GDN_HB)
        y_ref[...] = o.astype(y_ref.dtype)
        s_ref[...] = new_state

    def blk(unit):
        return pl.BlockSpec((c, GDN_SW), lambda hb, i: (i, unit // GDN_HB + hb))

    def cw(part):
        return pl.BlockSpec((CONV_K, GDN_SW), lambda hb, i: (0, part * GDN_STEPS + hb))

    row = pl.BlockSpec((1, LANE), lambda hb, i: (0, 0))
    return pl.pallas_call(
        body, name="gdn_fwd", grid=(GDN_STEPS, n),
        in_specs=[blk(U_Q), blk(U_K), blk(U_V), blk(U_AZ), pl.BlockSpec((c, LANE), lambda hb, i: (i, U_SMA)),
                  cw(0), cw(1), cw(2), row, row, row],
        out_specs=[pl.BlockSpec((c, GDN_SW), lambda hb, i: (i, hb)),
                   pl.BlockSpec((None, None, HEAD_DIM, GDN_SW), lambda hb, i: (hb, i, 0, 0))],
        out_shape=[jax.ShapeDtypeStruct((rows, GDN_WIDTH), bf16),
                   jax.ShapeDtypeStruct((GDN_STEPS, n, HEAD_DIM, GDN_SW), f32)],
        scratch_shapes=[pltpu.VMEM((HEAD_DIM, GDN_SW), f32), pltpu.VMEM((3, c + HALO, GDN_SW), f32)],
        compiler_params=_cparams(("arbitrary", "arbitrary")))(
            proj, proj, proj, proj, proj, conv_w, conv_w, conv_w, alog_row, dtb_row, normw)


def _conv_bwd(xin_ref, dyext_ref, p, cw_ref, dxc, dx_ref, dcw_ref, c):
    dyext_ref[p, 0:c, :] = dxc
    acc = None
    for k in range(CONV_K):
        term = cw_ref[pl.ds(k, 1), :] * dyext_ref[p, pl.ds(CONV_K - 1 - k, c), :]
        acc = term if acc is None else acc + term
        dcw_ref[pl.ds(k, 1), :] += jnp.sum(xin_ref[p, pl.ds(HALO - CONV_K + 1 + k, c), :] * dxc, axis=0, keepdims=True)
    dx_ref[...] = acc.astype(dx_ref.dtype)


def _gdn_bwd(proj, dy, ck, conv_w, alog_row, dtb_row, normw, scatter=None):
    rows = proj.shape[0]
    c = min(CHUNK, rows)
    n = rows // c
    halo_blocks = c // HALO
    bufs = scatter or []
    nb = len(bufs)
    n_in, n_out, n_scr = 16, 11, 3

    def body(*refs):
        core = refs[:n_in] + refs[n_in + nb:n_in + nb + n_out] + refs[n_in + 2 * nb + n_out:n_in + 2 * nb + n_out + n_scr]
        comm = (refs[n_in:n_in + nb], refs[n_in + nb + n_out:n_in + 2 * nb + n_out], *refs[n_in + 2 * nb + n_out + n_scr:])
        step_id = pl.program_id(0) * n + pl.program_id(1)
        if nb:
            @pl.when(step_id == 0)
            def _():
                _comm_phase("start", *comm, gather=False)

        chunk_step(*core)
        if nb:
            @pl.when(step_id == GDN_STEPS * n - 1)
            def _():
                _comm_phase("finish", *comm, gather=False)

    def chunk_step(q_ref, k_ref, v_ref, hq_ref, hk_ref, hv_ref, z_ref, sm_ref, cwq, cwk, cwv, al_ref, dt_ref, nw_ref,
                   ck_ref, dy_ref, dq_ref, dk_ref, dv_ref, dz_ref, dsm_ref, dcwq, dcwk, dcwv, dal_ref, ddt_ref, dnw_ref,
                   ds_ref, xin_ref, dyext_ref):
        hb = pl.program_id(0)
        i = pl.program_id(1)
        ci = n - 1 - i

        @pl.when(i == 0)
        def _():
            ds_ref[...] = jnp.zeros_like(ds_ref)
            dyext_ref[:, c:c + HALO, :] = jnp.zeros((3, HALO, GDN_SW), f32)
            for r in (dcwq, dcwk, dcwv, dal_ref, ddt_ref, dnw_ref):
                r[...] = jnp.zeros_like(r)

        @pl.when(i > 0)
        def _():
            dyext_ref[:, c:c + HALO, :] = dyext_ref[:, 0:HALO, :]

        first = (ci > 0).astype(f32)
        for p, (x_ref, halo_ref) in enumerate(((q_ref, hq_ref), (k_ref, hk_ref), (v_ref, hv_ref))):
            xin_ref[p, 0:HALO, :] = halo_ref[...] * first
            xin_ref[p, HALO:, :] = x_ref[...]
        qc = _conv_windows(xin_ref, 0, cwq, c)
        kc = _conv_windows(xin_ref, 1, cwk, c)
        vc = _conv_windows(xin_ref, 2, cwv, c)
        fn = functools.partial(_gdn_step, head0=hb * GDN_HB)
        _, vjp = jax.vjp(fn, qc, kc, vc, z_ref[...], sm_ref[...], al_ref[...], dt_ref[...], nw_ref[...], ck_ref[...])
        dqc, dkc, dvc, dz, dsm, dal, ddt, dnw, dstate = vjp((dy_ref[...].astype(f32), ds_ref[...]))
        ds_ref[...] = dstate
        dz_ref[...] = dz.astype(dz_ref.dtype)
        dsm_ref[...] = dsm
        dal_ref[...] += dal
        ddt_ref[...] += ddt
        dnw_ref[...] += dnw
        _conv_bwd(xin_ref, dyext_ref, 0, cwq, dqc, dq_ref, dcwq, c)
        _conv_bwd(xin_ref, dyext_ref, 1, cwk, dkc, dk_ref, dcwk, c)
        _conv_bwd(xin_ref, dyext_ref, 2, cwv, dvc, dv_ref, dcwv, c)

    def blk(unit):
        return pl.BlockSpec((c, GDN_SW), lambda hb, i: (n - 1 - i, unit // GDN_HB + hb))

    def halo(unit):
        return pl.BlockSpec((HALO, GDN_SW),
                            lambda hb, i: (jnp.maximum((n - 1 - i) * halo_blocks - 1, 0), unit // GDN_HB + hb))

    def cw(part):
        return pl.BlockSpec((CONV_K, GDN_SW), lambda hb, i: (0, part * GDN_STEPS + hb))

    row = pl.BlockSpec((1, LANE), lambda hb, i: (0, 0))
    hrow = pl.BlockSpec((None, 1, LANE), lambda hb, i: (hb, 0, 0))
    out_blk = pl.BlockSpec((c, GDN_SW), lambda hb, i: (n - 1 - i, hb))
    dcw = pl.BlockSpec((CONV_K, GDN_SW), lambda hb, i: (0, hb))
    wide = jax.ShapeDtypeStruct((rows, GDN_WIDTH), bf16)
    hrow_shape = jax.ShapeDtypeStruct((GDN_STEPS, 1, LANE), f32)
    dcw_shape = jax.ShapeDtypeStruct((CONV_K, GDN_WIDTH), f32)
    any_spec = pl.BlockSpec(memory_space=pl.ANY)
    comm_shapes, comm_sems = _comm_shapes(bufs, False) if nb else ([], [])
    return pl.pallas_call(
        body, name="gdn_bwd_scatter" if nb else "gdn_bwd", grid=(GDN_STEPS, n),
        in_specs=[blk(U_Q), blk(U_K), blk(U_V), halo(U_Q), halo(U_K), halo(U_V), blk(U_AZ),
                  pl.BlockSpec((c, LANE), lambda hb, i: (n - 1 - i, U_SMA)),
                  cw(0), cw(1), cw(2), row, row, row,
                  pl.BlockSpec((None, None, HEAD_DIM, GDN_SW), lambda hb, i: (hb, n - 1 - i, 0, 0)),
                  out_blk] + [any_spec] * nb,
        out_specs=[out_blk, out_blk, out_blk, out_blk,
                   pl.BlockSpec((None, c, LANE), lambda hb, i: (hb, n - 1 - i, 0)),
                   dcw, dcw, dcw, hrow, hrow, hrow] + [any_spec] * nb,
        out_shape=[wide, wide, wide, wide, jax.ShapeDtypeStruct((GDN_STEPS, rows, LANE), f32),
                   dcw_shape, dcw_shape, dcw_shape, hrow_shape, hrow_shape, hrow_shape] + comm_shapes,
        scratch_shapes=[pltpu.VMEM((HEAD_DIM, GDN_SW), f32), pltpu.VMEM((3, c + HALO, GDN_SW), f32),
                        pltpu.VMEM((3, c + HALO, GDN_SW), f32)] + comm_sems,
        compiler_params=_cparams(("arbitrary", "arbitrary")))(
            proj, proj, proj, proj, proj, proj, proj, proj, conv_w, conv_w, conv_w, alog_row, dtb_row, normw, ck, dy,
            *bufs)


M2_REP = M2_HEADS // M2_GROUPS
M2_GW = M2_REP * M2_HEAD_DIM
M2_GB = 2
M2_STEPS = M2_GROUPS // M2_GB
M2_XW = M2_GB * M2_GW
M2_BW = M2_GB * M2_STATE
M2_SH = M2_GB * M2_REP


def _ssd_step(xc, bc, cc, z, small, bias_x, bias_b, bias_c, alog_row, dtb_row, d_row, normw, state, grp0):
    c = xc.shape[0]
    causal = _tri_masks(c)[0]
    hd = M2_HEAD_DIM
    ones_l = jnp.ones((1, hd), f32)
    ones_r = jnp.ones((hd, 1), f32)
    lane = lax.broadcasted_iota(jnp.int32, (1, M2_GW), 1)
    lane_head = [((lane >= r * hd) & (lane < (r + 1) * hd)).astype(f32) for r in range(M2_REP)]
    xs = _silu(xc + bias_x)
    bms = _silu(bc + bias_b)
    cms = _silu(cc + bias_c)
    dt_all = _softplus(small + dtb_row)
    a_all = -jnp.exp(alog_row) * dt_all
    ac_all, act_all = _cumsum_all(a_all)
    ys, new_states = [], []
    for gi in range(M2_GB):
        bm = bms[:, gi * M2_STATE:(gi + 1) * M2_STATE]
        cm = cms[:, gi * M2_STATE:(gi + 1) * M2_STATE]
        xg = xs[:, gi * M2_GW:(gi + 1) * M2_GW]
        sg = state[gi * M2_GW:(gi + 1) * M2_GW]
        heads = [(grp0 + gi) * M2_REP + r for r in range(M2_REP)]
        ac_h = [_pick_col(ac_all, h) for h in heads]
        al_h = [a[c - 1:c, :] for a in ac_h]

        def wide(cols):
            return jnp.concatenate([v * ones_l for v in cols], axis=1)

        dt_w = wide([_pick_col(dt_all, h) for h in heads])
        ac_w = wide(ac_h)
        al_w = wide(al_h)
        dsk_w = wide([_pick_col(d_row, h) for h in heads])
        scores = dot_nt(cm, bm)
        m_wide = jnp.concatenate(
            [scores * jnp.exp(jnp.where(causal, a - _pick_row(act_all, h), -1e30)) for a, h in zip(ac_h, heads)], axis=1)
        xdt = xg * dt_w
        x_bd = jnp.concatenate([xdt * lane_head[r] for r in range(M2_REP)], axis=0)
        y_diag = dot_nn(m_wide, x_bd)
        states_new = dot_tn(xdt * jnp.exp(al_w - ac_w), bm)
        y_off = dot_nt(cm, sg) * jnp.exp(ac_w)
        eg_col = jnp.concatenate([jnp.exp(a) * ones_r for a in al_h], axis=0)
        new_states.append(sg * eg_col + states_new)
        y = (y_diag + y_off + dsk_w * xg) * _silu(z[:, gi * M2_GW:(gi + 1) * M2_GW])
        ys.append(y * lax.rsqrt(jnp.mean(y * y, axis=-1, keepdims=True) + NORM_EPS)
                  * normw[:, gi * M2_GW:(gi + 1) * M2_GW])
    return jnp.concatenate(ys, axis=-1), jnp.concatenate(new_states, axis=0)


def _conv_windows2(xin_ref, cw_ref, c):
    acc = None
    for k in range(CONV_K):
        term = cw_ref[pl.ds(k, 1), :] * xin_ref[pl.ds(HALO - CONV_K + 1 + k, c), :]
        acc = term if acc is None else acc + term
    return acc


def _ssd_specs(n, c, rev):
    def ci(i):
        return (n - 1 - i) if rev else i

    def blk(width, unit):
        return pl.BlockSpec((c, width), lambda g, i: (ci(i), unit * LANE // width + g))

    def par(rows_, width, col0):
        return pl.BlockSpec((rows_, width), lambda g, i: (0, col0 // width + g))

    return ci, blk, par


def _ssd_fwd(proj, conv_w, conv_b, alog_row, dtb_row, d_row, normw):
    rows = proj.shape[0]
    c = min(CHUNK, rows)
    n = rows // c
    _, blk, par = _ssd_specs(n, c, False)

    def body(x_ref, b_ref, c_ref, z_ref, sm_ref, cwx, cwb, cwc, bx, bb, bcc, al_ref, dt_ref, d_ref, nw_ref,
             y_ref, ck_ref, s_ref, xx_ref, xb_ref, xc_ref):
        g = pl.program_id(0)
        i = pl.program_id(1)

        @pl.when(i == 0)
        def _():
            s_ref[...] = jnp.zeros_like(s_ref)
            for r in (xx_ref, xb_ref, xc_ref):
                r[0:HALO, :] = jnp.zeros((HALO, r.shape[1]), f32)

        @pl.when(i > 0)
        def _():
            for r in (xx_ref, xb_ref, xc_ref):
                r[0:HALO, :] = r[c:c + HALO, :]

        xx_ref[HALO:, :] = x_ref[...]
        xb_ref[HALO:, :] = b_ref[...]
        xc_ref[HALO:, :] = c_ref[...]
        xc = _conv_windows2(xx_ref, cwx, c)
        bc = _conv_windows2(xb_ref, cwb, c)
        cc = _conv_windows2(xc_ref, cwc, c)
        state = s_ref[...]
        ck_ref[...] = state
        y, new_state = _ssd_step(xc, bc, cc, z_ref[...], sm_ref[...], bx[...], bb[...], bcc[...], al_ref[...],
                                 dt_ref[...], d_ref[...], nw_ref[...], state, g * M2_GB)
        y_ref[...] = y.astype(y_ref.dtype)
        s_ref[...] = new_state

    row = pl.BlockSpec((1, LANE), lambda g, i: (0, 0))
    off_b, off_c = M2_WIDTH, M2_WIDTH + M2_GROUPS * M2_STATE
    return pl.pallas_call(
        body, name="ssd_fwd", grid=(M2_STEPS, n),
        in_specs=[blk(M2_XW, U_CX), blk(M2_BW, U_CB), blk(M2_BW, U_CC), blk(M2_XW, U_CZ),
                  pl.BlockSpec((c, LANE), lambda g, i: (i, U_SMC)),
                  par(CONV_K, M2_XW, 0), par(CONV_K, M2_BW, off_b), par(CONV_K, M2_BW, off_c),
                  par(1, M2_XW, 0), par(1, M2_BW, off_b), par(1, M2_BW, off_c), row, row, row, par(1, M2_XW, 0)],
        out_specs=[pl.BlockSpec((c, M2_XW), lambda g, i: (i, g)),
                   pl.BlockSpec((None, None, M2_SH * M2_HEAD_DIM, M2_STATE), lambda g, i: (g, i, 0, 0))],
        out_shape=[jax.ShapeDtypeStruct((rows, M2_WIDTH), bf16),
                   jax.ShapeDtypeStruct((M2_STEPS, n, M2_SH * M2_HEAD_DIM, M2_STATE), f32)],
        scratch_shapes=[pltpu.VMEM((M2_SH * M2_HEAD_DIM, M2_STATE), f32), pltpu.VMEM((c + HALO, M2_XW), f32),
                        pltpu.VMEM((c + HALO, M2_BW), f32), pltpu.VMEM((c + HALO, M2_BW), f32)],
        compiler_params=_cparams(("arbitrary", "arbitrary")))(
            proj, proj, proj, proj, proj, conv_w, conv_w, conv_w, conv_b, conv_b, conv_b, alog_row, dtb_row, d_row, normw)


def _conv_bwd2(xin_ref, dyext_ref, cw_ref, dxc, dx_ref, dcw_ref, c):
    dyext_ref[0:c, :] = dxc
    acc = None
    for k in range(CONV_K):
        term = cw_ref[pl.ds(k, 1), :] * dyext_ref[pl.ds(CONV_K - 1 - k, c), :]
        acc = term if acc is None else acc + term
        dcw_ref[pl.ds(k, 1), :] += jnp.sum(xin_ref[pl.ds(HALO - CONV_K + 1 + k, c), :] * dxc, axis=0, keepdims=True)
    dx_ref[...] = acc.astype(dx_ref.dtype)


def _ssd_bwd(proj, dy, ck, conv_w, conv_b, alog_row, dtb_row, d_row, normw):
    rows = proj.shape[0]
    c = min(CHUNK, rows)
    n = rows // c
    halo_blocks = c // HALO
    _, blk, par = _ssd_specs(n, c, True)

    def body(x_ref, b_ref, c_ref, hx_ref, hb_ref, hc_ref, z_ref, sm_ref, cwx, cwb, cwc, bx, bb, bcc,
             al_ref, dt_ref, d_ref, nw_ref, ck_ref, dy_ref,
             dx_ref, db_ref, dc_ref, dz_ref, dsm_ref, dcwx, dcwb, dcwc, dbx, dbb, dbc, dal_ref, ddt_ref, dd_ref, dnw_ref,
             ds_ref, xx_ref, xb_ref, xc_ref, ex_ref, eb_ref, ec_ref):
        g = pl.program_id(0)
        i = pl.program_id(1)
        ci = n - 1 - i

        @pl.when(i == 0)
        def _():
            ds_ref[...] = jnp.zeros_like(ds_ref)
            for r in (ex_ref, eb_ref, ec_ref):
                r[c:c + HALO, :] = jnp.zeros((HALO, r.shape[1]), f32)
            for r in (dcwx, dcwb, dcwc, dbx, dbb, dbc, dal_ref, ddt_ref, dd_ref, dnw_ref):
                r[...] = jnp.zeros_like(r)

        @pl.when(i > 0)
        def _():
            for r in (ex_ref, eb_ref, ec_ref):
                r[c:c + HALO, :] = r[0:HALO, :]

        first = (ci > 0).astype(f32)
        for xin, x_in, halo_in in ((xx_ref, x_ref, hx_ref), (xb_ref, b_ref, hb_ref), (xc_ref, c_ref, hc_ref)):
            xin[0:HALO, :] = halo_in[...] * first
            xin[HALO:, :] = x_in[...]
        xc = _conv_windows2(xx_ref, cwx, c)
        bc = _conv_windows2(xb_ref, cwb, c)
        cc = _conv_windows2(xc_ref, cwc, c)
        fn = functools.partial(_ssd_step, grp0=g * M2_GB)
        _, vjp = jax.vjp(fn, xc, bc, cc, z_ref[...], sm_ref[...], bx[...], bb[...], bcc[...], al_ref[...], dt_ref[...],
                         d_ref[...], nw_ref[...], ck_ref[...])
        (dxc, dbc_, dcc, dz, dsm, gbx, gbb, gbc, dal, ddt, dd, dnw, dstate) = vjp((dy_ref[...].astype(f32), ds_ref[...]))
        ds_ref[...] = dstate
        dz_ref[...] = dz.astype(dz_ref.dtype)
        dsm_ref[...] = dsm
        dbx[...] += gbx
        dbb[...] += gbb
        dbc[...] += gbc
        dal_ref[...] += dal
        ddt_ref[...] += ddt
        dd_ref[...] += dd
        dnw_ref[...] += dnw
        _conv_bwd2(xx_ref, ex_ref, cwx, dxc, dx_ref, dcwx, c)
        _conv_bwd2(xb_ref, eb_ref, cwb, dbc_, db_ref, dcwb, c)
        _conv_bwd2(xc_ref, ec_ref, cwc, dcc, dc_ref, dcwc, c)

    def halo(width, unit):
        return pl.BlockSpec((HALO, width),
                            lambda g, i: (jnp.maximum((n - 1 - i) * halo_blocks - 1, 0), unit * LANE // width + g))

    row = pl.BlockSpec((1, LANE), lambda g, i: (0, 0))
    grow = pl.BlockSpec((None, 1, LANE), lambda g, i: (g, 0, 0))
    grow_shape = jax.ShapeDtypeStruct((M2_STEPS, 1, LANE), f32)
    ob_w = pl.BlockSpec((c, M2_XW), lambda g, i: (n - 1 - i, g))
    ob_n = pl.BlockSpec((c, M2_BW), lambda g, i: (n - 1 - i, g))
    off_b, off_c = M2_WIDTH, M2_WIDTH + M2_GROUPS * M2_STATE
    bc_w = M2_GROUPS * M2_STATE
    return pl.pallas_call(
        body, name="ssd_bwd", grid=(M2_STEPS, n),
        in_specs=[blk(M2_XW, U_CX), blk(M2_BW, U_CB), blk(M2_BW, U_CC),
                  halo(M2_XW, U_CX), halo(M2_BW, U_CB), halo(M2_BW, U_CC), blk(M2_XW, U_CZ),
                  pl.BlockSpec((c, LANE), lambda g, i: (n - 1 - i, U_SMC)),
                  par(CONV_K, M2_XW, 0), par(CONV_K, M2_BW, off_b), par(CONV_K, M2_BW, off_c),
                  par(1, M2_XW, 0), par(1, M2_BW, off_b), par(1, M2_BW, off_c), row, row, row, par(1, M2_XW, 0),
                  pl.BlockSpec((None, None, M2_SH * M2_HEAD_DIM, M2_STATE), lambda g, i: (g, n - 1 - i, 0, 0)),
                  ob_w],
        out_specs=[ob_w, ob_n, ob_n, ob_w, pl.BlockSpec((None, c, LANE), lambda g, i: (g, n - 1 - i, 0)),
                   par(CONV_K, M2_XW, 0), par(CONV_K, M2_BW, 0), par(CONV_K, M2_BW, 0),
                   par(1, M2_XW, 0), par(1, M2_BW, 0), par(1, M2_BW, 0), grow, grow, grow, par(1, M2_XW, 0)],
        out_shape=[jax.ShapeDtypeStruct((rows, M2_WIDTH), bf16), jax.ShapeDtypeStruct((rows, bc_w), bf16),
                   jax.ShapeDtypeStruct((rows, bc_w), bf16), jax.ShapeDtypeStruct((rows, M2_WIDTH), bf16),
                   jax.ShapeDtypeStruct((M2_STEPS, rows, LANE), f32),
                   jax.ShapeDtypeStruct((CONV_K, M2_WIDTH), f32), jax.ShapeDtypeStruct((CONV_K, bc_w), f32),
                   jax.ShapeDtypeStruct((CONV_K, bc_w), f32),
                   jax.ShapeDtypeStruct((1, M2_WIDTH), f32), jax.ShapeDtypeStruct((1, bc_w), f32),
                   jax.ShapeDtypeStruct((1, bc_w), f32), grow_shape, grow_shape, grow_shape,
                   jax.ShapeDtypeStruct((1, M2_WIDTH), f32)],
        scratch_shapes=[pltpu.VMEM((M2_SH * M2_HEAD_DIM, M2_STATE), f32),
                        pltpu.VMEM((c + HALO, M2_XW), f32), pltpu.VMEM((c + HALO, M2_BW), f32), pltpu.VMEM((c + HALO, M2_BW), f32),
                        pltpu.VMEM((c + HALO, M2_XW), f32), pltpu.VMEM((c + HALO, M2_BW), f32), pltpu.VMEM((c + HALO, M2_BW), f32)],
        compiler_params=_cparams(("arbitrary", "arbitrary")))(
            proj, proj, proj, proj, proj, proj, proj, proj, conv_w, conv_w, conv_w, conv_b, conv_b, conv_b,
            alog_row, dtb_row, d_row, normw, ck, dy)


S5_TILE = 128
S5_SW = S5_LANES // S5_BLK


def _scan_down(br, bi, ar, ai):
    t = br.shape[0]
    row = lax.broadcasted_iota(jnp.int32, br.shape, 0)
    d = 1
    while d < t:
        keep = row >= d
        sr = jnp.where(keep, pltpu.roll(br, d, 0), 0.0)
        si = jnp.where(keep, pltpu.roll(bi, d, 0), 0.0)
        br, bi = br + ar * sr - ai * si, bi + ar * si + ai * sr
        ar, ai = ar * ar - ai * ai, 2.0 * ar * ai
        d *= 2
    return br, bi


def _scan_up(br, bi, ar, ai):
    t = br.shape[0]
    row = lax.broadcasted_iota(jnp.int32, br.shape, 0)
    d = 1
    while d < t:
        keep = row < t - d
        sr = jnp.where(keep, pltpu.roll(br, t - d, 0), 0.0)
        si = jnp.where(keep, pltpu.roll(bi, t - d, 0), 0.0)
        br, bi = br + ar * sr - ai * si, bi + ar * si + ai * sr
        ar, ai = ar * ar - ai * ai, 2.0 * ar * ai
        d *= 2
    return br, bi


def _s5_states(u_j, bbr, bbi, ar, ai, cr, ci_):
    br = dot_nn(u_j, bbr)
    bi = dot_nn(u_j, bbi)
    row0 = lax.broadcasted_iota(jnp.int32, br.shape, 0) == 0
    br = br + jnp.where(row0, ar * cr - ai * ci_, 0.0)
    bi = bi + jnp.where(row0, ar * ci_ + ai * cr, 0.0)
    return _scan_down(br, bi, ar, ai)


def _s5_fwd(proj, a_rows, bbr, bbi, ccr, cci, d_row):
    rows = proj.shape[0]
    t = min(S5_TILE, rows)
    n = rows // t

    def body(u_ref, a_ref, bbr_ref, bbi_ref, ccr_ref, cci_ref, d_ref, y_ref, ck_ref, carry_ref):
        i = pl.program_id(0)

        @pl.when(i == 0)
        def _():
            carry_ref[...] = jnp.zeros_like(carry_ref)

        ck_ref[...] = carry_ref[...]
        for j in range(S5_BLK):
            lanes = pl.ds(j * S5_SW, S5_SW)
            ch = pl.ds(j * LANE, LANE)
            u_j = u_ref[:, ch]
            sr, si = _s5_states(u_j, bbr_ref[j], bbi_ref[j], a_ref[0:1, lanes], a_ref[1:2, lanes],
                                carry_ref[0:1, lanes], carry_ref[1:2, lanes])
            y_ref[:, ch] = dot_nn(sr, ccr_ref[j]) - dot_nn(si, cci_ref[j]) + d_ref[:, ch] * u_j
            carry_ref[0:1, lanes] = sr[t - 1:t, :]
            carry_ref[1:2, lanes] = si[t - 1:t, :]

    whole3 = lambda s: pl.BlockSpec(s, lambda i: (0, 0, 0))
    return pl.pallas_call(
        body, name="s5_fwd", grid=(n,),
        in_specs=[pl.BlockSpec((t, S5_WIDTH), lambda i: (i, U_SU // S5_BLK)),
                  pl.BlockSpec((2, S5_LANES), lambda i: (0, 0)),
                  whole3(bbr.shape), whole3(bbi.shape), whole3(ccr.shape), whole3(cci.shape),
                  pl.BlockSpec((1, S5_WIDTH), lambda i: (0, 0))],
        out_specs=[pl.BlockSpec((t, S5_WIDTH), lambda i: (i, 0)),
                   pl.BlockSpec((None, 2, S5_LANES), lambda i: (i, 0, 0))],
        out_shape=[jax.ShapeDtypeStruct((rows, S5_WIDTH), f32), jax.ShapeDtypeStruct((n, 2, S5_LANES), f32)],
        scratch_shapes=[pltpu.VMEM((2, S5_LANES), f32)],
        compiler_params=_cparams(("arbitrary",)))(proj, a_rows, bbr, bbi, ccr, cci, d_row)


def _s5_bwd(proj, dy, ck, a_rows, bbr, bbi, ccr, cci, d_row):
    rows = proj.shape[0]
    t = min(S5_TILE, rows)
    n = rows // t

    def body(u_ref, dy_ref, ck_ref, a_ref, bbr_ref, bbi_ref, ccr_ref, cci_ref, d_ref,
             du_ref, da_ref, dbbr_ref, dbbi_ref, dccr_ref, dcci_ref, dd_ref, lam_ref):
        i = pl.program_id(0)

        @pl.when(i == 0)
        def _():
            lam_ref[...] = jnp.zeros_like(lam_ref)
            for r in (da_ref, dbbr_ref, dbbi_ref, dccr_ref, dcci_ref, dd_ref):
                r[...] = jnp.zeros_like(r)

        for j in range(S5_BLK):
            lanes = pl.ds(j * S5_SW, S5_SW)
            ch = pl.ds(j * LANE, LANE)
            u_j = u_ref[:, ch]
            dy_j = dy_ref[:, ch]
            ar, ai = a_ref[0:1, lanes], a_ref[1:2, lanes]
            cr, ci_ = ck_ref[0:1, lanes], ck_ref[1:2, lanes]
            sr, si = _s5_states(u_j, bbr_ref[j], bbi_ref[j], ar, ai, cr, ci_)
            gr = dot_nt(dy_j, ccr_ref[j])
            gi = -dot_nt(dy_j, cci_ref[j])
            last = lax.broadcasted_iota(jnp.int32, gr.shape, 0) == t - 1
            lr0, li0 = lam_ref[0:1, lanes], lam_ref[1:2, lanes]
            gr = gr + jnp.where(last, ar * lr0 + ai * li0, 0.0)
            gi = gi + jnp.where(last, ar * li0 - ai * lr0, 0.0)
            lr, li = _scan_up(gr, gi, ar, -ai)
            lam_ref[0:1, lanes] = lr[0:1, :]
            lam_ref[1:2, lanes] = li[0:1, :]
            du_ref[:, ch] = (dot_nt(lr, bbr_ref[j]) + dot_nt(li, bbi_ref[j]) + d_ref[:, ch] * dy_j).astype(du_ref.dtype)
            dbbr_ref[j] += dot_tn(u_j, lr)
            dbbi_ref[j] += dot_tn(u_j, li)
            dccr_ref[j] += dot_tn(sr, dy_j)
            dcci_ref[j] += -dot_tn(si, dy_j)
            dd_ref[:, ch] += jnp.sum(dy_j * u_j, axis=0, keepdims=True)
            row0 = lax.broadcasted_iota(jnp.int32, sr.shape, 0) == 0
            pr = jnp.where(row0, cr, pltpu.roll(sr, 1, 0))
            pi = jnp.where(row0, ci_, pltpu.roll(si, 1, 0))
            da_ref[0:1, lanes] += jnp.sum(lr * pr + li * pi, axis=0, keepdims=True)
            da_ref[1:2, lanes] += jnp.sum(li * pr - lr * pi, axis=0, keepdims=True)

    whole3 = lambda s: pl.BlockSpec(s, lambda i: (0, 0, 0))
    whole2 = lambda s: pl.BlockSpec(s, lambda i: (0, 0))
    return pl.pallas_call(
        body, name="s5_bwd", grid=(n,),
        in_specs=[pl.BlockSpec((t, S5_WIDTH), lambda i: (n - 1 - i, U_SU // S5_BLK)),
                  pl.BlockSpec((t, S5_WIDTH), lambda i: (n - 1 - i, 0)),
                  pl.BlockSpec((None, 2, S5_LANES), lambda i: (n - 1 - i, 0, 0)),
                  whole2((2, S5_LANES)), whole3(bbr.shape), whole3(bbi.shape), whole3(ccr.shape), whole3(cci.shape),
                  whole2((1, S5_WIDTH))],
        out_specs=[pl.BlockSpec((t, S5_WIDTH), lambda i: (n - 1 - i, 0)), whole2((2, S5_LANES)),
                   whole3(bbr.shape), whole3(bbi.shape), whole3(ccr.shape), whole3(cci.shape), whole2((1, S5_WIDTH))],
        out_shape=[jax.ShapeDtypeStruct((rows, S5_WIDTH), bf16), jax.ShapeDtypeStruct((2, S5_LANES), f32),
                   jax.ShapeDtypeStruct(bbr.shape, f32), jax.ShapeDtypeStruct(bbi.shape, f32),
                   jax.ShapeDtypeStruct(ccr.shape, f32), jax.ShapeDtypeStruct(cci.shape, f32),
                   jax.ShapeDtypeStruct((1, S5_WIDTH), f32)],
        scratch_shapes=[pltpu.VMEM((2, S5_LANES), f32)],
        compiler_params=_cparams(("arbitrary",)))(proj, dy, ck, a_rows, bbr, bbi, ccr, cci, d_row)


def _s5_prep(lam_re, lam_im, log_step, b_re, b_im, c_re, c_im, d_skip):
    lam_re = jnp.minimum(lam_re, -1e-4)
    step = jnp.exp(log_step)[:, None]
    mag = jnp.exp(lam_re * step)
    ab_re = mag * jnp.cos(lam_im * step)
    ab_im = mag * jnp.sin(lam_im * step)
    den = lam_re * lam_re + lam_im * lam_im
    f_re = ((ab_re - 1.0) * lam_re + ab_im * lam_im) / den
    f_im = (ab_im * lam_re - (ab_re - 1.0) * lam_im) / den
    bb_re = f_re[..., None] * b_re - f_im[..., None] * b_im
    bb_im = f_re[..., None] * b_im + f_im[..., None] * b_re
    eye = jnp.eye(8, dtype=f32)

    def drive(bb):
        r = bb.reshape(S5_BLK, 8, S5_STATE, S5_GROUP_SIZE).transpose(0, 1, 3, 2)
        return (r[:, :, :, None, :] * eye[None, :, None, :, None]).reshape(S5_BLK, LANE, S5_SW)

    def readout(cc):
        r = cc.reshape(S5_BLK, 8, S5_GROUP_SIZE, S5_STATE).transpose(0, 1, 3, 2)
        return (r[:, :, :, None, :] * eye[None, :, None, :, None]).reshape(S5_BLK, S5_SW, LANE)

    a_rows = jnp.stack([ab_re.reshape(S5_LANES), ab_im.reshape(S5_LANES)])
    return a_rows, drive(bb_re), drive(bb_im), readout(c_re), readout(c_im), d_skip.reshape(1, S5_WIDTH)


def _adam_math(w, g, m, v):
    m = ADAM_B1 * m + (1.0 - ADAM_B1) * g
    v = ADAM_B2 * v + (1.0 - ADAM_B2) * (g * g)
    m_hat = m / (1.0 - ADAM_B1 ** ADAM_STEP)
    v_hat = v / (1.0 - ADAM_B2 ** ADAM_STEP)
    delta = -ADAM_LR * (m_hat / (jnp.sqrt(v_hat) + ADAM_EPS) + ADAM_WD * w)
    return delta, m, v


def _adamw(name, w, g, m, v):
    rows, width = w.shape
    t = rows
    for cand in (512, 256, 128, 64, 32, 16, 8):
        if rows % cand == 0 and cand * width * 4 * 7 * 2 <= VMEM_LIMIT // 2:
            t = cand
            break

    def body(w_ref, g_ref, m_ref, v_ref, d_ref, nm_ref, nv_ref):
        d, nm, nv = _adam_math(w_ref[...], g_ref[...], m_ref[...], v_ref[...])
        d_ref[...] = d
        nm_ref[...] = nm
        nv_ref[...] = nv

    spec = pl.BlockSpec((t, width), lambda i: (i, 0))
    shape = jax.ShapeDtypeStruct((rows, width), f32)
    return pl.pallas_call(body, name=name, grid=(rows // t,), in_specs=[spec] * 4, out_specs=[spec] * 3,
                          out_shape=[shape] * 3, compiler_params=_cparams(("parallel",)))(w, g, m, v)


def _adamw_slots(name, recvs, row_off, w, m, v):
    depth, rows, width = w.shape
    per_row = width * (depth * N_DEV * recvs[0].dtype.itemsize + 7 * 4) * 2
    t = next(cand for cand in (256, 128, 96, 64, 32, 16)
             if rows % cand == 0 and row_off % cand == 0 and cand * per_row <= VMEM_LIMIT * 2 // 3)
    first = row_off // t

    def body(*refs):
        r_refs = refs[:depth]
        w_ref, m_ref, v_ref, g_ref, d_ref, nm_ref, nv_ref = refs[depth:]
        layer = pl.program_id(0)
        for l, r_ref in enumerate(r_refs):
            @pl.when(layer == l)
            def _():
                g = r_ref[0].astype(f32)
                for s in range(1, N_DEV):
                    g = g + r_ref[s].astype(f32)
                d, nm, nv = _adam_math(w_ref[...], g, m_ref[...], v_ref[...])
                g_ref[...] = g
                d_ref[...] = d
                nm_ref[...] = nm
                nv_ref[...] = nv

    def recv_spec(l):
        return pl.BlockSpec((N_DEV, t, width), lambda layer, i: (0, first + jnp.where(layer == l, i, 0), 0))

    spec = pl.BlockSpec((None, t, width), lambda layer, i: (layer, i, 0))
    shape = jax.ShapeDtypeStruct((depth, rows, width), f32)
    return pl.pallas_call(
        body, name=name, grid=(depth, rows // t), in_specs=[recv_spec(l) for l in range(depth)] + [spec, spec, spec],
        out_specs=[spec] * 4, out_shape=[shape] * 4,
        compiler_params=_cparams(("arbitrary", "arbitrary")))(*recvs, w, m, v)


def _sum_slots(name, buf):
    _, rows, width = buf.shape
    t = next(cand for cand in (512, 256, 128, 64, 32, 16) if rows % cand == 0)

    def body(b_ref, o_ref):
        acc = b_ref[0].astype(f32)
        for s in range(1, N_DEV):
            acc = acc + b_ref[s].astype(f32)
        o_ref[...] = acc

    return pl.pallas_call(
        body, name=name, grid=(rows // t,), in_specs=[pl.BlockSpec((N_DEV, t, width), lambda i: (0, i, 0))],
        out_specs=pl.BlockSpec((t, width), lambda i: (i, 0)), out_shape=jax.ShapeDtypeStruct((rows, width), f32),
        compiler_params=_cparams(("parallel",)))(buf)


BIG = ("w_in", "s5_glu_w", "proj_a", "proj_b", "proj_c", "w_out")
CONV = ("gdn_conv_w", "m2_conv_w")
SMALL = ("norm_w", "gdn_a_log", "gdn_dt_bias", "gdn_norm_w", "s5_lam_re", "s5_lam_im", "s5_log_step",
         "s5_b_re", "s5_b_im", "s5_c_re", "s5_c_im", "s5_d", "s5_glu_b", "m2_conv_b", "m2_a_log", "m2_dt_bias",
         "m2_d", "m2_norm_w", "final_norm_w")
WEIGHTS = ("norm_w", "w_in", "gdn_conv_w", "gdn_a_log", "gdn_dt_bias", "gdn_norm_w", "s5_lam_re", "s5_lam_im",
           "s5_log_step", "s5_b_re", "s5_b_im", "s5_c_re", "s5_c_im", "s5_d", "s5_glu_w", "s5_glu_b", "m2_conv_w",
           "m2_conv_b", "m2_a_log", "m2_dt_bias", "m2_d", "m2_norm_w", "proj_a", "proj_b", "proj_c", "w_out",
           "final_norm_w")


PACK_ROWS = 8
PACK_TILE = 256


def _piece_rows(shape):
    size = 1
    for d in shape:
        size *= d
    rows = -(-size // LANE)
    return size, -(-rows // PACK_ROWS) * PACK_ROWS


def _pack(arrays):
    pieces = []
    for a in arrays:
        size, rows = _piece_rows(a.shape)
        flat = a.reshape(-1)
        if size != rows * LANE:
            flat = jnp.concatenate([flat, jnp.zeros((rows * LANE - size,), f32)])
        pieces.append(flat.reshape(rows, LANE))
    total = sum(p.shape[0] for p in pieces)
    tail = -total % PACK_TILE
    if tail:
        pieces.append(jnp.zeros((tail, LANE), f32))
    return jnp.concatenate(pieces, axis=0)


def _unpack(buf, shapes):
    out, off = [], 0
    for s in shapes:
        size, rows = _piece_rows(s)
        piece = buf[off:off + rows]
        out.append(piece.reshape(s) if size == rows * LANE else piece.reshape(-1)[:size].reshape(s))
        off += rows
    return out


def _win_to_padded(w):
    d = w.shape[0]
    z = lambda n: jnp.zeros((d, n), w.dtype)
    return jnp.concatenate([w[:, 8736:14880], w[:, 4112:5648], w[:, 0:4096], w[:, 5648:8720],
                            w[:, 4096:4112], z(LANE - 16), w[:, 8720:8736], z(LANE - 16), z(2 * LANE)], axis=1)


def _win_from_padded(g):
    u = LANE
    return jnp.concatenate([g[:, U_Q * u:U_CZ * u], g[:, U_SMA * u:U_SMA * u + 16], g[:, U_SU * u:U_Q * u],
                            g[:, U_CZ * u:U_SMA * u], g[:, U_SMC * u:U_SMC * u + 16], g[:, 0:U_SU * u]], axis=1)


def _lane_row(vals, offset):
    n = vals.shape[0]
    return jnp.concatenate([jnp.zeros((offset,), f32), vals, jnp.zeros((LANE - offset - n,), f32)]).reshape(1, LANE)


def _layer_fwd(x, lw, next_shards):
    h = _tok_fwd("rms_fwd", _rms_fn, [(x, D_MODEL, 0)], [lw["norm_w"].reshape(1, D_MODEL)], [(D_MODEL, bf16)], 256)[0]
    if next_shards is None:
        proj, gathered = _mm("proj_fwd", h, lw["w_in_p"], "nn", tk=D_MODEL), None
    else:
        proj, *gathered = _mm("proj_fwd_gather", h, lw["w_in_p"], "nn", tk=D_MODEL, comm=(next_shards, True))
    g_al, g_dt = _lane_row(lw["gdn_a_log"], GDN_HEADS), _lane_row(lw["gdn_dt_bias"], GDN_HEADS)
    g_nw = lw["gdn_norm_w"].reshape(1, HEAD_DIM)
    y_a, ck_a = _gdn_fwd(proj, lw["gdn_conv_w"], g_al, g_dt, g_nw)
    s5p = _s5_prep(lw["s5_lam_re"], lw["s5_lam_im"], lw["s5_log_step"], lw["s5_b_re"], lw["s5_b_im"],
                   lw["s5_c_re"], lw["s5_c_im"], lw["s5_d"])
    y_pre, ck_b = _s5_fwd(proj, *s5p)
    glu_b = lw["s5_glu_b"].reshape(1, S5_WIDTH)
    y_b = _tok_fwd("s5_tail_fwd", _s5_tail_fn, [(y_pre, S5_WIDTH, 0), (proj, S5_WIDTH, U_SG // S5_BLK)],
                   [lw["s5_glu_w"], glu_b], [(S5_WIDTH, bf16)], 256)[0]
    m_al, m_dt, m_d = _lane_row(lw["m2_a_log"], 0), _lane_row(lw["m2_dt_bias"], 0), _lane_row(lw["m2_d"], 0)
    m_cb = lw["m2_conv_b"].reshape(1, -1)
    m_nw = lw["m2_norm_w"].reshape(1, M2_WIDTH)
    y_c, ck_c = _ssd_fwd(proj, lw["m2_conv_w"], m_cb, m_al, m_dt, m_d, m_nw)
    pa = _mm("proj_a_fwd", y_a, lw["proj_a"], "nn")
    pb = _mm("proj_b_fwd", y_b, lw["proj_b"], "nn")
    pc = _mm("proj_c_fwd", y_c, lw["proj_c"], "nn")
    gate_acts = [(proj, D_MODEL, 0), (proj, D_MODEL, 1), (proj, D_MODEL, 2),
                 (pa, D_MODEL, 0), (pb, D_MODEL, 0), (pc, D_MODEL, 0)]
    merged = _tok_fwd("gate_fwd", _gate_fn, gate_acts, [], [(D_MODEL, bf16)], 128)[0]
    x_next = _mm("w_out_fwd", merged, lw["w_out"], "nn", residual=x)
    saved = dict(x=x, h=h, proj=proj, y_a=y_a, ck_a=ck_a, y_pre=y_pre, ck_b=ck_b, y_b=y_b, y_c=y_c, ck_c=ck_c,
                 pa=pa, pb=pb, pc=pc, merged=merged)
    return x_next, saved, gathered


def _layer_bwd(dx_next, lw, sv, later_grads, pack_own=None):
    rows = dx_next.shape[0]
    proj = sv["proj"]
    g = {}
    d_merged = _mm("w_out_bwd_x", dx_next, lw["w_out"], "nt", out_dtype=bf16)
    g["w_out"] = _mm("w_out_bwd_w", sv["merged"], dx_next, "tn", out_dtype=bf16)
    gate_acts = [(proj, D_MODEL, 0), (proj, D_MODEL, 1), (proj, D_MODEL, 2),
                 (sv["pa"], D_MODEL, 0), (sv["pb"], D_MODEL, 0), (sv["pc"], D_MODEL, 0)]
    dla, dlb, dlc, dpa, dpb, dpc = _tok_bwd(
        "gate_bwd", _gate_fn, gate_acts, [], [(d_merged, D_MODEL, 0)],
        [(0, bf16), (1, bf16), (2, bf16), (3, bf16), (4, bf16), (5, bf16)], 128)
    dy_a = _mm("proj_a_bwd_x", dpa, lw["proj_a"], "nt", out_dtype=bf16)
    dy_b = _mm("proj_b_bwd_x", dpb, lw["proj_b"], "nt")
    dy_c = _mm("proj_c_bwd_x", dpc, lw["proj_c"], "nt", out_dtype=bf16)
    g["proj_a"] = _mm("proj_a_bwd_w", sv["y_a"], dpa, "tn", out_dtype=bf16)
    g["proj_b"] = _mm("proj_b_bwd_w", sv["y_b"], dpb, "tn", out_dtype=bf16)
    g["proj_c"] = _mm("proj_c_bwd_w", sv["y_c"], dpc, "tn", out_dtype=bf16)

    m_al, m_dt, m_d = _lane_row(lw["m2_a_log"], 0), _lane_row(lw["m2_dt_bias"], 0), _lane_row(lw["m2_d"], 0)
    m_cb = lw["m2_conv_b"].reshape(1, -1)
    m_nw = lw["m2_norm_w"].reshape(1, M2_WIDTH)
    (dcx, dcb, dcc, dcz, dsmc, dcwx, dcwb, dcwc, dbx, dbb, dbc, dal, ddt, ddk, dnw) = _ssd_bwd(
        proj, dy_c, sv["ck_c"], lw["m2_conv_w"], m_cb, m_al, m_dt, m_d, m_nw)
    g["m2_conv_w"] = jnp.concatenate([dcwx, dcwb, dcwc], axis=1)
    g["m2_conv_b"] = jnp.concatenate([dbx, dbb, dbc], axis=1).reshape(-1)
    g["m2_a_log"] = jnp.sum(dal, axis=(0, 1))[:M2_HEADS]
    g["m2_dt_bias"] = jnp.sum(ddt, axis=(0, 1))[:M2_HEADS]
    g["m2_d"] = jnp.sum(ddk, axis=(0, 1))[:M2_HEADS]
    g["m2_norm_w"] = dnw.reshape(-1)
    dsmc = jnp.sum(dsmc, axis=0)

    glu_b = lw["s5_glu_b"].reshape(1, S5_WIDTH)
    dypre, dsg, dglu_w, dglu_b = _tok_bwd(
        "s5_tail_bwd", _s5_tail_fn, [(sv["y_pre"], S5_WIDTH, 0), (proj, S5_WIDTH, U_SG // S5_BLK)],
        [lw["s5_glu_w"], glu_b], [(dy_b, S5_WIDTH, 0)], [(0, f32), (1, bf16)], 256)
    g["s5_glu_w"] = dglu_w
    g["s5_glu_b"] = dglu_b.reshape(-1)
    s5_names = ("s5_lam_re", "s5_lam_im", "s5_log_step", "s5_b_re", "s5_b_im", "s5_c_re", "s5_c_im", "s5_d")
    s5p, s5_vjp = jax.vjp(_s5_prep, *[lw[k] for k in s5_names])
    dsu, da, dbbr, dbbi, dccr, dcci, dd = _s5_bwd(proj, dypre, sv["ck_b"], *s5p)
    for k, val in zip(s5_names, s5_vjp((da, dbbr, dbbi, dccr, dcci, dd))):
        g[k] = val

    g_al, g_dt = _lane_row(lw["gdn_a_log"], GDN_HEADS), _lane_row(lw["gdn_dt_bias"], GDN_HEADS)
    g_nw = lw["gdn_norm_w"].reshape(1, HEAD_DIM)
    (dq, dk, dv, daz, dsma, dcwq, dcwk, dcwv, dgal, dgdt, dgnw, *arrived) = _gdn_bwd(
        proj, dy_a, sv["ck_a"], lw["gdn_conv_w"], g_al, g_dt, g_nw, scatter=later_grads)
    g["gdn_conv_w"] = jnp.concatenate([dcwq, dcwk, dcwv], axis=1)
    g["gdn_a_log"] = jnp.sum(dgal, axis=(0, 1))[GDN_HEADS:2 * GDN_HEADS]
    g["gdn_dt_bias"] = jnp.sum(dgdt, axis=(0, 1))[GDN_HEADS:2 * GDN_HEADS]
    g["gdn_norm_w"] = jnp.sum(dgnw, axis=(0, 1))
    dsma = jnp.sum(dsma, axis=0)

    dproj = jnp.concatenate([dla, dlb, dlc, dsu, dsg, dq, dk, dv, daz, dcz, dcx, dcb, dcc, dsma.astype(bf16),
                             dsmc.astype(bf16), jnp.zeros((rows, 2 * LANE), bf16)], axis=1)
    g["w_in_p"] = _mm("proj_bwd_w", sv["h"], dproj, "tn", out_dtype=bf16)
    if pack_own is None:
        dh, own_arrived = _mm("proj_bwd_x", dproj, lw["w_in_p"], "nt"), None
    else:
        dh, *own_arrived = _mm("proj_bwd_x_scatter", dproj, lw["w_in_p"], "nt", comm=(pack_own(g), False))
    dx, dnorm = _tok_bwd("rms_bwd", _rms_fn, [(sv["x"], D_MODEL, 0)], [lw["norm_w"].reshape(1, D_MODEL)],
                         [(dh, D_MODEL, 0)], [(0, f32)], 256, residuals={0: dx_next})
    g["norm_w"] = dnorm.reshape(-1)
    return dx, g, (arrived if later_grads is not None else None), own_arrived


def kernel(x, norm_w, w_in, gdn_conv_w, gdn_a_log, gdn_dt_bias, gdn_norm_w, s5_lam_re, s5_lam_im, s5_log_step, s5_b_re, s5_b_im, s5_c_re, s5_c_im, s5_d, s5_glu_w, s5_glu_b, m2_conv_w, m2_conv_b, m2_a_log, m2_dt_bias, m2_d, m2_norm_w, proj_a, proj_b, proj_c, w_out, final_norm_w, loss_target, m_norm_w, m_w_in, m_gdn_conv_w, m_gdn_a_log, m_gdn_dt_bias, m_gdn_norm_w, m_s5_lam_re, m_s5_lam_im, m_s5_log_step, m_s5_b_re, m_s5_b_im, m_s5_c_re, m_s5_c_im, m_s5_d, m_s5_glu_w, m_s5_glu_b, m_m2_conv_w, m_m2_conv_b, m_m2_a_log, m_m2_dt_bias, m_m2_d, m_m2_norm_w, m_proj_a, m_proj_b, m_proj_c, m_w_out, m_final_norm_w, v_norm_w, v_w_in, v_gdn_conv_w, v_gdn_a_log, v_gdn_dt_bias, v_gdn_norm_w, v_s5_lam_re, v_s5_lam_im, v_s5_log_step, v_s5_b_re, v_s5_b_im, v_s5_c_re, v_s5_c_im, v_s5_d, v_s5_glu_w, v_s5_glu_b, v_m2_conv_w, v_m2_conv_b, v_m2_a_log, v_m2_dt_bias, v_m2_d, v_m2_norm_w, v_proj_a, v_proj_b, v_proj_c, v_w_out, v_final_norm_w):
    w = dict(norm_w=norm_w, w_in=w_in, gdn_conv_w=gdn_conv_w, gdn_a_log=gdn_a_log, gdn_dt_bias=gdn_dt_bias,
             gdn_norm_w=gdn_norm_w, s5_lam_re=s5_lam_re, s5_lam_im=s5_lam_im, s5_log_step=s5_log_step,
             s5_b_re=s5_b_re, s5_b_im=s5_b_im, s5_c_re=s5_c_re, s5_c_im=s5_c_im, s5_d=s5_d, s5_glu_w=s5_glu_w,
             s5_glu_b=s5_glu_b, m2_conv_w=m2_conv_w, m2_conv_b=m2_conv_b, m2_a_log=m2_a_log, m2_dt_bias=m2_dt_bias,
             m2_d=m2_d, m2_norm_w=m2_norm_w, proj_a=proj_a, proj_b=proj_b, proj_c=proj_c, w_out=w_out,
             final_norm_w=final_norm_w)
    mom = dict(norm_w=m_norm_w, w_in=m_w_in, gdn_conv_w=m_gdn_conv_w, gdn_a_log=m_gdn_a_log,
               gdn_dt_bias=m_gdn_dt_bias, gdn_norm_w=m_gdn_norm_w, s5_lam_re=m_s5_lam_re, s5_lam_im=m_s5_lam_im,
               s5_log_step=m_s5_log_step, s5_b_re=m_s5_b_re, s5_b_im=m_s5_b_im, s5_c_re=m_s5_c_re,
               s5_c_im=m_s5_c_im, s5_d=m_s5_d, s5_glu_w=m_s5_glu_w, s5_glu_b=m_s5_glu_b, m2_conv_w=m_m2_conv_w,
               m2_conv_b=m_m2_conv_b, m2_a_log=m_m2_a_log, m2_dt_bias=m_m2_dt_bias, m2_d=m_m2_d,
               m2_norm_w=m_m2_norm_w, proj_a=m_proj_a, proj_b=m_proj_b, proj_c=m_proj_c, w_out=m_w_out,
               final_norm_w=m_final_norm_w)
    var = dict(norm_w=v_norm_w, w_in=v_w_in, gdn_conv_w=v_gdn_conv_w, gdn_a_log=v_gdn_a_log,
               gdn_dt_bias=v_gdn_dt_bias, gdn_norm_w=v_gdn_norm_w, s5_lam_re=v_s5_lam_re, s5_lam_im=v_s5_lam_im,
               s5_log_step=v_s5_log_step, s5_b_re=v_s5_b_re, s5_b_im=v_s5_b_im, s5_c_re=v_s5_c_re,
               s5_c_im=v_s5_c_im, s5_d=v_s5_d, s5_glu_w=v_s5_glu_w, s5_glu_b=v_s5_glu_b, m2_conv_w=v_m2_conv_w,
               m2_conv_b=v_m2_conv_b, m2_a_log=v_m2_a_log, m2_dt_bias=v_m2_dt_bias, m2_d=v_m2_d,
               m2_norm_w=v_m2_norm_w, proj_a=v_proj_a, proj_b=v_proj_b, proj_c=v_proj_c, w_out=v_w_out,
               final_norm_w=v_final_norm_w)
    me = lax.axis_index("x") * 4 + lax.axis_index("y") * 2 + lax.axis_index("c")
    x2 = x[0]
    tgt = loss_target[0]

    ra, rb = proj_a.shape[1], proj_b.shape[1]
    pabc = jnp.concatenate([proj_a, proj_b, proj_c], axis=1).astype(bf16)
    w_in16, glu16, w_out16 = w_in.astype(bf16), s5_glu_w.astype(bf16), w_out.astype(bf16)

    def shards(i):
        return [w_in16[i], glu16[i], pabc[i], w_out16[i], gdn_conv_w[i], m2_conv_w[i]]

    def cols(gathered):
        return jnp.concatenate([gathered[d] for d in range(N_DEV)], axis=1)

    def rows_of(gathered):
        return gathered.reshape(-1, gathered.shape[-1])

    def full_weights(i, gathered):
        g_win, g_glu, g_pabc, g_wout, g_gcv, g_mcv = gathered
        pf = cols(g_pabc)
        lw = dict(w_in_p=_win_to_padded(cols(g_win)), s5_glu_w=rows_of(g_glu), proj_a=pf[:ra],
                  proj_b=pf[ra:ra + rb], proj_c=pf[ra + rb:], w_out=rows_of(g_wout),
                  gdn_conv_w=cols(g_gcv), m2_conv_w=cols(g_mcv))
        for k in SMALL:
            if k != "final_norm_w":
                lw[k] = w[k][i]
        return lw

    layers, saved = [], []
    act = x2
    gathered = _exchange("gather_weights", shards(0), True)
    for i in range(DEPTH):
        layers.append(full_weights(i, gathered))
        act, sv, gathered = _layer_fwd(act, layers[i], shards(i + 1) if i + 1 < DEPTH else None)
        saved.append(sv)
    loss_row, dact, dfinal = _loss_grad(act, final_norm_w.reshape(1, D_MODEL), tgt)
    loss = lax.psum(loss_row[0, 0], ("x", "y", "c"))

    def col_blocks(full):
        r = full.shape[0]
        return full.reshape(r, N_DEV, -1).transpose(1, 0, 2)

    def row_blocks(full):
        return full.reshape(N_DEV, -1, full.shape[1])

    def packed_grads(g):
        return [col_blocks(_win_from_padded(g["w_in_p"])), row_blocks(g["s5_glu_w"].astype(bf16)),
                col_blocks(jnp.concatenate([g["proj_a"], g["proj_b"], g["proj_c"]], axis=0)), row_blocks(g["w_out"])]

    grads, arrived = [None] * DEPTH, [None] * DEPTH
    later = None
    for i in reversed(range(DEPTH)):
        dact, grads[i], got, own = _layer_bwd(dact, layers[i], saved[i], later, packed_grads if i == 0 else None)
        if got is not None:
            arrived[i + 1] = got
        saved[i] = None
        if i > 0:
            later = packed_grads(grads[i])
    arrived[0] = own

    g_out, delta, new_m, new_v = {}, {}, {}, {}
    for k, buf, row_off in (("w_in", 0, 0), ("s5_glu_w", 1, 0), ("proj_a", 2, 0), ("proj_b", 2, ra),
                            ("proj_c", 2, ra + rb), ("w_out", 3, 0)):
        recvs = [arrived[i][buf] for i in range(DEPTH)]
        g_out[k], delta[k], new_m[k], new_v[k] = _adamw_slots("adamw_" + k, recvs, row_off, w[k], mom[k], var[k])

    names_small = [k for k in SMALL if k != "final_norm_w"]
    small_parts = [jnp.stack([grads[i][k] for i in range(DEPTH)]) for k in names_small + list(CONV)]
    small_parts.append(dfinal.reshape(-1))
    small_shapes = [p.shape for p in small_parts]
    small_local = _pack(small_parts)
    small_sum = _sum_slots("sum_small", _exchange("gather_small", [small_local], True)[0])
    unp = _unpack(small_sum, small_shapes)
    for k, val in zip(names_small, unp):
        g_out[k] = val
    for j, k in enumerate(CONV):
        width = w[k].shape[2]
        g_out[k] = lax.dynamic_slice_in_dim(unp[len(names_small) + j], me * width, width, axis=2)
    g_out["final_norm_w"] = unp[-1]

    rest = [k for k in WEIGHTS if k not in BIG]
    rest_shapes = [w[k].shape for k in rest]
    packed = [_pack([src[k] for k in rest]) for src in (w, g_out, mom, var)]
    d, nm, nv = _adamw("adamw_small", *packed)
    for dst, arr in ((delta, d), (new_m, nm), (new_v, nv)):
        for k, val in zip(rest, _unpack(arr, rest_shapes)):
            dst[k] = val

    grad_x = dact.reshape(x.shape)
    return (loss, grad_x, *[g_out[k] for k in WEIGHTS], *[delta[k] for k in WEIGHTS],
            *[new_m[k] for k in WEIGHTS], *[new_v[k] for k in WEIGHTS])
```

```python
import functools

import jax
import jax.numpy as jnp
from jax import lax
from jax.experimental import pallas as pl
from jax.experimental.pallas import tpu as pltpu

f32 = jnp.float32
bf16 = jnp.bfloat16

N_DEV = 8
DEPTH = 4
D_MODEL = 2048
GDN_HEADS = 8
HEAD_DIM = 128
GDN_WIDTH = 1024
S5_GROUPS = 48
S5_GROUP_SIZE = 16
S5_STATE = 64
S5_WIDTH = 768
S5_LANES = S5_GROUPS * S5_STATE
S5_BLK = 6
M2_HEADS = 16
M2_HEAD_DIM = 64
M2_WIDTH = 1024
M2_GROUPS = 4
M2_STATE = 128
CONV_K = 4
CHUNK = 64
HALO = 8
NORM_EPS = 1e-6
IN_DIM = 14880
LANE = 128
VMEM_LIMIT = 48 * 1024 * 1024

ADAM_LR = 0.001
ADAM_B1 = 0.9
ADAM_B2 = 0.999
ADAM_EPS = 1e-08
ADAM_WD = 0.01
ADAM_STEP = 10

U_MERGE, U_SU, U_SG, U_Q, U_K, U_V, U_AZ, U_CZ, U_CX, U_CB, U_CC, U_SMA, U_SMC = (
    0, 48, 54, 60, 68, 76, 84, 92, 100, 108, 112, 116, 117)
NP_UNITS = 120
NP_COLS = NP_UNITS * LANE


def _cparams(sem=None):
    return pltpu.CompilerParams(dimension_semantics=sem, vmem_limit_bytes=VMEM_LIMIT)


def _pick(dim, target):
    if dim <= target:
        return dim
    for t in range(target - target % LANE, 0, -LANE):
        if dim % t == 0:
            return t
    raise ValueError(f"no tile for {dim}")


def _bd(a, b, dims):
    return lax.dot_general(a.astype(bf16), b.astype(bf16), (dims, ((), ())), preferred_element_type=f32)


@jax.custom_vjp
def dot_nn(a, b):
    return _bd(a, b, ((1,), (0,)))


@jax.custom_vjp
def dot_nt(a, b):
    return _bd(a, b, ((1,), (1,)))


@jax.custom_vjp
def dot_tn(a, b):
    return _bd(a, b, ((0,), (0,)))


dot_nn.defvjp(lambda a, b: (dot_nn(a, b), (a, b)), lambda r, ct: (dot_nt(ct, r[1]), dot_tn(r[0], ct)))
dot_nt.defvjp(lambda a, b: (dot_nt(a, b), (a, b)), lambda r, ct: (dot_nn(ct, r[1]), dot_tn(ct, r[0])))
dot_tn.defvjp(lambda a, b: (dot_tn(a, b), (a, b)), lambda r, ct: (dot_nt(r[1], ct), dot_nn(r[0], ct)))

_HI = lax.Precision.HIGHEST


def _silu(x):
    return x * jax.nn.sigmoid(x)


def _softplus(x):
    return jnp.maximum(x, 0.0) + jnp.log(1.0 + jnp.exp(-jnp.abs(x)))


def _tri_masks(c):
    row = lax.broadcasted_iota(jnp.int32, (c, c), 0)
    col = lax.broadcasted_iota(jnp.int32, (c, c), 1)
    return row >= col, row > col, (row >= col).astype(f32), (row == col).astype(f32)


def _scan_add(x, reverse):
    t = x.shape[0]
    row = lax.broadcasted_iota(jnp.int32, x.shape, 0)
    d = 1
    while d < t:
        if reverse:
            x = x + jnp.where(row < t - d, pltpu.roll(x, t - d, 0), 0.0)
        else:
            x = x + jnp.where(row >= d, pltpu.roll(x, d, 0), 0.0)
        d *= 2
    return x


@jax.custom_vjp
def _cumsum_rows(x):
    return _scan_add(x, False)


_cumsum_rows.defvjp(lambda x: (_scan_add(x, False), None), lambda _, ct: (_scan_add(ct, True),))


def _cumsum_all(a):
    cum = _cumsum_rows(a)
    return cum, cum.T


def _onehot_lane(idx):
    return (lax.broadcasted_iota(jnp.int32, (1, LANE), 1) == idx).astype(f32)


def _onehot_sub(idx):
    return (lax.broadcasted_iota(jnp.int32, (LANE, 1), 0) == idx).astype(f32)


def _pick_col(x, idx):
    return jnp.sum(x * _onehot_lane(idx), axis=-1, keepdims=True)


def _pick_row(xt, idx):
    return jnp.sum(xt * _onehot_sub(idx), axis=0, keepdims=True)


def _peer(idx):
    return (idx // 4, (idx // 2) % 2, idx % 2)


def _remote(src, dst, send_sems, recv_sems, k, to):
    return pltpu.make_async_remote_copy(src_ref=src, dst_ref=dst, send_sem=send_sems.at[k], recv_sem=recv_sems.at[k],
                                        device_id=_peer(to), device_id_type=pl.DeviceIdType.MESH)


def _scatter_phase(phase, srcs, outs, send_sems, recv_sems, local_sems):
    nb = len(srcs)
    me = lax.axis_index("x") * 4 + lax.axis_index("y") * 2 + lax.axis_index("c")
    if phase == "forward":
        return
    mine = [pltpu.make_async_copy(srcs[b].at[me], outs[b].at[me], local_sems.at[b]) for b in range(nb)]
    sends = [_remote(srcs[b].at[(me + k) % N_DEV], outs[b].at[me], send_sems, recv_sems, (k - 1) * nb + b,
                     (me + k) % N_DEV) for k in range(1, N_DEV) for b in range(nb)]
    if phase == "start":
        for cp in mine + sends:
            cp.start()
    if phase == "finish":
        for k in range(1, N_DEV):
            frm = (me + N_DEV - k) % N_DEV
            for b in range(nb):
                _remote(srcs[b].at[me], outs[b].at[frm], send_sems, recv_sems, (k - 1) * nb + b, frm).wait_recv()
        for cp in sends:
            cp.wait_send()
        for cp in mine:
            cp.wait()


def _gather_phase(phase, srcs, outs, send_sems, recv_sems, local_sems):
    nb = len(srcs)
    x, y, c = lax.axis_index("x"), lax.axis_index("y"), lax.axis_index("c")
    me, sib = 4 * x + 2 * y + c, 4 * x + 2 * y + 1 - c
    chips = [(1 - x, y), (x, 1 - y), (1 - x, 1 - y)]
    same = [4 * cx + 2 * cy + c for cx, cy in chips]
    other = [4 * cx + 2 * cy + 1 - c for cx, cy in chips]

    def copy(b, k, src, block, to):
        return _remote(src, outs[b].at[block], send_sems, recv_sems, k * nb + b, to)

    def local():
        return [pltpu.make_async_copy(srcs[b], outs[b].at[me], local_sems.at[b]) for b in range(nb)]

    def first():
        return ([copy(b, 0, srcs[b], me, sib) for b in range(nb)]
                + [copy(b, 1 + j, srcs[b], me, same[j]) for j in range(3) for b in range(nb)])

    def passed(j, b):
        return copy(b, 4 + j, outs[b].at[same[j]], same[j], sib)

    if phase == "start":
        for cp in local() + first():
            cp.start()
    if phase == "forward":
        for j in range(3):
            for b in range(nb):
                copy(b, 1 + j, srcs[b], same[j], me).wait_recv()
                passed(j, b).start()
    if phase == "finish":
        for b in range(nb):
            copy(b, 0, srcs[b], sib, me).wait_recv()
        for j in range(3):
            for b in range(nb):
                copy(b, 4 + j, srcs[b], other[j], me).wait_recv()
        for cp in first() + [passed(j, b) for j in range(3) for b in range(nb)]:
            cp.wait_send()
        for cp in local():
            cp.wait()


def _comm_phase(phase, *refs, gather):
    (_gather_phase if gather else _scatter_phase)(phase, *refs)


def _comm_shapes(bufs, gather):
    nb = len(bufs)
    out_shape = [jax.ShapeDtypeStruct(((N_DEV,) + tuple(b.shape)) if gather else tuple(b.shape), b.dtype) for b in bufs]
    sems = [pltpu.SemaphoreType.DMA(((N_DEV - 1) * nb,)), pltpu.SemaphoreType.DMA(((N_DEV - 1) * nb,)),
            pltpu.SemaphoreType.DMA((nb,))]
    return out_shape, sems


def _exchange(name, bufs, gather):
    nb = len(bufs)

    def body(*refs):
        args = (refs[:nb], refs[nb:2 * nb], *refs[2 * nb:])
        for phase in ("start", "forward", "finish"):
            _comm_phase(phase, *args, gather=gather)

    any_spec = pl.BlockSpec(memory_space=pl.ANY)
    out_shape, sems = _comm_shapes(bufs, gather)
    return pl.pallas_call(body, name=name, in_specs=[any_spec] * nb, out_specs=[any_spec] * nb, out_shape=out_shape,
                          scratch_shapes=sems)(*bufs)


def _mm(name, a, b, mode, residual=None, out_dtype=f32, tm=1024, tn=1024, tk=1024, comm=None):
    if mode == "nn":
        (m, k), (_, n) = a.shape, b.shape
    elif mode == "nt":
        (m, k), (n, _) = a.shape, b.shape
    else:
        (k, m), (_, n) = a.shape, b.shape
    tm, tn, tk = _pick(m, tm), _pick(n, tn), _pick(k, tk)
    gm, gn, nk = m // tm, n // tn, k // tk
    dims = {"nn": ((1,), (0,)), "nt": ((1,), (1,)), "tn": ((0,), (0,))}[mode]
    has_res = residual is not None
    bufs, gather = comm if comm is not None else ([], True)
    nb = len(bufs)
    n_in = 2 + has_res
    forward_step = gm * gn * nk * 3 // 4

    def body(*refs):
        a_ref, b_ref = refs[:2]
        r_ref = refs[2] if has_res else None
        srcs = refs[n_in:n_in + nb]
        o_ref = refs[n_in + nb]
        outs = refs[n_in + nb + 1:n_in + 2 * nb + 1]
        scratch = refs[n_in + 2 * nb + 1:]
        acc_ref = scratch[0] if nk > 1 else None
        sems = scratch[1:] if nk > 1 else scratch
        i, j, kk = pl.program_id(0), pl.program_id(1), pl.program_id(2)

        step = (i * gn + j) * nk + kk
        if nb:
            @pl.when(step == 0)
            def _():
                _comm_phase("start", srcs, outs, *sems, gather=gather)

            @pl.when(step == forward_step)
            def _():
                _comm_phase("forward", srcs, outs, *sems, gather=gather)

        def finish(out):
            if has_res:
                out = out + r_ref[...].astype(f32)
            o_ref[...] = out.astype(o_ref.dtype)

        if nk == 1:
            finish(_bd(a_ref[...], b_ref[...], dims))
        else:
            @pl.when(kk == 0)
            def _():
                acc_ref[...] = jnp.zeros_like(acc_ref)

            acc_ref[...] += _bd(a_ref[...], b_ref[...], dims)

            @pl.when(kk == nk - 1)
            def _():
                finish(acc_ref[...])

        if nb:
            @pl.when(step == gm * gn * nk - 1)
            def _():
                _comm_phase("finish", srcs, outs, *sems, gather=gather)

    if mode == "tn":
        a_spec = pl.BlockSpec((tk, tm), lambda i, j, kk: (kk, i))
    else:
        a_spec = pl.BlockSpec((tm, tk), lambda i, j, kk: (i, kk))
    if mode == "nt":
        b_spec = pl.BlockSpec((tn, tk), lambda i, j, kk: (j, kk))
    else:
        b_spec = pl.BlockSpec((tk, tn), lambda i, j, kk: (kk, j))
    o_spec = pl.BlockSpec((tm, tn), lambda i, j, kk: (i, j))
    any_spec = pl.BlockSpec(memory_space=pl.ANY)
    in_specs = [a_spec, b_spec] + ([o_spec] if has_res else []) + [any_spec] * nb
    args = (a, b) + ((residual,) if has_res else ()) + tuple(bufs)
    comm_shapes, comm_sems = _comm_shapes(bufs, gather) if nb else ([], [])
    res = pl.pallas_call(
        body, name=name, grid=(gm, gn, nk), in_specs=in_specs, out_specs=[o_spec] + [any_spec] * nb,
        out_shape=[jax.ShapeDtypeStruct((m, n), out_dtype)] + comm_shapes,
        scratch_shapes=([pltpu.VMEM((tm, tn), f32)] if nk > 1 else []) + comm_sems,
        compiler_params=_cparams(("arbitrary",) * 3 if nb else ("parallel", "parallel", "arbitrary")))(*args)
    return res if nb else res[0]


def _act_spec(t, width, colblk):
    return pl.BlockSpec((t, width), lambda i: (i, colblk))


def _tok_fwd(name, fn, acts, params, outs, t):
    rows = acts[0][0].shape[0]
    t = min(t, rows)
    na, npar = len(acts), len(params)

    def body(*refs):
        a = [r[...].astype(f32) for r in refs[:na]]
        p = [r[...].astype(f32) for r in refs[na:na + npar]]
        res = fn(*a, *p)
        for o_ref, o in zip(refs[na + npar:], res):
            o_ref[...] = o.astype(o_ref.dtype)

    in_specs = [_act_spec(t, w, cb) for (_, w, cb) in acts]
    in_specs += [pl.BlockSpec(p.shape, lambda i: (0, 0)) for p in params]
    res = pl.pallas_call(
        body, name=name, grid=(rows // t,), in_specs=in_specs,
        out_specs=[_act_spec(t, w, 0) for (w, _) in outs],
        out_shape=[jax.ShapeDtypeStruct((rows, w), dt) for (w, dt) in outs],
        compiler_params=_cparams(("arbitrary",)))(*[a for (a, _, _) in acts], *params)
    return res


def _tok_bwd(name, fn, acts, params, cts, dact, t, residuals=None):
    rows = acts[0][0].shape[0]
    t = min(t, rows)
    residuals = residuals or {}
    res_ids = sorted(residuals)
    na, npar, nc, nr, nd = len(acts), len(params), len(cts), len(res_ids), len(dact)

    def body(*refs):
        a = [r[...].astype(f32) for r in refs[:na]]
        p = [r[...].astype(f32) for r in refs[na:na + npar]]
        ct = tuple(r[...].astype(f32) for r in refs[na + npar:na + npar + nc])
        rs = {idx: r[...].astype(f32) for idx, r in zip(res_ids, refs[na + npar + nc:na + npar + nc + nr])}
        orefs = refs[na + npar + nc + nr:]
        _, vjp = jax.vjp(fn, *a, *p)
        grads = vjp(ct)
        for o_ref, (idx, _) in zip(orefs[:nd], dact):
            g = grads[idx]
            if idx in rs:
                g = g + rs[idx]
            o_ref[...] = g.astype(o_ref.dtype)

        if npar:
            @pl.when(pl.program_id(0) == 0)
            def _():
                for o_ref in orefs[nd:]:
                    o_ref[...] = jnp.zeros_like(o_ref)

            for o_ref, g in zip(orefs[nd:], grads[na:]):
                o_ref[...] += g

    in_specs = [_act_spec(t, w, cb) for (_, w, cb) in acts]
    in_specs += [pl.BlockSpec(p.shape, lambda i: (0, 0)) for p in params]
    in_specs += [_act_spec(t, w, cb) for (_, w, cb) in cts]
    in_specs += [_act_spec(t, acts[idx][1], 0) for idx in res_ids]
    out_specs = [_act_spec(t, acts[idx][1], 0) for (idx, _) in dact]
    out_specs += [pl.BlockSpec(p.shape, lambda i: (0, 0)) for p in params]
    out_shape = [jax.ShapeDtypeStruct((rows, acts[idx][1]), dt) for (idx, dt) in dact]
    out_shape += [jax.ShapeDtypeStruct(p.shape, f32) for p in params]
    return pl.pallas_call(
        body, name=name, grid=(rows // t,), in_specs=in_specs, out_specs=out_specs, out_shape=out_shape,
        compiler_params=_cparams(("arbitrary",)))(
            *[a for (a, _, _) in acts], *params, *[c for (c, _, _) in cts], *[residuals[i] for i in res_ids])


def _rms_fn(x, w):
    return (x * lax.rsqrt(jnp.mean(x * x, axis=-1, keepdims=True) + NORM_EPS) * w,)


def _gate_fn(la, lb, lc, pa, pb, pc):
    return (jax.nn.sigmoid(la) * pa + jax.nn.sigmoid(lb) * pb + jax.nn.sigmoid(lc) * pc,)


def _s5_tail_fn(ypre, gate, glu_w, glu_b):
    y = jax.nn.gelu(ypre)
    y = y * jax.nn.sigmoid(dot_nn(y, glu_w) + glu_b)
    return (y * _silu(gate),)


def _loss_grad(x, w, target, t=256):
    rows, d = x.shape
    t = min(t, rows)

    def fn(xt, wt, tt):
        y = _rms_fn(xt, wt)[0]
        err = y - tt
        return 0.5 * jnp.sum(jnp.sum(err * err, axis=-1, keepdims=True), axis=0, keepdims=True) / d

    def body(x_ref, w_ref, t_ref, loss_ref, dx_ref, dw_ref):
        tt = t_ref[...]
        val, vjp = jax.vjp(lambda a, b: fn(a, b, tt), x_ref[...], w_ref[...])
        dx, dw = vjp(jnp.ones((1, 1), f32))
        dx_ref[...] = dx

        @pl.when(pl.program_id(0) == 0)
        def _():
            loss_ref[...] = jnp.zeros_like(loss_ref)
            dw_ref[...] = jnp.zeros_like(dw_ref)

        loss_ref[...] += val * jnp.ones((1, LANE), f32)
        dw_ref[...] += dw

    return pl.pallas_call(
        body, name="loss_grad", grid=(rows // t,),
        in_specs=[_act_spec(t, d, 0), pl.BlockSpec((1, d), lambda i: (0, 0)), _act_spec(t, d, 0)],
        out_specs=[pl.BlockSpec((1, LANE), lambda i: (0, 0)), _act_spec(t, d, 0), pl.BlockSpec((1, d), lambda i: (0, 0))],
        out_shape=[jax.ShapeDtypeStruct((1, LANE), f32), jax.ShapeDtypeStruct((rows, d), f32),
                   jax.ShapeDtypeStruct((1, d), f32)],
        compiler_params=_cparams(("arbitrary",)))(x, w, target)


GDN_HB = 4
GDN_SW = GDN_HB * HEAD_DIM
GDN_STEPS = GDN_HEADS // GDN_HB


def _tri_inv(a, eye, c):
    rows = a.shape[0]
    n = -a
    p = eye + n
    npow = dot_nn(n, n)
    levels = c.bit_length() - 1
    for j in range(2, levels):
        both = dot_nn(jnp.concatenate([p, npow], axis=0), npow)
        p, npow = p + both[:rows], both[rows:]
    return p + dot_nn(p, npow)


def _block_ids(rows, c):
    ri = lax.broadcasted_iota(jnp.int32, (rows, rows), 0)
    ci = lax.broadcasted_iota(jnp.int32, (rows, rows), 1)
    r1 = lax.broadcasted_iota(jnp.int32, (rows, 1), 0)
    rb, cb, r1b = 0, 0, 0
    for edge in range(c, rows, c):
        rb = rb + (ri >= edge).astype(jnp.int32)
        cb = cb + (ci >= edge).astype(jnp.int32)
        r1b = r1b + (r1 >= edge).astype(jnp.int32)
    return ri, ci, rb, cb, r1b


def _gdn_step(qc, kc, vc, z, small, alog_row, dtb_row, normw, s_cat, head0):
    c = qc.shape[0]
    hb = GDN_HB
    rows = hb * c

    def stack(x):
        return jnp.concatenate([x[:, r * HEAD_DIM:(r + 1) * HEAD_DIM] for r in range(hb)], axis=0)

    ri, ci, rb, cb, r1b = _block_ids(rows, c)
    same = rb == cb
    causal = same & (ri >= ci)
    strict = same & (ri > ci)
    eye = (ri == ci).astype(f32)
    head_rows = [(r1b == r).astype(f32) for r in range(hb)]

    def own_block(x):
        acc = None
        for r in range(hb):
            term = x[:, r * HEAD_DIM:(r + 1) * HEAD_DIM] * head_rows[r]
            acc = term if acc is None else acc + term
        return acc

    beta_all = jax.nn.sigmoid(small)
    g_all = -jnp.exp(alog_row) * _softplus(small + dtb_row)
    gc_all, gct_all = _cumsum_all(g_all)
    beta = jnp.concatenate([_pick_col(beta_all, head0 + r) for r in range(hb)], axis=0)
    gc = jnp.concatenate([_pick_col(gc_all, head0 + r + GDN_HEADS) for r in range(hb)], axis=0)
    gc_t = jnp.concatenate([_pick_row(gct_all, head0 + r + GDN_HEADS) for r in range(hb)], axis=1)
    g_last = [gc[(r + 1) * c - 1:(r + 1) * c, :] for r in range(hb)]
    gl = sum(head_rows[r] * g_last[r] for r in range(hb))

    q = _silu(stack(qc))
    k = _silu(stack(kc))
    v = _silu(stack(vc))
    q = q * lax.rsqrt(jnp.sum(q * q, axis=-1, keepdims=True) + NORM_EPS) * (HEAD_DIM ** -0.5)
    k = k * lax.rsqrt(jnp.sum(k * k, axis=-1, keepdims=True) + NORM_EPS)
    decay = jnp.exp(jnp.where(causal, gc - gc_t, -1e30))
    egc = jnp.exp(gc)
    kb = k * beta
    a_mat = jnp.where(strict, dot_nt(kb, k) * decay, 0.0)
    t_inv = _tri_inv(a_mat, eye, c)
    uw = dot_nn(t_inv, jnp.concatenate([v * beta, kb * egc], axis=1))
    u, w = uw[:, :HEAD_DIM], uw[:, HEAD_DIM:]
    qk = dot_nt(q, k) * decay
    on_state = dot_nn(jnp.concatenate([w, q * egc], axis=0), s_cat)
    v_new = u - own_block(on_state[:rows])
    out = own_block(on_state[rows:]) + dot_nn(qk, v_new)
    k_tail = k * jnp.exp(gl - gc)
    v_bd = jnp.concatenate([v_new * head_rows[r] for r in range(hb)], axis=1)
    eg_cat = jnp.concatenate([jnp.exp(g_last[r]) * jnp.ones((1, HEAD_DIM), f32) for r in range(hb)], axis=1)
    new_s = s_cat * eg_cat + dot_tn(k_tail, v_bd)
    o = out * lax.rsqrt(jnp.mean(out * out, axis=-1, keepdims=True) + NORM_EPS) * normw * _silu(stack(z))
    o = jnp.concatenate([o[r * c:(r + 1) * c] for r in range(hb)], axis=1)
    return o, new_s


def _conv_windows(xin_ref, p, cw_ref, c):
    acc = None
    for k in range(CONV_K):
        term = cw_ref[pl.ds(k, 1), :] * xin_ref[p, pl.ds(HALO - CONV_K + 1 + k, c), :]
        acc = term if acc is None else acc + term
    return acc


def _gdn_fwd(proj, conv_w, alog_row, dtb_row, normw):
    rows = proj.shape[0]
    c = min(CHUNK, rows)
    n = rows // c

    def body(q_ref, k_ref, v_ref, z_ref, sm_ref, cwq, cwk, cwv, al_ref, dt_ref, nw_ref, y_ref, ck_ref, s_ref, xin_ref):
        hb = pl.program_id(0)
        i = pl.program_id(1)

        @pl.when(i == 0)
        def _():
            s_ref[...] = jnp.zeros_like(s_ref)
            xin_ref[:, 0:HALO, :] = jnp.zeros((3, HALO, GDN_SW), f32)

        @pl.when(i > 0)
        def _():
            xin_ref[:, 0:HALO, :] = xin_ref[:, c:c + HALO, :]

        xin_ref[0, HALO:, :] = q_ref[...]
        xin_ref[1, HALO:, :] = k_ref[...]
        xin_ref[2, HALO:, :] = v_ref[...]
        qc = _conv_windows(xin_ref, 0, cwq, c)
        kc = _conv_windows(xin_ref, 1, cwk, c)
        vc = _conv_windows(xin_ref, 2, cwv, c)
        state = s_ref[...]
        ck_ref[...] = state
        o, new_state = _gdn_step(qc, kc, vc, z_ref[...], sm_ref[...], al_ref[...], dt_ref[...], nw_ref[...], state,
                                 hb * GDN_HB)
        y_ref[...] = o.astype(y_ref.dtype)
        s_ref[...] = new_state

    def blk(unit):
        return pl.BlockSpec((c, GDN_SW), lambda hb, i: (i, unit // GDN_HB + hb))

    def cw(part):
        return pl.BlockSpec((CONV_K, GDN_SW), lambda hb, i: (0, part * GDN_STEPS + hb))

    row = pl.BlockSpec((1, LANE), lambda hb, i: (0, 0))
    return pl.pallas_call(
        body, name="gdn_fwd", grid=(GDN_STEPS, n),
        in_specs=[blk(U_Q), blk(U_K), blk(U_V), blk(U_AZ), pl.BlockSpec((c, LANE), lambda hb, i: (i, U_SMA)),
                  cw(0), cw(1), cw(2), row, row, row],
        out_specs=[pl.BlockSpec((c, GDN_SW), lambda hb, i: (i, hb)),
                   pl.BlockSpec((None, None, HEAD_DIM, GDN_SW), lambda hb, i: (hb, i, 0, 0))],
        out_shape=[jax.ShapeDtypeStruct((rows, GDN_WIDTH), bf16),
                   jax.ShapeDtypeStruct((GDN_STEPS, n, HEAD_DIM, GDN_SW), f32)],
        scratch_shapes=[pltpu.VMEM((HEAD_DIM, GDN_SW), f32), pltpu.VMEM((3, c + HALO, GDN_SW), f32)],
        compiler_params=_cparams(("arbitrary", "arbitrary")))(
            proj, proj, proj, proj, proj, conv_w, conv_w, conv_w, alog_row, dtb_row, normw)


def _conv_bwd(xin_ref, dyext_ref, p, cw_ref, dxc, dx_ref, dcw_ref, c):
    dyext_ref[p, 0:c, :] = dxc
    acc = None
    for k in range(CONV_K):
        term = cw_ref[pl.ds(k, 1), :] * dyext_ref[p, pl.ds(CONV_K - 1 - k, c), :]
        acc = term if acc is None else acc + term
        dcw_ref[pl.ds(k, 1), :] += jnp.sum(xin_ref[p, pl.ds(HALO - CONV_K + 1 + k, c), :] * dxc, axis=0, keepdims=True)
    dx_ref[...] = acc.astype(dx_ref.dtype)


def _gdn_bwd(proj, dy, ck, conv_w, alog_row, dtb_row, normw, scatter=None):
    rows = proj.shape[0]
    c = min(CHUNK, rows)
    n = rows // c
    halo_blocks = c // HALO
    bufs = scatter or []
    nb = len(bufs)
    n_in, n_out, n_scr = 16, 11, 3

    def body(*refs):
        core = refs[:n_in] + refs[n_in + nb:n_in + nb + n_out] + refs[n_in + 2 * nb + n_out:n_in + 2 * nb + n_out + n_scr]
        comm = (refs[n_in:n_in + nb], refs[n_in + nb + n_out:n_in + 2 * nb + n_out], *refs[n_in + 2 * nb + n_out + n_scr:])
        step_id = pl.program_id(0) * n + pl.program_id(1)
        if nb:
            @pl.when(step_id == 0)
            def _():
                _comm_phase("start", *comm, gather=False)

        chunk_step(*core)
        if nb:
            @pl.when(step_id == GDN_STEPS * n - 1)
            def _():
                _comm_phase("finish", *comm, gather=False)

    def chunk_step(q_ref, k_ref, v_ref, hq_ref, hk_ref, hv_ref, z_ref, sm_ref, cwq, cwk, cwv, al_ref, dt_ref, nw_ref,
                   ck_ref, dy_ref, dq_ref, dk_ref, dv_ref, dz_ref, dsm_ref, dcwq, dcwk, dcwv, dal_ref, ddt_ref, dnw_ref,
                   ds_ref, xin_ref, dyext_ref):
        hb = pl.program_id(0)
        i = pl.program_id(1)
        ci = n - 1 - i

        @pl.when(i == 0)
        def _():
            ds_ref[...] = jnp.zeros_like(ds_ref)
            dyext_ref[:, c:c + HALO, :] = jnp.zeros((3, HALO, GDN_SW), f32)
            for r in (dcwq, dcwk, dcwv, dal_ref, ddt_ref, dnw_ref):
                r[...] = jnp.zeros_like(r)

        @pl.when(i > 0)
        def _():
            dyext_ref[:, c:c + HALO, :] = dyext_ref[:, 0:HALO, :]

        first = (ci > 0).astype(f32)
        for p, (x_ref, halo_ref) in enumerate(((q_ref, hq_ref), (k_ref, hk_ref), (v_ref, hv_ref))):
            xin_ref[p, 0:HALO, :] = halo_ref[...] * first
            xin_ref[p, HALO:, :] = x_ref[...]
        qc = _conv_windows(xin_ref, 0, cwq, c)
        kc = _conv_windows(xin_ref, 1, cwk, c)
        vc = _conv_windows(xin_ref, 2, cwv, c)
        fn = functools.partial(_gdn_step, head0=hb * GDN_HB)
        _, vjp = jax.vjp(fn, qc, kc, vc, z_ref[...], sm_ref[...], al_ref[...], dt_ref[...], nw_ref[...], ck_ref[...])
        dqc, dkc, dvc, dz, dsm, dal, ddt, dnw, dstate = vjp((dy_ref[...].astype(f32), ds_ref[...]))
        ds_ref[...] = dstate
        dz_ref[...] = dz.astype(dz_ref.dtype)
        dsm_ref[...] = dsm
        dal_ref[...] += dal
        ddt_ref[...] += ddt
        dnw_ref[...] += dnw
        _conv_bwd(xin_ref, dyext_ref, 0, cwq, dqc, dq_ref, dcwq, c)
        _conv_bwd(xin_ref, dyext_ref, 1, cwk, dkc, dk_ref, dcwk, c)
        _conv_bwd(xin_ref, dyext_ref, 2, cwv, dvc, dv_ref, dcwv, c)

    def blk(unit):
        return pl.BlockSpec((c, GDN_SW), lambda hb, i: (n - 1 - i, unit // GDN_HB + hb))

    def halo(unit):
        return pl.BlockSpec((HALO, GDN_SW),
                            lambda hb, i: (jnp.maximum((n - 1 - i) * halo_blocks - 1, 0), unit // GDN_HB + hb))

    def cw(part):
        return pl.BlockSpec((CONV_K, GDN_SW), lambda hb, i: (0, part * GDN_STEPS + hb))

    row = pl.BlockSpec((1, LANE), lambda hb, i: (0, 0))
    hrow = pl.BlockSpec((None, 1, LANE), lambda hb, i: (hb, 0, 0))
    out_blk = pl.BlockSpec((c, GDN_SW), lambda hb, i: (n - 1 - i, hb))
    dcw = pl.BlockSpec((CONV_K, GDN_SW), lambda hb, i: (0, hb))
    wide = jax.ShapeDtypeStruct((rows, GDN_WIDTH), bf16)
    hrow_shape = jax.ShapeDtypeStruct((GDN_STEPS, 1, LANE), f32)
    dcw_shape = jax.ShapeDtypeStruct((CONV_K, GDN_WIDTH), f32)
    any_spec = pl.BlockSpec(memory_space=pl.ANY)
    comm_shapes, comm_sems = _comm_shapes(bufs, False) if nb else ([], [])
    return pl.pallas_call(
        body, name="gdn_bwd_scatter" if nb else "gdn_bwd", grid=(GDN_STEPS, n),
        in_specs=[blk(U_Q), blk(U_K), blk(U_V), halo(U_Q), halo(U_K), halo(U_V), blk(U_AZ),
                  pl.BlockSpec((c, LANE), lambda hb, i: (n - 1 - i, U_SMA)),
                  cw(0), cw(1), cw(2), row, row, row,
                  pl.BlockSpec((None, None, HEAD_DIM, GDN_SW), lambda hb, i: (hb, n - 1 - i, 0, 0)),
                  out_blk] + [any_spec] * nb,
        out_specs=[out_blk, out_blk, out_blk, out_blk,
                   pl.BlockSpec((None, c, LANE), lambda hb, i: (hb, n - 1 - i, 0)),
                   dcw, dcw, dcw, hrow, hrow, hrow] + [any_spec] * nb,
        out_shape=[wide, wide, wide, wide, jax.ShapeDtypeStruct((GDN_STEPS, rows, LANE), f32),
                   dcw_shape, dcw_shape, dcw_shape, hrow_shape, hrow_shape, hrow_shape] + comm_shapes,
        scratch_shapes=[pltpu.VMEM((HEAD_DIM, GDN_SW), f32), pltpu.VMEM((3, c + HALO, GDN_SW), f32),
                        pltpu.VMEM((3, c + HALO, GDN_SW), f32)] + comm_sems,
        compiler_params=_cparams(("arbitrary", "arbitrary")))(
            proj, proj, proj, proj, proj, proj, proj, proj, conv_w, conv_w, conv_w, alog_row, dtb_row, normw, ck, dy,
            *bufs)


M2_REP = M2_HEADS // M2_GROUPS
M2_GW = M2_REP * M2_HEAD_DIM
M2_GB = 2
M2_STEPS = M2_GROUPS // M2_GB
M2_XW = M2_GB * M2_GW
M2_BW = M2_GB * M2_STATE
M2_SH = M2_GB * M2_REP


def _ssd_step(xc, bc, cc, z, small, bias_x, bias_b, bias_c, alog_row, dtb_row, d_row, normw, state, grp0):
    c = xc.shape[0]
    causal = _tri_masks(c)[0]
    hd = M2_HEAD_DIM
    ones_l = jnp.ones((1, hd), f32)
    ones_r = jnp.ones((hd, 1), f32)
    lane = lax.broadcasted_iota(jnp.int32, (1, M2_GW), 1)
    lane_head = [((lane >= r * hd) & (lane < (r + 1) * hd)).astype(f32) for r in range(M2_REP)]
    xs = _silu(xc + bias_x)
    bms = _silu(bc + bias_b)
    cms = _silu(cc + bias_c)
    dt_all = _softplus(small + dtb_row)
    a_all = -jnp.exp(alog_row) * dt_all
    ac_all, act_all = _cumsum_all(a_all)
    ys, new_states = [], []
    for gi in range(M2_GB):
        bm = bms[:, gi * M2_STATE:(gi + 1) * M2_STATE]
        cm = cms[:, gi * M2_STATE:(gi + 1) * M2_STATE]
        xg = xs[:, gi * M2_GW:(gi + 1) * M2_GW]
        sg = state[gi * M2_GW:(gi + 1) * M2_GW]
        heads = [(grp0 + gi) * M2_REP + r for r in range(M2_REP)]
        ac_h = [_pick_col(ac_all, h) for h in heads]
        al_h = [a[c - 1:c, :] for a in ac_h]

        def wide(cols):
            return jnp.concatenate([v * ones_l for v in cols], axis=1)

        dt_w = wide([_pick_col(dt_all, h) for h in heads])
        ac_w = wide(ac_h)
        al_w = wide(al_h)
        dsk_w = wide([_pick_col(d_row, h) for h in heads])
        scores = dot_nt(cm, bm)
        m_wide = jnp.concatenate(
            [scores * jnp.exp(jnp.where(causal, a - _pick_row(act_all, h), -1e30)) for a, h in zip(ac_h, heads)], axis=1)
        xdt = xg * dt_w
        x_bd = jnp.concatenate([xdt * lane_head[r] for r in range(M2_REP)], axis=0)
        y_diag = dot_nn(m_wide, x_bd)
        states_new = dot_tn(xdt * jnp.exp(al_w - ac_w), bm)
        y_off = dot_nt(cm, sg) * jnp.exp(ac_w)
        eg_col = jnp.concatenate([jnp.exp(a) * ones_r for a in al_h], axis=0)
        new_states.append(sg * eg_col + states_new)
        y = (y_diag + y_off + dsk_w * xg) * _silu(z[:, gi * M2_GW:(gi + 1) * M2_GW])
        ys.append(y * lax.rsqrt(jnp.mean(y * y, axis=-1, keepdims=True) + NORM_EPS)
                  * normw[:, gi * M2_GW:(gi + 1) * M2_GW])
    return jnp.concatenate(ys, axis=-1), jnp.concatenate(new_states, axis=0)


def _conv_windows2(xin_ref, cw_ref, c):
    acc = None
    for k in range(CONV_K):
        term = cw_ref[pl.ds(k, 1), :] * xin_ref[pl.ds(HALO - CONV_K + 1 + k, c), :]
        acc = term if acc is None else acc + term
    return acc


def _ssd_specs(n, c, rev):
    def ci(i):
        return (n - 1 - i) if rev else i

    def blk(width, unit):
        return pl.BlockSpec((c, width), lambda g, i: (ci(i), unit * LANE // width + g))

    def par(rows_, width, col0):
        return pl.BlockSpec((rows_, width), lambda g, i: (0, col0 // width + g))

    return ci, blk, par


def _ssd_fwd(proj, conv_w, conv_b, alog_row, dtb_row, d_row, normw):
    rows = proj.shape[0]
    c = min(CHUNK, rows)
    n = rows // c
    _, blk, par = _ssd_specs(n, c, False)

    def body(x_ref, b_ref, c_ref, z_ref, sm_ref, cwx, cwb, cwc, bx, bb, bcc, al_ref, dt_ref, d_ref, nw_ref,
             y_ref, ck_ref, s_ref, xx_ref, xb_ref, xc_ref):
        g = pl.program_id(0)
        i = pl.program_id(1)

        @pl.when(i == 0)
        def _():
            s_ref[...] = jnp.zeros_like(s_ref)
            for r in (xx_ref, xb_ref, xc_ref):
                r[0:HALO, :] = jnp.zeros((HALO, r.shape[1]), f32)

        @pl.when(i > 0)
        def _():
            for r in (xx_ref, xb_ref, xc_ref):
                r[0:HALO, :] = r[c:c + HALO, :]

        xx_ref[HALO:, :] = x_ref[...]
        xb_ref[HALO:, :] = b_ref[...]
        xc_ref[HALO:, :] = c_ref[...]
        xc = _conv_windows2(xx_ref, cwx, c)
        bc = _conv_windows2(xb_ref, cwb, c)
        cc = _conv_windows2(xc_ref, cwc, c)
        state = s_ref[...]
        ck_ref[...] = state
        y, new_state = _ssd_step(xc, bc, cc, z_ref[...], sm_ref[...], bx[...], bb[...], bcc[...], al_ref[...],
                                 dt_ref[...], d_ref[...], nw_ref[...], state, g * M2_GB)
        y_ref[...] = y.astype(y_ref.dtype)
        s_ref[...] = new_state

    row = pl.BlockSpec((1, LANE), lambda g, i: (0, 0))
    off_b, off_c = M2_WIDTH, M2_WIDTH + M2_GROUPS * M2_STATE
    return pl.pallas_call(
        body, name="ssd_fwd", grid=(M2_STEPS, n),
        in_specs=[blk(M2_XW, U_CX), blk(M2_BW, U_CB), blk(M2_BW, U_CC), blk(M2_XW, U_CZ),
                  pl.BlockSpec((c, LANE), lambda g, i: (i, U_SMC)),
                  par(CONV_K, M2_XW, 0), par(CONV_K, M2_BW, off_b), par(CONV_K, M2_BW, off_c),
                  par(1, M2_XW, 0), par(1, M2_BW, off_b), par(1, M2_BW, off_c), row, row, row, par(1, M2_XW, 0)],
        out_specs=[pl.BlockSpec((c, M2_XW), lambda g, i: (i, g)),
                   pl.BlockSpec((None, None, M2_SH * M2_HEAD_DIM, M2_STATE), lambda g, i: (g, i, 0, 0))],
        out_shape=[jax.ShapeDtypeStruct((rows, M2_WIDTH), bf16),
                   jax.ShapeDtypeStruct((M2_STEPS, n, M2_SH * M2_HEAD_DIM, M2_STATE), f32)],
        scratch_shapes=[pltpu.VMEM((M2_SH * M2_HEAD_DIM, M2_STATE), f32), pltpu.VMEM((c + HALO, M2_XW), f32),
                        pltpu.VMEM((c + HALO, M2_BW), f32), pltpu.VMEM((c + HALO, M2_BW), f32)],
        compiler_params=_cparams(("arbitrary", "arbitrary")))(
            proj, proj, proj, proj, proj, conv_w, conv_w, conv_w, conv_b, conv_b, conv_b, alog_row, dtb_row, d_row, normw)


def _conv_bwd2(xin_ref, dyext_ref, cw_ref, dxc, dx_ref, dcw_ref, c):
    dyext_ref[0:c, :] = dxc
    acc = None
    for k in range(CONV_K):
        term = cw_ref[pl.ds(k, 1), :] * dyext_ref[pl.ds(CONV_K - 1 - k, c), :]
        acc = term if acc is None else acc + term
        dcw_ref[pl.ds(k, 1), :] += jnp.sum(xin_ref[pl.ds(HALO - CONV_K + 1 + k, c), :] * dxc, axis=0, keepdims=True)
    dx_ref[...] = acc.astype(dx_ref.dtype)


def _ssd_bwd(proj, dy, ck, conv_w, conv_b, alog_row, dtb_row, d_row, normw):
    rows = proj.shape[0]
    c = min(CHUNK, rows)
    n = rows // c
    halo_blocks = c // HALO
    _, blk, par = _ssd_specs(n, c, True)

    def body(x_ref, b_ref, c_ref, hx_ref, hb_ref, hc_ref, z_ref, sm_ref, cwx, cwb, cwc, bx, bb, bcc,
             al_ref, dt_ref, d_ref, nw_ref, ck_ref, dy_ref,
             dx_ref, db_ref, dc_ref, dz_ref, dsm_ref, dcwx, dcwb, dcwc, dbx, dbb, dbc, dal_ref, ddt_ref, dd_ref, dnw_ref,
             ds_ref, xx_ref, xb_ref, xc_ref, ex_ref, eb_ref, ec_ref):
        g = pl.program_id(0)
        i = pl.program_id(1)
        ci = n - 1 - i

        @pl.when(i == 0)
        def _():
            ds_ref[...] = jnp.zeros_like(ds_ref)
            for r in (ex_ref, eb_ref, ec_ref):
                r[c:c + HALO, :] = jnp.zeros((HALO, r.shape[1]), f32)
            for r in (dcwx, dcwb, dcwc, dbx, dbb, dbc, dal_ref, ddt_ref, dd_ref, dnw_ref):
                r[...] = jnp.zeros_like(r)

        @pl.when(i > 0)
        def _():
            for r in (ex_ref, eb_ref, ec_ref):
                r[c:c + HALO, :] = r[0:HALO, :]

        first = (ci > 0).astype(f32)
        for xin, x_in, halo_in in ((xx_ref, x_ref, hx_ref), (xb_ref, b_ref, hb_ref), (xc_ref, c_ref, hc_ref)):
            xin[0:HALO, :] = halo_in[...] * first
            xin[HALO:, :] = x_in[...]
        xc = _conv_windows2(xx_ref, cwx, c)
        bc = _conv_windows2(xb_ref, cwb, c)
        cc = _conv_windows2(xc_ref, cwc, c)
        fn = functools.partial(_ssd_step, grp0=g * M2_GB)
        _, vjp = jax.vjp(fn, xc, bc, cc, z_ref[...], sm_ref[...], bx[...], bb[...], bcc[...], al_ref[...], dt_ref[...],
                         d_ref[...], nw_ref[...], ck_ref[...])
        (dxc, dbc_, dcc, dz, dsm, gbx, gbb, gbc, dal, ddt, dd, dnw, dstate) = vjp((dy_ref[...].astype(f32), ds_ref[...]))
        ds_ref[...] = dstate
        dz_ref[...] = dz.astype(dz_ref.dtype)
        dsm_ref[...] = dsm
        dbx[...] += gbx
        dbb[...] += gbb
        dbc[...] += gbc
        dal_ref[...] += dal
        ddt_ref[...] += ddt
        dd_ref[...] += dd
        dnw_ref[...] += dnw
        _conv_bwd2(xx_ref, ex_ref, cwx, dxc, dx_ref, dcwx, c)
        _conv_bwd2(xb_ref, eb_ref, cwb, dbc_, db_ref, dcwb, c)
        _conv_bwd2(xc_ref, ec_ref, cwc, dcc, dc_ref, dcwc, c)

    def halo(width, unit):
        return pl.BlockSpec((HALO, width),
                            lambda g, i: (jnp.maximum((n - 1 - i) * halo_blocks - 1, 0), unit * LANE // width + g))

    row = pl.BlockSpec((1, LANE), lambda g, i: (0, 0))
    grow = pl.BlockSpec((None, 1, LANE), lambda g, i: (g, 0, 0))
    grow_shape = jax.ShapeDtypeStruct((M2_STEPS, 1, LANE), f32)
    ob_w = pl.BlockSpec((c, M2_XW), lambda g, i: (n - 1 - i, g))
    ob_n = pl.BlockSpec((c, M2_BW), lambda g, i: (n - 1 - i, g))
    off_b, off_c = M2_WIDTH, M2_WIDTH + M2_GROUPS * M2_STATE
    bc_w = M2_GROUPS * M2_STATE
    return pl.pallas_call(
        body, name="ssd_bwd", grid=(M2_STEPS, n),
        in_specs=[blk(M2_XW, U_CX), blk(M2_BW, U_CB), blk(M2_BW, U_CC),
                  halo(M2_XW, U_CX), halo(M2_BW, U_CB), halo(M2_BW, U_CC), blk(M2_XW, U_CZ),
                  pl.BlockSpec((c, LANE), lambda g, i: (n - 1 - i, U_SMC)),
                  par(CONV_K, M2_XW, 0), par(CONV_K, M2_BW, off_b), par(CONV_K, M2_BW, off_c),
                  par(1, M2_XW, 0), par(1, M2_BW, off_b), par(1, M2_BW, off_c), row, row, row, par(1, M2_XW, 0),
                  pl.BlockSpec((None, None, M2_SH * M2_HEAD_DIM, M2_STATE), lambda g, i: (g, n - 1 - i, 0, 0)),
                  ob_w],
        out_specs=[ob_w, ob_n, ob_n, ob_w, pl.BlockSpec((None, c, LANE), lambda g, i: (g, n - 1 - i, 0)),
                   par(CONV_K, M2_XW, 0), par(CONV_K, M2_BW, 0), par(CONV_K, M2_BW, 0),
                   par(1, M2_XW, 0), par(1, M2_BW, 0), par(1, M2_BW, 0), grow, grow, grow, par(1, M2_XW, 0)],
        out_shape=[jax.ShapeDtypeStruct((rows, M2_WIDTH), bf16), jax.ShapeDtypeStruct((rows, bc_w), bf16),
                   jax.ShapeDtypeStruct((rows, bc_w), bf16), jax.ShapeDtypeStruct((rows, M2_WIDTH), bf16),
                   jax.ShapeDtypeStruct((M2_STEPS, rows, LANE), f32),
                   jax.ShapeDtypeStruct((CONV_K, M2_WIDTH), f32), jax.ShapeDtypeStruct((CONV_K, bc_w), f32),
                   jax.ShapeDtypeStruct((CONV_K, bc_w), f32),
                   jax.ShapeDtypeStruct((1, M2_WIDTH), f32), jax.ShapeDtypeStruct((1, bc_w), f32),
                   jax.ShapeDtypeStruct((1, bc_w), f32), grow_shape, grow_shape, grow_shape,
                   jax.ShapeDtypeStruct((1, M2_WIDTH), f32)],
        scratch_shapes=[pltpu.VMEM((M2_SH * M2_HEAD_DIM, M2_STATE), f32),
                        pltpu.VMEM((c + HALO, M2_XW), f32), pltpu.VMEM((c + HALO, M2_BW), f32), pltpu.VMEM((c + HALO, M2_BW), f32),
                        pltpu.VMEM((c + HALO, M2_XW), f32), pltpu.VMEM((c + HALO, M2_BW), f32), pltpu.VMEM((c + HALO, M2_BW), f32)],
        compiler_params=_cparams(("arbitrary", "arbitrary")))(
            proj, proj, proj, proj, proj, proj, proj, proj, conv_w, conv_w, conv_w, conv_b, conv_b, conv_b,
            alog_row, dtb_row, d_row, normw, ck, dy)


S5_TILE = 128
S5_SW = S5_LANES // S5_BLK


def _scan_down(br, bi, ar, ai):
    t = br.shape[0]
    row = lax.broadcasted_iota(jnp.int32, br.shape, 0)
    d = 1
    while d < t:
        keep = row >= d
        sr = jnp.where(keep, pltpu.roll(br, d, 0), 0.0)
        si = jnp.where(keep, pltpu.roll(bi, d, 0), 0.0)
        br, bi = br + ar * sr - ai * si, bi + ar * si + ai * sr
        ar, ai = ar * ar - ai * ai, 2.0 * ar * ai
        d *= 2
    return br, bi


def _scan_up(br, bi, ar, ai):
    t = br.shape[0]
    row = lax.broadcasted_iota(jnp.int32, br.shape, 0)
    d = 1
    while d < t:
        keep = row < t - d
        sr = jnp.where(keep, pltpu.roll(br, t - d, 0), 0.0)
        si = jnp.where(keep, pltpu.roll(bi, t - d, 0), 0.0)
        br, bi = br + ar * sr - ai * si, bi + ar * si + ai * sr
        ar, ai = ar * ar - ai * ai, 2.0 * ar * ai
        d *= 2
    return br, bi


def _s5_states(u_j, bbr, bbi, ar, ai, cr, ci_):
    br = dot_nn(u_j, bbr)
    bi = dot_nn(u_j, bbi)
    row0 = lax.broadcasted_iota(jnp.int32, br.shape, 0) == 0
    br = br + jnp.where(row0, ar * cr - ai * ci_, 0.0)
    bi = bi + jnp.where(row0, ar * ci_ + ai * cr, 0.0)
    return _scan_down(br, bi, ar, ai)


def _s5_fwd(proj, a_rows, bbr, bbi, ccr, cci, d_row):
    rows = proj.shape[0]
    t = min(S5_TILE, rows)
    n = rows // t

    def body(u_ref, a_ref, bbr_ref, bbi_ref, ccr_ref, cci_ref, d_ref, y_ref, ck_ref, carry_ref):
        i = pl.program_id(0)

        @pl.when(i == 0)
        def _():
            carry_ref[...] = jnp.zeros_like(carry_ref)

        ck_ref[...] = carry_ref[...]
        for j in range(S5_BLK):
            lanes = pl.ds(j * S5_SW, S5_SW)
            ch = pl.ds(j * LANE, LANE)
            u_j = u_ref[:, ch]
            sr, si = _s5_states(u_j, bbr_ref[j], bbi_ref[j], a_ref[0:1, lanes], a_ref[1:2, lanes],
                                carry_ref[0:1, lanes], carry_ref[1:2, lanes])
            y_ref[:, ch] = dot_nn(sr, ccr_ref[j]) - dot_nn(si, cci_ref[j]) + d_ref[:, ch] * u_j
            carry_ref[0:1, lanes] = sr[t - 1:t, :]
            carry_ref[1:2, lanes] = si[t - 1:t, :]

    whole3 = lambda s: pl.BlockSpec(s, lambda i: (0, 0, 0))
    return pl.pallas_call(
        body, name="s5_fwd", grid=(n,),
        in_specs=[pl.BlockSpec((t, S5_WIDTH), lambda i: (i, U_SU // S5_BLK)),
                  pl.BlockSpec((2, S5_LANES), lambda i: (0, 0)),
                  whole3(bbr.shape), whole3(bbi.shape), whole3(ccr.shape), whole3(cci.shape),
                  pl.BlockSpec((1, S5_WIDTH), lambda i: (0, 0))],
        out_specs=[pl.BlockSpec((t, S5_WIDTH), lambda i: (i, 0)),
                   pl.BlockSpec((None, 2, S5_LANES), lambda i: (i, 0, 0))],
        out_shape=[jax.ShapeDtypeStruct((rows, S5_WIDTH), f32), jax.ShapeDtypeStruct((n, 2, S5_LANES), f32)],
        scratch_shapes=[pltpu.VMEM((2, S5_LANES), f32)],
        compiler_params=_cparams(("arbitrary",)))(proj, a_rows, bbr, bbi, ccr, cci, d_row)


def _s5_bwd(proj, dy, ck, a_rows, bbr, bbi, ccr, cci, d_row):
    rows = proj.shape[0]
    t = min(S5_TILE, rows)
    n = rows // t

    def body(u_ref, dy_ref, ck_ref, a_ref, bbr_ref, bbi_ref, ccr_ref, cci_ref, d_ref,
             du_ref, da_ref, dbbr_ref, dbbi_ref, dccr_ref, dcci_ref, dd_ref, lam_ref):
        i = pl.program_id(0)

        @pl.when(i == 0)
        def _():
            lam_ref[...] = jnp.zeros_like(lam_ref)
            for r in (da_ref, dbbr_ref, dbbi_ref, dccr_ref, dcci_ref, dd_ref):
                r[...] = jnp.zeros_like(r)

        for j in range(S5_BLK):
            lanes = pl.ds(j * S5_SW, S5_SW)
            ch = pl.ds(j * LANE, LANE)
            u_j = u_ref[:, ch]
            dy_j = dy_ref[:, ch]
            ar, ai = a_ref[0:1, lanes], a_ref[1:2, lanes]
            cr, ci_ = ck_ref[0:1, lanes], ck_ref[1:2, lanes]
            sr, si = _s5_states(u_j, bbr_ref[j], bbi_ref[j], ar, ai, cr, ci_)
            gr = dot_nt(dy_j, ccr_ref[j])
            gi = -dot_nt(dy_j, cci_ref[j])
            last = lax.broadcasted_iota(jnp.int32, gr.shape, 0) == t - 1
            lr0, li0 = lam_ref[0:1, lanes], lam_ref[1:2, lanes]
            gr = gr + jnp.where(last, ar * lr0 + ai * li0, 0.0)
            gi = gi + jnp.where(last, ar * li0 - ai * lr0, 0.0)
            lr, li = _scan_up(gr, gi, ar, -ai)
            lam_ref[0:1, lanes] = lr[0:1, :]
            lam_ref[1:2, lanes] = li[0:1, :]
            du_ref[:, ch] = (dot_nt(lr, bbr_ref[j]) + dot_nt(li, bbi_ref[j]) + d_ref[:, ch] * dy_j).astype(du_ref.dtype)
            dbbr_ref[j] += dot_tn(u_j, lr)
            dbbi_ref[j] += dot_tn(u_j, li)
            dccr_ref[j] += dot_tn(sr, dy_j)
            dcci_ref[j] += -dot_tn(si, dy_j)
            dd_ref[:, ch] += jnp.sum(dy_j * u_j, axis=0, keepdims=True)
            row0 = lax.broadcasted_iota(jnp.int32, sr.shape, 0) == 0
            pr = jnp.where(row0, cr, pltpu.roll(sr, 1, 0))
            pi = jnp.where(row0, ci_, pltpu.roll(si, 1, 0))
            da_ref[0:1, lanes] += jnp.sum(lr * pr + li * pi, axis=0, keepdims=True)
            da_ref[1:2, lanes] += jnp.sum(li * pr - lr * pi, axis=0, keepdims=True)

    whole3 = lambda s: pl.BlockSpec(s, lambda i: (0, 0, 0))
    whole2 = lambda s: pl.BlockSpec(s, lambda i: (0, 0))
    return pl.pallas_call(
        body, name="s5_bwd", grid=(n,),
        in_specs=[pl.BlockSpec((t, S5_WIDTH), lambda i: (n - 1 - i, U_SU // S5_BLK)),
                  pl.BlockSpec((t, S5_WIDTH), lambda i: (n - 1 - i, 0)),
                  pl.BlockSpec((None, 2, S5_LANES), lambda i: (n - 1 - i, 0, 0)),
                  whole2((2, S5_LANES)), whole3(bbr.shape), whole3(bbi.shape), whole3(ccr.shape), whole3(cci.shape),
                  whole2((1, S5_WIDTH))],
        out_specs=[pl.BlockSpec((t, S5_WIDTH), lambda i: (n - 1 - i, 0)), whole2((2, S5_LANES)),
                   whole3(bbr.shape), whole3(bbi.shape), whole3(ccr.shape), whole3(cci.shape), whole2((1, S5_WIDTH))],
        out_shape=[jax.ShapeDtypeStruct((rows, S5_WIDTH), bf16), jax.ShapeDtypeStruct((2, S5_LANES), f32),
                   jax.ShapeDtypeStruct(bbr.shape, f32), jax.ShapeDtypeStruct(bbi.shape, f32),
                   jax.ShapeDtypeStruct(ccr.shape, f32), jax.ShapeDtypeStruct(cci.shape, f32),
                   jax.ShapeDtypeStruct((1, S5_WIDTH), f32)],
        scratch_shapes=[pltpu.VMEM((2, S5_LANES), f32)],
        compiler_params=_cparams(("arbitrary",)))(proj, dy, ck, a_rows, bbr, bbi, ccr, cci, d_row)


def _s5_prep(lam_re, lam_im, log_step, b_re, b_im, c_re, c_im, d_skip):
    lam_re = jnp.minimum(lam_re, -1e-4)
    step = jnp.exp(log_step)[:, None]
    mag = jnp.exp(lam_re * step)
    ab_re = mag * jnp.cos(lam_im * step)
    ab_im = mag * jnp.sin(lam_im * step)
    den = lam_re * lam_re + lam_im * lam_im
    f_re = ((ab_re - 1.0) * lam_re + ab_im * lam_im) / den
    f_im = (ab_im * lam_re - (ab_re - 1.0) * lam_im) / den
    bb_re = f_re[..., None] * b_re - f_im[..., None] * b_im
    bb_im = f_re[..., None] * b_im + f_im[..., None] * b_re
    eye = jnp.eye(8, dtype=f32)

    def drive(bb):
        r = bb.reshape(S5_BLK, 8, S5_STATE, S5_GROUP_SIZE).transpose(0, 1, 3, 2)
        return (r[:, :, :, None, :] * eye[None, :, None, :, None]).reshape(S5_BLK, LANE, S5_SW)

    def readout(cc):
        r = cc.reshape(S5_BLK, 8, S5_GROUP_SIZE, S5_STATE).transpose(0, 1, 3, 2)
        return (r[:, :, :, None, :] * eye[None, :, None, :, None]).reshape(S5_BLK, S5_SW, LANE)

    a_rows = jnp.stack([ab_re.reshape(S5_LANES), ab_im.reshape(S5_LANES)])
    return a_rows, drive(bb_re), drive(bb_im), readout(c_re), readout(c_im), d_skip.reshape(1, S5_WIDTH)


def _adam_math(w, g, m, v):
    m = ADAM_B1 * m + (1.0 - ADAM_B1) * g
    v = ADAM_B2 * v + (1.0 - ADAM_B2) * (g * g)
    m_hat = m / (1.0 - ADAM_B1 ** ADAM_STEP)
    v_hat = v / (1.0 - ADAM_B2 ** ADAM_STEP)
    delta = -ADAM_LR * (m_hat / (jnp.sqrt(v_hat) + ADAM_EPS) + ADAM_WD * w)
    return delta, m, v


def _adamw(name, w, g, m, v):
    rows, width = w.shape
    t = rows
    for cand in (512, 256, 128, 64, 32, 16, 8):
        if rows % cand == 0 and cand * width * 4 * 7 * 2 <= VMEM_LIMIT // 2:
            t = cand
            break

    def body(w_ref, g_ref, m_ref, v_ref, d_ref, nm_ref, nv_ref):
        d, nm, nv = _adam_math(w_ref[...], g_ref[...], m_ref[...], v_ref[...])
        d_ref[...] = d
        nm_ref[...] = nm
        nv_ref[...] = nv

    spec = pl.BlockSpec((t, width), lambda i: (i, 0))
    shape = jax.ShapeDtypeStruct((rows, width), f32)
    return pl.pallas_call(body, name=name, grid=(rows // t,), in_specs=[spec] * 4, out_specs=[spec] * 3,
                          out_shape=[shape] * 3, compiler_params=_cparams(("parallel",)))(w, g, m, v)


def _adamw_slots(name, recvs, row_off, w, m, v):
    depth, rows, width = w.shape
    per_row = width * (depth * N_DEV * recvs[0].dtype.itemsize + 7 * 4) * 2
    t = next(cand for cand in (256, 128, 96, 64, 32, 16)
             if rows % cand == 0 and row_off % cand == 0 and cand * per_row <= VMEM_LIMIT * 2 // 3)
    first = row_off // t

    def body(*refs):
        r_refs = refs[:depth]
        w_ref, m_ref, v_ref, g_ref, d_ref, nm_ref, nv_ref = refs[depth:]
        layer = pl.program_id(0)
        for l, r_ref in enumerate(r_refs):
            @pl.when(layer == l)
            def _():
                g = r_ref[0].astype(f32)
                for s in range(1, N_DEV):
                    g = g + r_ref[s].astype(f32)
                d, nm, nv = _adam_math(w_ref[...], g, m_ref[...], v_ref[...])
                g_ref[...] = g
                d_ref[...] = d
                nm_ref[...] = nm
                nv_ref[...] = nv

    def recv_spec(l):
        return pl.BlockSpec((N_DEV, t, width), lambda layer, i: (0, first + jnp.where(layer == l, i, 0), 0))

    spec = pl.BlockSpec((None, t, width), lambda layer, i: (layer, i, 0))
    shape = jax.ShapeDtypeStruct((depth, rows, width), f32)
    return pl.pallas_call(
        body, name=name, grid=(depth, rows // t), in_specs=[recv_spec(l) for l in range(depth)] + [spec, spec, spec],
        out_specs=[spec] * 4, out_shape=[shape] * 4,
        compiler_params=_cparams(("arbitrary", "arbitrary")))(*recvs, w, m, v)


def _sum_slots(name, buf):
    _, rows, width = buf.shape
    t = next(cand for cand in (512, 256, 128, 64, 32, 16) if rows % cand == 0)

    def body(b_ref, o_ref):
        acc = b_ref[0].astype(f32)
        for s in range(1, N_DEV):
            acc = acc + b_ref[s].astype(f32)
        o_ref[...] = acc

    return pl.pallas_call(
        body, name=name, grid=(rows // t,), in_specs=[pl.BlockSpec((N_DEV, t, width), lambda i: (0, i, 0))],
        out_specs=pl.BlockSpec((t, width), lambda i: (i, 0)), out_shape=jax.ShapeDtypeStruct((rows, width), f32),
        compiler_params=_cparams(("parallel",)))(buf)


BIG = ("w_in", "s5_glu_w", "proj_a", "proj_b", "proj_c", "w_out")
CONV = ("gdn_conv_w", "m2_conv_w")
SMALL = ("norm_w", "gdn_a_log", "gdn_dt_bias", "gdn_norm_w", "s5_lam_re", "s5_lam_im", "s5_log_step",
         "s5_b_re", "s5_b_im", "s5_c_re", "s5_c_im", "s5_d", "s5_glu_b", "m2_conv_b", "m2_a_log", "m2_dt_bias",
         "m2_d", "m2_norm_w", "final_norm_w")
WEIGHTS = ("norm_w", "w_in", "gdn_conv_w", "gdn_a_log", "gdn_dt_bias", "gdn_norm_w", "s5_lam_re", "s5_lam_im",
           "s5_log_step", "s5_b_re", "s5_b_im", "s5_c_re", "s5_c_im", "s5_d", "s5_glu_w", "s5_glu_b", "m2_conv_w",
           "m2_conv_b", "m2_a_log", "m2_dt_bias", "m2_d", "m2_norm_w", "proj_a", "proj_b", "proj_c", "w_out",
           "final_norm_w")


PACK_ROWS = 8
PACK_TILE = 256


def _piece_rows(shape):
    size = 1
    for d in shape:
        size *= d
    rows = -(-size // LANE)
    return size, -(-rows // PACK_ROWS) * PACK_ROWS


def _pack(arrays):
    pieces = []
    for a in arrays:
        size, rows = _piece_rows(a.shape)
        flat = a.reshape(-1)
        if size != rows * LANE:
            flat = jnp.concatenate([flat, jnp.zeros((rows * LANE - size,), f32)])
        pieces.append(flat.reshape(rows, LANE))
    total = sum(p.shape[0] for p in pieces)
    tail = -total % PACK_TILE
    if tail:
        pieces.append(jnp.zeros((tail, LANE), f32))
    return jnp.concatenate(pieces, axis=0)


def _unpack(buf, shapes):
    out, off = [], 0
    for s in shapes:
        size, rows = _piece_rows(s)
        piece = buf[off:off + rows]
        out.append(piece.reshape(s) if size == rows * LANE else piece.reshape(-1)[:size].reshape(s))
        off += rows
    return out


def _win_to_padded(w):
    d = w.shape[0]
    z = lambda n: jnp.zeros((d, n), w.dtype)
    return jnp.concatenate([w[:, 8736:14880], w[:, 4112:5648], w[:, 0:4096], w[:, 5648:8720],
                            w[:, 4096:4112], z(LANE - 16), w[:, 8720:8736], z(LANE - 16), z(2 * LANE)], axis=1)


def _win_from_padded(g):
    u = LANE
    return jnp.concatenate([g[:, U_Q * u:U_CZ * u], g[:, U_SMA * u:U_SMA * u + 16], g[:, U_SU * u:U_Q * u],
                            g[:, U_CZ * u:U_SMA * u], g[:, U_SMC * u:U_SMC * u + 16], g[:, 0:U_SU * u]], axis=1)


def _lane_row(vals, offset):
    n = vals.shape[0]
    return jnp.concatenate([jnp.zeros((offset,), f32), vals, jnp.zeros((LANE - offset - n,), f32)]).reshape(1, LANE)


def _layer_fwd(x, lw, next_shards):
    h = _tok_fwd("rms_fwd", _rms_fn, [(x, D_MODEL, 0)], [lw["norm_w"].reshape(1, D_MODEL)], [(D_MODEL, bf16)], 256)[0]
    if next_shards is None:
        proj, gathered = _mm("proj_fwd", h, lw["w_in_p"], "nn", tk=D_MODEL), None
    else:
        proj, *gathered = _mm("proj_fwd_gather", h, lw["w_in_p"], "nn", tk=D_MODEL, comm=(next_shards, True))
    g_al, g_dt = _lane_row(lw["gdn_a_log"], GDN_HEADS), _lane_row(lw["gdn_dt_bias"], GDN_HEADS)
    g_nw = lw["gdn_norm_w"].reshape(1, HEAD_DIM)
    y_a, ck_a = _gdn_fwd(proj, lw["gdn_conv_w"], g_al, g_dt, g_nw)
    s5p = _s5_prep(lw["s5_lam_re"], lw["s5_lam_im"], lw["s5_log_step"], lw["s5_b_re"], lw["s5_b_im"],
                   lw["s5_c_re"], lw["s5_c_im"], lw["s5_d"])
    y_pre, ck_b = _s5_fwd(proj, *s5p)
    glu_b = lw["s5_glu_b"].reshape(1, S5_WIDTH)
    y_b = _tok_fwd("s5_tail_fwd", _s5_tail_fn, [(y_pre, S5_WIDTH, 0), (proj, S5_WIDTH, U_SG // S5_BLK)],
                   [lw["s5_glu_w"], glu_b], [(S5_WIDTH, bf16)], 256)[0]
    m_al, m_dt, m_d = _lane_row(lw["m2_a_log"], 0), _lane_row(lw["m2_dt_bias"], 0), _lane_row(lw["m2_d"], 0)
    m_cb = lw["m2_conv_b"].reshape(1, -1)
    m_nw = lw["m2_norm_w"].reshape(1, M2_WIDTH)
    y_c, ck_c = _ssd_fwd(proj, lw["m2_conv_w"], m_cb, m_al, m_dt, m_d, m_nw)
    pa = _mm("proj_a_fwd", y_a, lw["proj_a"], "nn")
    pb = _mm("proj_b_fwd", y_b, lw["proj_b"], "nn")
    pc = _mm("proj_c_fwd", y_c, lw["proj_c"], "nn")
    gate_acts = [(proj, D_MODEL, 0), (proj, D_MODEL, 1), (proj, D_MODEL, 2),
                 (pa, D_MODEL, 0), (pb, D_MODEL, 0), (pc, D_MODEL, 0)]
    merged = _tok_fwd("gate_fwd", _gate_fn, gate_acts, [], [(D_MODEL, bf16)], 128)[0]
    x_next = _mm("w_out_fwd", merged, lw["w_out"], "nn", residual=x)
    saved = dict(x=x, h=h, proj=proj, y_a=y_a, ck_a=ck_a, y_pre=y_pre, ck_b=ck_b, y_b=y_b, y_c=y_c, ck_c=ck_c,
                 pa=pa, pb=pb, pc=pc, merged=merged)
    return x_next, saved, gathered


def _layer_bwd(dx_next, lw, sv, later_grads, pack_own=None):
    rows = dx_next.shape[0]
    proj = sv["proj"]
    g = {}
    d_merged = _mm("w_out_bwd_x", dx_next, lw["w_out"], "nt", out_dtype=bf16)
    g["w_out"] = _mm("w_out_bwd_w", sv["merged"], dx_next, "tn", out_dtype=bf16)
    gate_acts = [(proj, D_MODEL, 0), (proj, D_MODEL, 1), (proj, D_MODEL, 2),
                 (sv["pa"], D_MODEL, 0), (sv["pb"], D_MODEL, 0), (sv["pc"], D_MODEL, 0)]
    dla, dlb, dlc, dpa, dpb, dpc = _tok_bwd(
        "gate_bwd", _gate_fn, gate_acts, [], [(d_merged, D_MODEL, 0)],
        [(0, bf16), (1, bf16), (2, bf16), (3, bf16), (4, bf16), (5, bf16)], 128)
    dy_a = _mm("proj_a_bwd_x", dpa, lw["proj_a"], "nt", out_dtype=bf16)
    dy_b = _mm("proj_b_bwd_x", dpb, lw["proj_b"], "nt")
    dy_c = _mm("proj_c_bwd_x", dpc, lw["proj_c"], "nt", out_dtype=bf16)
    g["proj_a"] = _mm("proj_a_bwd_w", sv["y_a"], dpa, "tn", out_dtype=bf16)
    g["proj_b"] = _mm("proj_b_bwd_w", sv["y_b"], dpb, "tn", out_dtype=bf16)
    g["proj_c"] = _mm("proj_c_bwd_w", sv["y_c"], dpc, "tn", out_dtype=bf16)

    m_al, m_dt, m_d = _lane_row(lw["m2_a_log"], 0), _lane_row(lw["m2_dt_bias"], 0), _lane_row(lw["m2_d"], 0)
    m_cb = lw["m2_conv_b"].reshape(1, -1)
    m_nw = lw["m2_norm_w"].reshape(1, M2_WIDTH)
    (dcx, dcb, dcc, dcz, dsmc, dcwx, dcwb, dcwc, dbx, dbb, dbc, dal, ddt, ddk, dnw) = _ssd_bwd(
        proj, dy_c, sv["ck_c"], lw["m2_conv_w"], m_cb, m_al, m_dt, m_d, m_nw)
    g["m2_conv_w"] = jnp.concatenate([dcwx, dcwb, dcwc], axis=1)
    g["m2_conv_b"] = jnp.concatenate([dbx, dbb, dbc], axis=1).reshape(-1)
    g["m2_a_log"] = jnp.sum(dal, axis=(0, 1))[:M2_HEADS]
    g["m2_dt_bias"] = jnp.sum(ddt, axis=(0, 1))[:M2_HEADS]
    g["m2_d"] = jnp.sum(ddk, axis=(0, 1))[:M2_HEADS]
    g["m2_norm_w"] = dnw.reshape(-1)
    dsmc = jnp.sum(dsmc, axis=0)

    glu_b = lw["s5_glu_b"].reshape(1, S5_WIDTH)
    dypre, dsg, dglu_w, dglu_b = _tok_bwd(
        "s5_tail_bwd", _s5_tail_fn, [(sv["y_pre"], S5_WIDTH, 0), (proj, S5_WIDTH, U_SG // S5_BLK)],
        [lw["s5_glu_w"], glu_b], [(dy_b, S5_WIDTH, 0)], [(0, f32), (1, bf16)], 256)
    g["s5_glu_w"] = dglu_w
    g["s5_glu_b"] = dglu_b.reshape(-1)
    s5_names = ("s5_lam_re", "s5_lam_im", "s5_log_step", "s5_b_re", "s5_b_im", "s5_c_re", "s5_c_im", "s5_d")
    s5p, s5_vjp = jax.vjp(_s5_prep, *[lw[k] for k in s5_names])
    dsu, da, dbbr, dbbi, dccr, dcci, dd = _s5_bwd(proj, dypre, sv["ck_b"], *s5p)
    for k, val in zip(s5_names, s5_vjp((da, dbbr, dbbi, dccr, dcci, dd))):
        g[k] = val

    g_al, g_dt = _lane_row(lw["gdn_a_log"], GDN_HEADS), _lane_row(lw["gdn_dt_bias"], GDN_HEADS)
    g_nw = lw["gdn_norm_w"].reshape(1, HEAD_DIM)
    (dq, dk, dv, daz, dsma, dcwq, dcwk, dcwv, dgal, dgdt, dgnw, *arrived) = _gdn_bwd(
        proj, dy_a, sv["ck_a"], lw["gdn_conv_w"], g_al, g_dt, g_nw, scatter=later_grads)
    g["gdn_conv_w"] = jnp.concatenate([dcwq, dcwk, dcwv], axis=1)
    g["gdn_a_log"] = jnp.sum(dgal, axis=(0, 1))[GDN_HEADS:2 * GDN_HEADS]
    g["gdn_dt_bias"] = jnp.sum(dgdt, axis=(0, 1))[GDN_HEADS:2 * GDN_HEADS]
    g["gdn_norm_w"] = jnp.sum(dgnw, axis=(0, 1))
    dsma = jnp.sum(dsma, axis=0)

    dproj = jnp.concatenate([dla, dlb, dlc, dsu, dsg, dq, dk, dv, daz, dcz, dcx, dcb, dcc, dsma.astype(bf16),
                             dsmc.astype(bf16), jnp.zeros((rows, 2 * LANE), bf16)], axis=1)
    if pack_own is None:
        g["w_in_p"] = _mm("proj_bwd_w", sv["h"], dproj, "tn", out_dtype=bf16)
        dh, own_arrived = _mm("proj_bwd_x", dproj, lw["w_in_p"], "nt"), None
    else:
        g["w_in_p"], *rest_arrived = _mm("proj_bwd_w_scatter", sv["h"], dproj, "tn", out_dtype=bf16,
                                         comm=(pack_own(g, False), False))
        dh, *win_arrived = _mm("proj_bwd_x_scatter", dproj, lw["w_in_p"], "nt", comm=(pack_own(g, True), False))
        own_arrived = win_arrived + rest_arrived
    dx, dnorm = _tok_bwd("rms_bwd", _rms_fn, [(sv["x"], D_MODEL, 0)], [lw["norm_w"].reshape(1, D_MODEL)],
                         [(dh, D_MODEL, 0)], [(0, f32)], 256, residuals={0: dx_next})
    g["norm_w"] = dnorm.reshape(-1)
    return dx, g, (arrived if later_grads is not None else None), own_arrived


def kernel(x, norm_w, w_in, gdn_conv_w, gdn_a_log, gdn_dt_bias, gdn_norm_w, s5_lam_re, s5_lam_im, s5_log_step, s5_b_re, s5_b_im, s5_c_re, s5_c_im, s5_d, s5_glu_w, s5_glu_b, m2_conv_w, m2_conv_b, m2_a_log, m2_dt_bias, m2_d, m2_norm_w, proj_a, proj_b, proj_c, w_out, final_norm_w, loss_target, m_norm_w, m_w_in, m_gdn_conv_w, m_gdn_a_log, m_gdn_dt_bias, m_gdn_norm_w, m_s5_lam_re, m_s5_lam_im, m_s5_log_step, m_s5_b_re, m_s5_b_im, m_s5_c_re, m_s5_c_im, m_s5_d, m_s5_glu_w, m_s5_glu_b, m_m2_conv_w, m_m2_conv_b, m_m2_a_log, m_m2_dt_bias, m_m2_d, m_m2_norm_w, m_proj_a, m_proj_b, m_proj_c, m_w_out, m_final_norm_w, v_norm_w, v_w_in, v_gdn_conv_w, v_gdn_a_log, v_gdn_dt_bias, v_gdn_norm_w, v_s5_lam_re, v_s5_lam_im, v_s5_log_step, v_s5_b_re, v_s5_b_im, v_s5_c_re, v_s5_c_im, v_s5_d, v_s5_glu_w, v_s5_glu_b, v_m2_conv_w, v_m2_conv_b, v_m2_a_log, v_m2_dt_bias, v_m2_d, v_m2_norm_w, v_proj_a, v_proj_b, v_proj_c, v_w_out, v_final_norm_w):
    w = dict(norm_w=norm_w, w_in=w_in, gdn_conv_w=gdn_conv_w, gdn_a_log=gdn_a_log, gdn_dt_bias=gdn_dt_bias,
             gdn_norm_w=gdn_norm_w, s5_lam_re=s5_lam_re, s5_lam_im=s5_lam_im, s5_log_step=s5_log_step,
             s5_b_re=s5_b_re, s5_b_im=s5_b_im, s5_c_re=s5_c_re, s5_c_im=s5_c_im, s5_d=s5_d, s5_glu_w=s5_glu_w,
             s5_glu_b=s5_glu_b, m2_conv_w=m2_conv_w, m2_conv_b=m2_conv_b, m2_a_log=m2_a_log, m2_dt_bias=m2_dt_bias,
             m2_d=m2_d, m2_norm_w=m2_norm_w, proj_a=proj_a, proj_b=proj_b, proj_c=proj_c, w_out=w_out,
             final_norm_w=final_norm_w)
    mom = dict(norm_w=m_norm_w, w_in=m_w_in, gdn_conv_w=m_gdn_conv_w, gdn_a_log=m_gdn_a_log,
               gdn_dt_bias=m_gdn_dt_bias, gdn_norm_w=m_gdn_norm_w, s5_lam_re=m_s5_lam_re, s5_lam_im=m_s5_lam_im,
               s5_log_step=m_s5_log_step, s5_b_re=m_s5_b_re, s5_b_im=m_s5_b_im, s5_c_re=m_s5_c_re,
               s5_c_im=m_s5_c_im, s5_d=m_s5_d, s5_glu_w=m_s5_glu_w, s5_glu_b=m_s5_glu_b, m2_conv_w=m_m2_conv_w,
               m2_conv_b=m_m2_conv_b, m2_a_log=m_m2_a_log, m2_dt_bias=m_m2_dt_bias, m2_d=m_m2_d,
               m2_norm_w=m_m2_norm_w, proj_a=m_proj_a, proj_b=m_proj_b, proj_c=m_proj_c, w_out=m_w_out,
               final_norm_w=m_final_norm_w)
    var = dict(norm_w=v_norm_w, w_in=v_w_in, gdn_conv_w=v_gdn_conv_w, gdn_a_log=v_gdn_a_log,
               gdn_dt_bias=v_gdn_dt_bias, gdn_norm_w=v_gdn_norm_w, s5_lam_re=v_s5_lam_re, s5_lam_im=v_s5_lam_im,
               s5_log_step=v_s5_log_step, s5_b_re=v_s5_b_re, s5_b_im=v_s5_b_im, s5_c_re=v_s5_c_re,
               s5_c_im=v_s5_c_im, s5_d=v_s5_d, s5_glu_w=v_s5_glu_w, s5_glu_b=v_s5_glu_b, m2_conv_w=v_m2_conv_w,
               m2_conv_b=v_m2_conv_b, m2_a_log=v_m2_a_log, m2_dt_bias=v_m2_dt_bias, m2_d=v_m2_d,
               m2_norm_w=v_m2_norm_w, proj_a=v_proj_a, proj_b=v_proj_b, proj_c=v_proj_c, w_out=v_w_out,
               final_norm_w=v_final_norm_w)
    me = lax.axis_index("x") * 4 + lax.axis_index("y") * 2 + lax.axis_index("c")
    x2 = x[0]
    tgt = loss_target[0]

    ra, rb = proj_a.shape[1], proj_b.shape[1]
    pabc = jnp.concatenate([proj_a, proj_b, proj_c], axis=1).astype(bf16)
    w_in16, glu16, w_out16 = w_in.astype(bf16), s5_glu_w.astype(bf16), w_out.astype(bf16)

    def shards(i):
        return [w_in16[i], glu16[i], pabc[i], w_out16[i], gdn_conv_w[i], m2_conv_w[i]]

    def cols(gathered):
        return jnp.concatenate([gathered[d] for d in range(N_DEV)], axis=1)

    def rows_of(gathered):
        return gathered.reshape(-1, gathered.shape[-1])

    def full_weights(i, gathered):
        g_win, g_glu, g_pabc, g_wout, g_gcv, g_mcv = gathered
        pf = cols(g_pabc)
        lw = dict(w_in_p=_win_to_padded(cols(g_win)), s5_glu_w=rows_of(g_glu), proj_a=pf[:ra],
                  proj_b=pf[ra:ra + rb], proj_c=pf[ra + rb:], w_out=rows_of(g_wout),
                  gdn_conv_w=cols(g_gcv), m2_conv_w=cols(g_mcv))
        for k in SMALL:
            if k != "final_norm_w":
                lw[k] = w[k][i]
        return lw

    layers, saved = [], []
    act = x2
    gathered = _exchange("gather_weights", shards(0), True)
    for i in range(DEPTH):
        layers.append(full_weights(i, gathered))
        act, sv, gathered = _layer_fwd(act, layers[i], shards(i + 1) if i + 1 < DEPTH else None)
        saved.append(sv)
    loss_row, dact, dfinal = _loss_grad(act, final_norm_w.reshape(1, D_MODEL), tgt)
    loss = lax.psum(loss_row[0, 0], ("x", "y", "c"))

    def col_blocks(full):
        r = full.shape[0]
        return full.reshape(r, N_DEV, -1).transpose(1, 0, 2)

    def row_blocks(full):
        return full.reshape(N_DEV, -1, full.shape[1])

    def packed_grads(g, w_in_part=None):
        first = [col_blocks(_win_from_padded(g["w_in_p"]))] if w_in_part in (None, True) else []
        if w_in_part is True:
            return first
        return first + [row_blocks(g["s5_glu_w"].astype(bf16)),
                        col_blocks(jnp.concatenate([g["proj_a"], g["proj_b"], g["proj_c"]], axis=0)),
                        row_blocks(g["w_out"])]

    grads, arrived = [None] * DEPTH, [None] * DEPTH
    later = None
    for i in reversed(range(DEPTH)):
        dact, grads[i], got, own = _layer_bwd(dact, layers[i], saved[i], later, packed_grads if i == 0 else None)
        if got is not None:
            arrived[i + 1] = got
        saved[i] = None
        if i > 0:
            later = packed_grads(grads[i])
    arrived[0] = own

    g_out, delta, new_m, new_v = {}, {}, {}, {}
    for k, buf, row_off in (("w_in", 0, 0), ("s5_glu_w", 1, 0), ("proj_a", 2, 0), ("proj_b", 2, ra),
                            ("proj_c", 2, ra + rb), ("w_out", 3, 0)):
        recvs = [arrived[i][buf] for i in range(DEPTH)]
        g_out[k], delta[k], new_m[k], new_v[k] = _adamw_slots("adamw_" + k, recvs, row_off, w[k], mom[k], var[k])

    names_small = [k for k in SMALL if k != "final_norm_w"]
    small_parts = [jnp.stack([grads[i][k] for i in range(DEPTH)]) for k in names_small + list(CONV)]
    small_parts.append(dfinal.reshape(-1))
    small_shapes = [p.shape for p in small_parts]
    small_local = _pack(small_parts)
    small_sum = _sum_slots("sum_small", _exchange("gather_small", [small_local], True)[0])
    unp = _unpack(small_sum, small_shapes)
    for k, val in zip(names_small, unp):
        g_out[k] = val
    for j, k in enumerate(CONV):
        width = w[k].shape[2]
        g_out[k] = lax.dynamic_slice_in_dim(unp[len(names_small) + j], me * width, width, axis=2)
    g_out["final_norm_w"] = unp[-1]

    rest = [k for k in WEIGHTS if k not in BIG]
    rest_shapes = [w[k].shape for k in rest]
    packed = [_pack([src[k] for k in rest]) for src in (w, g_out, mom, var)]
    d, nm, nv = _adamw("adamw_small", *packed)
    for dst, arr in ((delta, d), (new_m, nm), (new_v, nv)):
        for k, val in zip(rest, _unpack(arr, rest_shapes)):
            dst[k] = val

    grad_x = dact.reshape(x.shape)
    return (loss, grad_x, *[g_out[k] for k in WEIGHTS], *[delta[k] for k in WEIGHTS],
            *[new_m[k] for k in WEIGHTS], *[new_v[k] for k in WEIGHTS])
```

```python
import functools

import jax
import jax.numpy as jnp
from jax import lax
from jax.experimental import pallas as pl
from jax.experimental.pallas import tpu as pltpu

f32 = jnp.float32
bf16 = jnp.bfloat16

N_DEV = 8
DEPTH = 4
D_MODEL = 2048
GDN_HEADS = 8
HEAD_DIM = 128
GDN_WIDTH = 1024
S5_GROUPS = 48
S5_GROUP_SIZE = 16
S5_STATE = 64
S5_WIDTH = 768
S5_LANES = S5_GROUPS * S5_STATE
S5_BLK = 6
M2_HEADS = 16
M2_HEAD_DIM = 64
M2_WIDTH = 1024
M2_GROUPS = 4
M2_STATE = 128
CONV_K = 4
CHUNK = 64
HALO = 8
NORM_EPS = 1e-6
IN_DIM = 14880
LANE = 128
VMEM_LIMIT = 48 * 1024 * 1024

ADAM_LR = 0.001
ADAM_B1 = 0.9
ADAM_B2 = 0.999
ADAM_EPS = 1e-08
ADAM_WD = 0.01
ADAM_STEP = 10

U_MERGE, U_SU, U_SG, U_Q, U_K, U_V, U_AZ, U_CZ, U_CX, U_CB, U_CC, U_SMA, U_SMC = (
    0, 48, 54, 60, 68, 76, 84, 92, 100, 108, 112, 116, 117)
NP_UNITS = 120
NP_COLS = NP_UNITS * LANE


def _cparams(sem=None):
    return pltpu.CompilerParams(dimension_semantics=sem, vmem_limit_bytes=VMEM_LIMIT)


def _pick(dim, target):
    if dim <= target:
        return dim
    for t in range(target - target % LANE, 0, -LANE):
        if dim % t == 0:
            return t
    raise ValueError(f"no tile for {dim}")


def _bd(a, b, dims):
    return lax.dot_general(a.astype(bf16), b.astype(bf16), (dims, ((), ())), preferred_element_type=f32)


@jax.custom_vjp
def dot_nn(a, b):
    return _bd(a, b, ((1,), (0,)))


@jax.custom_vjp
def dot_nt(a, b):
    return _bd(a, b, ((1,), (1,)))


@jax.custom_vjp
def dot_tn(a, b):
    return _bd(a, b, ((0,), (0,)))


dot_nn.defvjp(lambda a, b: (dot_nn(a, b), (a, b)), lambda r, ct: (dot_nt(ct, r[1]), dot_tn(r[0], ct)))
dot_nt.defvjp(lambda a, b: (dot_nt(a, b), (a, b)), lambda r, ct: (dot_nn(ct, r[1]), dot_tn(ct, r[0])))
dot_tn.defvjp(lambda a, b: (dot_tn(a, b), (a, b)), lambda r, ct: (dot_nt(r[1], ct), dot_nn(r[0], ct)))

_HI = lax.Precision.HIGHEST


def _silu(x):
    return x * jax.nn.sigmoid(x)


def _softplus(x):
    return jnp.maximum(x, 0.0) + jnp.log(1.0 + jnp.exp(-jnp.abs(x)))


def _tri_masks(c):
    row = lax.broadcasted_iota(jnp.int32, (c, c), 0)
    col = lax.broadcasted_iota(jnp.int32, (c, c), 1)
    return row >= col, row > col, (row >= col).astype(f32), (row == col).astype(f32)


def _scan_add(x, reverse):
    t = x.shape[0]
    row = lax.broadcasted_iota(jnp.int32, x.shape, 0)
    d = 1
    while d < t:
        if reverse:
            x = x + jnp.where(row < t - d, pltpu.roll(x, t - d, 0), 0.0)
        else:
            x = x + jnp.where(row >= d, pltpu.roll(x, d, 0), 0.0)
        d *= 2
    return x


@jax.custom_vjp
def _cumsum_rows(x):
    return _scan_add(x, False)


_cumsum_rows.defvjp(lambda x: (_scan_add(x, False), None), lambda _, ct: (_scan_add(ct, True),))


def _cumsum_all(a):
    cum = _cumsum_rows(a)
    return cum, cum.T


def _onehot_lane(idx):
    return (lax.broadcasted_iota(jnp.int32, (1, LANE), 1) == idx).astype(f32)


def _onehot_sub(idx):
    return (lax.broadcasted_iota(jnp.int32, (LANE, 1), 0) == idx).astype(f32)


def _pick_col(x, idx):
    return jnp.sum(x * _onehot_lane(idx), axis=-1, keepdims=True)


def _pick_row(xt, idx):
    return jnp.sum(xt * _onehot_sub(idx), axis=0, keepdims=True)


def _peer(idx):
    return (idx // 4, (idx // 2) % 2, idx % 2)


def _remote(src, dst, send_sems, recv_sems, k, to):
    return pltpu.make_async_remote_copy(src_ref=src, dst_ref=dst, send_sem=send_sems.at[k], recv_sem=recv_sems.at[k],
                                        device_id=_peer(to), device_id_type=pl.DeviceIdType.MESH)


def _scatter_phase(phase, srcs, outs, send_sems, recv_sems, local_sems):
    nb = len(srcs)
    me = lax.axis_index("x") * 4 + lax.axis_index("y") * 2 + lax.axis_index("c")
    if phase == "forward":
        return
    mine = [pltpu.make_async_copy(srcs[b].at[me], outs[b].at[me], local_sems.at[b]) for b in range(nb)]
    sends = [_remote(srcs[b].at[(me + k) % N_DEV], outs[b].at[me], send_sems, recv_sems, (k - 1) * nb + b,
                     (me + k) % N_DEV) for k in range(1, N_DEV) for b in range(nb)]
    if phase == "start":
        for cp in mine + sends:
            cp.start()
    if phase == "finish":
        for k in range(1, N_DEV):
            frm = (me + N_DEV - k) % N_DEV
            for b in range(nb):
                _remote(srcs[b].at[me], outs[b].at[frm], send_sems, recv_sems, (k - 1) * nb + b, frm).wait_recv()
        for cp in sends:
            cp.wait_send()
        for cp in mine:
            cp.wait()


def _gather_phase(phase, srcs, outs, send_sems, recv_sems, local_sems):
    nb = len(srcs)
    x, y, c = lax.axis_index("x"), lax.axis_index("y"), lax.axis_index("c")
    me, sib = 4 * x + 2 * y + c, 4 * x + 2 * y + 1 - c
    chips = [(1 - x, y), (x, 1 - y), (1 - x, 1 - y)]
    same = [4 * cx + 2 * cy + c for cx, cy in chips]
    other = [4 * cx + 2 * cy + 1 - c for cx, cy in chips]

    def copy(b, k, src, block, to):
        return _remote(src, outs[b].at[block], send_sems, recv_sems, k * nb + b, to)

    def local():
        return [pltpu.make_async_copy(srcs[b], outs[b].at[me], local_sems.at[b]) for b in range(nb)]

    def first():
        return ([copy(b, 0, srcs[b], me, sib) for b in range(nb)]
                + [copy(b, 1 + j, srcs[b], me, same[j]) for j in range(3) for b in range(nb)])

    def passed(j, b):
        return copy(b, 4 + j, outs[b].at[same[j]], same[j], sib)

    if phase == "start":
        for cp in local() + first():
            cp.start()
    if phase == "forward":
        for j in range(3):
            for b in range(nb):
                copy(b, 1 + j, srcs[b], same[j], me).wait_recv()
                passed(j, b).start()
    if phase == "finish":
        for b in range(nb):
            copy(b, 0, srcs[b], sib, me).wait_recv()
        for j in range(3):
            for b in range(nb):
                copy(b, 4 + j, srcs[b], other[j], me).wait_recv()
        for cp in first() + [passed(j, b) for j in range(3) for b in range(nb)]:
            cp.wait_send()
        for cp in local():
            cp.wait()


def _comm_phase(phase, *refs, gather):
    (_gather_phase if gather else _scatter_phase)(phase, *refs)


def _comm_shapes(bufs, gather):
    nb = len(bufs)
    out_shape = [jax.ShapeDtypeStruct(((N_DEV,) + tuple(b.shape)) if gather else tuple(b.shape), b.dtype) for b in bufs]
    sems = [pltpu.SemaphoreType.DMA(((N_DEV - 1) * nb,)), pltpu.SemaphoreType.DMA(((N_DEV - 1) * nb,)),
            pltpu.SemaphoreType.DMA((nb,))]
    return out_shape, sems


def _exchange(name, bufs, gather):
    nb = len(bufs)

    def body(*refs):
        args = (refs[:nb], refs[nb:2 * nb], *refs[2 * nb:])
        for phase in ("start", "forward", "finish"):
            _comm_phase(phase, *args, gather=gather)

    any_spec = pl.BlockSpec(memory_space=pl.ANY)
    out_shape, sems = _comm_shapes(bufs, gather)
    return pl.pallas_call(body, name=name, in_specs=[any_spec] * nb, out_specs=[any_spec] * nb, out_shape=out_shape,
                          scratch_shapes=sems)(*bufs)


def _mm(name, a, b, mode, residual=None, out_dtype=f32, tm=1024, tn=1024, tk=1024, comm=None):
    if mode == "nn":
        (m, k), (_, n) = a.shape, b.shape
    elif mode == "nt":
        (m, k), (n, _) = a.shape, b.shape
    else:
        (k, m), (_, n) = a.shape, b.shape
    tm, tn, tk = _pick(m, tm), _pick(n, tn), _pick(k, tk)
    gm, gn, nk = m // tm, n // tn, k // tk
    dims = {"nn": ((1,), (0,)), "nt": ((1,), (1,)), "tn": ((0,), (0,))}[mode]
    has_res = residual is not None
    bufs, gather = comm if comm is not None else ([], True)
    nb = len(bufs)
    n_in = 2 + has_res
    forward_step = gm * gn * nk * 3 // 4

    def body(*refs):
        a_ref, b_ref = refs[:2]
        r_ref = refs[2] if has_res else None
        srcs = refs[n_in:n_in + nb]
        o_ref = refs[n_in + nb]
        outs = refs[n_in + nb + 1:n_in + 2 * nb + 1]
        scratch = refs[n_in + 2 * nb + 1:]
        acc_ref = scratch[0] if nk > 1 else None
        sems = scratch[1:] if nk > 1 else scratch
        i, j, kk = pl.program_id(0), pl.program_id(1), pl.program_id(2)

        step = (i * gn + j) * nk + kk
        if nb:
            @pl.when(step == 0)
            def _():
                _comm_phase("start", srcs, outs, *sems, gather=gather)

            @pl.when(step == forward_step)
            def _():
                _comm_phase("forward", srcs, outs, *sems, gather=gather)

        def finish(out):
            if has_res:
                out = out + r_ref[...].astype(f32)
            o_ref[...] = out.astype(o_ref.dtype)

        if nk == 1:
            finish(_bd(a_ref[...], b_ref[...], dims))
        else:
            @pl.when(kk == 0)
            def _():
                acc_ref[...] = jnp.zeros_like(acc_ref)

            acc_ref[...] += _bd(a_ref[...], b_ref[...], dims)

            @pl.when(kk == nk - 1)
            def _():
                finish(acc_ref[...])

        if nb:
            @pl.when(step == gm * gn * nk - 1)
            def _():
                _comm_phase("finish", srcs, outs, *sems, gather=gather)

    if mode == "tn":
        a_spec = pl.BlockSpec((tk, tm), lambda i, j, kk: (kk, i))
    else:
        a_spec = pl.BlockSpec((tm, tk), lambda i, j, kk: (i, kk))
    if mode == "nt":
        b_spec = pl.BlockSpec((tn, tk), lambda i, j, kk: (j, kk))
    else:
        b_spec = pl.BlockSpec((tk, tn), lambda i, j, kk: (kk, j))
    o_spec = pl.BlockSpec((tm, tn), lambda i, j, kk: (i, j))
    any_spec = pl.BlockSpec(memory_space=pl.ANY)
    in_specs = [a_spec, b_spec] + ([o_spec] if has_res else []) + [any_spec] * nb
    args = (a, b) + ((residual,) if has_res else ()) + tuple(bufs)
    comm_shapes, comm_sems = _comm_shapes(bufs, gather) if nb else ([], [])
    res = pl.pallas_call(
        body, name=name, grid=(gm, gn, nk), in_specs=in_specs, out_specs=[o_spec] + [any_spec] * nb,
        out_shape=[jax.ShapeDtypeStruct((m, n), out_dtype)] + comm_shapes,
        scratch_shapes=([pltpu.VMEM((tm, tn), f32)] if nk > 1 else []) + comm_sems,
        compiler_params=_cparams(("arbitrary",) * 3 if nb else ("parallel", "parallel", "arbitrary")))(*args)
    return res if nb else res[0]


def _act_spec(t, width, colblk):
    return pl.BlockSpec((t, width), lambda i: (i, colblk))


def _tok_fwd(name, fn, acts, params, outs, t):
    rows = acts[0][0].shape[0]
    t = min(t, rows)
    na, npar = len(acts), len(params)

    def body(*refs):
        a = [r[...].astype(f32) for r in refs[:na]]
        p = [r[...].astype(f32) for r in refs[na:na + npar]]
        res = fn(*a, *p)
        for o_ref, o in zip(refs[na + npar:], res):
            o_ref[...] = o.astype(o_ref.dtype)

    in_specs = [_act_spec(t, w, cb) for (_, w, cb) in acts]
    in_specs += [pl.BlockSpec(p.shape, lambda i: (0, 0)) for p in params]
    res = pl.pallas_call(
        body, name=name, grid=(rows // t,), in_specs=in_specs,
        out_specs=[_act_spec(t, w, 0) for (w, _) in outs],
        out_shape=[jax.ShapeDtypeStruct((rows, w), dt) for (w, dt) in outs],
        compiler_params=_cparams(("arbitrary",)))(*[a for (a, _, _) in acts], *params)
    return res


def _tok_bwd(name, fn, acts, params, cts, dact, t, residuals=None):
    rows = acts[0][0].shape[0]
    t = min(t, rows)
    residuals = residuals or {}
    res_ids = sorted(residuals)
    na, npar, nc, nr, nd = len(acts), len(params), len(cts), len(res_ids), len(dact)

    def body(*refs):
        a = [r[...].astype(f32) for r in refs[:na]]
        p = [r[...].astype(f32) for r in refs[na:na + npar]]
        ct = tuple(r[...].astype(f32) for r in refs[na + npar:na + npar + nc])
        rs = {idx: r[...].astype(f32) for idx, r in zip(res_ids, refs[na + npar + nc:na + npar + nc + nr])}
        orefs = refs[na + npar + nc + nr:]
        _, vjp = jax.vjp(fn, *a, *p)
        grads = vjp(ct)
        for o_ref, (idx, _) in zip(orefs[:nd], dact):
            g = grads[idx]
            if idx in rs:
                g = g + rs[idx]
            o_ref[...] = g.astype(o_ref.dtype)

        if npar:
            @pl.when(pl.program_id(0) == 0)
            def _():
                for o_ref in orefs[nd:]:
                    o_ref[...] = jnp.zeros_like(o_ref)

            for o_ref, g in zip(orefs[nd:], grads[na:]):
                o_ref[...] += g

    in_specs = [_act_spec(t, w, cb) for (_, w, cb) in acts]
    in_specs += [pl.BlockSpec(p.shape, lambda i: (0, 0)) for p in params]
    in_specs += [_act_spec(t, w, cb) for (_, w, cb) in cts]
    in_specs += [_act_spec(t, acts[idx][1], 0) for idx in res_ids]
    out_specs = [_act_spec(t, acts[idx][1], 0) for (idx, _) in dact]
    out_specs += [pl.BlockSpec(p.shape, lambda i: (0, 0)) for p in params]
    out_shape = [jax.ShapeDtypeStruct((rows, acts[idx][1]), dt) for (idx, dt) in dact]
    out_shape += [jax.ShapeDtypeStruct(p.shape, f32) for p in params]
    return pl.pallas_call(
        body, name=name, grid=(rows // t,), in_specs=in_specs, out_specs=out_specs, out_shape=out_shape,
        compiler_params=_cparams(("arbitrary",)))(
            *[a for (a, _, _) in acts], *params, *[c for (c, _, _) in cts], *[residuals[i] for i in res_ids])


def _rms_fn(x, w):
    return (x * lax.rsqrt(jnp.mean(x * x, axis=-1, keepdims=True) + NORM_EPS) * w,)


def _gate_fn(la, lb, lc, pa, pb, pc):
    return (jax.nn.sigmoid(la) * pa + jax.nn.sigmoid(lb) * pb + jax.nn.sigmoid(lc) * pc,)


def _s5_tail_fn(ypre, gate, glu_w, glu_b):
    y = jax.nn.gelu(ypre)
    y = y * jax.nn.sigmoid(dot_nn(y, glu_w) + glu_b)
    return (y * _silu(gate),)


def _loss_grad(x, w, target, t=256):
    rows, d = x.shape
    t = min(t, rows)

    def fn(xt, wt, tt):
        y = _rms_fn(xt, wt)[0]
        err = y - tt
        return 0.5 * jnp.sum(jnp.sum(err * err, axis=-1, keepdims=True), axis=0, keepdims=True) / d

    def body(x_ref, w_ref, t_ref, loss_ref, dx_ref, dw_ref):
        tt = t_ref[...]
        val, vjp = jax.vjp(lambda a, b: fn(a, b, tt), x_ref[...], w_ref[...])
        dx, dw = vjp(jnp.ones((1, 1), f32))
        dx_ref[...] = dx

        @pl.when(pl.program_id(0) == 0)
        def _():
            loss_ref[...] = jnp.zeros_like(loss_ref)
            dw_ref[...] = jnp.zeros_like(dw_ref)

        loss_ref[...] += val * jnp.ones((1, LANE), f32)
        dw_ref[...] += dw

    return pl.pallas_call(
        body, name="loss_grad", grid=(rows // t,),
        in_specs=[_act_spec(t, d, 0), pl.BlockSpec((1, d), lambda i: (0, 0)), _act_spec(t, d, 0)],
        out_specs=[pl.BlockSpec((1, LANE), lambda i: (0, 0)), _act_spec(t, d, 0), pl.BlockSpec((1, d), lambda i: (0, 0))],
        out_shape=[jax.ShapeDtypeStruct((1, LANE), f32), jax.ShapeDtypeStruct((rows, d), f32),
                   jax.ShapeDtypeStruct((1, d), f32)],
        compiler_params=_cparams(("arbitrary",)))(x, w, target)


GDN_HB = 4
GDN_SW = GDN_HB * HEAD_DIM
GDN_STEPS = GDN_HEADS // GDN_HB


def _tri_inv(a, eye, c):
    rows = a.shape[0]
    n = -a
    p = eye + n
    npow = dot_nn(n, n)
    levels = c.bit_length() - 1
    for j in range(2, levels):
        both = dot_nn(jnp.concatenate([p, npow], axis=0), npow)
        p, npow = p + both[:rows], both[rows:]
    return p + dot_nn(p, npow)


def _block_ids(rows, c):
    ri = lax.broadcasted_iota(jnp.int32, (rows, rows), 0)
    ci = lax.broadcasted_iota(jnp.int32, (rows, rows), 1)
    r1 = lax.broadcasted_iota(jnp.int32, (rows, 1), 0)
    rb, cb, r1b = 0, 0, 0
    for edge in range(c, rows, c):
        rb = rb + (ri >= edge).astype(jnp.int32)
        cb = cb + (ci >= edge).astype(jnp.int32)
        r1b = r1b + (r1 >= edge).astype(jnp.int32)
    return ri, ci, rb, cb, r1b


def _gdn_step(qc, kc, vc, z, small, alog_row, dtb_row, normw, s_cat, head0):
    c = qc.shape[0]
    hb = GDN_HB
    rows = hb * c

    def stack(x):
        return jnp.concatenate([x[:, r * HEAD_DIM:(r + 1) * HEAD_DIM] for r in range(hb)], axis=0)

    ri, ci, rb, cb, r1b = _block_ids(rows, c)
    same = rb == cb
    causal = same & (ri >= ci)
    strict = same & (ri > ci)
    eye = (ri == ci).astype(f32)
    head_rows = [(r1b == r).astype(f32) for r in range(hb)]

    def own_block(x):
        acc = None
        for r in range(hb):
            term = x[:, r * HEAD_DIM:(r + 1) * HEAD_DIM] * head_rows[r]
            acc = term if acc is None else acc + term
        return acc

    beta_all = jax.nn.sigmoid(small)
    g_all = -jnp.exp(alog_row) * _softplus(small + dtb_row)
    gc_all, gct_all = _cumsum_all(g_all)
    beta = jnp.concatenate([_pick_col(beta_all, head0 + r) for r in range(hb)], axis=0)
    gc = jnp.concatenate([_pick_col(gc_all, head0 + r + GDN_HEADS) for r in range(hb)], axis=0)
    gc_t = jnp.concatenate([_pick_row(gct_all, head0 + r + GDN_HEADS) for r in range(hb)], axis=1)
    g_last = [gc[(r + 1) * c - 1:(r + 1) * c, :] for r in range(hb)]
    gl = sum(head_rows[r] * g_last[r] for r in range(hb))

    q = _silu(stack(qc))
    k = _silu(stack(kc))
    v = _silu(stack(vc))
    q = q * lax.rsqrt(jnp.sum(q * q, axis=-1, keepdims=True) + NORM_EPS) * (HEAD_DIM ** -0.5)
    k = k * lax.rsqrt(jnp.sum(k * k, axis=-1, keepdims=True) + NORM_EPS)
    decay = jnp.exp(jnp.where(causal, gc - gc_t, -1e30))
    egc = jnp.exp(gc)
    kb = k * beta
    a_mat = jnp.where(strict, dot_nt(kb, k) * decay, 0.0)
    t_inv = _tri_inv(a_mat, eye, c)
    uw = dot_nn(t_inv, jnp.concatenate([v * beta, kb * egc], axis=1))
    u, w = uw[:, :HEAD_DIM], uw[:, HEAD_DIM:]
    qk = dot_nt(q, k) * decay
    on_state = dot_nn(jnp.concatenate([w, q * egc], axis=0), s_cat)
    v_new = u - own_block(on_state[:rows])
    out = own_block(on_state[rows:]) + dot_nn(qk, v_new)
    k_tail = k * jnp.exp(gl - gc)
    v_bd = jnp.concatenate([v_new * head_rows[r] for r in range(hb)], axis=1)
    eg_cat = jnp.concatenate([jnp.exp(g_last[r]) * jnp.ones((1, HEAD_DIM), f32) for r in range(hb)], axis=1)
    new_s = s_cat * eg_cat + dot_tn(k_tail, v_bd)
    o = out * lax.rsqrt(jnp.mean(out * out, axis=-1, keepdims=True) + NORM_EPS) * normw * _silu(stack(z))
    o = jnp.concatenate([o[r * c:(r + 1) * c] for r in range(hb)], axis=1)
    return o, new_s


def _conv_windows(xin_ref, p, cw_ref, c):
    acc = None
    for k in range(CONV_K):
        term = cw_ref[pl.ds(k, 1), :] * xin_ref[p, pl.ds(HALO - CONV_K + 1 + k, c), :]
        acc = term if acc is None else acc + term
    return acc


def _gdn_fwd(proj, conv_w, alog_row, dtb_row, normw):
    rows = proj.shape[0]
    c = min(CHUNK, rows)
    n = rows // c

    def body(q_ref, k_ref, v_ref, z_ref, sm_ref, cwq, cwk, cwv, al_ref, dt_ref, nw_ref, y_ref, ck_ref, s_ref, xin_ref):
        hb = pl.program_id(0)
        i = pl.program_id(1)

        @pl.when(i == 0)
        def _():
            s_ref[...] = jnp.zeros_like(s_ref)
            xin_ref[:, 0:HALO, :] = jnp.zeros((3, HALO, GDN_SW), f32)

        @pl.when(i > 0)
        def _():
            xin_ref[:, 0:HALO, :] = xin_ref[:, c:c + HALO, :]

        xin_ref[0, HALO:, :] = q_ref[...]
        xin_ref[1, HALO:, :] = k_ref[...]
        xin_ref[2, HALO:, :] = v_ref[...]
        qc = _conv_windows(xin_ref, 0, cwq, c)
        kc = _conv_windows(xin_ref, 1, cwk, c)
        vc = _conv_windows(xin_ref, 2, cwv, c)
        state = s_ref[...]
        ck_ref[...] = state
        o, new_state = _gdn_step(qc, kc, vc, z_ref[...], sm_ref[...], al_ref[...], dt_ref[...], nw_ref[...], state,
                                 hb * GDN_HB)
        y_ref[...] = o.astype(y_ref.dtype)
        s_ref[...] = new_state

    def blk(unit):
        return pl.BlockSpec((c, GDN_SW), lambda hb, i: (i, unit // GDN_HB + hb))

    def cw(part):
        return pl.BlockSpec((CONV_K, GDN_SW), lambda hb, i: (0, part * GDN_STEPS + hb))

    row = pl.BlockSpec((1, LANE), lambda hb, i: (0, 0))
    return pl.pallas_call(
        body, name="gdn_fwd", grid=(GDN_STEPS, n),
        in_specs=[blk(U_Q), blk(U_K), blk(U_V), blk(U_AZ), pl.BlockSpec((c, LANE), lambda hb, i: (i, U_SMA)),
                  cw(0), cw(1), cw(2), row, row, row],
        out_specs=[pl.BlockSpec((c, GDN_SW), lambda hb, i: (i, hb)),
                   pl.BlockSpec((None, None, HEAD_DIM, GDN_SW), lambda hb, i: (hb, i, 0, 0))],
        out_shape=[jax.ShapeDtypeStruct((rows, GDN_WIDTH), bf16),
                   jax.ShapeDtypeStruct((GDN_STEPS, n, HEAD_DIM, GDN_SW), f32)],
        scratch_shapes=[pltpu.VMEM((HEAD_DIM, GDN_SW), f32), pltpu.VMEM((3, c + HALO, GDN_SW), f32)],
        compiler_params=_cparams(("arbitrary", "arbitrary")))(
            proj, proj, proj, proj, proj, conv_w, conv_w, conv_w, alog_row, dtb_row, normw)


def _conv_bwd(xin_ref, dyext_ref, p, cw_ref, dxc, dx_ref, dcw_ref, c):
    dyext_ref[p, 0:c, :] = dxc
    acc = None
    for k in range(CONV_K):
        term = cw_ref[pl.ds(k, 1), :] * dyext_ref[p, pl.ds(CONV_K - 1 - k, c), :]
        acc = term if acc is None else acc + term
        dcw_ref[pl.ds(k, 1), :] += jnp.sum(xin_ref[p, pl.ds(HALO - CONV_K + 1 + k, c), :] * dxc, axis=0, keepdims=True)
    dx_ref[...] = acc.astype(dx_ref.dtype)


def _gdn_bwd(proj, dy, ck, conv_w, alog_row, dtb_row, normw, scatter=None):
    rows = proj.shape[0]
    c = min(CHUNK, rows)
    n = rows // c
    halo_blocks = c // HALO
    bufs = scatter or []
    nb = len(bufs)
    n_in, n_out, n_scr = 16, 11, 3

    def body(*refs):
        core = refs[:n_in] + refs[n_in + nb:n_in + nb + n_out] + refs[n_in + 2 * nb + n_out:n_in + 2 * nb + n_out + n_scr]
        comm = (refs[n_in:n_in + nb], refs[n_in + nb + n_out:n_in + 2 * nb + n_out], *refs[n_in + 2 * nb + n_out + n_scr:])
        step_id = pl.program_id(0) * n + pl.program_id(1)
        if nb:
            @pl.when(step_id == 0)
            def _():
                _comm_phase("start", *comm, gather=False)

        chunk_step(*core)
        if nb:
            @pl.when(step_id == GDN_STEPS * n - 1)
            def _():
                _comm_phase("finish", *comm, gather=False)

    def chunk_step(q_ref, k_ref, v_ref, hq_ref, hk_ref, hv_ref, z_ref, sm_ref, cwq, cwk, cwv, al_ref, dt_ref, nw_ref,
                   ck_ref, dy_ref, dq_ref, dk_ref, dv_ref, dz_ref, dsm_ref, dcwq, dcwk, dcwv, dal_ref, ddt_ref, dnw_ref,
                   ds_ref, xin_ref, dyext_ref):
        hb = pl.program_id(0)
        i = pl.program_id(1)
        ci = n - 1 - i

        @pl.when(i == 0)
        def _():
            ds_ref[...] = jnp.zeros_like(ds_ref)
            dyext_ref[:, c:c + HALO, :] = jnp.zeros((3, HALO, GDN_SW), f32)
            for r in (dcwq, dcwk, dcwv, dal_ref, ddt_ref, dnw_ref):
                r[...] = jnp.zeros_like(r)

        @pl.when(i > 0)
        def _():
            dyext_ref[:, c:c + HALO, :] = dyext_ref[:, 0:HALO, :]

        first = (ci > 0).astype(f32)
        for p, (x_ref, halo_ref) in enumerate(((q_ref, hq_ref), (k_ref, hk_ref), (v_ref, hv_ref))):
            xin_ref[p, 0:HALO, :] = halo_ref[...] * first
            xin_ref[p, HALO:, :] = x_ref[...]
        qc = _conv_windows(xin_ref, 0, cwq, c)
        kc = _conv_windows(xin_ref, 1, cwk, c)
        vc = _conv_windows(xin_ref, 2, cwv, c)
        fn = functools.partial(_gdn_step, head0=hb * GDN_HB)
        _, vjp = jax.vjp(fn, qc, kc, vc, z_ref[...], sm_ref[...], al_ref[...], dt_ref[...], nw_ref[...], ck_ref[...])
        dqc, dkc, dvc, dz, dsm, dal, ddt, dnw, dstate = vjp((dy_ref[...].astype(f32), ds_ref[...]))
        ds_ref[...] = dstate
        dz_ref[...] = dz.astype(dz_ref.dtype)
        dsm_ref[...] = dsm
        dal_ref[...] += dal
        ddt_ref[...] += ddt
        dnw_ref[...] += dnw
        _conv_bwd(xin_ref, dyext_ref, 0, cwq, dqc, dq_ref, dcwq, c)
        _conv_bwd(xin_ref, dyext_ref, 1, cwk, dkc, dk_ref, dcwk, c)
        _conv_bwd(xin_ref, dyext_ref, 2, cwv, dvc, dv_ref, dcwv, c)

    def blk(unit):
        return pl.BlockSpec((c, GDN_SW), lambda hb, i: (n - 1 - i, unit // GDN_HB + hb))

    def halo(unit):
        return pl.BlockSpec((HALO, GDN_SW),
                            lambda hb, i: (jnp.maximum((n - 1 - i) * halo_blocks - 1, 0), unit // GDN_HB + hb))

    def cw(part):
        return pl.BlockSpec((CONV_K, GDN_SW), lambda hb, i: (0, part * GDN_STEPS + hb))

    row = pl.BlockSpec((1, LANE), lambda hb, i: (0, 0))
    hrow = pl.BlockSpec((None, 1, LANE), lambda hb, i: (hb, 0, 0))
    out_blk = pl.BlockSpec((c, GDN_SW), lambda hb, i: (n - 1 - i, hb))
    dcw = pl.BlockSpec((CONV_K, GDN_SW), lambda hb, i: (0, hb))
    wide = jax.ShapeDtypeStruct((rows, GDN_WIDTH), bf16)
    hrow_shape = jax.ShapeDtypeStruct((GDN_STEPS, 1, LANE), f32)
    dcw_shape = jax.ShapeDtypeStruct((CONV_K, GDN_WIDTH), f32)
    any_spec = pl.BlockSpec(memory_space=pl.ANY)
    comm_shapes, comm_sems = _comm_shapes(bufs, False) if nb else ([], [])
    return pl.pallas_call(
        body, name="gdn_bwd_scatter" if nb else "gdn_bwd", grid=(GDN_STEPS, n),
        in_specs=[blk(U_Q), blk(U_K), blk(U_V), halo(U_Q), halo(U_K), halo(U_V), blk(U_AZ),
                  pl.BlockSpec((c, LANE), lambda hb, i: (n - 1 - i, U_SMA)),
                  cw(0), cw(1), cw(2), row, row, row,
                  pl.BlockSpec((None, None, HEAD_DIM, GDN_SW), lambda hb, i: (hb, n - 1 - i, 0, 0)),
                  out_blk] + [any_spec] * nb,
        out_specs=[out_blk, out_blk, out_blk, out_blk,
                   pl.BlockSpec((None, c, LANE), lambda hb, i: (hb, n - 1 - i, 0)),
                   dcw, dcw, dcw, hrow, hrow, hrow] + [any_spec] * nb,
        out_shape=[wide, wide, wide, wide, jax.ShapeDtypeStruct((GDN_STEPS, rows, LANE), f32),
                   dcw_shape, dcw_shape, dcw_shape, hrow_shape, hrow_shape, hrow_shape] + comm_shapes,
        scratch_shapes=[pltpu.VMEM((HEAD_DIM, GDN_SW), f32), pltpu.VMEM((3, c + HALO, GDN_SW), f32),
                        pltpu.VMEM((3, c + HALO, GDN_SW), f32)] + comm_sems,
        compiler_params=_cparams(("arbitrary", "arbitrary")))(
            proj, proj, proj, proj, proj, proj, proj, proj, conv_w, conv_w, conv_w, alog_row, dtb_row, normw, ck, dy,
            *bufs)


M2_REP = M2_HEADS // M2_GROUPS
M2_GW = M2_REP * M2_HEAD_DIM
M2_GB = 2
M2_STEPS = M2_GROUPS // M2_GB
M2_XW = M2_GB * M2_GW
M2_BW = M2_GB * M2_STATE
M2_SH = M2_GB * M2_REP


def _ssd_step(xc, bc, cc, z, small, bias_x, bias_b, bias_c, alog_row, dtb_row, d_row, normw, state, grp0):
    c = xc.shape[0]
    causal = _tri_masks(c)[0]
    hd = M2_HEAD_DIM
    ones_l = jnp.ones((1, hd), f32)
    ones_r = jnp.ones((hd, 1), f32)
    lane = lax.broadcasted_iota(jnp.int32, (1, M2_GW), 1)
    lane_head = [((lane >= r * hd) & (lane < (r + 1) * hd)).astype(f32) for r in range(M2_REP)]
    xs = _silu(xc + bias_x)
    bms = _silu(bc + bias_b)
    cms = _silu(cc + bias_c)
    dt_all = _softplus(small + dtb_row)
    a_all = -jnp.exp(alog_row) * dt_all
    ac_all, act_all = _cumsum_all(a_all)
    ys, new_states = [], []
    for gi in range(M2_GB):
        bm = bms[:, gi * M2_STATE:(gi + 1) * M2_STATE]
        cm = cms[:, gi * M2_STATE:(gi + 1) * M2_STATE]
        xg = xs[:, gi * M2_GW:(gi + 1) * M2_GW]
        sg = state[gi * M2_GW:(gi + 1) * M2_GW]
        heads = [(grp0 + gi) * M2_REP + r for r in range(M2_REP)]
        ac_h = [_pick_col(ac_all, h) for h in heads]
        al_h = [a[c - 1:c, :] for a in ac_h]

        def wide(cols):
            return jnp.concatenate([v * ones_l for v in cols], axis=1)

        dt_w = wide([_pick_col(dt_all, h) for h in heads])
        ac_w = wide(ac_h)
        al_w = wide(al_h)
        dsk_w = wide([_pick_col(d_row, h) for h in heads])
        scores = dot_nt(cm, bm)
        m_wide = jnp.concatenate(
            [scores * jnp.exp(jnp.where(causal, a - _pick_row(act_all, h), -1e30)) for a, h in zip(ac_h, heads)], axis=1)
        xdt = xg * dt_w
        x_bd = jnp.concatenate([xdt * lane_head[r] for r in range(M2_REP)], axis=0)
        y_diag = dot_nn(m_wide, x_bd)
        states_new = dot_tn(xdt * jnp.exp(al_w - ac_w), bm)
        y_off = dot_nt(cm, sg) * jnp.exp(ac_w)
        eg_col = jnp.concatenate([jnp.exp(a) * ones_r for a in al_h], axis=0)
        new_states.append(sg * eg_col + states_new)
        y = (y_diag + y_off + dsk_w * xg) * _silu(z[:, gi * M2_GW:(gi + 1) * M2_GW])
        ys.append(y * lax.rsqrt(jnp.mean(y * y, axis=-1, keepdims=True) + NORM_EPS)
                  * normw[:, gi * M2_GW:(gi + 1) * M2_GW])
    return jnp.concatenate(ys, axis=-1), jnp.concatenate(new_states, axis=0)


def _conv_windows2(xin_ref, cw_ref, c):
    acc = None
    for k in range(CONV_K):
        term = cw_ref[pl.ds(k, 1), :] * xin_ref[pl.ds(HALO - CONV_K + 1 + k, c), :]
        acc = term if acc is None else acc + term
    return acc


def _ssd_specs(n, c, rev):
    def ci(i):
        return (n - 1 - i) if rev else i

    def blk(width, unit):
        return pl.BlockSpec((c, width), lambda g, i: (ci(i), unit * LANE // width + g))

    def par(rows_, width, col0):
        return pl.BlockSpec((rows_, width), lambda g, i: (0, col0 // width + g))

    return ci, blk, par


def _ssd_fwd(proj, conv_w, conv_b, alog_row, dtb_row, d_row, normw):
    rows = proj.shape[0]
    c = min(CHUNK, rows)
    n = rows // c
    _, blk, par = _ssd_specs(n, c, False)

    def body(x_ref, b_ref, c_ref, z_ref, sm_ref, cwx, cwb, cwc, bx, bb, bcc, al_ref, dt_ref, d_ref, nw_ref,
             y_ref, ck_ref, s_ref, xx_ref, xb_ref, xc_ref):
        g = pl.program_id(0)
        i = pl.program_id(1)

        @pl.when(i == 0)
        def _():
            s_ref[...] = jnp.zeros_like(s_ref)
            for r in (xx_ref, xb_ref, xc_ref):
                r[0:HALO, :] = jnp.zeros((HALO, r.shape[1]), f32)

        @pl.when(i > 0)
        def _():
            for r in (xx_ref, xb_ref, xc_ref):
                r[0:HALO, :] = r[c:c + HALO, :]

        xx_ref[HALO:, :] = x_ref[...]
        xb_ref[HALO:, :] = b_ref[...]
        xc_ref[HALO:, :] = c_ref[...]
        xc = _conv_windows2(xx_ref, cwx, c)
        bc = _conv_windows2(xb_ref, cwb, c)
        cc = _conv_windows2(xc_ref, cwc, c)
        state = s_ref[...]
        ck_ref[...] = state
        y, new_state = _ssd_step(xc, bc, cc, z_ref[...], sm_ref[...], bx[...], bb[...], bcc[...], al_ref[...],
                                 dt_ref[...], d_ref[...], nw_ref[...], state, g * M2_GB)
        y_ref[...] = y.astype(y_ref.dtype)
        s_ref[...] = new_state

    row = pl.BlockSpec((1, LANE), lambda g, i: (0, 0))
    off_b, off_c = M2_WIDTH, M2_WIDTH + M2_GROUPS * M2_STATE
    return pl.pallas_call(
        body, name="ssd_fwd", grid=(M2_STEPS, n),
        in_specs=[blk(M2_XW, U_CX), blk(M2_BW, U_CB), blk(M2_BW, U_CC), blk(M2_XW, U_CZ),
                  pl.BlockSpec((c, LANE), lambda g, i: (i, U_SMC)),
                  par(CONV_K, M2_XW, 0), par(CONV_K, M2_BW, off_b), par(CONV_K, M2_BW, off_c),
                  par(1, M2_XW, 0), par(1, M2_BW, off_b), par(1, M2_BW, off_c), row, row, row, par(1, M2_XW, 0)],
        out_specs=[pl.BlockSpec((c, M2_XW), lambda g, i: (i, g)),
                   pl.BlockSpec((None, None, M2_SH * M2_HEAD_DIM, M2_STATE), lambda g, i: (g, i, 0, 0))],
        out_shape=[jax.ShapeDtypeStruct((rows, M2_WIDTH), bf16),
                   jax.ShapeDtypeStruct((M2_STEPS, n, M2_SH * M2_HEAD_DIM, M2_STATE), f32)],
        scratch_shapes=[pltpu.VMEM((M2_SH * M2_HEAD_DIM, M2_STATE), f32), pltpu.VMEM((c + HALO, M2_XW), f32),
                        pltpu.VMEM((c + HALO, M2_BW), f32), pltpu.VMEM((c + HALO, M2_BW), f32)],
        compiler_params=_cparams(("arbitrary", "arbitrary")))(
            proj, proj, proj, proj, proj, conv_w, conv_w, conv_w, conv_b, conv_b, conv_b, alog_row, dtb_row, d_row, normw)


def _conv_bwd2(xin_ref, dyext_ref, cw_ref, dxc, dx_ref, dcw_ref, c):
    dyext_ref[0:c, :] = dxc
    acc = None
    for k in range(CONV_K):
        term = cw_ref[pl.ds(k, 1), :] * dyext_ref[pl.ds(CONV_K - 1 - k, c), :]
        acc = term if acc is None else acc + term
        dcw_ref[pl.ds(k, 1), :] += jnp.sum(xin_ref[pl.ds(HALO - CONV_K + 1 + k, c), :] * dxc, axis=0, keepdims=True)
    dx_ref[...] = acc.astype(dx_ref.dtype)


def _ssd_bwd(proj, dy, ck, conv_w, conv_b, alog_row, dtb_row, d_row, normw):
    rows = proj.shape[0]
    c = min(CHUNK, rows)
    n = rows // c
    halo_blocks = c // HALO
    _, blk, par = _ssd_specs(n, c, True)

    def body(x_ref, b_ref, c_ref, hx_ref, hb_ref, hc_ref, z_ref, sm_ref, cwx, cwb, cwc, bx, bb, bcc,
             al_ref, dt_ref, d_ref, nw_ref, ck_ref, dy_ref,
             dx_ref, db_ref, dc_ref, dz_ref, dsm_ref, dcwx, dcwb, dcwc, dbx, dbb, dbc, dal_ref, ddt_ref, dd_ref, dnw_ref,
             ds_ref, xx_ref, xb_ref, xc_ref, ex_ref, eb_ref, ec_ref):
        g = pl.program_id(0)
        i = pl.program_id(1)
        ci = n - 1 - i

        @pl.when(i == 0)
        def _():
            ds_ref[...] = jnp.zeros_like(ds_ref)
            for r in (ex_ref, eb_ref, ec_ref):
                r[c:c + HALO, :] = jnp.zeros((HALO, r.shape[1]), f32)
            for r in (dcwx, dcwb, dcwc, dbx, dbb, dbc, dal_ref, ddt_ref, dd_ref, dnw_ref):
                r[...] = jnp.zeros_like(r)

        @pl.when(i > 0)
        def _():
            for r in (ex_ref, eb_ref, ec_ref):
                r[c:c + HALO, :] = r[0:HALO, :]

        first = (ci > 0).astype(f32)
        for xin, x_in, halo_in in ((xx_ref, x_ref, hx_ref), (xb_ref, b_ref, hb_ref), (xc_ref, c_ref, hc_ref)):
            xin[0:HALO, :] = halo_in[...] * first
            xin[HALO:, :] = x_in[...]
        xc = _conv_windows2(xx_ref, cwx, c)
        bc = _conv_windows2(xb_ref, cwb, c)
        cc = _conv_windows2(xc_ref, cwc, c)
        fn = functools.partial(_ssd_step, grp0=g * M2_GB)
        _, vjp = jax.vjp(fn, xc, bc, cc, z_ref[...], sm_ref[...], bx[...], bb[...], bcc[...], al_ref[...], dt_ref[...],
                         d_ref[...], nw_ref[...], ck_ref[...])
        (dxc, dbc_, dcc, dz, dsm, gbx, gbb, gbc, dal, ddt, dd, dnw, dstate) = vjp((dy_ref[...].astype(f32), ds_ref[...]))
        ds_ref[...] = dstate
        dz_ref[...] = dz.astype(dz_ref.dtype)
        dsm_ref[...] = dsm
        dbx[...] += gbx
        dbb[...] += gbb
        dbc[...] += gbc
        dal_ref[...] += dal
        ddt_ref[...] += ddt
        dd_ref[...] += dd
        dnw_ref[...] += dnw
        _conv_bwd2(xx_ref, ex_ref, cwx, dxc, dx_ref, dcwx, c)
        _conv_bwd2(xb_ref, eb_ref, cwb, dbc_, db_ref, dcwb, c)
        _conv_bwd2(xc_ref, ec_ref, cwc, dcc, dc_ref, dcwc, c)

    def halo(width, unit):
        return pl.BlockSpec((HALO, width),
                            lambda g, i: (jnp.maximum((n - 1 - i) * halo_blocks - 1, 0), unit * LANE // width + g))

    row = pl.BlockSpec((1, LANE), lambda g, i: (0, 0))
    grow = pl.BlockSpec((None, 1, LANE), lambda g, i: (g, 0, 0))
    grow_shape = jax.ShapeDtypeStruct((M2_STEPS, 1, LANE), f32)
    ob_w = pl.BlockSpec((c, M2_XW), lambda g, i: (n - 1 - i, g))
    ob_n = pl.BlockSpec((c, M2_BW), lambda g, i: (n - 1 - i, g))
    off_b, off_c = M2_WIDTH, M2_WIDTH + M2_GROUPS * M2_STATE
    bc_w = M2_GROUPS * M2_STATE
    return pl.pallas_call(
        body, name="ssd_bwd", grid=(M2_STEPS, n),
        in_specs=[blk(M2_XW, U_CX), blk(M2_BW, U_CB), blk(M2_BW, U_CC),
                  halo(M2_XW, U_CX), halo(M2_BW, U_CB), halo(M2_BW, U_CC), blk(M2_XW, U_CZ),
                  pl.BlockSpec((c, LANE), lambda g, i: (n - 1 - i, U_SMC)),
                  par(CONV_K, M2_XW, 0), par(CONV_K, M2_BW, off_b), par(CONV_K, M2_BW, off_c),
                  par(1, M2_XW, 0), par(1, M2_BW, off_b), par(1, M2_BW, off_c), row, row, row, par(1, M2_XW, 0),
                  pl.BlockSpec((None, None, M2_SH * M2_HEAD_DIM, M2_STATE), lambda g, i: (g, n - 1 - i, 0, 0)),
                  ob_w],
        out_specs=[ob_w, ob_n, ob_n, ob_w, pl.BlockSpec((None, c, LANE), lambda g, i: (g, n - 1 - i, 0)),
                   par(CONV_K, M2_XW, 0), par(CONV_K, M2_BW, 0), par(CONV_K, M2_BW, 0),
                   par(1, M2_XW, 0), par(1, M2_BW, 0), par(1, M2_BW, 0), grow, grow, grow, par(1, M2_XW, 0)],
        out_shape=[jax.ShapeDtypeStruct((rows, M2_WIDTH), bf16), jax.ShapeDtypeStruct((rows, bc_w), bf16),
                   jax.ShapeDtypeStruct((rows, bc_w), bf16), jax.ShapeDtypeStruct((rows, M2_WIDTH), bf16),
                   jax.ShapeDtypeStruct((M2_STEPS, rows, LANE), f32),
                   jax.ShapeDtypeStruct((CONV_K, M2_WIDTH), f32), jax.ShapeDtypeStruct((CONV_K, bc_w), f32),
                   jax.ShapeDtypeStruct((CONV_K, bc_w), f32),
                   jax.ShapeDtypeStruct((1, M2_WIDTH), f32), jax.ShapeDtypeStruct((1, bc_w), f32),
                   jax.ShapeDtypeStruct((1, bc_w), f32), grow_shape, grow_shape, grow_shape,
                   jax.ShapeDtypeStruct((1, M2_WIDTH), f32)],
        scratch_shapes=[pltpu.VMEM((M2_SH * M2_HEAD_DIM, M2_STATE), f32),
                        pltpu.VMEM((c + HALO, M2_XW), f32), pltpu.VMEM((c + HALO, M2_BW), f32), pltpu.VMEM((c + HALO, M2_BW), f32),
                        pltpu.VMEM((c + HALO, M2_XW), f32), pltpu.VMEM((c + HALO, M2_BW), f32), pltpu.VMEM((c + HALO, M2_BW), f32)],
        compiler_params=_cparams(("arbitrary", "arbitrary")))(
            proj, proj, proj, proj, proj, proj, proj, proj, conv_w, conv_w, conv_w, conv_b, conv_b, conv_b,
            alog_row, dtb_row, d_row, normw, ck, dy)


S5_TILE = 64
S5_SW = S5_LANES // S5_BLK


def _scan_down(br, bi, ar, ai):
    t = br.shape[0]
    row = lax.broadcasted_iota(jnp.int32, br.shape, 0)
    d = 1
    while d < t:
        keep = row >= d
        sr = jnp.where(keep, pltpu.roll(br, d, 0), 0.0)
        si = jnp.where(keep, pltpu.roll(bi, d, 0), 0.0)
        br, bi = br + ar * sr - ai * si, bi + ar * si + ai * sr
        ar, ai = ar * ar - ai * ai, 2.0 * ar * ai
        d *= 2
    return br, bi


def _scan_up(br, bi, ar, ai):
    t = br.shape[0]
    row = lax.broadcasted_iota(jnp.int32, br.shape, 0)
    d = 1
    while d < t:
        keep = row < t - d
        sr = jnp.where(keep, pltpu.roll(br, t - d, 0), 0.0)
        si = jnp.where(keep, pltpu.roll(bi, t - d, 0), 0.0)
        br, bi = br + ar * sr - ai * si, bi + ar * si + ai * sr
        ar, ai = ar * ar - ai * ai, 2.0 * ar * ai
        d *= 2
    return br, bi


def _s5_states(u_j, bbr, bbi, ar, ai, cr, ci_):
    br = dot_nn(u_j, bbr)
    bi = dot_nn(u_j, bbi)
    row0 = lax.broadcasted_iota(jnp.int32, br.shape, 0) == 0
    br = br + jnp.where(row0, ar * cr - ai * ci_, 0.0)
    bi = bi + jnp.where(row0, ar * ci_ + ai * cr, 0.0)
    return _scan_down(br, bi, ar, ai)


def _s5_fwd(proj, a_rows, bbr, bbi, ccr, cci, d_row):
    rows = proj.shape[0]
    t = min(S5_TILE, rows)
    n = rows // t

    def body(u_ref, a_ref, bbr_ref, bbi_ref, ccr_ref, cci_ref, d_ref, y_ref, ck_ref, carry_ref):
        i = pl.program_id(0)

        @pl.when(i == 0)
        def _():
            carry_ref[...] = jnp.zeros_like(carry_ref)

        ck_ref[...] = carry_ref[...]
        for j in range(S5_BLK):
            lanes = pl.ds(j * S5_SW, S5_SW)
            ch = pl.ds(j * LANE, LANE)
            u_j = u_ref[:, ch]
            sr, si = _s5_states(u_j, bbr_ref[j], bbi_ref[j], a_ref[0:1, lanes], a_ref[1:2, lanes],
                                carry_ref[0:1, lanes], carry_ref[1:2, lanes])
            y_ref[:, ch] = dot_nn(sr, ccr_ref[j]) - dot_nn(si, cci_ref[j]) + d_ref[:, ch] * u_j
            carry_ref[0:1, lanes] = sr[t - 1:t, :]
            carry_ref[1:2, lanes] = si[t - 1:t, :]

    whole3 = lambda s: pl.BlockSpec(s, lambda i: (0, 0, 0))
    return pl.pallas_call(
        body, name="s5_fwd", grid=(n,),
        in_specs=[pl.BlockSpec((t, S5_WIDTH), lambda i: (i, U_SU // S5_BLK)),
                  pl.BlockSpec((2, S5_LANES), lambda i: (0, 0)),
                  whole3(bbr.shape), whole3(bbi.shape), whole3(ccr.shape), whole3(cci.shape),
                  pl.BlockSpec((1, S5_WIDTH), lambda i: (0, 0))],
        out_specs=[pl.BlockSpec((t, S5_WIDTH), lambda i: (i, 0)),
                   pl.BlockSpec((None, 2, S5_LANES), lambda i: (i, 0, 0))],
        out_shape=[jax.ShapeDtypeStruct((rows, S5_WIDTH), f32), jax.ShapeDtypeStruct((n, 2, S5_LANES), f32)],
        scratch_shapes=[pltpu.VMEM((2, S5_LANES), f32)],
        compiler_params=_cparams(("arbitrary",)))(proj, a_rows, bbr, bbi, ccr, cci, d_row)


def _s5_bwd(proj, dy, ck, a_rows, bbr, bbi, ccr, cci, d_row):
    rows = proj.shape[0]
    t = min(S5_TILE, rows)
    n = rows // t

    def body(u_ref, dy_ref, ck_ref, a_ref, bbr_ref, bbi_ref, ccr_ref, cci_ref, d_ref,
             du_ref, da_ref, dbbr_ref, dbbi_ref, dccr_ref, dcci_ref, dd_ref, lam_ref):
        i = pl.program_id(0)

        @pl.when(i == 0)
        def _():
            lam_ref[...] = jnp.zeros_like(lam_ref)
            for r in (da_ref, dbbr_ref, dbbi_ref, dccr_ref, dcci_ref, dd_ref):
                r[...] = jnp.zeros_like(r)

        for j in range(S5_BLK):
            lanes = pl.ds(j * S5_SW, S5_SW)
            ch = pl.ds(j * LANE, LANE)
            u_j = u_ref[:, ch]
            dy_j = dy_ref[:, ch]
            ar, ai = a_ref[0:1, lanes], a_ref[1:2, lanes]
            cr, ci_ = ck_ref[0:1, lanes], ck_ref[1:2, lanes]
            sr, si = _s5_states(u_j, bbr_ref[j], bbi_ref[j], ar, ai, cr, ci_)
            gr = dot_nt(dy_j, ccr_ref[j])
            gi = -dot_nt(dy_j, cci_ref[j])
            last = lax.broadcasted_iota(jnp.int32, gr.shape, 0) == t - 1
            lr0, li0 = lam_ref[0:1, lanes], lam_ref[1:2, lanes]
            gr = gr + jnp.where(last, ar * lr0 + ai * li0, 0.0)
            gi = gi + jnp.where(last, ar * li0 - ai * lr0, 0.0)
            lr, li = _scan_up(gr, gi, ar, -ai)
            lam_ref[0:1, lanes] = lr[0:1, :]
            lam_ref[1:2, lanes] = li[0:1, :]
            du_ref[:, ch] = (dot_nt(lr, bbr_ref[j]) + dot_nt(li, bbi_ref[j]) + d_ref[:, ch] * dy_j).astype(du_ref.dtype)
            dbbr_ref[j] += dot_tn(u_j, lr)
            dbbi_ref[j] += dot_tn(u_j, li)
            dccr_ref[j] += dot_tn(sr, dy_j)
            dcci_ref[j] += -dot_tn(si, dy_j)
            dd_ref[:, ch] += jnp.sum(dy_j * u_j, axis=0, keepdims=True)
            row0 = lax.broadcasted_iota(jnp.int32, sr.shape, 0) == 0
            pr = jnp.where(row0, cr, pltpu.roll(sr, 1, 0))
            pi = jnp.where(row0, ci_, pltpu.roll(si, 1, 0))
            da_ref[0:1, lanes] += jnp.sum(lr * pr + li * pi, axis=0, keepdims=True)
            da_ref[1:2, lanes] += jnp.sum(li * pr - lr * pi, axis=0, keepdims=True)

    whole3 = lambda s: pl.BlockSpec(s, lambda i: (0, 0, 0))
    whole2 = lambda s: pl.BlockSpec(s, lambda i: (0, 0))
    return pl.pallas_call(
        body, name="s5_bwd", grid=(n,),
        in_specs=[pl.BlockSpec((t, S5_WIDTH), lambda i: (n - 1 - i, U_SU // S5_BLK)),
                  pl.BlockSpec((t, S5_WIDTH), lambda i: (n - 1 - i, 0)),
                  pl.BlockSpec((None, 2, S5_LANES), lambda i: (n - 1 - i, 0, 0)),
                  whole2((2, S5_LANES)), whole3(bbr.shape), whole3(bbi.shape), whole3(ccr.shape), whole3(cci.shape),
                  whole2((1, S5_WIDTH))],
        out_specs=[pl.BlockSpec((t, S5_WIDTH), lambda i: (n - 1 - i, 0)), whole2((2, S5_LANES)),
                   whole3(bbr.shape), whole3(bbi.shape), whole3(ccr.shape), whole3(cci.shape), whole2((1, S5_WIDTH))],
        out_shape=[jax.ShapeDtypeStruct((rows, S5_WIDTH), bf16), jax.ShapeDtypeStruct((2, S5_LANES), f32),
                   jax.ShapeDtypeStruct(bbr.shape, f32), jax.ShapeDtypeStruct(bbi.shape, f32),
                   jax.ShapeDtypeStruct(ccr.shape, f32), jax.ShapeDtypeStruct(cci.shape, f32),
                   jax.ShapeDtypeStruct((1, S5_WIDTH), f32)],
        scratch_shapes=[pltpu.VMEM((2, S5_LANES), f32)],
        compiler_params=_cparams(("arbitrary",)))(proj, dy, ck, a_rows, bbr, bbi, ccr, cci, d_row)


def _s5_prep(lam_re, lam_im, log_step, b_re, b_im, c_re, c_im, d_skip):
    lam_re = jnp.minimum(lam_re, -1e-4)
    step = jnp.exp(log_step)[:, None]
    mag = jnp.exp(lam_re * step)
    ab_re = mag * jnp.cos(lam_im * step)
    ab_im = mag * jnp.sin(lam_im * step)
    den = lam_re * lam_re + lam_im * lam_im
    f_re = ((ab_re - 1.0) * lam_re + ab_im * lam_im) / den
    f_im = (ab_im * lam_re - (ab_re - 1.0) * lam_im) / den
    bb_re = f_re[..., None] * b_re - f_im[..., None] * b_im
    bb_im = f_re[..., None] * b_im + f_im[..., None] * b_re
    eye = jnp.eye(8, dtype=f32)

    def drive(bb):
        r = bb.reshape(S5_BLK, 8, S5_STATE, S5_GROUP_SIZE).transpose(0, 1, 3, 2)
        return (r[:, :, :, None, :] * eye[None, :, None, :, None]).reshape(S5_BLK, LANE, S5_SW)

    def readout(cc):
        r = cc.reshape(S5_BLK, 8, S5_GROUP_SIZE, S5_STATE).transpose(0, 1, 3, 2)
        return (r[:, :, :, None, :] * eye[None, :, None, :, None]).reshape(S5_BLK, S5_SW, LANE)

    a_rows = jnp.stack([ab_re.reshape(S5_LANES), ab_im.reshape(S5_LANES)])
    return a_rows, drive(bb_re), drive(bb_im), readout(c_re), readout(c_im), d_skip.reshape(1, S5_WIDTH)


def _adam_math(w, g, m, v):
    m = ADAM_B1 * m + (1.0 - ADAM_B1) * g
    v = ADAM_B2 * v + (1.0 - ADAM_B2) * (g * g)
    m_hat = m / (1.0 - ADAM_B1 ** ADAM_STEP)
    v_hat = v / (1.0 - ADAM_B2 ** ADAM_STEP)
    delta = -ADAM_LR * (m_hat / (jnp.sqrt(v_hat) + ADAM_EPS) + ADAM_WD * w)
    return delta, m, v


def _adamw(name, w, g, m, v):
    rows, width = w.shape
    t = rows
    for cand in (512, 256, 128, 64, 32, 16, 8):
        if rows % cand == 0 and cand * width * 4 * 7 * 2 <= VMEM_LIMIT // 2:
            t = cand
            break

    def body(w_ref, g_ref, m_ref, v_ref, d_ref, nm_ref, nv_ref):
        d, nm, nv = _adam_math(w_ref[...], g_ref[...], m_ref[...], v_ref[...])
        d_ref[...] = d
        nm_ref[...] = nm
        nv_ref[...] = nv

    spec = pl.BlockSpec((t, width), lambda i: (i, 0))
    shape = jax.ShapeDtypeStruct((rows, width), f32)
    return pl.pallas_call(body, name=name, grid=(rows // t,), in_specs=[spec] * 4, out_specs=[spec] * 3,
                          out_shape=[shape] * 3, compiler_params=_cparams(("parallel",)))(w, g, m, v)


def _adamw_slots(name, recvs, row_off, w, m, v):
    depth, rows, width = w.shape
    per_row = width * (depth * N_DEV * recvs[0].dtype.itemsize + 7 * 4) * 2
    t = next(cand for cand in (256, 128, 96, 64, 32, 16)
             if rows % cand == 0 and row_off % cand == 0 and cand * per_row <= VMEM_LIMIT * 2 // 3)
    first = row_off // t

    def body(*refs):
        r_refs = refs[:depth]
        w_ref, m_ref, v_ref, g_ref, d_ref, nm_ref, nv_ref = refs[depth:]
        layer = pl.program_id(0)
        for l, r_ref in enumerate(r_refs):
            @pl.when(layer == l)
            def _():
                g = r_ref[0].astype(f32)
                for s in range(1, N_DEV):
                    g = g + r_ref[s].astype(f32)
                d, nm, nv = _adam_math(w_ref[...], g, m_ref[...], v_ref[...])
                g_ref[...] = g
                d_ref[...] = d
                nm_ref[...] = nm
                nv_ref[...] = nv

    def recv_spec(l):
        return pl.BlockSpec((N_DEV, t, width), lambda layer, i: (0, first + jnp.where(layer == l, i, 0), 0))

    spec = pl.BlockSpec((None, t, width), lambda layer, i: (layer, i, 0))
    shape = jax.ShapeDtypeStruct((depth, rows, width), f32)
    return pl.pallas_call(
        body, name=name, grid=(depth, rows // t), in_specs=[recv_spec(l) for l in range(depth)] + [spec, spec, spec],
        out_specs=[spec] * 4, out_shape=[shape] * 4,
        compiler_params=_cparams(("arbitrary", "arbitrary")))(*recvs, w, m, v)


def _sum_slots(name, buf):
    _, rows, width = buf.shape
    t = next(cand for cand in (512, 256, 128, 64, 32, 16) if rows % cand == 0)

    def body(b_ref, o_ref):
        acc = b_ref[0].astype(f32)
        for s in range(1, N_DEV):
            acc = acc + b_ref[s].astype(f32)
        o_ref[...] = acc

    return pl.pallas_call(
        body, name=name, grid=(rows // t,), in_specs=[pl.BlockSpec((N_DEV, t, width), lambda i: (0, i, 0))],
        out_specs=pl.BlockSpec((t, width), lambda i: (i, 0)), out_shape=jax.ShapeDtypeStruct((rows, width), f32),
        compiler_params=_cparams(("parallel",)))(buf)


BIG = ("w_in", "s5_glu_w", "proj_a", "proj_b", "proj_c", "w_out")
CONV = ("gdn_conv_w", "m2_conv_w")
SMALL = ("norm_w", "gdn_a_log", "gdn_dt_bias", "gdn_norm_w", "s5_lam_re", "s5_lam_im", "s5_log_step",
         "s5_b_re", "s5_b_im", "s5_c_re", "s5_c_im", "s5_d", "s5_glu_b", "m2_conv_b", "m2_a_log", "m2_dt_bias",
         "m2_d", "m2_norm_w", "final_norm_w")
WEIGHTS = ("norm_w", "w_in", "gdn_conv_w", "gdn_a_log", "gdn_dt_bias", "gdn_norm_w", "s5_lam_re", "s5_lam_im",
           "s5_log_step", "s5_b_re", "s5_b_im", "s5_c_re", "s5_c_im", "s5_d", "s5_glu_w", "s5_glu_b", "m2_conv_w",
           "m2_conv_b", "m2_a_log", "m2_dt_bias", "m2_d", "m2_norm_w", "proj_a", "proj_b", "proj_c", "w_out",
           "final_norm_w")


PACK_ROWS = 8
PACK_TILE = 256


def _piece_rows(shape):
    size = 1
    for d in shape:
        size *= d
    rows = -(-size // LANE)
    return size, -(-rows // PACK_ROWS) * PACK_ROWS


def _pack(arrays):
    pieces = []
    for a in arrays:
        size, rows = _piece_rows(a.shape)
        flat = a.reshape(-1)
        if size != rows * LANE:
            flat = jnp.concatenate([flat, jnp.zeros((rows * LANE - size,), f32)])
        pieces.append(flat.reshape(rows, LANE))
    total = sum(p.shape[0] for p in pieces)
    tail = -total % PACK_TILE
    if tail:
        pieces.append(jnp.zeros((tail, LANE), f32))
    return jnp.concatenate(pieces, axis=0)


def _unpack(buf, shapes):
    out, off = [], 0
    for s in shapes:
        size, rows = _piece_rows(s)
        piece = buf[off:off + rows]
        out.append(piece.reshape(s) if size == rows * LANE else piece.reshape(-1)[:size].reshape(s))
        off += rows
    return out


def _win_to_padded(w):
    d = w.shape[0]
    z = lambda n: jnp.zeros((d, n), w.dtype)
    return jnp.concatenate([w[:, 8736:14880], w[:, 4112:5648], w[:, 0:4096], w[:, 5648:8720],
                            w[:, 4096:4112], z(LANE - 16), w[:, 8720:8736], z(LANE - 16), z(2 * LANE)], axis=1)


def _win_from_padded(g):
    u = LANE
    return jnp.concatenate([g[:, U_Q * u:U_CZ * u], g[:, U_SMA * u:U_SMA * u + 16], g[:, U_SU * u:U_Q * u],
                            g[:, U_CZ * u:U_SMA * u], g[:, U_SMC * u:U_SMC * u + 16], g[:, 0:U_SU * u]], axis=1)


def _lane_row(vals, offset):
    n = vals.shape[0]
    return jnp.concatenate([jnp.zeros((offset,), f32), vals, jnp.zeros((LANE - offset - n,), f32)]).reshape(1, LANE)


def _layer_fwd(x, lw, next_shards):
    h = _tok_fwd("rms_fwd", _rms_fn, [(x, D_MODEL, 0)], [lw["norm_w"].reshape(1, D_MODEL)], [(D_MODEL, bf16)], 256)[0]
    if next_shards is None:
        proj, gathered = _mm("proj_fwd", h, lw["w_in_p"], "nn", tk=D_MODEL), None
    else:
        proj, *gathered = _mm("proj_fwd_gather", h, lw["w_in_p"], "nn", tk=D_MODEL, comm=(next_shards, True))
    g_al, g_dt = _lane_row(lw["gdn_a_log"], GDN_HEADS), _lane_row(lw["gdn_dt_bias"], GDN_HEADS)
    g_nw = lw["gdn_norm_w"].reshape(1, HEAD_DIM)
    y_a, ck_a = _gdn_fwd(proj, lw["gdn_conv_w"], g_al, g_dt, g_nw)
    s5p = _s5_prep(lw["s5_lam_re"], lw["s5_lam_im"], lw["s5_log_step"], lw["s5_b_re"], lw["s5_b_im"],
                   lw["s5_c_re"], lw["s5_c_im"], lw["s5_d"])
    y_pre, ck_b = _s5_fwd(proj, *s5p)
    glu_b = lw["s5_glu_b"].reshape(1, S5_WIDTH)
    y_b = _tok_fwd("s5_tail_fwd", _s5_tail_fn, [(y_pre, S5_WIDTH, 0), (proj, S5_WIDTH, U_SG // S5_BLK)],
                   [lw["s5_glu_w"], glu_b], [(S5_WIDTH, bf16)], 256)[0]
    m_al, m_dt, m_d = _lane_row(lw["m2_a_log"], 0), _lane_row(lw["m2_dt_bias"], 0), _lane_row(lw["m2_d"], 0)
    m_cb = lw["m2_conv_b"].reshape(1, -1)
    m_nw = lw["m2_norm_w"].reshape(1, M2_WIDTH)
    y_c, ck_c = _ssd_fwd(proj, lw["m2_conv_w"], m_cb, m_al, m_dt, m_d, m_nw)
    pa = _mm("proj_a_fwd", y_a, lw["proj_a"], "nn")
    pb = _mm("proj_b_fwd", y_b, lw["proj_b"], "nn")
    pc = _mm("proj_c_fwd", y_c, lw["proj_c"], "nn")
    gate_acts = [(proj, D_MODEL, 0), (proj, D_MODEL, 1), (proj, D_MODEL, 2),
                 (pa, D_MODEL, 0), (pb, D_MODEL, 0), (pc, D_MODEL, 0)]
    merged = _tok_fwd("gate_fwd", _gate_fn, gate_acts, [], [(D_MODEL, bf16)], 128)[0]
    x_next = _mm("w_out_fwd", merged, lw["w_out"], "nn", residual=x)
    saved = dict(x=x, h=h, proj=proj, y_a=y_a, ck_a=ck_a, y_pre=y_pre, ck_b=ck_b, y_b=y_b, y_c=y_c, ck_c=ck_c,
                 pa=pa, pb=pb, pc=pc, merged=merged)
    return x_next, saved, gathered


def _layer_bwd(dx_next, lw, sv, later_grads, pack_own=None):
    rows = dx_next.shape[0]
    proj = sv["proj"]
    g = {}
    d_merged = _mm("w_out_bwd_x", dx_next, lw["w_out"], "nt", out_dtype=bf16)
    g["w_out"] = _mm("w_out_bwd_w", sv["merged"], dx_next, "tn", out_dtype=bf16)
    gate_acts = [(proj, D_MODEL, 0), (proj, D_MODEL, 1), (proj, D_MODEL, 2),
                 (sv["pa"], D_MODEL, 0), (sv["pb"], D_MODEL, 0), (sv["pc"], D_MODEL, 0)]
    dla, dlb, dlc, dpa, dpb, dpc = _tok_bwd(
        "gate_bwd", _gate_fn, gate_acts, [], [(d_merged, D_MODEL, 0)],
        [(0, bf16), (1, bf16), (2, bf16), (3, bf16), (4, bf16), (5, bf16)], 128)
    dy_a = _mm("proj_a_bwd_x", dpa, lw["proj_a"], "nt", out_dtype=bf16)
    dy_b = _mm("proj_b_bwd_x", dpb, lw["proj_b"], "nt")
    dy_c = _mm("proj_c_bwd_x", dpc, lw["proj_c"], "nt", out_dtype=bf16)
    g["proj_a"] = _mm("proj_a_bwd_w", sv["y_a"], dpa, "tn", out_dtype=bf16)
    g["proj_b"] = _mm("proj_b_bwd_w", sv["y_b"], dpb, "tn", out_dtype=bf16)
    g["proj_c"] = _mm("proj_c_bwd_w", sv["y_c"], dpc, "tn", out_dtype=bf16)

    m_al, m_dt, m_d = _lane_row(lw["m2_a_log"], 0), _lane_row(lw["m2_dt_bias"], 0), _lane_row(lw["m2_d"], 0)
    m_cb = lw["m2_conv_b"].reshape(1, -1)
    m_nw = lw["m2_norm_w"].reshape(1, M2_WIDTH)
    (dcx, dcb, dcc, dcz, dsmc, dcwx, dcwb, dcwc, dbx, dbb, dbc, dal, ddt, ddk, dnw) = _ssd_bwd(
        proj, dy_c, sv["ck_c"], lw["m2_conv_w"], m_cb, m_al, m_dt, m_d, m_nw)
    g["m2_conv_w"] = jnp.concatenate([dcwx, dcwb, dcwc], axis=1)
    g["m2_conv_b"] = jnp.concatenate([dbx, dbb, dbc], axis=1).reshape(-1)
    g["m2_a_log"] = jnp.sum(dal, axis=(0, 1))[:M2_HEADS]
    g["m2_dt_bias"] = jnp.sum(ddt, axis=(0, 1))[:M2_HEADS]
    g["m2_d"] = jnp.sum(ddk, axis=(0, 1))[:M2_HEADS]
    g["m2_norm_w"] = dnw.reshape(-1)
    dsmc = jnp.sum(dsmc, axis=0)

    glu_b = lw["s5_glu_b"].reshape(1, S5_WIDTH)
    dypre, dsg, dglu_w, dglu_b = _tok_bwd(
        "s5_tail_bwd", _s5_tail_fn, [(sv["y_pre"], S5_WIDTH, 0), (proj, S5_WIDTH, U_SG // S5_BLK)],
        [lw["s5_glu_w"], glu_b], [(dy_b, S5_WIDTH, 0)], [(0, f32), (1, bf16)], 256)
    g["s5_glu_w"] = dglu_w
    g["s5_glu_b"] = dglu_b.reshape(-1)
    s5_names = ("s5_lam_re", "s5_lam_im", "s5_log_step", "s5_b_re", "s5_b_im", "s5_c_re", "s5_c_im", "s5_d")
    s5p, s5_vjp = jax.vjp(_s5_prep, *[lw[k] for k in s5_names])
    dsu, da, dbbr, dbbi, dccr, dcci, dd = _s5_bwd(proj, dypre, sv["ck_b"], *s5p)
    for k, val in zip(s5_names, s5_vjp((da, dbbr, dbbi, dccr, dcci, dd))):
        g[k] = val

    g_al, g_dt = _lane_row(lw["gdn_a_log"], GDN_HEADS), _lane_row(lw["gdn_dt_bias"], GDN_HEADS)
    g_nw = lw["gdn_norm_w"].reshape(1, HEAD_DIM)
    (dq, dk, dv, daz, dsma, dcwq, dcwk, dcwv, dgal, dgdt, dgnw, *arrived) = _gdn_bwd(
        proj, dy_a, sv["ck_a"], lw["gdn_conv_w"], g_al, g_dt, g_nw, scatter=later_grads)
    g["gdn_conv_w"] = jnp.concatenate([dcwq, dcwk, dcwv], axis=1)
    g["gdn_a_log"] = jnp.sum(dgal, axis=(0, 1))[GDN_HEADS:2 * GDN_HEADS]
    g["gdn_dt_bias"] = jnp.sum(dgdt, axis=(0, 1))[GDN_HEADS:2 * GDN_HEADS]
    g["gdn_norm_w"] = jnp.sum(dgnw, axis=(0, 1))
    dsma = jnp.sum(dsma, axis=0)

    dproj = jnp.concatenate([dla, dlb, dlc, dsu, dsg, dq, dk, dv, daz, dcz, dcx, dcb, dcc, dsma.astype(bf16),
                             dsmc.astype(bf16), jnp.zeros((rows, 2 * LANE), bf16)], axis=1)
    if pack_own is None:
        g["w_in_p"] = _mm("proj_bwd_w", sv["h"], dproj, "tn", out_dtype=bf16)
        dh, own_arrived = _mm("proj_bwd_x", dproj, lw["w_in_p"], "nt"), None
    else:
        g["w_in_p"], *rest_arrived = _mm("proj_bwd_w_scatter", sv["h"], dproj, "tn", out_dtype=bf16,
                                         comm=(pack_own(g, False), False))
        dh, *win_arrived = _mm("proj_bwd_x_scatter", dproj, lw["w_in_p"], "nt", comm=(pack_own(g, True), False))
        own_arrived = win_arrived + rest_arrived
    dx, dnorm = _tok_bwd("rms_bwd", _rms_fn, [(sv["x"], D_MODEL, 0)], [lw["norm_w"].reshape(1, D_MODEL)],
                         [(dh, D_MODEL, 0)], [(0, f32)], 256, residuals={0: dx_next})
    g["norm_w"] = dnorm.reshape(-1)
    return dx, g, (arrived if later_grads is not None else None), own_arrived


def kernel(x, norm_w, w_in, gdn_conv_w, gdn_a_log, gdn_dt_bias, gdn_norm_w, s5_lam_re, s5_lam_im, s5_log_step, s5_b_re, s5_b_im, s5_c_re, s5_c_im, s5_d, s5_glu_w, s5_glu_b, m2_conv_w, m2_conv_b, m2_a_log, m2_dt_bias, m2_d, m2_norm_w, proj_a, proj_b, proj_c, w_out, final_norm_w, loss_target, m_norm_w, m_w_in, m_gdn_conv_w, m_gdn_a_log, m_gdn_dt_bias, m_gdn_norm_w, m_s5_lam_re, m_s5_lam_im, m_s5_log_step, m_s5_b_re, m_s5_b_im, m_s5_c_re, m_s5_c_im, m_s5_d, m_s5_glu_w, m_s5_glu_b, m_m2_conv_w, m_m2_conv_b, m_m2_a_log, m_m2_dt_bias, m_m2_d, m_m2_norm_w, m_proj_a, m_proj_b, m_proj_c, m_w_out, m_final_norm_w, v_norm_w, v_w_in, v_gdn_conv_w, v_gdn_a_log, v_gdn_dt_bias, v_gdn_norm_w, v_s5_lam_re, v_s5_lam_im, v_s5_log_step, v_s5_b_re, v_s5_b_im, v_s5_c_re, v_s5_c_im, v_s5_d, v_s5_glu_w, v_s5_glu_b, v_m2_conv_w, v_m2_conv_b, v_m2_a_log, v_m2_dt_bias, v_m2_d, v_m2_norm_w, v_proj_a, v_proj_b, v_proj_c, v_w_out, v_final_norm_w):
    w = dict(norm_w=norm_w, w_in=w_in, gdn_conv_w=gdn_conv_w, gdn_a_log=gdn_a_log, gdn_dt_bias=gdn_dt_bias,
             gdn_norm_w=gdn_norm_w, s5_lam_re=s5_lam_re, s5_lam_im=s5_lam_im, s5_log_step=s5_log_step,
             s5_b_re=s5_b_re, s5_b_im=s5_b_im, s5_c_re=s5_c_re, s5_c_im=s5_c_im, s5_d=s5_d, s5_glu_w=s5_glu_w,
             s5_glu_b=s5_glu_b, m2_conv_w=m2_conv_w, m2_conv_b=m2_conv_b, m2_a_log=m2_a_log, m2_dt_bias=m2_dt_bias,
             m2_d=m2_d, m2_norm_w=m2_norm_w, proj_a=proj_a, proj_b=proj_b, proj_c=proj_c, w_out=w_out,
             final_norm_w=final_norm_w)
    mom = dict(norm_w=m_norm_w, w_in=m_w_in, gdn_conv_w=m_gdn_conv_w, gdn_a_log=m_gdn_a_log,
               gdn_dt_bias=m_gdn_dt_bias, gdn_norm_w=m_gdn_norm_w, s5_lam_re=m_s5_lam_re, s5_lam_im=m_s5_lam_im,
               s5_log_step=m_s5_log_step, s5_b_re=m_s5_b_re, s5_b_im=m_s5_b_im, s5_c_re=m_s5_c_re,
               s5_c_im=m_s5_c_im, s5_d=m_s5_d, s5_glu_w=m_s5_glu_w, s5_glu_b=m_s5_glu_b, m2_conv_w=m_m2_conv_w,
               m2_conv_b=m_m2_conv_b, m2_a_log=m_m2_a_log, m2_dt_bias=m_m2_dt_bias, m2_d=m_m2_d,
               m2_norm_w=m_m2_norm_w, proj_a=m_proj_a, proj_b=m_proj_b, proj_c=m_proj_c, w_out=m_w_out,
               final_norm_w=m_final_norm_w)
    var = dict(norm_w=v_norm_w, w_in=v_w_in, gdn_conv_w=v_gdn_conv_w, gdn_a_log=v_gdn_a_log,
               gdn_dt_bias=v_gdn_dt_bias, gdn_norm_w=v_gdn_norm_w, s5_lam_re=v_s5_lam_re, s5_lam_im=v_s5_lam_im,
               s5_log_step=v_s5_log_step, s5_b_re=v_s5_b_re, s5_b_im=v_s5_b_im, s5_c_re=v_s5_c_re,
               s5_c_im=v_s5_c_im, s5_d=v_s5_d, s5_glu_w=v_s5_glu_w, s5_glu_b=v_s5_glu_b, m2_conv_w=v_m2_conv_w,
               m2_conv_b=v_m2_conv_b, m2_a_log=v_m2_a_log, m2_dt_bias=v_m2_dt_bias, m2_d=v_m2_d,
               m2_norm_w=v_m2_norm_w, proj_a=v_proj_a, proj_b=v_proj_b, proj_c=v_proj_c, w_out=v_w_out,
               final_norm_w=v_final_norm_w)
    me = lax.axis_index("x") * 4 + lax.axis_index("y") * 2 + lax.axis_index("c")
    x2 = x[0]
    tgt = loss_target[0]

    ra, rb = proj_a.shape[1], proj_b.shape[1]
    pabc = jnp.concatenate([proj_a, proj_b, proj_c], axis=1).astype(bf16)
    w_in16, glu16, w_out16 = w_in.astype(bf16), s5_glu_w.astype(bf16), w_out.astype(bf16)

    def shards(i):
        return [w_in16[i], glu16[i], pabc[i], w_out16[i], gdn_conv_w[i], m2_conv_w[i]]

    def cols(gathered):
        return jnp.concatenate([gathered[d] for d in range(N_DEV)], axis=1)

    def rows_of(gathered):
        return gathered.reshape(-1, gathered.shape[-1])

    def full_weights(i, gathered):
        g_win, g_glu, g_pabc, g_wout, g_gcv, g_mcv = gathered
        pf = cols(g_pabc)
        lw = dict(w_in_p=_win_to_padded(cols(g_win)), s5_glu_w=rows_of(g_glu), proj_a=pf[:ra],
                  proj_b=pf[ra:ra + rb], proj_c=pf[ra + rb:], w_out=rows_of(g_wout),
                  gdn_conv_w=cols(g_gcv), m2_conv_w=cols(g_mcv))
        for k in SMALL:
            if k != "final_norm_w":
                lw[k] = w[k][i]
        return lw

    layers, saved = [], []
    act = x2
    gathered = _exchange("gather_weights", shards(0), True)
    for i in range(DEPTH):
        layers.append(full_weights(i, gathered))
        act, sv, gathered = _layer_fwd(act, layers[i], shards(i + 1) if i + 1 < DEPTH else None)
        saved.append(sv)
    loss_row, dact, dfinal = _loss_grad(act, final_norm_w.reshape(1, D_MODEL), tgt)
    loss = lax.psum(loss_row[0, 0], ("x", "y", "c"))

    def col_blocks(full):
        r = full.shape[0]
        return full.reshape(r, N_DEV, -1).transpose(1, 0, 2)

    def row_blocks(full):
        return full.reshape(N_DEV, -1, full.shape[1])

    def packed_grads(g, w_in_part=None):
        first = [col_blocks(_win_from_padded(g["w_in_p"]))] if w_in_part in (None, True) else []
        if w_in_part is True:
            return first
        return first + [row_blocks(g["s5_glu_w"].astype(bf16)),
                        col_blocks(jnp.concatenate([g["proj_a"], g["proj_b"], g["proj_c"]], axis=0)),
                        row_blocks(g["w_out"])]

    grads, arrived = [None] * DEPTH, [None] * DEPTH
    later = None
    for i in reversed(range(DEPTH)):
        dact, grads[i], got, own = _layer_bwd(dact, layers[i], saved[i], later, packed_grads if i == 0 else None)
        if got is not None:
            arrived[i + 1] = got
        saved[i] = None
        if i > 0:
            later = packed_grads(grads[i])
    arrived[0] = own

    g_out, delta, new_m, new_v = {}, {}, {}, {}
    for k, buf, row_off in (("w_in", 0, 0), ("s5_glu_w", 1, 0), ("proj_a", 2, 0), ("proj_b", 2, ra),
                            ("proj_c", 2, ra + rb), ("w_out", 3, 0)):
        recvs = [arrived[i][buf] for i in range(DEPTH)]
        g_out[k], delta[k], new_m[k], new_v[k] = _adamw_slots("adamw_" + k, recvs, row_off, w[k], mom[k], var[k])

    names_small = [k for k in SMALL if k != "final_norm_w"]
    small_parts = [jnp.stack([grads[i][k] for i in range(DEPTH)]) for k in names_small + list(CONV)]
    small_parts.append(dfinal.reshape(-1))
    small_shapes = [p.shape for p in small_parts]
    small_local = _pack(small_parts)
    small_sum = _sum_slots("sum_small", _exchange("gather_small", [small_local], True)[0])
    unp = _unpack(small_sum, small_shapes)
    for k, val in zip(names_small, unp):
        g_out[k] = val
    for j, k in enumerate(CONV):
        width = w[k].shape[2]
        g_out[k] = lax.dynamic_slice_in_dim(unp[len(names_small) + j], me * width, width, axis=2)
    g_out["final_norm_w"] = unp[-1]

    rest = [k for k in WEIGHTS if k not in BIG]
    rest_shapes = [w[k].shape for k in rest]
    packed = [_pack([src[k] for k in rest]) for src in (w, g_out, mom, var)]
    d, nm, nv = _adamw("adamw_small", *packed)
    for dst, arr in ((delta, d), (new_m, nm), (new_v, nv)):
        for k, val in zip(rest, _unpack(arr, rest_shapes)):
            dst[k] = val

    grad_x = dact.reshape(x.shape)
    return (loss, grad_x, *[g_out[k] for k in WEIGHTS], *[delta[k] for k in WEIGHTS],
            *[new_m[k] for k in WEIGHTS], *[new_v[k] for k in WEIGHTS])
```
